```python
import jax, jax.numpy as jnp
from jax import lax
import numpy as np

D_MODEL = 2048
BATCH = 4
SEQ = 4096
DEPTH = 1
DEC_BATCH = 128
DEC_SEQ = 4
PAST_LEN = 16384
PAGE_SIZE = 128

N_META = 16
A_HEADS = 8
A_DK = 128
A_DV = 128
A_CONV = 4
A_CHUNK = 64
A_QK = A_HEADS * A_DK
A_V = A_HEADS * A_DV
A_CONV_DIM = 2 * A_QK + A_V
B_HEADS = 16
B_KV_HEADS = 4
B_HD = 64
B_GROUP = B_HEADS // B_KV_HEADS
B_Q = B_HEADS * B_HD
B_KV = B_KV_HEADS * B_HD
WINDOW = 128
ROPE_THETA = 10000.0
N_GROUPS = 4
EXPERTS_PER_GROUP = 8
N_EXPERTS = N_GROUPS * EXPERTS_PER_GROUP
TOP_K = 2
D_EXPERT = 512
MOE_BLOCK = 128
EPS = 1e-6
PROJ_SIZES = (A_CONV_DIM, A_HEADS, A_HEADS, A_V, B_Q, B_KV, B_KV, D_MODEL, D_MODEL)
IN_DIM = A_CONV_DIM + 2 * A_HEADS + A_V + B_Q + 2 * B_KV + 2 * D_MODEL

kernel_name = "hybrid_gdn_swa_hiermoe_step"

F32 = jnp.float32


def rmsnorm(x, w):
    xf = x.astype(F32)
    y = xf * lax.rsqrt(jnp.mean(xf * xf, -1, keepdims=True) + EPS)
    return (y * w.astype(F32)).astype(x.dtype)


def l2norm(x):
    return x * lax.rsqrt(jnp.sum(x * x, -1, keepdims=True) + EPS)


def split_proj(p):
    idx, off = [], 0
    for s in PROJ_SIZES[:-1]:
        off += s
        idx.append(off)
    return jnp.split(p, idx, axis=-1)


def causal_conv(u, buf, w):
    full = jnp.concatenate([buf.astype(u.dtype), u], 1)
    L = u.shape[1]
    out = sum(full[:, j:j + L] * w[j] for j in range(A_CONV))
    return jax.nn.silu(out), full[:, -(A_CONV - 1):]


def delta_prep(qkv, b_logit, a_logit, a_log, dt_bias):
    Bn, L, _ = qkv.shape
    qkv = qkv.astype(F32)
    q, k, v = jnp.split(qkv, [A_QK, 2 * A_QK], -1)
    heads = lambda t, d: jnp.swapaxes(t.reshape(Bn, L, A_HEADS, d), 1, 2)
    q = l2norm(heads(q, A_DK)) * (A_DK ** -0.5)
    k = l2norm(heads(k, A_DK))
    v = heads(v, A_DV)
    beta = jnp.swapaxes(jax.nn.sigmoid(b_logit.astype(F32)), 1, 2)
    g = jnp.swapaxes(-jnp.exp(a_log.astype(F32)) * jax.nn.softplus(a_logit.astype(F32) + dt_bias.astype(F32)), 1, 2)
    return q, k, v, g, beta


def delta_chunk(S, q, k, v, g, beta):
    S = S.astype(F32)
    C = q.shape[2]
    G = jnp.cumsum(g, -1)
    lower = jnp.tril(jnp.ones((C, C), bool))
    strict = jnp.tril(jnp.ones((C, C), bool), -1)
    gamma = jnp.exp(jnp.where(lower, G[..., :, None] - G[..., None, :], -jnp.inf))
    kb = k * beta[..., None]
    M = jnp.where(strict, jnp.einsum('bhid,bhjd->bhij', kb, k) * gamma, 0.0)
    eye = jnp.eye(C, dtype=F32)
    rhs = jnp.concatenate([v * beta[..., None], kb * jnp.exp(G)[..., None]], -1)
    sol = lax.linalg.triangular_solve(jnp.broadcast_to(eye, M.shape) + M, rhs,
                                      left_side=True, lower=True, unit_diagonal=True)
    u, w = sol[..., :A_DV], sol[..., A_DV:]
    v_new = u - jnp.einsum('bhcd,bhde->bhce', w, S)
    attn = jnp.einsum('bhid,bhjd->bhij', q, k) * gamma
    o = jnp.einsum('bhcd,bhde->bhce', q * jnp.exp(G)[..., None], S) + jnp.einsum('bhij,bhje->bhie', attn, v_new)
    G_last = G[..., -1:]
    S_new = S * jnp.exp(G_last)[..., None] + jnp.einsum('bhcd,bhce->bhde', k * jnp.exp(G_last - G)[..., None], v_new)
    return S_new, o


def deltanet_prompt(q, k, v, g, beta):
    Bn, L = q.shape[0], q.shape[2]
    n = (L - N_META) // A_CHUNK
    S = jnp.zeros((Bn, A_HEADS, A_DK, A_DV), F32)
    S, o_meta = delta_chunk(S, q[:, :, :N_META], k[:, :, :N_META], v[:, :, :N_META],
                            g[:, :, :N_META], beta[:, :, :N_META])

    def to_chunks(t):
        t = t[:, :, N_META:]
        return jnp.moveaxis(t.reshape(t.shape[0], t.shape[1], n, A_CHUNK, *t.shape[3:]), 2, 0)

    S, o = lax.scan(lambda s, xs: delta_chunk(s, *xs), S, tuple(to_chunks(t) for t in (q, k, v, g, beta)))
    o = jnp.moveaxis(o, 0, 2).reshape(Bn, A_HEADS, n * A_CHUNK, A_DV)
    return S, jnp.concatenate([o_meta, o], 2)


def delta_out(o, z, norm_w):
    Bn, _, L, _ = o.shape
    o = rmsnorm(jnp.swapaxes(o, 1, 2), norm_w)
    return (o * jax.nn.silu(z.astype(F32)).reshape(Bn, L, A_HEADS, A_DV)).reshape(Bn, L, A_V)


def rope(x, pos):
    half = B_HD // 2
    inv_freq = ROPE_THETA ** (-jnp.arange(half, dtype=F32) / half)
    ang = pos.astype(F32)[:, None] * inv_freq[None, :]
    cos, sin = jnp.cos(ang)[:, None, :], jnp.sin(ang)[:, None, :]
    xf = x.astype(F32)
    x1, x2 = xf[..., :half], xf[..., half:]
    return jnp.concatenate([x1 * cos - x2 * sin, x2 * cos + x1 * sin], -1)


def sink_attention(q, k, v, mask, sinks):
    s = jnp.einsum('...qhgd,...shd->...hgqs', q.astype(F32), k.astype(F32)) * (B_HD ** -0.5)
    s = jnp.where(mask[..., None, None, :, :], s, -jnp.inf)
    sink = jnp.broadcast_to(sinks.astype(F32).reshape(B_KV_HEADS, B_GROUP, 1, 1), s.shape[:-1] + (1,))
    p = jax.nn.softmax(jnp.concatenate([s, sink], -1), -1)[..., :-1]
    return jnp.einsum('...hgqs,...shd->...qhgd', p, v.astype(F32))


def swa_prompt(q, k, v, sinks):
    Bn, L = q.shape[:2]
    n = (L - N_META) // WINDOW
    km, vm = k[:, :N_META], v[:, :N_META]
    qm = q[:, :N_META].reshape(Bn, N_META, B_KV_HEADS, B_GROUP, B_HD)
    o_meta = sink_attention(qm, km, vm, jnp.tril(jnp.ones((N_META, N_META), bool)), sinks)
    qr = q[:, N_META:].reshape(Bn, n, WINDOW, B_KV_HEADS, B_GROUP, B_HD)

    def band_keys(t):
        tm = jnp.broadcast_to(t[:, None, :N_META], (Bn, n, N_META, B_KV_HEADS, B_HD))
        tr = t[:, N_META:]
        tp = jnp.concatenate([jnp.zeros_like(tr[:, :WINDOW]), tr], 1).reshape(Bn, n + 1, WINDOW, B_KV_HEADS, B_HD)
        return jnp.concatenate([tm, tp[:, :-1], tp[:, 1:]], 2)

    i = jnp.arange(WINDOW)[:, None]
    j = jnp.arange(2 * WINDOW)[None, :]
    blk = jnp.arange(n)[:, None, None]
    band = (j > i) & (j <= i + WINDOW) & ((blk > 0) | (j >= WINDOW))
    mask = jnp.concatenate([jnp.ones((n, WINDOW, N_META), bool), band], -1)
    o_real = sink_attention(qr, band_keys(k), band_keys(v), mask, sinks)
    y = jnp.concatenate([o_meta.reshape(Bn, N_META, B_Q), o_real.reshape(Bn, n * WINDOW, B_Q)], 1)
    new_k = jnp.concatenate([km, k[:, -WINDOW:]], 1)
    new_v = jnp.concatenate([vm, v[:, -WINDOW:]], 1)
    return y, new_k, new_v


def swa_sample(q, k, v, cache_k, cache_v, sinks):
    Bn, T = q.shape[:2]
    kk = jnp.concatenate([cache_k.astype(k.dtype), k], 1)
    vv = jnp.concatenate([cache_v.astype(v.dtype), v], 1)
    t = jnp.arange(T)[:, None]
    r = jnp.arange(N_META + WINDOW + T)[None, :]
    wj = r - N_META
    win_pos = PAST_LEN - WINDOW + wj
    nm = r - N_META - WINDOW
    mask = ((r < N_META)
            | ((wj >= 0) & (wj < WINDOW) & (win_pos >= N_META) & (wj >= t + 1))
            | ((nm >= 0) & (nm <= t) & (nm > t - WINDOW)))
    o = sink_attention(q.reshape(Bn, T, B_KV_HEADS, B_GROUP, B_HD), kk, vv, mask, sinks)
    new_k = jnp.concatenate([kk[:, :N_META], kk[:, -WINDOW:]], 1)
    new_v = jnp.concatenate([vv[:, :N_META], vv[:, -WINDOW:]], 1)
    return o.reshape(Bn, T, B_Q), new_k, new_v


def hier_moe(x, w_rg, b_rg, w_re, b_re, w_gate, w_up, w_down):
    shp = x.shape
    xt = x.reshape(-1, D_MODEL)
    T = xt.shape[0]
    lg = jnp.matmul(xt, w_rg).astype(F32) + b_rg.astype(F32)
    pg = jax.nn.softmax(lg, -1)
    p_grp, grp = lax.top_k(pg, 1)
    le = (jnp.matmul(xt, w_re).astype(F32) + b_re.astype(F32)).reshape(T, N_GROUPS, EXPERTS_PER_GROUP)
    le = jnp.take_along_axis(le, grp[:, :, None], 1)[:, 0]
    top_p, top_i = lax.top_k(jax.nn.softmax(le, -1), TOP_K)
    gate = p_grp * top_p / jnp.sum(top_p, -1, keepdims=True)
    expert = grp * EXPERTS_PER_GROUP + top_i
    A = T * TOP_K
    e_flat = expert.reshape(-1)
    tok_flat = jnp.repeat(jnp.arange(T, dtype=jnp.int32), TOP_K)
    order = jnp.argsort(e_flat)
    e_s, tok_s, g_s = e_flat[order], tok_flat[order], gate.reshape(-1)[order]
    counts = jnp.zeros((N_EXPERTS,), jnp.int32).at[e_flat].add(1)
    start = jnp.cumsum(counts) - counts
    padded = (counts + MOE_BLOCK - 1) // MOE_BLOCK * MOE_BLOCK
    pend = jnp.cumsum(padded)
    pstart = pend - padded
    dest = pstart[e_s] + (jnp.arange(A, dtype=jnp.int32) - start[e_s])
    n_blocks = -(-A // MOE_BLOCK) + N_EXPERTS
    P = n_blocks * MOE_BLOCK
    slot_tok = jnp.full((P,), T, jnp.int32).at[dest].set(tok_s)
    x_pad = jnp.concatenate([xt, jnp.zeros((1, D_MODEL), xt.dtype)], 0)
    xs = x_pad[slot_tok].reshape(n_blocks, MOE_BLOCK, D_MODEL)
    blk_e = jnp.minimum(jnp.searchsorted(pend, jnp.arange(n_blocks, dtype=jnp.int32) * MOE_BLOCK, side='right'),
                        N_EXPERTS - 1)

    def run(args):
        xb, e = args
        hb = jax.nn.silu(xb @ w_gate[e]) * (xb @ w_up[e])
        return hb @ w_down[e]

    ys = lax.map(run, (xs, blk_e)).reshape(P, D_MODEL)
    out = jnp.zeros((T, D_MODEL), F32).at[tok_s].add(g_s[:, None] * ys[dest].astype(F32))
    return out.astype(x.dtype).reshape(shp)


def trunk_layer(h, pos, conv_buf, s0, kv_cache, wts, prompt):
    (norm1_w, w_in, conv_w, a_log, dt_bias, a_norm_w, w_up_a, q_norm_w, k_norm_w, sinks,
     w_up_b, w_o, norm2_w, w_rg, b_rg, w_re, b_re, w_gate, w_up, w_down) = wts
    dt = h.dtype
    Bn, L, _ = h.shape
    xn = rmsnorm(h, norm1_w)
    qkv_a, b_a, a_a, z_a, q_b, k_b, v_b, g_a, g_b = split_proj(xn @ w_in)
    qkv, conv_new = causal_conv(qkv_a, conv_buf, conv_w)
    q, k, v, g, beta = delta_prep(qkv, b_a, a_a, a_log, dt_bias)
    if prompt:
        s_new, o = deltanet_prompt(q, k, v, g, beta)
    else:
        s_new, o = delta_chunk(s0, q, k, v, g, beta)
    y_a = delta_out(o, z_a, a_norm_w)
    qb = rope(rmsnorm(q_b.reshape(Bn, L, B_HEADS, B_HD), q_norm_w), pos)
    kb = rope(rmsnorm(k_b.reshape(Bn, L, B_KV_HEADS, B_HD), k_norm_w), pos)
    vb = v_b.reshape(Bn, L, B_KV_HEADS, B_HD)
    if prompt:
        y_b, new_k, new_v = swa_prompt(qb, kb, vb, sinks)
    else:
        y_b, new_k, new_v = swa_sample(qb, kb, vb, kv_cache[0], kv_cache[1], sinks)
    merged = jax.nn.sigmoid(g_a.astype(F32)) * (y_a @ w_up_a) + jax.nn.sigmoid(g_b.astype(F32)) * (y_b @ w_up_b)
    h = (h + merged @ w_o).astype(dt)
    h = (h + hier_moe(rmsnorm(h, norm2_w), w_rg, b_rg, w_re, b_re, w_gate, w_up, w_down)).astype(dt)
    return h, (s_new, conv_new, new_k, new_v)


def setup_inputs(seed: int = 0) -> dict:
    key = jax.random.key(seed)
    ks = jax.random.split(key, 32)
    nrm = lambda k, shape, scale: jax.random.normal(k, shape, F32) * scale
    gain = lambda k, shape: 1.0 + 0.02 * jax.random.normal(k, shape, F32)
    dt0 = jnp.exp(jax.random.uniform(ks[10], (DEPTH, A_HEADS), F32, np.log(1e-3), np.log(1e-1)))
    return {
        "x_prompt": nrm(ks[0], (BATCH, SEQ, D_MODEL), 1.0),
        "x_sample": nrm(ks[1], (DEC_BATCH, DEC_SEQ, D_MODEL), 1.0),
        "state_delta": nrm(ks[2], (DEPTH, DEC_BATCH, A_HEADS, A_DK, A_DV), 0.5),
        "state_conv": nrm(ks[3], (DEPTH, DEC_BATCH, A_CONV - 1, A_CONV_DIM), 1.0),
        "cache_swa_k": nrm(ks[4], (DEPTH, DEC_BATCH, N_META + WINDOW, B_KV_HEADS, B_HD), 1.0),
        "cache_swa_v": nrm(ks[5], (DEPTH, DEC_BATCH, N_META + WINDOW, B_KV_HEADS, B_HD), 1.0),
        "meta_tokens": nrm(ks[6], (N_META, D_MODEL), 1.0),
        "norm1_w": gain(ks[7], (DEPTH, D_MODEL)),
        "w_in": nrm(ks[8], (DEPTH, D_MODEL, IN_DIM), D_MODEL ** -0.5),
        "conv_w": nrm(ks[9], (DEPTH, A_CONV, A_CONV_DIM), A_CONV ** -0.5),
        "a_log": jnp.log(jax.random.uniform(ks[11], (DEPTH, A_HEADS), F32, 1.0, 16.0)),
        "dt_bias": dt0 + jnp.log(-jnp.expm1(-dt0)),
        "a_norm_w": gain(ks[12], (DEPTH, A_DV)),
        "w_up_a": nrm(ks[13], (DEPTH, A_V, D_MODEL), A_V ** -0.5),
        "q_norm_w": gain(ks[14], (DEPTH, B_HD)),
        "k_norm_w": gain(ks[15], (DEPTH, B_HD)),
        "sinks": nrm(ks[16], (DEPTH, B_HEADS), 1.0),
        "w_up_b": nrm(ks[17], (DEPTH, B_Q, D_MODEL), B_Q ** -0.5),
        "w_o": nrm(ks[18], (DEPTH, D_MODEL, D_MODEL), D_MODEL ** -0.5),
        "norm2_w": gain(ks[19], (DEPTH, D_MODEL)),
        "w_router_group": nrm(ks[20], (DEPTH, D_MODEL, N_GROUPS), D_MODEL ** -0.5),
        "b_router_group": nrm(ks[21], (DEPTH, N_GROUPS), 0.01),
        "w_router_expert": nrm(ks[22], (DEPTH, D_MODEL, N_EXPERTS), D_MODEL ** -0.5),
        "b_router_expert": nrm(ks[23], (DEPTH, N_EXPERTS), 0.01),
        "w_gate": nrm(ks[24], (DEPTH, N_EXPERTS, D_MODEL, D_EXPERT), D_MODEL ** -0.5),
        "w_up": nrm(ks[25], (DEPTH, N_EXPERTS, D_MODEL, D_EXPERT), D_MODEL ** -0.5),
        "w_down": nrm(ks[26], (DEPTH, N_EXPERTS, D_EXPERT, D_MODEL), D_EXPERT ** -0.5),
    }


def reference(x_prompt, x_sample, state_delta, state_conv, cache_swa_k, cache_swa_v, meta_tokens,
              norm1_w, w_in, conv_w, a_log, dt_bias, a_norm_w, w_up_a, q_norm_w, k_norm_w, sinks,
              w_up_b, w_o, norm2_w, w_router_group, b_router_group, w_router_expert, b_router_expert,
              w_gate, w_up, w_down):
    layer_w = (norm1_w, w_in, conv_w, a_log, dt_bias, a_norm_w, w_up_a, q_norm_w, k_norm_w, sinks,
               w_up_b, w_o, norm2_w, w_router_group, b_router_group, w_router_expert, b_router_expert,
               w_gate, w_up, w_down)
    Bp, Sp = x_prompt.shape[:2]
    hp = jnp.concatenate([jnp.broadcast_to(meta_tokens.astype(x_prompt.dtype)[None], (Bp, N_META, D_MODEL)),
                          x_prompt], 1)
    hs = x_sample
    pos_p = jnp.arange(N_META + Sp, dtype=jnp.int32)
    pos_s = PAST_LEN + jnp.arange(x_sample.shape[1], dtype=jnp.int32)
    conv0 = jnp.zeros((Bp, A_CONV - 1, A_CONV_DIM), x_prompt.dtype)
    new_p, new_s = [], []
    for l in range(DEPTH):
        wts = tuple(w[l] for w in layer_w)
        hp, st_p = trunk_layer(hp, pos_p, conv0, None, None, wts, True)
        hs, st_s = trunk_layer(hs, pos_s, state_conv[l], state_delta[l], (cache_swa_k[l], cache_swa_v[l]), wts, False)
        new_p.append(st_p)
        new_s.append(st_s)
    stk = lambda sts, i: jnp.stack([s[i] for s in sts])
    y_prompt = hp[:, N_META:]
    return (y_prompt, hs, stk(new_p, 0), stk(new_p, 1), stk(new_p, 2), stk(new_p, 3),
            stk(new_s, 0), stk(new_s, 1), stk(new_s, 2), stk(new_s, 3))
```

```python
import functools

import jax
import jax.numpy as jnp
from jax import lax
from jax.experimental import pallas as pl
from jax.experimental.pallas import tpu as pltpu

F32 = jnp.float32
BF16 = jnp.bfloat16

D_MODEL = 2048
N_META = 16
A_HEADS = 8
A_DK = 128
A_DV = 128
A_CONV = 4
A_CHUNK = 64
A_QK = A_HEADS * A_DK
A_V = A_HEADS * A_DV
A_CONV_DIM = 2 * A_QK + A_V
B_HEADS = 16
B_KV_HEADS = 4
B_HD = 64
B_GROUP = B_HEADS // B_KV_HEADS
B_Q = B_HEADS * B_HD
B_KV = B_KV_HEADS * B_HD
WINDOW = 128
ROPE_THETA = 10000.0
PAST_LEN = 16384
N_GROUPS = 4
EXPERTS_PER_GROUP = 8
N_EXPERTS = N_GROUPS * EXPERTS_PER_GROUP
TOP_K = 2
D_EXPERT = 512
EPS = 1e-6

COL_GA = 0
COL_GB = COL_GA + D_MODEL
COL_QKV = COL_GB + D_MODEL
COL_Z = COL_QKV + A_CONV_DIM
COL_QB = COL_Z + A_V
COL_KB = COL_QB + B_Q
COL_VB = COL_KB + B_KV
PROJ_MAIN = COL_VB + B_KV
SMALL_PAD = 128
ROUTER_PAD = 128

MOE_ROWS = 256
VMEM_LIMIT = 56 * 1024 * 1024


def _row_chunk(tm):
    return 256 if tm % 256 == 0 else tm // 2


def _inproj_kernel(x_ref, g_ref, w_ref, ws_ref, o_ref, os_ref, xn_ref, *, tm):
    rc = _row_chunk(tm)

    @pl.when(pl.program_id(1) == 0)
    def _():
        def body(r, c):
            sl = pl.ds(pl.multiple_of(r * rc, 8), rc)
            x = x_ref[sl, :]
            ms = jnp.mean(x * x, axis=-1, keepdims=True)
            xn_ref[sl, :] = ((x * lax.rsqrt(ms + EPS)) * g_ref[...]).astype(BF16)
            return c

        lax.fori_loop(0, tm // rc, body, 0)
        os_ref[...] = jnp.dot(xn_ref[...], ws_ref[...], preferred_element_type=F32)

    o_ref[...] = jnp.dot(xn_ref[...], w_ref[...], preferred_element_type=F32)


def _inproj(x, gain, w_main, w_small, tm, tn=512):
    m = x.shape[0]
    assert m % tm == 0 and PROJ_MAIN % tn == 0
    return pl.pallas_call(
        functools.partial(_inproj_kernel, tm=tm),
        grid=(m // tm, PROJ_MAIN // tn),
        in_specs=[
            pl.BlockSpec((tm, D_MODEL), lambda i, j: (i, 0)),
            pl.BlockSpec((1, D_MODEL), lambda i, j: (0, 0)),
            pl.BlockSpec((D_MODEL, tn), lambda i, j: (0, j)),
            pl.BlockSpec((D_MODEL, SMALL_PAD), lambda i, j: (0, 0)),
        ],
        out_specs=[
            pl.BlockSpec((tm, tn), lambda i, j: (i, j)),
            pl.BlockSpec((tm, SMALL_PAD), lambda i, j: (i, 0)),
        ],
        out_shape=[
            jax.ShapeDtypeStruct((m, PROJ_MAIN), F32),
            jax.ShapeDtypeStruct((m, SMALL_PAD), F32),
        ],
        scratch_shapes=[pltpu.VMEM((tm, D_MODEL), BF16)],
        compiler_params=pltpu.CompilerParams(
            dimension_semantics=("arbitrary", "arbitrary"), vmem_limit_bytes=VMEM_LIMIT),
        name="inproj",
    )(x, gain, w_main, w_small)


def _merge_kernel(ya_ref, yb_ref, ga_ref, gb_ref, wa_ref, wb_ref, o_ref):
    ua = jnp.dot(ya_ref[...].astype(BF16), wa_ref[...], preferred_element_type=F32)
    ub = jnp.dot(yb_ref[...].astype(BF16), wb_ref[...], preferred_element_type=F32)
    merged = jax.nn.sigmoid(ga_ref[...]) * ua + jax.nn.sigmoid(gb_ref[...]) * ub
    o_ref[...] = merged.astype(BF16)


def _merge(ya, yb, proj, wa, wb, tm):
    m = ya.shape[0]
    assert m % tm == 0
    return pl.pallas_call(
        _merge_kernel,
        grid=(m // tm,),
        in_specs=[
            pl.BlockSpec((tm, A_V), lambda i: (i, 0)),
            pl.BlockSpec((tm, B_Q), lambda i: (i, 0)),
            pl.BlockSpec((tm, D_MODEL), lambda i: (i, COL_GA // D_MODEL)),
            pl.BlockSpec((tm, D_MODEL), lambda i: (i, COL_GB // D_MODEL)),
            pl.BlockSpec((A_V, D_MODEL), lambda i: (0, 0)),
            pl.BlockSpec((B_Q, D_MODEL), lambda i: (0, 0)),
        ],
        out_specs=pl.BlockSpec((tm, D_MODEL), lambda i: (i, 0)),
        out_shape=jax.ShapeDtypeStruct((m, D_MODEL), BF16),
        compiler_params=pltpu.CompilerParams(
            dimension_semantics=("arbitrary",), vmem_limit_bytes=VMEM_LIMIT),
        name="merge",
    )(ya, yb, proj, proj, wa, wb)


def _outproj_kernel(m_ref, h_ref, wo_ref, g_ref, wr_ref, br_ref, h2_ref, xn_ref, lg_ref):
    h2 = h_ref[...] + jnp.dot(m_ref[...], wo_ref[...], preferred_element_type=F32)
    h2_ref[...] = h2
    ms = jnp.mean(h2 * h2, axis=-1, keepdims=True)
    xn = ((h2 * lax.rsqrt(ms + EPS)) * g_ref[...]).astype(BF16)
    xn_ref[...] = xn
    lg_ref[...] = jnp.dot(xn, wr_ref[...], preferred_element_type=F32) + br_ref[...]


def _outproj(merged, h, wo, gain2, w_router, b_router, tm):
    m = h.shape[0]
    assert m % tm == 0
    return pl.pallas_call(
        _outproj_kernel,
        grid=(m // tm,),
        in_specs=[
            pl.BlockSpec((tm, D_MODEL), lambda i: (i, 0)),
            pl.BlockSpec((tm, D_MODEL), lambda i: (i, 0)),
            pl.BlockSpec((D_MODEL, D_MODEL), lambda i: (0, 0)),
            pl.BlockSpec((1, D_MODEL), lambda i: (0, 0)),
            pl.BlockSpec((D_MODEL, ROUTER_PAD), lambda i: (0, 0)),
            pl.BlockSpec((1, ROUTER_PAD), lambda i: (0, 0)),
        ],
        out_specs=[
            pl.BlockSpec((tm, D_MODEL), lambda i: (i, 0)),
            pl.BlockSpec((tm, D_MODEL), lambda i: (i, 0)),
            pl.BlockSpec((tm, ROUTER_PAD), lambda i: (i, 0)),
        ],
        out_shape=[
            jax.ShapeDtypeStruct((m, D_MODEL), F32),
            jax.ShapeDtypeStruct((m, D_MODEL), BF16),
            jax.ShapeDtypeStruct((m, ROUTER_PAD), F32),
        ],
        compiler_params=pltpu.CompilerParams(
            dimension_semantics=("arbitrary",), vmem_limit_bytes=VMEM_LIMIT),
        name="outproj",
    )(merged, h, wo, gain2, w_router, b_router)


def _moe_kernel(blk_e_ref, nused_ref, x_ref, wg_ref, wu_ref, wd_ref, o_ref, wgb, wub, wdb):
    i = pl.program_id(0)
    e = blk_e_ref[i]
    e_prev = blk_e_ref[jnp.maximum(i - 1, 0)]
    used = i < nused_ref[0]

    @pl.when(jnp.logical_and(used, jnp.logical_or(i == 0, e != e_prev)))
    def _():
        def cast_in(r, c):
            sl = pl.ds(pl.multiple_of(r * 256, 256), 256)
            wgb[sl, :] = wg_ref[sl, :].astype(BF16)
            wub[sl, :] = wu_ref[sl, :].astype(BF16)
            return c

        lax.fori_loop(0, D_MODEL // 256, cast_in, 0)

        def cast_out(r, c):
            sl = pl.ds(pl.multiple_of(r * 128, 128), 128)
            wdb[sl, :] = wd_ref[sl, :].astype(BF16)
            return c

        lax.fori_loop(0, D_EXPERT // 128, cast_out, 0)

    @pl.when(used)
    def _():
        x = x_ref[...]
        g = jnp.dot(x, wgb[...], preferred_element_type=F32)
        u = jnp.dot(x, wub[...], preferred_element_type=F32)
        hb = (g * jax.nn.sigmoid(g)) * u
        o_ref[...] = jnp.dot(hb.astype(BF16), wdb[...], preferred_element_type=F32)

    @pl.when(jnp.logical_not(used))
    def _():
        o_ref[...] = jnp.zeros_like(o_ref)


def _moe_ffn(xs, blk_e, n_used, w_gate, w_up, w_down):
    p = xs.shape[0]
    nb = p // MOE_ROWS
    grid_spec = pltpu.PrefetchScalarGridSpec(
        num_scalar_prefetch=2,
        grid=(nb,),
        in_specs=[
            pl.BlockSpec((MOE_ROWS, D_MODEL), lambda i, be, nu: (i, 0)),
            pl.BlockSpec((None, D_MODEL, D_EXPERT), lambda i, be, nu: (be[i], 0, 0)),
            pl.BlockSpec((None, D_MODEL, D_EXPERT), lambda i, be, nu: (be[i], 0, 0)),
            pl.BlockSpec((None, D_EXPERT, D_MODEL), lambda i, be, nu: (be[i], 0, 0)),
        ],
        out_specs=pl.BlockSpec((MOE_ROWS, D_MODEL), lambda i, be, nu: (i, 0)),
        scratch_shapes=[
            pltpu.VMEM((D_MODEL, D_EXPERT), BF16),
            pltpu.VMEM((D_MODEL, D_EXPERT), BF16),
            pltpu.VMEM((D_EXPERT, D_MODEL), BF16),
        ],
    )
    return pl.pallas_call(
        _moe_kernel,
        grid_spec=grid_spec,
        out_shape=jax.ShapeDtypeStruct((p, D_MODEL), F32),
        compiler_params=pltpu.CompilerParams(
            dimension_semantics=("arbitrary",), vmem_limit_bytes=VMEM_LIMIT),
        name="moe_ffn",
    )(blk_e, n_used, xs, w_gate, w_up, w_down)


def _rmsnorm(x, w):
    y = x * lax.rsqrt(jnp.mean(x * x, -1, keepdims=True) + EPS)
    return y * w


def _l2norm(x):
    return x * lax.rsqrt(jnp.sum(x * x, -1, keepdims=True) + EPS)


def _causal_conv(u, buf, w):
    full = jnp.concatenate([buf, u], 1)
    L = u.shape[1]
    out = sum(full[:, j:j + L] * w[j] for j in range(A_CONV))
    return jax.nn.silu(out), full[:, -(A_CONV - 1):]


def _delta_prep(qkv, b_logit, a_logit, a_log, dt_bias):
    Bn, L, _ = qkv.shape
    q, k, v = jnp.split(qkv, [A_QK, 2 * A_QK], -1)
    heads = lambda t, d: jnp.swapaxes(t.reshape(Bn, L, A_HEADS, d), 1, 2)
    q = _l2norm(heads(q, A_DK)) * (A_DK ** -0.5)
    k = _l2norm(heads(k, A_DK))
    v = heads(v, A_DV)
    beta = jnp.swapaxes(jax.nn.sigmoid(b_logit), 1, 2)
    g = jnp.swapaxes(-jnp.exp(a_log) * jax.nn.softplus(a_logit + dt_bias), 1, 2)
    return q, k, v, g, beta


def _delta_chunk(S, q, k, v, g, beta):
    C = q.shape[2]
    G = jnp.cumsum(g, -1)
    lower = jnp.tril(jnp.ones((C, C), bool))
    strict = jnp.tril(jnp.ones((C, C), bool), -1)
    gamma = jnp.exp(jnp.where(lower, G[..., :, None] - G[..., None, :], -jnp.inf))
    kb = k * beta[..., None]
    M = jnp.where(strict, jnp.einsum('bhid,bhjd->bhij', kb, k) * gamma, 0.0)
    eye = jnp.eye(C, dtype=F32)
    rhs = jnp.concatenate([v * beta[..., None], kb * jnp.exp(G)[..., None]], -1)
    sol = lax.linalg.triangular_solve(jnp.broadcast_to(eye, M.shape) + M, rhs,
                                      left_side=True, lower=True, unit_diagonal=True)
    u, w = sol[..., :A_DV], sol[..., A_DV:]
    v_new = u - jnp.einsum('bhcd,bhde->bhce', w, S)
    attn = jnp.einsum('bhid,bhjd->bhij', q, k) * gamma
    o = (jnp.einsum('bhcd,bhde->bhce', q * jnp.exp(G)[..., None], S)
         + jnp.einsum('bhij,bhje->bhie', attn, v_new))
    G_last = G[..., -1:]
    S_new = (S * jnp.exp(G_last)[..., None]
             + jnp.einsum('bhcd,bhce->bhde', k * jnp.exp(G_last - G)[..., None], v_new))
    return S_new, o


def _deltanet_prompt(q, k, v, g, beta):
    Bn, L = q.shape[0], q.shape[2]
    n = (L - N_META) // A_CHUNK
    S = jnp.zeros((Bn, A_HEADS, A_DK, A_DV), F32)
    S, o_meta = _delta_chunk(S, q[:, :, :N_META], k[:, :, :N_META], v[:, :, :N_META],
                             g[:, :, :N_META], beta[:, :, :N_META])

    def to_chunks(t):
        t = t[:, :, N_META:]
        return jnp.moveaxis(t.reshape(t.shape[0], t.shape[1], n, A_CHUNK, *t.shape[3:]), 2, 0)

    S, o = lax.scan(lambda s, xs: _delta_chunk(s, *xs), S, tuple(to_chunks(t) for t in (q, k, v, g, beta)))
    o = jnp.moveaxis(o, 0, 2).reshape(Bn, A_HEADS, n * A_CHUNK, A_DV)
    return S, jnp.concatenate([o_meta, o], 2)


def _delta_out(o, z, norm_w):
    Bn, _, L, _ = o.shape
    o = _rmsnorm(jnp.swapaxes(o, 1, 2), norm_w)
    return (o * jax.nn.silu(z).reshape(Bn, L, A_HEADS, A_DV)).reshape(Bn, L, A_V)


def _rope(x, pos):
    half = B_HD // 2
    inv_freq = ROPE_THETA ** (-jnp.arange(half, dtype=F32) / half)
    ang = pos.astype(F32)[:, None] * inv_freq[None, :]
    cos, sin = jnp.cos(ang)[:, None, :], jnp.sin(ang)[:, None, :]
    x1, x2 = x[..., :half], x[..., half:]
    return jnp.concatenate([x1 * cos - x2 * sin, x2 * cos + x1 * sin], -1)


def _sink_attention(q, k, v, mask, sinks):
    s = jnp.einsum('...qhgd,...shd->...hgqs', q, k) * (B_HD ** -0.5)
    s = jnp.where(mask[..., None, None, :, :], s, -jnp.inf)
    sink = jnp.broadcast_to(sinks.reshape(B_KV_HEADS, B_GROUP, 1, 1), s.shape[:-1] + (1,))
    p = jax.nn.softmax(jnp.concatenate([s, sink], -1), -1)[..., :-1]
    return jnp.einsum('...hgqs,...shd->...qhgd', p, v)


def _swa_prompt(q, k, v, sinks):
    Bn, L = q.shape[:2]
    n = (L - N_META) // WINDOW
    km, vm = k[:, :N_META], v[:, :N_META]
    qm = q[:, :N_META].reshape(Bn, N_META, B_KV_HEADS, B_GROUP, B_HD)
    o_meta = _sink_attention(qm, km, vm, jnp.tril(jnp.ones((N_META, N_META), bool)), sinks)
    qr = q[:, N_META:].reshape(Bn, n, WINDOW, B_KV_HEADS, B_GROUP, B_HD)

    def band_keys(t):
        tm = jnp.broadcast_to(t[:, None, :N_META], (Bn, n, N_META, B_KV_HEADS, B_HD))
        tr = t[:, N_META:]
        tp = jnp.concatenate([jnp.zeros_like(tr[:, :WINDOW]), tr], 1).reshape(Bn, n + 1, WINDOW, B_KV_HEADS, B_HD)
        return jnp.concatenate([tm, tp[:, :-1], tp[:, 1:]], 2)

    i = jnp.arange(WINDOW)[:, None]
    j = jnp.arange(2 * WINDOW)[None, :]
    blk = jnp.arange(n)[:, None, None]
    band = (j > i) & (j <= i + WINDOW) & ((blk > 0) | (j >= WINDOW))
    mask = jnp.concatenate([jnp.ones((n, WINDOW, N_META), bool), band], -1)
    o_real = _sink_attention(qr, band_keys(k), band_keys(v), mask, sinks)
    y = jnp.concatenate([o_meta.reshape(Bn, N_META, B_Q), o_real.reshape(Bn, n * WINDOW, B_Q)], 1)
    new_k = jnp.concatenate([km, k[:, -WINDOW:]], 1)
    new_v = jnp.concatenate([vm, v[:, -WINDOW:]], 1)
    return y, new_k, new_v


def _swa_sample(q, k, v, cache_k, cache_v, sinks):
    Bn, T = q.shape[:2]
    kk = jnp.concatenate([cache_k, k], 1)
    vv = jnp.concatenate([cache_v, v], 1)
    t = jnp.arange(T)[:, None]
    r = jnp.arange(N_META + WINDOW + T)[None, :]
    wj = r - N_META
    win_pos = PAST_LEN - WINDOW + wj
    nm = r - N_META - WINDOW
    mask = ((r < N_META)
            | ((wj >= 0) & (wj < WINDOW) & (win_pos >= N_META) & (wj >= t + 1))
            | ((nm >= 0) & (nm <= t) & (nm > t - WINDOW)))
    o = _sink_attention(q.reshape(Bn, T, B_KV_HEADS, B_GROUP, B_HD), kk, vv, mask, sinks)
    new_k = jnp.concatenate([kk[:, :N_META], kk[:, -WINDOW:]], 1)
    new_v = jnp.concatenate([vv[:, :N_META], vv[:, -WINDOW:]], 1)
    return o.reshape(Bn, T, B_Q), new_k, new_v


def _mixers(proj, small, pos, conv_buf, s0, kv_cache, wts, prompt):
    conv_w, a_log, dt_bias, a_norm_w, q_norm_w, k_norm_w, sinks = wts
    Bn, L, _ = proj.shape
    qkv_a = proj[..., COL_QKV:COL_QKV + A_CONV_DIM]
    z_a = proj[..., COL_Z:COL_Z + A_V]
    q_b = proj[..., COL_QB:COL_QB + B_Q]
    k_b = proj[..., COL_KB:COL_KB + B_KV]
    v_b = proj[..., COL_VB:COL_VB + B_KV]
    b_a = small[..., :A_HEADS]
    a_a = small[..., A_HEADS:2 * A_HEADS]
    qkv, conv_new = _causal_conv(qkv_a, conv_buf, conv_w)
    q, k, v, g, beta = _delta_prep(qkv, b_a, a_a, a_log, dt_bias)
    if prompt:
        s_new, o = _deltanet_prompt(q, k, v, g, beta)
    else:
        s_new, o = _delta_chunk(s0, q, k, v, g, beta)
    y_a = _delta_out(o, z_a, a_norm_w)
    qb = _rope(_rmsnorm(q_b.reshape(Bn, L, B_HEADS, B_HD), q_norm_w), pos)
    kb = _rope(_rmsnorm(k_b.reshape(Bn, L, B_KV_HEADS, B_HD), k_norm_w), pos)
    vb = v_b.reshape(Bn, L, B_KV_HEADS, B_HD)
    if prompt:
        y_b, new_k, new_v = _swa_prompt(qb, kb, vb, sinks)
    else:
        y_b, new_k, new_v = _swa_sample(qb, kb, vb, kv_cache[0], kv_cache[1], sinks)
    return y_a, y_b, (s_new, conv_new, new_k, new_v)


def _hier_moe(xn2, logits, w_gate, w_up, w_down):
    T = xn2.shape[0]
    lg = logits[:, :N_GROUPS]
    pg = jax.nn.softmax(lg, -1)
    p_grp, grp = lax.top_k(pg, 1)
    le = logits[:, N_GROUPS:N_GROUPS + N_EXPERTS].reshape(T, N_GROUPS, EXPERTS_PER_GROUP)
    le = jnp.take_along_axis(le, grp[:, :, None], 1)[:, 0]
    top_p, top_i = lax.top_k(jax.nn.softmax(le, -1), TOP_K)
    gate = p_grp * top_p / jnp.sum(top_p, -1, keepdims=True)
    expert = (grp * EXPERTS_PER_GROUP + top_i).astype(jnp.int32)

    A = T * TOP_K
    nb = -(-A // MOE_ROWS) + N_EXPERTS
    P = nb * MOE_ROWS
    e_flat = expert.reshape(-1)
    onehot = (e_flat[:, None] == jnp.arange(N_EXPERTS, dtype=jnp.int32)[None, :]).astype(jnp.int32)
    csum = jnp.cumsum(onehot, 0)
    rank = jnp.take_along_axis(csum, e_flat[:, None], 1)[:, 0] - 1
    counts = csum[-1]
    padded = (counts + MOE_ROWS - 1) // MOE_ROWS * MOE_ROWS
    pend = jnp.cumsum(padded)
    pstart = pend - padded
    dest = pstart[e_flat] + rank
    tok_flat = jnp.repeat(jnp.arange(T, dtype=jnp.int32), TOP_K)
    slot_tok = jnp.full((P,), T, jnp.int32).at[dest].set(tok_flat)
    x_pad = jnp.concatenate([xn2, jnp.zeros((1, D_MODEL), xn2.dtype)], 0)
    xs = x_pad[slot_tok]
    blk_e = jnp.minimum(
        jnp.searchsorted(pend, jnp.arange(nb, dtype=jnp.int32) * MOE_ROWS, side='right'),
        N_EXPERTS - 1).astype(jnp.int32)
    n_used = (pend[-1:] // MOE_ROWS).astype(jnp.int32)
    ys = _moe_ffn(xs, blk_e, n_used, w_gate, w_up, w_down)
    d2 = dest.reshape(T, TOP_K)
    return gate[:, 0:1] * ys[d2[:, 0]] + gate[:, 1:2] * ys[d2[:, 1]]


def kernel(x_prompt, x_sample, state_delta, state_conv, cache_swa_k, cache_swa_v, meta_tokens,
           norm1_w, w_in, conv_w, a_log, dt_bias, a_norm_w, w_up_a, q_norm_w, k_norm_w, sinks,
           w_up_b, w_o, norm2_w, w_router_group, b_router_group, w_router_expert, b_router_expert,
           w_gate, w_up, w_down):
    Bp, Sp, _ = x_prompt.shape
    Bs, Ss, _ = x_sample.shape
    l = 0
    w = w_in[l]
    offs = [0]
    for s in (A_CONV_DIM, A_HEADS, A_HEADS, A_V, B_Q, B_KV, B_KV, D_MODEL, D_MODEL):
        offs.append(offs[-1] + s)
    seg = lambda a: w[:, offs[a]:offs[a + 1]]
    w_main = jnp.concatenate([seg(7), seg(8), seg(0), seg(3), seg(4), seg(5), seg(6)], 1).astype(BF16)
    w_small = jnp.concatenate(
        [seg(1), seg(2), jnp.zeros((D_MODEL, SMALL_PAD - 2 * A_HEADS), F32)], 1).astype(BF16)
    wa = w_up_a[l].astype(BF16)
    wb = w_up_b[l].astype(BF16)
    wo = w_o[l].astype(BF16)
    w_router = jnp.concatenate(
        [w_router_group[l], w_router_expert[l],
         jnp.zeros((D_MODEL, ROUTER_PAD - N_GROUPS - N_EXPERTS), F32)], 1).astype(BF16)
    b_router = jnp.concatenate(
        [b_router_group[l], b_router_expert[l],
         jnp.zeros((ROUTER_PAD - N_GROUPS - N_EXPERTS,), F32)])[None, :]
    g1 = norm1_w[l][None, :]
    g2 = norm2_w[l][None, :]

    xp = x_prompt.reshape(Bp * Sp, D_MODEL)
    xs = x_sample.reshape(Bs * Ss, D_MODEL)
    x_small = jnp.concatenate([xs, meta_tokens], 0)
    n_small = x_small.shape[0]

    proj_p, small_p = _inproj(xp, g1, w_main, w_small, tm=1024)
    proj_s, small_s = _inproj(x_small, g1, w_main, w_small, tm=n_small)

    mix_w = (conv_w[l], a_log[l], dt_bias[l], a_norm_w[l], q_norm_w[l], k_norm_w[l], sinks[l])
    meta_proj = jnp.broadcast_to(proj_s[Bs * Ss:][None], (Bp, N_META, PROJ_MAIN))
    meta_small = jnp.broadcast_to(small_s[Bs * Ss:][None], (Bp, N_META, SMALL_PAD))
    full_proj = jnp.concatenate([meta_proj, proj_p.reshape(Bp, Sp, PROJ_MAIN)], 1)
    full_small = jnp.concatenate([meta_small, small_p.reshape(Bp, Sp, SMALL_PAD)], 1)
    pos_p = jnp.arange(N_META + Sp, dtype=jnp.int32)
    conv0 = jnp.zeros((Bp, A_CONV - 1, A_CONV_DIM), F32)
    ya_p, yb_p, st_p = _mixers(full_proj, full_small, pos_p, conv0, None, None, mix_w, True)
    ya_p = ya_p[:, N_META:].reshape(Bp * Sp, A_V)
    yb_p = yb_p[:, N_META:].reshape(Bp * Sp, B_Q)
    pos_s = PAST_LEN + jnp.arange(Ss, dtype=jnp.int32)
    ya_s, yb_s, st_s = _mixers(proj_s[:Bs * Ss].reshape(Bs, Ss, PROJ_MAIN),
                               small_s[:Bs * Ss].reshape(Bs, Ss, SMALL_PAD), pos_s,
                               state_conv[l], state_delta[l], (cache_swa_k[l], cache_swa_v[l]),
                               mix_w, False)
    ya_s = ya_s.reshape(Bs * Ss, A_V)
    yb_s = yb_s.reshape(Bs * Ss, B_Q)

    merged_p = _merge(ya_p, yb_p, proj_p, wa, wb, tm=256)
    merged_s = _merge(ya_s, yb_s, proj_s[:Bs * Ss], wa, wb, tm=256)
    h2_p, xn2_p, lg_p = _outproj(merged_p, xp, wo, g2, w_router, b_router, tm=256)
    h2_s, xn2_s, lg_s = _outproj(merged_s, xs, wo, g2, w_router, b_router, tm=256)

    h2 = jnp.concatenate([h2_p, h2_s], 0)
    xn2 = jnp.concatenate([xn2_p, xn2_s], 0)
    lg = jnp.concatenate([lg_p, lg_s], 0)
    h3 = h2 + _hier_moe(xn2, lg, w_gate[l], w_up[l], w_down[l])

    y_prompt = h3[:Bp * Sp].reshape(Bp, Sp, D_MODEL)
    y_sample = h3[Bp * Sp:].reshape(Bs, Ss, D_MODEL)
    return (y_prompt, y_sample, st_p[0][None], st_p[1][None], st_p[2][None], st_p[3][None],
            st_s[0][None], st_s[1][None], st_s[2][None], st_s[3][None])
```

```python
import functools

import jax
import jax.numpy as jnp
from jax import lax
from jax.experimental import pallas as pl
from jax.experimental.pallas import tpu as pltpu

F32 = jnp.float32
BF16 = jnp.bfloat16

D_MODEL = 2048
N_META = 16
A_HEADS = 8
A_DK = 128
A_DV = 128
A_CONV = 4
A_CHUNK = 64
A_QK = A_HEADS * A_DK
A_V = A_HEADS * A_DV
A_CONV_DIM = 2 * A_QK + A_V
B_HEADS = 16
B_KV_HEADS = 4
B_HD = 64
B_GROUP = B_HEADS // B_KV_HEADS
B_Q = B_HEADS * B_HD
B_KV = B_KV_HEADS * B_HD
WINDOW = 128
ROPE_THETA = 10000.0
PAST_LEN = 16384
N_GROUPS = 4
EXPERTS_PER_GROUP = 8
N_EXPERTS = N_GROUPS * EXPERTS_PER_GROUP
TOP_K = 2
D_EXPERT = 512
EPS = 1e-6

COL_GA = 0
COL_GB = COL_GA + D_MODEL
COL_QKV = COL_GB + D_MODEL
COL_Z = COL_QKV + A_CONV_DIM
COL_QB = COL_Z + A_V
COL_KB = COL_QB + B_Q
COL_VB = COL_KB + B_KV
PROJ_MAIN = COL_VB + B_KV
SMALL_PAD = 128
ROUTER_PAD = 128
LANE_BETA = 0
LANE_DECAY = A_HEADS

SUBLANES = 8
MOE_ROWS = 256
VMEM_LIMIT = 56 * 1024 * 1024


def _row_chunk(tm):
    return 256 if tm % 256 == 0 else tm // 2


def _bdot(a, b):
    return jnp.dot(a, b, preferred_element_type=F32)


def _inproj_kernel(x_ref, g_ref, w_ref, ws_ref, o_ref, os_ref, xn_ref, *, tm):
    rc = _row_chunk(tm)

    @pl.when(pl.program_id(1) == 0)
    def _():
        def body(r, c):
            sl = pl.ds(pl.multiple_of(r * rc, 8), rc)
            x = x_ref[sl, :]
            ms = jnp.mean(x * x, axis=-1, keepdims=True)
            xn_ref[sl, :] = ((x * lax.rsqrt(ms + EPS)) * g_ref[...]).astype(BF16)
            return c

        lax.fori_loop(0, tm // rc, body, 0)
        os_ref[...] = _bdot(xn_ref[...], ws_ref[...])

    o_ref[...] = _bdot(xn_ref[...], w_ref[...])


def _inproj(x, gain, w_main, w_small, tm, tn=512):
    m = x.shape[0]
    assert m % tm == 0 and PROJ_MAIN % tn == 0
    return pl.pallas_call(
        functools.partial(_inproj_kernel, tm=tm),
        grid=(m // tm, PROJ_MAIN // tn),
        in_specs=[
            pl.BlockSpec((tm, D_MODEL), lambda i, j: (i, 0)),
            pl.BlockSpec((1, D_MODEL), lambda i, j: (0, 0)),
            pl.BlockSpec((D_MODEL, tn), lambda i, j: (0, j)),
            pl.BlockSpec((D_MODEL, SMALL_PAD), lambda i, j: (0, 0)),
        ],
        out_specs=[
            pl.BlockSpec((tm, tn), lambda i, j: (i, j)),
            pl.BlockSpec((tm, SMALL_PAD), lambda i, j: (i, 0)),
        ],
        out_shape=[
            jax.ShapeDtypeStruct((m, PROJ_MAIN), F32),
            jax.ShapeDtypeStruct((m, SMALL_PAD), F32),
        ],
        scratch_shapes=[pltpu.VMEM((tm, D_MODEL), BF16)],
        compiler_params=pltpu.CompilerParams(
            dimension_semantics=("arbitrary", "arbitrary"), vmem_limit_bytes=VMEM_LIMIT),
        name="inproj",
    )(x, gain, w_main, w_small)


def _split2(a):
    hi = a.astype(BF16)
    lo = (a - hi.astype(F32)).astype(BF16)
    return hi, lo


def _dot3(a, b):
    a_hi, a_lo = _split2(a)
    b_hi, b_lo = _split2(b)
    return _bdot(a_hi, b_hi) + (_bdot(a_hi, b_lo) + _bdot(a_lo, b_hi))


def _dot_exact_lhs(a_bf16, b):
    b1 = b.astype(BF16)
    r1 = b - b1.astype(F32)
    b2 = r1.astype(BF16)
    b3 = (r1 - b2.astype(F32)).astype(BF16)
    return _bdot(a_bf16, b1) + (_bdot(a_bf16, b2) + _bdot(a_bf16, b3))


def _unit_lower_inverse(m, ri, ci, c):
    same = lambda n: (ri >> n) == (ci >> n)
    eye = (ri == ci).astype(F32)
    n1 = jnp.where(same(3), m, 0.0)
    n2 = _dot3(n1, n1)
    n4 = _dot3(n2, n2)
    p = eye - n1
    p = p + _dot3(p, n2)
    p = p + _dot3(p, n4)
    lg = 3
    while (1 << lg) < c:
        off = jnp.where(jnp.logical_and(same(lg + 1), jnp.logical_not(same(lg))), m, 0.0)
        p = p - _dot3(_dot3(p, off), p)
        lg += 1
    return p


def _delta_kernel(qc, kc, vc, zc, qp, kp, vp, qf, kf, vf, sm_ref, cw_ref, alog_ref, dtb_ref, anw_ref, s0_ref,
                  ya_ref, s_ref, *, C, n_valid):
    first = pl.program_id(1) == 0

    @pl.when(first)
    def _():
        s_ref[...] = s0_ref[...]

    rows = lax.broadcasted_iota(jnp.int32, (C, 1), 0)
    valid = rows < n_valid
    ri = lax.broadcasted_iota(jnp.int32, (C, C), 0)
    ci = lax.broadcasted_iota(jnp.int32, (C, C), 1)
    lower = ri >= ci
    strict = ri > ci

    def conv(cur_ref, prev_ref, first_ref, col0):
        prev = jnp.where(first, first_ref[...], prev_ref[...])
        x = jnp.concatenate([prev, cur_ref[...]], 0)
        w = cw_ref[:, col0:col0 + A_QK]
        acc = x[SUBLANES:SUBLANES + C] * w[A_CONV - 1:A_CONV]
        for s in range(1, A_CONV):
            acc = acc + x[SUBLANES - s:SUBLANES - s + C] * w[A_CONV - 1 - s:A_CONV - s]
        return acc * jax.nn.sigmoid(acc)

    qx = conv(qc, qp, qf, 0)
    kx = conv(kc, kp, kf, A_QK)
    vx = conv(vc, vp, vf, 2 * A_QK)

    sm = sm_ref[...]
    beta_all = jnp.where(valid, jax.nn.sigmoid(sm), 0.0)
    g_all = jnp.where(valid, -jnp.exp(alog_ref[...]) * jax.nn.softplus(sm + dtb_ref[...]), 0.0)
    g_cum = _dot_exact_lhs(lower.astype(BF16), g_all)
    g_cum_t = g_cum.T
    e_g = jnp.exp(g_cum)
    g_last = g_cum[C - 1:C, :]
    e_rest = jnp.exp(g_last - g_cum)
    e_last = jnp.exp(g_last)

    for h in range(A_HEADS):
        sl = slice(h * A_DK, (h + 1) * A_DK)
        ld = LANE_DECAY + h
        qh = qx[:, sl]
        kh = kx[:, sl]
        qh = qh * lax.rsqrt(jnp.sum(qh * qh, -1, keepdims=True) + EPS) * (A_DK ** -0.5)
        kh = kh * lax.rsqrt(jnp.sum(kh * kh, -1, keepdims=True) + EPS)
        kh = jnp.where(valid, kh, 0.0)
        vh = jnp.where(valid, vx[:, sl], 0.0)
        bcol = beta_all[:, LANE_BETA + h:LANE_BETA + h + 1]
        gcol = g_cum[:, ld:ld + 1]
        grow = g_cum_t[ld:ld + 1, :]
        ecol = e_g[:, ld:ld + 1]
        gamma = jnp.exp(jnp.where(lower, gcol - grow, -jnp.inf))
        kb = kh * bcol
        a1 = lax.dot_general(jnp.concatenate([kb, qh], 0).astype(BF16), kh.astype(BF16),
                             (((1,), (1,)), ((), ())), preferred_element_type=F32)
        m = jnp.where(strict, a1[:C] * gamma, 0.0)
        attn = a1[C:] * gamma
        rhs = jnp.concatenate([vh * bcol, kb * ecol], 1)
        sol = _dot3(_unit_lower_inverse(m, ri, ci, C), rhs)
        u = sol[:, :A_DV]
        w = sol[:, A_DV:]
        s_old = s_ref[0, h]
        wq = _bdot(jnp.concatenate([w, qh * ecol], 0).astype(BF16), s_old.astype(BF16))
        v_new = u - wq[:C]
        kg_t = (kh * e_rest[:, ld:ld + 1]).T
        r2 = _bdot(jnp.concatenate([attn, kg_t], 0).astype(BF16), v_new.astype(BF16))
        o = wq[C:] + r2[:C]
        s_ref[0, h] = s_old * e_last[:, ld:ld + 1] + r2[C:]
        on = (o * lax.rsqrt(jnp.mean(o * o, -1, keepdims=True) + EPS)) * anw_ref[...]
        zh = zc[:, sl]
        ya_ref[:, sl] = (on * (zh * jax.nn.sigmoid(zh))).astype(BF16)


def _delta_mixer(proj, small, first_prev, s0, conv_w, alog_row, dtb_row, anw_row, *, nb, nc, C, n_valid,
                 shared_first):
    rows = nb * nc * C
    assert proj.shape[0] == rows and C % SUBLANES == 0
    cpb = C // SUBLANES
    cq, ck, cv, cz = (COL_QKV // A_QK, COL_QKV // A_QK + 1, COL_QKV // A_QK + 2, COL_Z // A_V)
    fb = (lambda b: 0) if shared_first else (lambda b: b)
    cur = lambda col: pl.BlockSpec((C, A_QK), lambda b, c: (b * nc + c, col))
    prev = lambda col: pl.BlockSpec(
        (SUBLANES, A_QK), lambda b, c: (jnp.maximum((b * nc + c) * cpb - 1, 0), col))
    frst = lambda j: pl.BlockSpec((None, SUBLANES, A_QK), lambda b, c: (fb(b), 0, j))
    row1 = lambda n: pl.BlockSpec((1, n), lambda b, c: (0, 0))
    return pl.pallas_call(
        functools.partial(_delta_kernel, C=C, n_valid=n_valid),
        grid=(nb, nc),
        in_specs=[
            cur(cq), cur(ck), cur(cv), cur(cz),
            prev(cq), prev(ck), prev(cv),
            frst(0), frst(1), frst(2),
            pl.BlockSpec((C, SMALL_PAD), lambda b, c: (b * nc + c, 0)),
            pl.BlockSpec((A_CONV, A_CONV_DIM), lambda b, c: (0, 0)),
            row1(SMALL_PAD), row1(SMALL_PAD), row1(A_DV),
            pl.BlockSpec((1, A_HEADS, A_DK, A_DV), lambda b, c: (fb(b), 0, 0, 0)),
        ],
        out_specs=[
            pl.BlockSpec((C, A_V), lambda b, c: (b * nc + c, 0)),
            pl.BlockSpec((1, A_HEADS, A_DK, A_DV), lambda b, c: (b, 0, 0, 0)),
        ],
        out_shape=[
            jax.ShapeDtypeStruct((rows, A_V), BF16),
            jax.ShapeDtypeStruct((nb, A_HEADS, A_DK, A_DV), F32),
        ],
        compiler_params=pltpu.CompilerParams(
            dimension_semantics=("arbitrary", "arbitrary"), vmem_limit_bytes=VMEM_LIMIT),
        name="delta_mixer",
    )(proj, proj, proj, proj, proj, proj, proj, first_prev, first_prev, first_prev, small,
      conv_w, alog_row, dtb_row, anw_row, s0)


def _merge_kernel(ya_ref, yb_ref, ga_ref, gb_ref, wa_ref, wb_ref, o_ref):
    ua = _bdot(ya_ref[...].astype(BF16), wa_ref[...])
    ub = _bdot(yb_ref[...].astype(BF16), wb_ref[...])
    merged = jax.nn.sigmoid(ga_ref[...]) * ua + jax.nn.sigmoid(gb_ref[...]) * ub
    o_ref[...] = merged.astype(BF16)


def _merge(ya, yb, proj, wa, wb, tm):
    m = ya.shape[0]
    assert m % tm == 0
    return pl.pallas_call(
        _merge_kernel,
        grid=(m // tm,),
        in_specs=[
            pl.BlockSpec((tm, A_V), lambda i: (i, 0)),
            pl.BlockSpec((tm, B_Q), lambda i: (i, 0)),
            pl.BlockSpec((tm, D_MODEL), lambda i: (i, COL_GA // D_MODEL)),
            pl.BlockSpec((tm, D_MODEL), lambda i: (i, COL_GB // D_MODEL)),
            pl.BlockSpec((A_V, D_MODEL), lambda i: (0, 0)),
            pl.BlockSpec((B_Q, D_MODEL), lambda i: (0, 0)),
        ],
        out_specs=pl.BlockSpec((tm, D_MODEL), lambda i: (i, 0)),
        out_shape=jax.ShapeDtypeStruct((m, D_MODEL), BF16),
        compiler_params=pltpu.CompilerParams(
            dimension_semantics=("arbitrary",), vmem_limit_bytes=VMEM_LIMIT),
        name="merge",
    )(ya, yb, proj, proj, wa, wb)


def _outproj_kernel(m_ref, h_ref, wo_ref, g_ref, wr_ref, br_ref, h2_ref, xn_ref, lg_ref):
    h2 = h_ref[...] + _bdot(m_ref[...], wo_ref[...])
    h2_ref[...] = h2
    ms = jnp.mean(h2 * h2, axis=-1, keepdims=True)
    xn = ((h2 * lax.rsqrt(ms + EPS)) * g_ref[...]).astype(BF16)
    xn_ref[...] = xn
    lg_ref[...] = _bdot(xn, wr_ref[...]) + br_ref[...]


def _outproj(merged, h, wo, gain2, w_router, b_router, tm):
    m = h.shape[0]
    assert m % tm == 0
    return pl.pallas_call(
        _outproj_kernel,
        grid=(m // tm,),
        in_specs=[
            pl.BlockSpec((tm, D_MODEL), lambda i: (i, 0)),
            pl.BlockSpec((tm, D_MODEL), lambda i: (i, 0)),
            pl.BlockSpec((D_MODEL, D_MODEL), lambda i: (0, 0)),
            pl.BlockSpec((1, D_MODEL), lambda i: (0, 0)),
            pl.BlockSpec((D_MODEL, ROUTER_PAD), lambda i: (0, 0)),
            pl.BlockSpec((1, ROUTER_PAD), lambda i: (0, 0)),
        ],
        out_specs=[
            pl.BlockSpec((tm, D_MODEL), lambda i: (i, 0)),
            pl.BlockSpec((tm, D_MODEL), lambda i: (i, 0)),
            pl.BlockSpec((tm, ROUTER_PAD), lambda i: (i, 0)),
        ],
        out_shape=[
            jax.ShapeDtypeStruct((m, D_MODEL), F32),
            jax.ShapeDtypeStruct((m, D_MODEL), BF16),
            jax.ShapeDtypeStruct((m, ROUTER_PAD), F32),
        ],
        compiler_params=pltpu.CompilerParams(
            dimension_semantics=("arbitrary",), vmem_limit_bytes=VMEM_LIMIT),
        name="outproj",
    )(merged, h, wo, gain2, w_router, b_router)


def _moe_kernel(blk_e_ref, nused_ref, x_ref, wg_ref, wu_ref, wd_ref, o_ref, wgb, wub, wdb):
    i = pl.program_id(0)
    e = blk_e_ref[i]
    e_prev = blk_e_ref[jnp.maximum(i - 1, 0)]
    used = i < nused_ref[0]

    @pl.when(jnp.logical_and(used, jnp.logical_or(i == 0, e != e_prev)))
    def _():
        def cast_in(r, c):
            sl = pl.ds(pl.multiple_of(r * 256, 256), 256)
            wgb[sl, :] = wg_ref[sl, :].astype(BF16)
            wub[sl, :] = wu_ref[sl, :].astype(BF16)
            return c

        lax.fori_loop(0, D_MODEL // 256, cast_in, 0)

        def cast_out(r, c):
            sl = pl.ds(pl.multiple_of(r * 128, 128), 128)
            wdb[sl, :] = wd_ref[sl, :].astype(BF16)
            return c

        lax.fori_loop(0, D_EXPERT // 128, cast_out, 0)

    @pl.when(used)
    def _():
        x = x_ref[...]
        g = _bdot(x, wgb[...])
        u = _bdot(x, wub[...])
        hb = (g * jax.nn.sigmoid(g)) * u
        o_ref[...] = _bdot(hb.astype(BF16), wdb[...])

    @pl.when(jnp.logical_not(used))
    def _():
        o_ref[...] = jnp.zeros_like(o_ref)


def _moe_ffn(xs, blk_e, n_used, w_gate, w_up, w_down):
    p = xs.shape[0]
    nb = p // MOE_ROWS
    grid_spec = pltpu.PrefetchScalarGridSpec(
        num_scalar_prefetch=2,
        grid=(nb,),
        in_specs=[
            pl.BlockSpec((MOE_ROWS, D_MODEL), lambda i, be, nu: (i, 0)),
            pl.BlockSpec((None, D_MODEL, D_EXPERT), lambda i, be, nu: (be[i], 0, 0)),
            pl.BlockSpec((None, D_MODEL, D_EXPERT), lambda i, be, nu: (be[i], 0, 0)),
            pl.BlockSpec((None, D_EXPERT, D_MODEL), lambda i, be, nu: (be[i], 0, 0)),
        ],
        out_specs=pl.BlockSpec((MOE_ROWS, D_MODEL), lambda i, be, nu: (i, 0)),
        scratch_shapes=[
            pltpu.VMEM((D_MODEL, D_EXPERT), BF16),
            pltpu.VMEM((D_MODEL, D_EXPERT), BF16),
            pltpu.VMEM((D_EXPERT, D_MODEL), BF16),
        ],
    )
    return pl.pallas_call(
        _moe_kernel,
        grid_spec=grid_spec,
        out_shape=jax.ShapeDtypeStruct((p, D_MODEL), F32),
        compiler_params=pltpu.CompilerParams(
            dimension_semantics=("arbitrary",), vmem_limit_bytes=VMEM_LIMIT),
        name="moe_ffn",
    )(blk_e, n_used, xs, w_gate, w_up, w_down)


def _rmsnorm(x, w):
    y = x * lax.rsqrt(jnp.mean(x * x, -1, keepdims=True) + EPS)
    return y * w


def _rope(x, pos):
    half = B_HD // 2
    inv_freq = ROPE_THETA ** (-jnp.arange(half, dtype=F32) / half)
    ang = pos.astype(F32)[:, None] * inv_freq[None, :]
    cos, sin = jnp.cos(ang)[:, None, :], jnp.sin(ang)[:, None, :]
    x1, x2 = x[..., :half], x[..., half:]
    return jnp.concatenate([x1 * cos - x2 * sin, x2 * cos + x1 * sin], -1)


def _sink_attention(q, k, v, mask, sinks):
    s = jnp.einsum('...qhgd,...shd->...hgqs', q, k) * (B_HD ** -0.5)
    s = jnp.where(mask[..., None, None, :, :], s, -jnp.inf)
    sink = jnp.broadcast_to(sinks.reshape(B_KV_HEADS, B_GROUP, 1, 1), s.shape[:-1] + (1,))
    p = jax.nn.softmax(jnp.concatenate([s, sink], -1), -1)[..., :-1]
    return jnp.einsum('...hgqs,...shd->...qhgd', p, v)


def _swa_prompt(q, k, v, sinks):
    Bn, L = q.shape[:2]
    n = (L - N_META) // WINDOW
    km, vm = k[:, :N_META], v[:, :N_META]
    qr = q[:, N_META:].reshape(Bn, n, WINDOW, B_KV_HEADS, B_GROUP, B_HD)

    def band_keys(t):
        tm = jnp.broadcast_to(t[:, None, :N_META], (Bn, n, N_META, B_KV_HEADS, B_HD))
        tr = t[:, N_META:]
        tp = jnp.concatenate([jnp.zeros_like(tr[:, :WINDOW]), tr], 1).reshape(Bn, n + 1, WINDOW, B_KV_HEADS, B_HD)
        return jnp.concatenate([tm, tp[:, :-1], tp[:, 1:]], 2)

    i = jnp.arange(WINDOW)[:, None]
    j = jnp.arange(2 * WINDOW)[None, :]
    blk = jnp.arange(n)[:, None, None]
    band = (j > i) & (j <= i + WINDOW) & ((blk > 0) | (j >= WINDOW))
    mask = jnp.concatenate([jnp.ones((n, WINDOW, N_META), bool), band], -1)
    o_real = _sink_attention(qr, band_keys(k), band_keys(v), mask, sinks)
    new_k = jnp.concatenate([km, k[:, -WINDOW:]], 1)
    new_v = jnp.concatenate([vm, v[:, -WINDOW:]], 1)
    return o_real.reshape(Bn * n * WINDOW, B_Q), new_k, new_v


def _swa_sample(q, k, v, cache_k, cache_v, sinks):
    Bn, T = q.shape[:2]
    kk = jnp.concatenate([cache_k, k], 1)
    vv = jnp.concatenate([cache_v, v], 1)
    t = jnp.arange(T)[:, None]
    r = jnp.arange(N_META + WINDOW + T)[None, :]
    wj = r - N_META
    win_pos = PAST_LEN - WINDOW + wj
    nm = r - N_META - WINDOW
    mask = ((r < N_META)
            | ((wj >= 0) & (wj < WINDOW) & (win_pos >= N_META) & (wj >= t + 1))
            | ((nm >= 0) & (nm <= t) & (nm > t - WINDOW)))
    o = _sink_attention(q.reshape(Bn, T, B_KV_HEADS, B_GROUP, B_HD), kk, vv, mask, sinks)
    new_k = jnp.concatenate([kk[:, :N_META], kk[:, -WINDOW:]], 1)
    new_v = jnp.concatenate([vv[:, :N_META], vv[:, -WINDOW:]], 1)
    return o.reshape(Bn * T, B_Q), new_k, new_v


def _swa_inputs(proj3, pos, q_norm_w, k_norm_w):
    Bn, L, _ = proj3.shape
    q_b = proj3[..., COL_QB:COL_QB + B_Q]
    k_b = proj3[..., COL_KB:COL_KB + B_KV]
    v_b = proj3[..., COL_VB:COL_VB + B_KV]
    qb = _rope(_rmsnorm(q_b.reshape(Bn, L, B_HEADS, B_HD), q_norm_w), pos)
    kb = _rope(_rmsnorm(k_b.reshape(Bn, L, B_KV_HEADS, B_HD), k_norm_w), pos)
    return qb, kb, v_b.reshape(Bn, L, B_KV_HEADS, B_HD)


def _hier_moe(xn2, logits, w_gate, w_up, w_down):
    T = xn2.shape[0]
    lg = logits[:, :N_GROUPS]
    pg = jax.nn.softmax(lg, -1)
    p_grp, grp = lax.top_k(pg, 1)
    le = logits[:, N_GROUPS:N_GROUPS + N_EXPERTS].reshape(T, N_GROUPS, EXPERTS_PER_GROUP)
    le = jnp.take_along_axis(le, grp[:, :, None], 1)[:, 0]
    top_p, top_i = lax.top_k(jax.nn.softmax(le, -1), TOP_K)
    gate = p_grp * top_p / jnp.sum(top_p, -1, keepdims=True)
    expert = (grp * EXPERTS_PER_GROUP + top_i).astype(jnp.int32)

    A = T * TOP_K
    nb = -(-A // MOE_ROWS) + N_EXPERTS
    P = nb * MOE_ROWS
    e_flat = expert.reshape(-1)
    onehot = (e_flat[:, None] == jnp.arange(N_EXPERTS, dtype=jnp.int32)[None, :]).astype(jnp.int32)
    csum = jnp.cumsum(onehot, 0)
    rank = jnp.take_along_axis(csum, e_flat[:, None], 1)[:, 0] - 1
    counts = csum[-1]
    padded = (counts + MOE_ROWS - 1) // MOE_ROWS * MOE_ROWS
    pend = jnp.cumsum(padded)
    pstart = pend - padded
    dest = pstart[e_flat] + rank
    tok_flat = jnp.repeat(jnp.arange(T, dtype=jnp.int32), TOP_K)
    slot_tok = jnp.full((P,), T, jnp.int32).at[dest].set(tok_flat)
    x_pad = jnp.concatenate([xn2, jnp.zeros((1, D_MODEL), xn2.dtype)], 0)
    xs = x_pad[slot_tok]
    blk_e = jnp.minimum(
        jnp.searchsorted(pend, jnp.arange(nb, dtype=jnp.int32) * MOE_ROWS, side='right'),
        N_EXPERTS - 1).astype(jnp.int32)
    n_used = (pend[-1:] // MOE_ROWS).astype(jnp.int32)
    ys = _moe_ffn(xs, blk_e, n_used, w_gate, w_up, w_down)
    d2 = dest.reshape(T, TOP_K)
    return gate[:, 0:1] * ys[d2[:, 0]] + gate[:, 1:2] * ys[d2[:, 1]]


def _lane_row(vals, lane0):
    return jnp.zeros((1, SMALL_PAD), F32).at[0, lane0:lane0 + vals.shape[0]].set(vals)


def kernel(x_prompt, x_sample, state_delta, state_conv, cache_swa_k, cache_swa_v, meta_tokens,
           norm1_w, w_in, conv_w, a_log, dt_bias, a_norm_w, w_up_a, q_norm_w, k_norm_w, sinks,
           w_up_b, w_o, norm2_w, w_router_group, b_router_group, w_router_expert, b_router_expert,
           w_gate, w_up, w_down):
    Bp, Sp, _ = x_prompt.shape
    Bs, Ss, _ = x_sample.shape
    n_s = Bs * Ss
    l = 0
    w = w_in[l]
    offs = [0]
    for s in (A_CONV_DIM, A_HEADS, A_HEADS, A_V, B_Q, B_KV, B_KV, D_MODEL, D_MODEL):
        offs.append(offs[-1] + s)
    seg = lambda a: w[:, offs[a]:offs[a + 1]]
    w_main = jnp.concatenate([seg(7), seg(8), seg(0), seg(3), seg(4), seg(5), seg(6)], 1).astype(BF16)
    w_small = jnp.concatenate(
        [seg(1), seg(2), jnp.zeros((D_MODEL, SMALL_PAD - 2 * A_HEADS), F32)], 1).astype(BF16)
    wa = w_up_a[l].astype(BF16)
    wb = w_up_b[l].astype(BF16)
    wo = w_o[l].astype(BF16)
    w_router = jnp.concatenate(
        [w_router_group[l], w_router_expert[l],
         jnp.zeros((D_MODEL, ROUTER_PAD - N_GROUPS - N_EXPERTS), F32)], 1).astype(BF16)
    b_router = jnp.concatenate(
        [b_router_group[l], b_router_expert[l],
         jnp.zeros((ROUTER_PAD - N_GROUPS - N_EXPERTS,), F32)])[None, :]
    g1 = norm1_w[l][None, :]
    g2 = norm2_w[l][None, :]
    alog_row = _lane_row(a_log[l], LANE_DECAY)
    dtb_row = _lane_row(dt_bias[l], LANE_DECAY)
    anw_row = a_norm_w[l][None, :]

    xp = x_prompt.reshape(Bp * Sp, D_MODEL)
    xs = x_sample.reshape(n_s, D_MODEL)
    x_small = jnp.concatenate([xs, meta_tokens], 0)
    n_small = x_small.shape[0]

    proj_p, small_p = _inproj(xp, g1, w_main, w_small, tm=1024)
    proj_s, small_s = _inproj(x_small, g1, w_main, w_small, tm=n_small)

    dm = functools.partial(_delta_mixer, conv_w=conv_w[l], alog_row=alog_row, dtb_row=dtb_row, anw_row=anw_row)
    pad_rows = lambda a, n: jnp.concatenate([a, jnp.zeros((n - a.shape[0],) + a.shape[1:], a.dtype)], 0)
    meta_proj = pad_rows(proj_s[n_s:], A_CHUNK)
    meta_small = pad_rows(small_s[n_s:], A_CHUNK)
    zero_prev = jnp.zeros((1, SUBLANES, A_CONV_DIM), F32)
    zero_state = jnp.zeros((1, A_HEADS, A_DK, A_DV), F32)
    _, s_meta = dm(meta_proj, meta_small, zero_prev, zero_state, nb=1, nc=1, C=A_CHUNK, n_valid=N_META,
                   shared_first=True)
    meta_tail = proj_s[n_s + N_META - SUBLANES:, COL_QKV:COL_QKV + A_CONV_DIM][None]
    ya_p, sdelta_p = dm(proj_p, small_p, meta_tail, s_meta, nb=Bp, nc=Sp // A_CHUNK, C=A_CHUNK,
                        n_valid=A_CHUNK, shared_first=True)
    conv_p = proj_p.reshape(Bp, Sp, PROJ_MAIN)[:, Sp - (A_CONV - 1):, COL_QKV:COL_QKV + A_CONV_DIM]
    CS = SUBLANES
    samp_proj = jnp.pad(proj_s[:n_s].reshape(Bs, Ss, PROJ_MAIN), ((0, 0), (0, CS - Ss), (0, 0)))
    samp_small = jnp.pad(small_s[:n_s].reshape(Bs, Ss, SMALL_PAD), ((0, 0), (0, CS - Ss), (0, 0)))
    samp_prev = jnp.pad(state_conv[l], ((0, 0), (SUBLANES - (A_CONV - 1), 0), (0, 0)))
    ya_s8, sdelta_s = dm(samp_proj.reshape(Bs * CS, PROJ_MAIN), samp_small.reshape(Bs * CS, SMALL_PAD),
                         samp_prev, state_delta[l], nb=Bs, nc=1, C=CS, n_valid=Ss, shared_first=False)
    ya_s = ya_s8.reshape(Bs, CS, A_V)[:, :Ss].reshape(n_s, A_V)
    qkv_s = proj_s[:n_s, COL_QKV:COL_QKV + A_CONV_DIM].reshape(Bs, Ss, A_CONV_DIM)
    conv_s = jnp.concatenate([state_conv[l], qkv_s], 1)[:, -(A_CONV - 1):]

    sw_cols = slice(COL_QB, PROJ_MAIN)
    meta_sw = jnp.broadcast_to(proj_s[n_s:][None], (Bp, N_META, PROJ_MAIN))
    full_sw = jnp.concatenate([meta_sw, proj_p.reshape(Bp, Sp, PROJ_MAIN)], 1)
    pos_p = jnp.arange(N_META + Sp, dtype=jnp.int32)
    qb, kb, vb = _swa_inputs(full_sw, pos_p, q_norm_w[l], k_norm_w[l])
    yb_p, swk_p, swv_p = _swa_prompt(qb, kb, vb, sinks[l])
    pos_s = PAST_LEN + jnp.arange(Ss, dtype=jnp.int32)
    qs, ks, vs = _swa_inputs(proj_s[:n_s].reshape(Bs, Ss, PROJ_MAIN), pos_s, q_norm_w[l], k_norm_w[l])
    yb_s, swk_s, swv_s = _swa_sample(qs, ks, vs, cache_swa_k[l], cache_swa_v[l], sinks[l])
    del sw_cols

    merged_p = _merge(ya_p, yb_p, proj_p, wa, wb, tm=256)
    merged_s = _merge(ya_s, yb_s, proj_s[:n_s], wa, wb, tm=256)
    h2_p, xn2_p, lg_p = _outproj(merged_p, xp, wo, g2, w_router, b_router, tm=256)
    h2_s, xn2_s, lg_s = _outproj(merged_s, xs, wo, g2, w_router, b_router, tm=256)

    h2 = jnp.concatenate([h2_p, h2_s], 0)
    xn2 = jnp.concatenate([xn2_p, xn2_s], 0)
    lg = jnp.concatenate([lg_p, lg_s], 0)
    h3 = h2 + _hier_moe(xn2, lg, w_gate[l], w_up[l], w_down[l])

    y_prompt = h3[:Bp * Sp].reshape(Bp, Sp, D_MODEL)
    y_sample = h3[Bp * Sp:].reshape(Bs, Ss, D_MODEL)
    return (y_prompt, y_sample, sdelta_p[None], conv_p[None], swk_p[None], swv_p[None],
            sdelta_s[None], conv_s[None], swk_s[None], swv_s[None])
```

```python
import functools

import jax
import jax.numpy as jnp
from jax import lax
from jax.experimental import pallas as pl
from jax.experimental.pallas import tpu as pltpu

F32 = jnp.float32
BF16 = jnp.bfloat16

D_MODEL = 2048
N_META = 16
A_HEADS = 8
A_DK = 128
A_DV = 128
A_CONV = 4
A_CHUNK = 64
A_QK = A_HEADS * A_DK
A_V = A_HEADS * A_DV
A_CONV_DIM = 2 * A_QK + A_V
B_HEADS = 16
B_KV_HEADS = 4
B_HD = 64
B_GROUP = B_HEADS // B_KV_HEADS
B_Q = B_HEADS * B_HD
B_KV = B_KV_HEADS * B_HD
WINDOW = 128
ROPE_THETA = 10000.0
PAST_LEN = 16384
N_GROUPS = 4
EXPERTS_PER_GROUP = 8
N_EXPERTS = N_GROUPS * EXPERTS_PER_GROUP
TOP_K = 2
D_EXPERT = 512
EPS = 1e-6

COL_GA = 0
COL_GB = COL_GA + D_MODEL
COL_QKV = COL_GB + D_MODEL
COL_Z = COL_QKV + A_CONV_DIM
COL_QB = COL_Z + A_V
COL_KB = COL_QB + B_Q
COL_VB = COL_KB + B_KV
PROJ_MAIN = COL_VB + B_KV
SMALL_PAD = 128
ROUTER_PAD = 128
LANE_BETA = 0
LANE_DECAY = A_HEADS

SUBLANES = 8
MOE_ROWS = 256
VMEM_LIMIT = 56 * 1024 * 1024


def _row_chunk(tm):
    return 256 if tm % 256 == 0 else tm // 2


def _bdot(a, b):
    return jnp.dot(a, b, preferred_element_type=F32)


def _inproj_kernel(x_ref, g_ref, w_ref, ws_ref, o_ref, os_ref, xn_ref, *, tm):
    rc = _row_chunk(tm)

    @pl.when(pl.program_id(1) == 0)
    def _():
        def body(r, c):
            sl = pl.ds(pl.multiple_of(r * rc, 8), rc)
            x = x_ref[sl, :]
            ms = jnp.mean(x * x, axis=-1, keepdims=True)
            xn_ref[sl, :] = ((x * lax.rsqrt(ms + EPS)) * g_ref[...]).astype(BF16)
            return c

        lax.fori_loop(0, tm // rc, body, 0)
        os_ref[...] = _bdot(xn_ref[...], ws_ref[...])

    o_ref[...] = _bdot(xn_ref[...], w_ref[...])


def _inproj(x, gain, w_main, w_small, tm, tn=512):
    m = x.shape[0]
    assert m % tm == 0 and PROJ_MAIN % tn == 0
    return pl.pallas_call(
        functools.partial(_inproj_kernel, tm=tm),
        grid=(m // tm, PROJ_MAIN // tn),
        in_specs=[
            pl.BlockSpec((tm, D_MODEL), lambda i, j: (i, 0)),
            pl.BlockSpec((1, D_MODEL), lambda i, j: (0, 0)),
            pl.BlockSpec((D_MODEL, tn), lambda i, j: (0, j)),
            pl.BlockSpec((D_MODEL, SMALL_PAD), lambda i, j: (0, 0)),
        ],
        out_specs=[
            pl.BlockSpec((tm, tn), lambda i, j: (i, j)),
            pl.BlockSpec((tm, SMALL_PAD), lambda i, j: (i, 0)),
        ],
        out_shape=[
            jax.ShapeDtypeStruct((m, PROJ_MAIN), F32),
            jax.ShapeDtypeStruct((m, SMALL_PAD), F32),
        ],
        scratch_shapes=[pltpu.VMEM((tm, D_MODEL), BF16)],
        compiler_params=pltpu.CompilerParams(
            dimension_semantics=("arbitrary", "arbitrary"), vmem_limit_bytes=VMEM_LIMIT),
        name="inproj",
    )(x, gain, w_main, w_small)


def _split2(a):
    hi = a.astype(BF16)
    lo = (a - hi.astype(F32)).astype(BF16)
    return hi, lo


def _dot3(a, b):
    a_hi, a_lo = _split2(a)
    b_hi, b_lo = _split2(b)
    return _bdot(a_hi, b_hi) + (_bdot(a_hi, b_lo) + _bdot(a_lo, b_hi))


def _dot_exact_lhs(a_bf16, b):
    b1 = b.astype(BF16)
    r1 = b - b1.astype(F32)
    b2 = r1.astype(BF16)
    b3 = (r1 - b2.astype(F32)).astype(BF16)
    return _bdot(a_bf16, b1) + (_bdot(a_bf16, b2) + _bdot(a_bf16, b3))


def _unit_lower_inverse(m, ri, ci, c):
    same = lambda n: (ri >> n) == (ci >> n)
    eye = (ri == ci).astype(F32)
    n1 = jnp.where(same(3), m, 0.0)
    n2 = _dot3(n1, n1)
    n4 = _dot3(n2, n2)
    p = eye - n1
    p = p + _dot3(p, n2)
    p = p + _dot3(p, n4)
    lg = 3
    while (1 << lg) < c:
        off = jnp.where(jnp.logical_and(same(lg + 1), jnp.logical_not(same(lg))), m, 0.0)
        p = p - _dot3(_dot3(p, off), p)
        lg += 1
    return p


def _delta_kernel(qc, kc, vc, zc, qp, kp, vp, qf, kf, vf, sm_ref, cw_ref, alog_ref, dtb_ref, anw_ref, s0_ref,
                  ya_ref, s_ref, *, C, n_valid):
    first = pl.program_id(1) == 0

    @pl.when(first)
    def _():
        s_ref[...] = s0_ref[...]

    rows = lax.broadcasted_iota(jnp.int32, (C, 1), 0)
    valid = rows < n_valid
    ri = lax.broadcasted_iota(jnp.int32, (C, C), 0)
    ci = lax.broadcasted_iota(jnp.int32, (C, C), 1)
    lower = ri >= ci
    strict = ri > ci

    def conv(cur_ref, prev_ref, first_ref, col0):
        prev = jnp.where(first, first_ref[...], prev_ref[...])
        x = jnp.concatenate([prev, cur_ref[...]], 0)
        w = cw_ref[:, col0:col0 + A_QK]
        acc = x[SUBLANES:SUBLANES + C] * w[A_CONV - 1:A_CONV]
        for s in range(1, A_CONV):
            acc = acc + x[SUBLANES - s:SUBLANES - s + C] * w[A_CONV - 1 - s:A_CONV - s]
        return acc * jax.nn.sigmoid(acc)

    qx = conv(qc, qp, qf, 0)
    kx = conv(kc, kp, kf, A_QK)
    vx = conv(vc, vp, vf, 2 * A_QK)

    sm = sm_ref[...]
    beta_all = jnp.where(valid, jax.nn.sigmoid(sm), 0.0)
    g_all = jnp.where(valid, -jnp.exp(alog_ref[...]) * jax.nn.softplus(sm + dtb_ref[...]), 0.0)
    g_cum = _dot_exact_lhs(lower.astype(BF16), g_all)
    g_cum_t = g_cum.T
    e_g = jnp.exp(g_cum)
    g_last = g_cum[C - 1:C, :]
    e_rest = jnp.exp(g_last - g_cum)
    e_last = jnp.exp(g_last)

    for h in range(A_HEADS):
        sl = slice(h * A_DK, (h + 1) * A_DK)
        ld = LANE_DECAY + h
        qh = qx[:, sl]
        kh = kx[:, sl]
        qh = qh * lax.rsqrt(jnp.sum(qh * qh, -1, keepdims=True) + EPS) * (A_DK ** -0.5)
        kh = kh * lax.rsqrt(jnp.sum(kh * kh, -1, keepdims=True) + EPS)
        kh = jnp.where(valid, kh, 0.0)
        vh = jnp.where(valid, vx[:, sl], 0.0)
        bcol = beta_all[:, LANE_BETA + h:LANE_BETA + h + 1]
        gcol = g_cum[:, ld:ld + 1]
        grow = g_cum_t[ld:ld + 1, :]
        ecol = e_g[:, ld:ld + 1]
        gamma = jnp.exp(jnp.where(lower, gcol - grow, -jnp.inf))
        kb = kh * bcol
        a1 = lax.dot_general(jnp.concatenate([kb, qh], 0).astype(BF16), kh.astype(BF16),
                             (((1,), (1,)), ((), ())), preferred_element_type=F32)
        m = jnp.where(strict, a1[:C] * gamma, 0.0)
        attn = a1[C:] * gamma
        rhs = jnp.concatenate([vh * bcol, kb * ecol], 1)
        sol = _dot3(_unit_lower_inverse(m, ri, ci, C), rhs)
        u = sol[:, :A_DV]
        w = sol[:, A_DV:]
        s_old = s_ref[0, h]
        wq = _bdot(jnp.concatenate([w, qh * ecol], 0).astype(BF16), s_old.astype(BF16))
        v_new = u - wq[:C]
        kg_t = (kh * e_rest[:, ld:ld + 1]).T
        r2 = _bdot(jnp.concatenate([attn, kg_t], 0).astype(BF16), v_new.astype(BF16))
        o = wq[C:] + r2[:C]
        s_ref[0, h] = s_old * e_last[:, ld:ld + 1] + r2[C:]
        on = (o * lax.rsqrt(jnp.mean(o * o, -1, keepdims=True) + EPS)) * anw_ref[...]
        zh = zc[:, sl]
        ya_ref[:, sl] = (on * (zh * jax.nn.sigmoid(zh))).astype(BF16)


def _delta_mixer(proj, small, first_prev, s0, conv_w, alog_row, dtb_row, anw_row, *, nb, nc, C, n_valid,
                 shared_first):
    rows = nb * nc * C
    assert proj.shape[0] == rows and C % SUBLANES == 0
    cpb = C // SUBLANES
    cq, ck, cv, cz = (COL_QKV // A_QK, COL_QKV // A_QK + 1, COL_QKV // A_QK + 2, COL_Z // A_V)
    fb = (lambda b: 0) if shared_first else (lambda b: b)
    cur = lambda col: pl.BlockSpec((C, A_QK), lambda b, c: (b * nc + c, col))
    prev = lambda col: pl.BlockSpec(
        (SUBLANES, A_QK), lambda b, c: (jnp.maximum((b * nc + c) * cpb - 1, 0), col))
    frst = lambda j: pl.BlockSpec((None, SUBLANES, A_QK), lambda b, c: (fb(b), 0, j))
    row1 = lambda n: pl.BlockSpec((1, n), lambda b, c: (0, 0))
    return pl.pallas_call(
        functools.partial(_delta_kernel, C=C, n_valid=n_valid),
        grid=(nb, nc),
        in_specs=[
            cur(cq), cur(ck), cur(cv), cur(cz),
            prev(cq), prev(ck), prev(cv),
            frst(0), frst(1), frst(2),
            pl.BlockSpec((C, SMALL_PAD), lambda b, c: (b * nc + c, 0)),
            pl.BlockSpec((A_CONV, A_CONV_DIM), lambda b, c: (0, 0)),
            row1(SMALL_PAD), row1(SMALL_PAD), row1(A_DV),
            pl.BlockSpec((1, A_HEADS, A_DK, A_DV), lambda b, c: (fb(b), 0, 0, 0)),
        ],
        out_specs=[
            pl.BlockSpec((C, A_V), lambda b, c: (b * nc + c, 0)),
            pl.BlockSpec((1, A_HEADS, A_DK, A_DV), lambda b, c: (b, 0, 0, 0)),
        ],
        out_shape=[
            jax.ShapeDtypeStruct((rows, A_V), BF16),
            jax.ShapeDtypeStruct((nb, A_HEADS, A_DK, A_DV), F32),
        ],
        compiler_params=pltpu.CompilerParams(
            dimension_semantics=("arbitrary", "arbitrary"), vmem_limit_bytes=VMEM_LIMIT),
        name="delta_mixer",
    )(proj, proj, proj, proj, proj, proj, proj, first_prev, first_prev, first_prev, small,
      conv_w, alog_row, dtb_row, anw_row, s0)


def _merge_kernel(ya_ref, yb_ref, ga_ref, gb_ref, wa_ref, wb_ref, o_ref):
    ua = _bdot(ya_ref[...].astype(BF16), wa_ref[...])
    ub = _bdot(yb_ref[...].astype(BF16), wb_ref[...])
    merged = jax.nn.sigmoid(ga_ref[...]) * ua + jax.nn.sigmoid(gb_ref[...]) * ub
    o_ref[...] = merged.astype(BF16)


def _merge(ya, yb, proj, wa, wb, tm):
    m = ya.shape[0]
    assert m % tm == 0
    return pl.pallas_call(
        _merge_kernel,
        grid=(m // tm,),
        in_specs=[
            pl.BlockSpec((tm, A_V), lambda i: (i, 0)),
            pl.BlockSpec((tm, B_Q), lambda i: (i, 0)),
            pl.BlockSpec((tm, D_MODEL), lambda i: (i, COL_GA // D_MODEL)),
            pl.BlockSpec((tm, D_MODEL), lambda i: (i, COL_GB // D_MODEL)),
            pl.BlockSpec((A_V, D_MODEL), lambda i: (0, 0)),
            pl.BlockSpec((B_Q, D_MODEL), lambda i: (0, 0)),
        ],
        out_specs=pl.BlockSpec((tm, D_MODEL), lambda i: (i, 0)),
        out_shape=jax.ShapeDtypeStruct((m, D_MODEL), BF16),
        compiler_params=pltpu.CompilerParams(
            dimension_semantics=("arbitrary",), vmem_limit_bytes=VMEM_LIMIT),
        name="merge",
    )(ya, yb, proj, proj, wa, wb)


def _outproj_kernel(m_ref, h_ref, wo_ref, g_ref, wr_ref, br_ref, h2_ref, xn_ref, lg_ref):
    h2 = h_ref[...] + _bdot(m_ref[...], wo_ref[...])
    h2_ref[...] = h2
    ms = jnp.mean(h2 * h2, axis=-1, keepdims=True)
    xn = ((h2 * lax.rsqrt(ms + EPS)) * g_ref[...]).astype(BF16)
    xn_ref[...] = xn
    lg_ref[...] = _bdot(xn, wr_ref[...]) + br_ref[...]


def _outproj(merged, h, wo, gain2, w_router, b_router, tm):
    m = h.shape[0]
    assert m % tm == 0
    return pl.pallas_call(
        _outproj_kernel,
        grid=(m // tm,),
        in_specs=[
            pl.BlockSpec((tm, D_MODEL), lambda i: (i, 0)),
            pl.BlockSpec((tm, D_MODEL), lambda i: (i, 0)),
            pl.BlockSpec((D_MODEL, D_MODEL), lambda i: (0, 0)),
            pl.BlockSpec((1, D_MODEL), lambda i: (0, 0)),
            pl.BlockSpec((D_MODEL, ROUTER_PAD), lambda i: (0, 0)),
            pl.BlockSpec((1, ROUTER_PAD), lambda i: (0, 0)),
        ],
        out_specs=[
            pl.BlockSpec((tm, D_MODEL), lambda i: (i, 0)),
            pl.BlockSpec((tm, D_MODEL), lambda i: (i, 0)),
            pl.BlockSpec((tm, ROUTER_PAD), lambda i: (i, 0)),
        ],
        out_shape=[
            jax.ShapeDtypeStruct((m, D_MODEL), F32),
            jax.ShapeDtypeStruct((m, D_MODEL), BF16),
            jax.ShapeDtypeStruct((m, ROUTER_PAD), F32),
        ],
        compiler_params=pltpu.CompilerParams(
            dimension_semantics=("arbitrary",), vmem_limit_bytes=VMEM_LIMIT),
        name="outproj",
    )(merged, h, wo, gain2, w_router, b_router)


def _moe_kernel(blk_e_ref, nused_ref, x_ref, wg_ref, wu_ref, wd_ref, o_ref, wgb, wub, wdb):
    i = pl.program_id(0)
    e = blk_e_ref[i]
    e_prev = blk_e_ref[jnp.maximum(i - 1, 0)]
    used = i < nused_ref[0]

    @pl.when(jnp.logical_and(used, jnp.logical_or(i == 0, e != e_prev)))
    def _():
        def cast_in(r, c):
            sl = pl.ds(pl.multiple_of(r * 256, 256), 256)
            wgb[sl, :] = wg_ref[sl, :].astype(BF16)
            wub[sl, :] = wu_ref[sl, :].astype(BF16)
            return c

        lax.fori_loop(0, D_MODEL // 256, cast_in, 0)

        def cast_out(r, c):
            sl = pl.ds(pl.multiple_of(r * 128, 128), 128)
            wdb[sl, :] = wd_ref[sl, :].astype(BF16)
            return c

        lax.fori_loop(0, D_EXPERT // 128, cast_out, 0)

    @pl.when(used)
    def _():
        x = x_ref[...]
        g = _bdot(x, wgb[...])
        u = _bdot(x, wub[...])
        hb = (g * jax.nn.sigmoid(g)) * u
        o_ref[...] = _bdot(hb.astype(BF16), wdb[...])

    @pl.when(jnp.logical_not(used))
    def _():
        o_ref[...] = jnp.zeros_like(o_ref)


def _moe_ffn(xs, blk_e, n_used, w_gate, w_up, w_down):
    p = xs.shape[0]
    nb = p // MOE_ROWS
    grid_spec = pltpu.PrefetchScalarGridSpec(
        num_scalar_prefetch=2,
        grid=(nb,),
        in_specs=[
            pl.BlockSpec((MOE_ROWS, D_MODEL), lambda i, be, nu: (i, 0)),
            pl.BlockSpec((None, D_MODEL, D_EXPERT), lambda i, be, nu: (be[i], 0, 0)),
            pl.BlockSpec((None, D_MODEL, D_EXPERT), lambda i, be, nu: (be[i], 0, 0)),
            pl.BlockSpec((None, D_EXPERT, D_MODEL), lambda i, be, nu: (be[i], 0, 0)),
        ],
        out_specs=pl.BlockSpec((MOE_ROWS, D_MODEL), lambda i, be, nu: (i, 0)),
        scratch_shapes=[
            pltpu.VMEM((D_MODEL, D_EXPERT), BF16),
            pltpu.VMEM((D_MODEL, D_EXPERT), BF16),
            pltpu.VMEM((D_EXPERT, D_MODEL), BF16),
        ],
    )
    return pl.pallas_call(
        _moe_kernel,
        grid_spec=grid_spec,
        out_shape=jax.ShapeDtypeStruct((p, D_MODEL), F32),
        compiler_params=pltpu.CompilerParams(
            dimension_semantics=("arbitrary",), vmem_limit_bytes=VMEM_LIMIT),
        name="moe_ffn",
    )(blk_e, n_used, xs, w_gate, w_up, w_down)


HALF_TILE = 64
LANES = 128


def _dot_exact_rhs(a, b01):
    a1 = a.astype(BF16)
    r1 = a - a1.astype(F32)
    a2 = r1.astype(BF16)
    a3 = (r1 - a2.astype(F32)).astype(BF16)
    return _bdot(a1, b01) + (_bdot(a2, b01) + _bdot(a3, b01))


def _group_rms_rope(x, gain, cos128, sin128, bd, bdt):
    r, w = x.shape
    ssq = _dot_exact_rhs(x * x, bd)
    rs = lax.rsqrt(ssq * (1.0 / B_HD) + EPS)
    y = (x * _dot_exact_rhs(rs, bdt)) * gain
    reps = w // LANES
    cosw = jnp.concatenate([cos128] * reps, 1)
    sinw = jnp.concatenate([sin128] * reps, 1)
    lane = lax.broadcasted_iota(jnp.int32, (r, w), 1)
    swapped = jnp.where((lane & (B_HD // 2)) == 0,
                        pltpu.roll(y, w - B_HD // 2, 1), pltpu.roll(y, B_HD // 2, 1))
    return y * cosw + swapped * sinw


def _kv_tiles(tile, odd, lo):
    rolled = pltpu.roll(tile, HALF_TILE, 1)
    dup = jnp.where(lo, rolled, tile) if odd else jnp.where(lo, tile, rolled)
    return dup.astype(BF16), jnp.where(lo, dup, 0.0).astype(BF16), jnp.where(lo, 0.0, dup).astype(BF16)


def _nt(a, b):
    return lax.dot_general(a, b, (((1,), (1,)), ((), ())), preferred_element_type=F32)


def _sink_softmax_pv(scores, values, sink):
    m = jnp.max(scores[0], -1, keepdims=True)
    for s in scores[1:]:
        m = jnp.maximum(m, jnp.max(s, -1, keepdims=True))
    m = jnp.maximum(m, sink)
    den = jnp.exp(sink - m)
    acc = None
    for s, v in zip(scores, values):
        e = jnp.exp(s - m)
        den = den + jnp.sum(e, -1, keepdims=True)
        pv = _bdot(e.astype(BF16), v)
        acc = pv if acc is None else acc + pv
    return acc / den


def _rope_small_kernel(q_ref, k_ref, cos_ref, sin_ref, gq_ref, gk_ref, bd_ref, bdt_ref, qo_ref, ko_ref):
    cos = cos_ref[...]
    sin = sin_ref[...]
    qo_ref[...] = _group_rms_rope(q_ref[...], gq_ref[...], cos, sin, bd_ref[...], bdt_ref[...])
    ko_ref[...] = _group_rms_rope(k_ref[...], gk_ref[...], cos, sin, bd_ref[:B_KV, :], bdt_ref[:, :B_KV])


def _rope_small(proj, cos, sin, gq, gk, bd, bdt):
    m = proj.shape[0]
    full = lambda shape: pl.BlockSpec(shape, lambda i: (0, 0))
    return pl.pallas_call(
        _rope_small_kernel,
        grid=(1,),
        in_specs=[
            pl.BlockSpec((m, B_Q), lambda i: (0, COL_QB // B_Q)),
            pl.BlockSpec((m, B_KV), lambda i: (0, COL_KB // B_KV)),
            full((m, LANES)), full((m, LANES)), full((1, B_Q)), full((1, B_KV)),
            full((B_Q, LANES)), full((LANES, B_Q)),
        ],
        out_specs=[full((m, B_Q)), full((m, B_KV))],
        out_shape=[jax.ShapeDtypeStruct((m, B_Q), F32), jax.ShapeDtypeStruct((m, B_KV), F32)],
        compiler_params=pltpu.CompilerParams(
            dimension_semantics=("arbitrary",), vmem_limit_bytes=VMEM_LIMIT),
        name="rope_small",
    )(proj, proj, cos, sin, gq, gk, bd, bdt)


def _swa_prompt_kernel(sinks_ref, q_ref, k_ref, v_ref, cos_ref, sin_ref, km_ref, vm_ref, gq_ref, gk_ref,
                       bd_ref, bdt_ref, y_ref, kc_ref, vc_ref, kprev, vloprev, vhiprev):
    n = pl.program_id(1)

    @pl.when(n == 0)
    def _():
        kprev[...] = jnp.zeros_like(kprev)
        vloprev[...] = jnp.zeros_like(vloprev)
        vhiprev[...] = jnp.zeros_like(vhiprev)

    cos = cos_ref[...]
    sin = sin_ref[...]
    q = _group_rms_rope(q_ref[...], gq_ref[...], cos, sin, bd_ref[...], bdt_ref[...])
    k = _group_rms_rope(k_ref[...], gk_ref[...], cos, sin, bd_ref[:B_KV, :], bdt_ref[:, :B_KV])
    v = v_ref[...]
    kc_ref[...] = k
    vc_ref[...] = v
    km = km_ref[...]
    vm = vm_ref[...]

    qi = lax.broadcasted_iota(jnp.int32, (WINDOW, WINDOW), 0)
    kj = lax.broadcasted_iota(jnp.int32, (WINDOW, WINDOW), 1)
    cur_ok = kj <= qi
    prev_ok = jnp.logical_and(kj > qi, n > 0)
    lo = lax.broadcasted_iota(jnp.int32, (WINDOW, LANES), 1) < HALF_TILE
    lo_m = lax.broadcasted_iota(jnp.int32, (N_META, LANES), 1) < HALF_TILE
    scale = B_HD ** -0.5

    for g in range(B_KV_HEADS):
        t = g // 2
        tl = slice(t * LANES, (t + 1) * LANES)
        k2c, _, _ = _kv_tiles(k[:, tl], g % 2, lo)
        _, vlo_c, vhi_c = _kv_tiles(v[:, tl], g % 2, lo)
        k2m, _, _ = _kv_tiles(km[:, tl], g % 2, lo_m)
        _, vlo_m, vhi_m = _kv_tiles(vm[:, tl], g % 2, lo_m)
        k2p = kprev[g]
        vlo_p = vloprev[g]
        vhi_p = vhiprev[g]
        pair = None
        for hh in range(B_GROUP):
            h = g * B_GROUP + hh
            tq, half = h // 2, h % 2
            qt = q[:, tq * LANES:(tq + 1) * LANES]
            qm = (jnp.where(lo, qt, 0.0) if half == 0 else jnp.where(lo, 0.0, qt)).astype(BF16)
            s_cur = jnp.where(cur_ok, _nt(qm, k2c) * scale, -jnp.inf)
            s_prev = jnp.where(prev_ok, _nt(qm, k2p) * scale, -jnp.inf)
            s_meta = _nt(qm, k2m) * scale
            vals = (vlo_m, vlo_p, vlo_c) if half == 0 else (vhi_m, vhi_p, vhi_c)
            o = _sink_softmax_pv([s_meta, s_prev, s_cur], vals, sinks_ref[h])
            if half == 0:
                pair = o
            else:
                y_ref[:, tq * LANES:(tq + 1) * LANES] = (pair + o).astype(BF16)
        kprev[g] = k2c
        vloprev[g] = vlo_c
        vhiprev[g] = vhi_c


def _swa_prompt(proj, cos, sin, k_meta, v_meta, gq, gk, bd, bdt, sinks, nb, nblk):
    rows = nb * nblk * WINDOW
    assert proj.shape[0] == rows
    return pl.pallas_call(
        _swa_prompt_kernel,
        grid=(nb, nblk),
        in_specs=[
            pl.BlockSpec(memory_space=pltpu.SMEM),
            pl.BlockSpec((WINDOW, B_Q), lambda b, n: (b * nblk + n, COL_QB // B_Q)),
            pl.BlockSpec((WINDOW, B_KV), lambda b, n: (b * nblk + n, COL_KB // B_KV)),
            pl.BlockSpec((WINDOW, B_KV), lambda b, n: (b * nblk + n, COL_VB // B_KV)),
            pl.BlockSpec((WINDOW, LANES), lambda b, n: (n, 0)),
            pl.BlockSpec((WINDOW, LANES), lambda b, n: (n, 0)),
            pl.BlockSpec((N_META, B_KV), lambda b, n: (0, 0)),
            pl.BlockSpec((N_META, B_KV), lambda b, n: (0, 0)),
            pl.BlockSpec((1, B_Q), lambda b, n: (0, 0)),
            pl.BlockSpec((1, B_KV), lambda b, n: (0, 0)),
            pl.BlockSpec((B_Q, LANES), lambda b, n: (0, 0)),
            pl.BlockSpec((LANES, B_Q), lambda b, n: (0, 0)),
        ],
        out_specs=[
            pl.BlockSpec((WINDOW, B_Q), lambda b, n: (b * nblk + n, 0)),
            pl.BlockSpec((None, WINDOW, B_KV), lambda b, n: (b, 0, 0)),
            pl.BlockSpec((None, WINDOW, B_KV), lambda b, n: (b, 0, 0)),
        ],
        scratch_shapes=[pltpu.VMEM((B_KV_HEADS, WINDOW, LANES), BF16)] * 3,
        out_shape=[
            jax.ShapeDtypeStruct((rows, B_Q), BF16),
            jax.ShapeDtypeStruct((nb, WINDOW, B_KV), F32),
            jax.ShapeDtypeStruct((nb, WINDOW, B_KV), F32),
        ],
        compiler_params=pltpu.CompilerParams(
            dimension_semantics=("arbitrary", "arbitrary"), vmem_limit_bytes=VMEM_LIMIT),
        name="swa_prompt",
    )(sinks, proj, proj, proj, cos, sin, k_meta, v_meta, gq, gk, bd, bdt)


def _swa_sample_kernel(sinks_ref, q_ref, k_ref, v_ref, y_ref, *, n_keys, n_new):
    q = q_ref[...]
    k = k_ref[...]
    v = v_ref[...]
    rq = q.shape[0]
    t = lax.broadcasted_iota(jnp.int32, (rq, n_keys), 0)
    r = lax.broadcasted_iota(jnp.int32, (rq, n_keys), 1)
    wj = r - N_META
    win_pos = PAST_LEN - WINDOW + wj
    nm = r - N_META - WINDOW
    ok = ((r < N_META)
          | ((wj >= 0) & (wj < WINDOW) & (win_pos >= N_META) & (wj >= t + 1))
          | ((nm >= 0) & (nm <= t) & (nm > t - WINDOW) & (nm < n_new)))
    lo_q = lax.broadcasted_iota(jnp.int32, (rq, LANES), 1) < HALF_TILE
    lo_k = lax.broadcasted_iota(jnp.int32, (n_keys, LANES), 1) < HALF_TILE
    scale = B_HD ** -0.5
    for g in range(B_KV_HEADS):
        tl = slice((g // 2) * LANES, (g // 2 + 1) * LANES)
        k2, _, _ = _kv_tiles(k[:, tl], g % 2, lo_k)
        _, vlo, vhi = _kv_tiles(v[:, tl], g % 2, lo_k)
        pair = None
        for hh in range(B_GROUP):
            h = g * B_GROUP + hh
            tq, half = h // 2, h % 2
            qt = q[:, tq * LANES:(tq + 1) * LANES]
            qm = (jnp.where(lo_q, qt, 0.0) if half == 0 else jnp.where(lo_q, 0.0, qt)).astype(BF16)
            s = jnp.where(ok, _nt(qm, k2) * scale, -jnp.inf)
            o = _sink_softmax_pv([s], [vlo if half == 0 else vhi], sinks_ref[h])
            if half == 0:
                pair = o
            else:
                y_ref[:, tq * LANES:(tq + 1) * LANES] = (pair + o).astype(BF16)


def _swa_sample(q8, kk, vv, sinks, n_new):
    nb, rq, _ = q8.shape
    n_keys = kk.shape[1]
    return pl.pallas_call(
        functools.partial(_swa_sample_kernel, n_keys=n_keys, n_new=n_new),
        grid=(nb,),
        in_specs=[
            pl.BlockSpec(memory_space=pltpu.SMEM),
            pl.BlockSpec((None, rq, B_Q), lambda b: (b, 0, 0)),
            pl.BlockSpec((None, n_keys, B_KV), lambda b: (b, 0, 0)),
            pl.BlockSpec((None, n_keys, B_KV), lambda b: (b, 0, 0)),
        ],
        out_specs=pl.BlockSpec((None, rq, B_Q), lambda b: (b, 0, 0)),
        out_shape=jax.ShapeDtypeStruct((nb, rq, B_Q), BF16),
        compiler_params=pltpu.CompilerParams(
            dimension_semantics=("arbitrary",), vmem_limit_bytes=VMEM_LIMIT),
        name="swa_sample",
    )(sinks, q8, kk, vv)


def _rope_tables(pos):
    half = B_HD // 2
    inv_freq = ROPE_THETA ** (-jnp.arange(half, dtype=F32) / half)
    ang = pos.astype(F32)[:, None] * inv_freq[None, :]
    cos, sin = jnp.cos(ang), jnp.sin(ang)
    return jnp.concatenate([cos, cos, cos, cos], 1), jnp.concatenate([-sin, sin, -sin, sin], 1)


def _hier_moe(xn2, logits, w_gate, w_up, w_down):
    T = xn2.shape[0]
    lg = logits[:, :N_GROUPS]
    pg = jax.nn.softmax(lg, -1)
    p_grp, grp = lax.top_k(pg, 1)
    le = logits[:, N_GROUPS:N_GROUPS + N_EXPERTS].reshape(T, N_GROUPS, EXPERTS_PER_GROUP)
    le = jnp.take_along_axis(le, grp[:, :, None], 1)[:, 0]
    top_p, top_i = lax.top_k(jax.nn.softmax(le, -1), TOP_K)
    gate = p_grp * top_p / jnp.sum(top_p, -1, keepdims=True)
    expert = (grp * EXPERTS_PER_GROUP + top_i).astype(jnp.int32)

    A = T * TOP_K
    nb = -(-A // MOE_ROWS) + N_EXPERTS
    P = nb * MOE_ROWS
    e_flat = expert.reshape(-1)
    onehot = (e_flat[:, None] == jnp.arange(N_EXPERTS, dtype=jnp.int32)[None, :]).astype(jnp.int32)
    csum = jnp.cumsum(onehot, 0)
    rank = jnp.take_along_axis(csum, e_flat[:, None], 1)[:, 0] - 1
    counts = csum[-1]
    padded = (counts + MOE_ROWS - 1) // MOE_ROWS * MOE_ROWS
    pend = jnp.cumsum(padded)
    pstart = pend - padded
    dest = pstart[e_flat] + rank
    tok_flat = jnp.repeat(jnp.arange(T, dtype=jnp.int32), TOP_K)
    slot_tok = jnp.full((P,), T, jnp.int32).at[dest].set(tok_flat)
    x_pad = jnp.concatenate([xn2, jnp.zeros((1, D_MODEL), xn2.dtype)], 0)
    xs = x_pad[slot_tok]
    blk_e = jnp.minimum(
        jnp.searchsorted(pend, jnp.arange(nb, dtype=jnp.int32) * MOE_ROWS, side='right'),
        N_EXPERTS - 1).astype(jnp.int32)
    n_used = (pend[-1:] // MOE_ROWS).astype(jnp.int32)
    ys = _moe_ffn(xs, blk_e, n_used, w_gate, w_up, w_down)
    d2 = dest.reshape(T, TOP_K)
    return gate[:, 0:1] * ys[d2[:, 0]] + gate[:, 1:2] * ys[d2[:, 1]]


def _lane_row(vals, lane0):
    return jnp.zeros((1, SMALL_PAD), F32).at[0, lane0:lane0 + vals.shape[0]].set(vals)


def kernel(x_prompt, x_sample, state_delta, state_conv, cache_swa_k, cache_swa_v, meta_tokens,
           norm1_w, w_in, conv_w, a_log, dt_bias, a_norm_w, w_up_a, q_norm_w, k_norm_w, sinks,
           w_up_b, w_o, norm2_w, w_router_group, b_router_group, w_router_expert, b_router_expert,
           w_gate, w_up, w_down):
    Bp, Sp, _ = x_prompt.shape
    Bs, Ss, _ = x_sample.shape
    n_s = Bs * Ss
    l = 0
    w = w_in[l]
    offs = [0]
    for s in (A_CONV_DIM, A_HEADS, A_HEADS, A_V, B_Q, B_KV, B_KV, D_MODEL, D_MODEL):
        offs.append(offs[-1] + s)
    seg = lambda a: w[:, offs[a]:offs[a + 1]]
    w_main = jnp.concatenate([seg(7), seg(8), seg(0), seg(3), seg(4), seg(5), seg(6)], 1).astype(BF16)
    w_small = jnp.concatenate(
        [seg(1), seg(2), jnp.zeros((D_MODEL, SMALL_PAD - 2 * A_HEADS), F32)], 1).astype(BF16)
    wa = w_up_a[l].astype(BF16)
    wb = w_up_b[l].astype(BF16)
    wo = w_o[l].astype(BF16)
    w_router = jnp.concatenate(
        [w_router_group[l], w_router_expert[l],
         jnp.zeros((D_MODEL, ROUTER_PAD - N_GROUPS - N_EXPERTS), F32)], 1).astype(BF16)
    b_router = jnp.concatenate(
        [b_router_group[l], b_router_expert[l],
         jnp.zeros((ROUTER_PAD - N_GROUPS - N_EXPERTS,), F32)])[None, :]
    g1 = norm1_w[l][None, :]
    g2 = norm2_w[l][None, :]
    alog_row = _lane_row(a_log[l], LANE_DECAY)
    dtb_row = _lane_row(dt_bias[l], LANE_DECAY)
    anw_row = a_norm_w[l][None, :]

    xp = x_prompt.reshape(Bp * Sp, D_MODEL)
    xs = x_sample.reshape(n_s, D_MODEL)
    x_small = jnp.concatenate([xs, meta_tokens], 0)
    n_small = x_small.shape[0]

    proj_p, small_p = _inproj(xp, g1, w_main, w_small, tm=1024)
    proj_s, small_s = _inproj(x_small, g1, w_main, w_small, tm=n_small)

    dm = functools.partial(_delta_mixer, conv_w=conv_w[l], alog_row=alog_row, dtb_row=dtb_row, anw_row=anw_row)
    pad_rows = lambda a, n: jnp.concatenate([a, jnp.zeros((n - a.shape[0],) + a.shape[1:], a.dtype)], 0)
    meta_proj = pad_rows(proj_s[n_s:], A_CHUNK)
    meta_small = pad_rows(small_s[n_s:], A_CHUNK)
    zero_prev = jnp.zeros((1, SUBLANES, A_CONV_DIM), F32)
    zero_state = jnp.zeros((1, A_HEADS, A_DK, A_DV), F32)
    _, s_meta = dm(meta_proj, meta_small, zero_prev, zero_state, nb=1, nc=1, C=A_CHUNK, n_valid=N_META,
                   shared_first=True)
    meta_tail = proj_s[n_s + N_META - SUBLANES:, COL_QKV:COL_QKV + A_CONV_DIM][None]
    ya_p, sdelta_p = dm(proj_p, small_p, meta_tail, s_meta, nb=Bp, nc=Sp // A_CHUNK, C=A_CHUNK,
                        n_valid=A_CHUNK, shared_first=True)
    conv_p = proj_p.reshape(Bp, Sp, PROJ_MAIN)[:, Sp - (A_CONV - 1):, COL_QKV:COL_QKV + A_CONV_DIM]
    CS = SUBLANES
    samp_proj = jnp.pad(proj_s[:n_s].reshape(Bs, Ss, PROJ_MAIN), ((0, 0), (0, CS - Ss), (0, 0)))
    samp_small = jnp.pad(small_s[:n_s].reshape(Bs, Ss, SMALL_PAD), ((0, 0), (0, CS - Ss), (0, 0)))
    samp_prev = jnp.pad(state_conv[l], ((0, 0), (SUBLANES - (A_CONV - 1), 0), (0, 0)))
    ya_s8, sdelta_s = dm(samp_proj.reshape(Bs * CS, PROJ_MAIN), samp_small.reshape(Bs * CS, SMALL_PAD),
                         samp_prev, state_delta[l], nb=Bs, nc=1, C=CS, n_valid=Ss, shared_first=False)
    ya_s = ya_s8.reshape(Bs, CS, A_V)[:, :Ss].reshape(n_s, A_V)
    qkv_s = proj_s[:n_s, COL_QKV:COL_QKV + A_CONV_DIM].reshape(Bs, Ss, A_CONV_DIM)
    conv_s = jnp.concatenate([state_conv[l], qkv_s], 1)[:, -(A_CONV - 1):]

    gq = jnp.tile(q_norm_w[l], B_HEADS)[None]
    gk = jnp.tile(k_norm_w[l], B_KV_HEADS)[None]
    bd = ((jnp.arange(B_Q) // B_HD)[:, None] == jnp.arange(LANES)[None, :]).astype(BF16)
    bdt = bd.T
    pos_small = jnp.concatenate([PAST_LEN + jnp.arange(n_s, dtype=jnp.int32) % Ss,
                                 jnp.arange(N_META, dtype=jnp.int32)])
    cos_s, sin_s = _rope_tables(pos_small)
    q_rot_s, k_rot_s = _rope_small(proj_s, cos_s, sin_s, gq, gk, bd, bdt)
    k_meta = k_rot_s[n_s:]
    v_meta = proj_s[n_s:, COL_VB:COL_VB + B_KV]
    cos_p, sin_p = _rope_tables(N_META + jnp.arange(Sp, dtype=jnp.int32))
    yb_p, k_last, v_last = _swa_prompt(proj_p, cos_p, sin_p, k_meta, v_meta, gq, gk, bd, bdt, sinks[l],
                                       Bp, Sp // WINDOW)
    cache_shape = (N_META + WINDOW, B_KV_HEADS, B_HD)
    bcast_meta = lambda a: jnp.broadcast_to(a[None], (Bp, N_META, B_KV))
    swk_p = jnp.concatenate([bcast_meta(k_meta), k_last], 1).reshape((Bp,) + cache_shape)
    swv_p = jnp.concatenate([bcast_meta(v_meta), v_last], 1).reshape((Bp,) + cache_shape)
    n_cache = N_META + WINDOW
    key_pad = -(n_cache + Ss) % SUBLANES
    zpad = jnp.zeros((Bs, key_pad, B_KV), F32)
    kk = jnp.concatenate([cache_swa_k[l].reshape(Bs, n_cache, B_KV), k_rot_s[:n_s].reshape(Bs, Ss, B_KV), zpad], 1)
    vv = jnp.concatenate([cache_swa_v[l].reshape(Bs, n_cache, B_KV),
                          proj_s[:n_s, COL_VB:COL_VB + B_KV].reshape(Bs, Ss, B_KV), zpad], 1)
    q8 = jnp.pad(q_rot_s[:n_s].reshape(Bs, Ss, B_Q), ((0, 0), (0, SUBLANES - Ss), (0, 0)))
    yb_s = _swa_sample(q8, kk, vv, sinks[l], n_new=Ss)[:, :Ss].reshape(n_s, B_Q)
    new_cache = lambda t: jnp.concatenate(
        [t[:, :N_META], t[:, n_cache + Ss - WINDOW:n_cache + Ss]], 1).reshape((Bs,) + cache_shape)
    swk_s, swv_s = new_cache(kk), new_cache(vv)

    merged_p = _merge(ya_p, yb_p, proj_p, wa, wb, tm=256)
    merged_s = _merge(ya_s, yb_s, proj_s[:n_s], wa, wb, tm=256)
    h2_p, xn2_p, lg_p = _outproj(merged_p, xp, wo, g2, w_router, b_router, tm=256)
    h2_s, xn2_s, lg_s = _outproj(merged_s, xs, wo, g2, w_router, b_router, tm=256)

    h2 = jnp.concatenate([h2_p, h2_s], 0)
    xn2 = jnp.concatenate([xn2_p, xn2_s], 0)
    lg = jnp.concatenate([lg_p, lg_s], 0)
    h3 = h2 + _hier_moe(xn2, lg, w_gate[l], w_up[l], w_down[l])

    y_prompt = h3[:Bp * Sp].reshape(Bp, Sp, D_MODEL)
    y_sample = h3[Bp * Sp:].reshape(Bs, Ss, D_MODEL)
    return (y_prompt, y_sample, sdelta_p[None], conv_p[None], swk_p[None], swv_p[None],
            sdelta_s[None], conv_s[None], swk_s[None], swv_s[None])
```

```python
import functools

import jax
import jax.numpy as jnp
from jax import lax
from jax.experimental import pallas as pl
from jax.experimental.pallas import tpu as pltpu

F32 = jnp.float32
BF16 = jnp.bfloat16

D_MODEL = 2048
N_META = 16
A_HEADS = 8
A_DK = 128
A_DV = 128
A_CONV = 4
A_CHUNK = 64
A_QK = A_HEADS * A_DK
A_V = A_HEADS * A_DV
A_CONV_DIM = 2 * A_QK + A_V
B_HEADS = 16
B_KV_HEADS = 4
B_HD = 64
B_GROUP = B_HEADS // B_KV_HEADS
B_Q = B_HEADS * B_HD
B_KV = B_KV_HEADS * B_HD
WINDOW = 128
ROPE_THETA = 10000.0
PAST_LEN = 16384
N_GROUPS = 4
EXPERTS_PER_GROUP = 8
N_EXPERTS = N_GROUPS * EXPERTS_PER_GROUP
TOP_K = 2
D_EXPERT = 512
EPS = 1e-6

COL_GA = 0
COL_GB = COL_GA + D_MODEL
COL_QKV = COL_GB + D_MODEL
COL_Z = COL_QKV + A_CONV_DIM
COL_QB = COL_Z + A_V
COL_KB = COL_QB + B_Q
COL_VB = COL_KB + B_KV
PROJ_MAIN = COL_VB + B_KV
SMALL_PAD = 128
ROUTER_PAD = 128
LANE_BETA = 0
LANE_DECAY = A_HEADS

SUBLANES = 8
MOE_ROWS = 256
VMEM_LIMIT = 56 * 1024 * 1024


def _row_chunk(tm):
    return 256 if tm % 256 == 0 else tm


def _bdot(a, b):
    return jnp.dot(a, b, preferred_element_type=F32)


def _inproj_kernel(x_ref, g_ref, w_ref, ws_ref, o_ref, os_ref, xn_ref, *, tm):
    rc = _row_chunk(tm)

    @pl.when(pl.program_id(1) == 0)
    def _():
        def norm_rows(sl):
            x = x_ref[sl, :]
            ms = jnp.mean(x * x, axis=-1, keepdims=True)
            xn_ref[sl, :] = ((x * lax.rsqrt(ms + EPS)) * g_ref[...]).astype(BF16)

        def body(r, c):
            norm_rows(pl.ds(pl.multiple_of(r * rc, rc), rc))
            return c

        if tm == rc:
            norm_rows(pl.ds(0, tm))
        else:
            lax.fori_loop(0, tm // rc, body, 0)
        os_ref[...] = _bdot(xn_ref[...], ws_ref[...])

    o_ref[...] = _bdot(xn_ref[...], w_ref[...])


def _inproj(x, gain, w_main, w_small, tm, tn=512):
    m = x.shape[0]
    assert m % tm == 0 and PROJ_MAIN % tn == 0
    return pl.pallas_call(
        functools.partial(_inproj_kernel, tm=tm),
        grid=(m // tm, PROJ_MAIN // tn),
        in_specs=[
            pl.BlockSpec((tm, D_MODEL), lambda i, j: (i, 0)),
            pl.BlockSpec((1, D_MODEL), lambda i, j: (0, 0)),
            pl.BlockSpec((D_MODEL, tn), lambda i, j: (0, j)),
            pl.BlockSpec((D_MODEL, SMALL_PAD), lambda i, j: (0, 0)),
        ],
        out_specs=[
            pl.BlockSpec((tm, tn), lambda i, j: (i, j)),
            pl.BlockSpec((tm, SMALL_PAD), lambda i, j: (i, 0)),
        ],
        out_shape=[
            jax.ShapeDtypeStruct((m, PROJ_MAIN), F32),
            jax.ShapeDtypeStruct((m, SMALL_PAD), F32),
        ],
        scratch_shapes=[pltpu.VMEM((tm, D_MODEL), BF16)],
        compiler_params=pltpu.CompilerParams(
            dimension_semantics=("arbitrary", "arbitrary"), vmem_limit_bytes=VMEM_LIMIT),
        name="inproj",
    )(x, gain, w_main, w_small)


def _split2(a):
    hi = a.astype(BF16)
    lo = (a - hi.astype(F32)).astype(BF16)
    return hi, lo


def _dot3(a, b):
    a_hi, a_lo = _split2(a)
    b_hi, b_lo = _split2(b)
    return _bdot(a_hi, b_hi) + (_bdot(a_hi, b_lo) + _bdot(a_lo, b_hi))


def _dot_exact_lhs(a_bf16, b):
    b1 = b.astype(BF16)
    r1 = b - b1.astype(F32)
    b2 = r1.astype(BF16)
    b3 = (r1 - b2.astype(F32)).astype(BF16)
    return _bdot(a_bf16, b1) + (_bdot(a_bf16, b2) + _bdot(a_bf16, b3))


def _unit_lower_inverses(ms, ri, ci, c):
    same = lambda n: (ri >> n) == (ci >> n)
    eye = (ri == ci).astype(F32)
    blk8 = same(3)
    n1 = [jnp.where(blk8, m, 0.0) for m in ms]
    n2 = [_dot3(a, a) for a in n1]
    ps = [eye - a for a in n1]
    ps = [p + _dot3(p, b) for p, b in zip(ps, n2)]
    n4 = [_dot3(b, b) for b in n2]
    ps = [p + _dot3(p, d) for p, d in zip(ps, n4)]
    lg = 3
    while (1 << lg) < c:
        sel = jnp.logical_and(same(lg + 1), jnp.logical_not(same(lg)))
        ts = [_dot3(p, jnp.where(sel, m, 0.0)) for p, m in zip(ps, ms)]
        ps = [p - _dot3(t, p) for p, t in zip(ps, ts)]
        lg += 1
    return ps


def _delta_kernel(qc, kc, vc, zc, qp, kp, vp, qf, kf, vf, sm_ref, cw_ref, alog_ref, dtb_ref, anw_ref, s0_ref,
                  ya_ref, s_ref, *, C, n_valid):
    first = pl.program_id(1) == 0

    @pl.when(first)
    def _():
        s_ref[...] = s0_ref[...]

    rows = lax.broadcasted_iota(jnp.int32, (C, 1), 0)
    valid = rows < n_valid
    ri = lax.broadcasted_iota(jnp.int32, (C, C), 0)
    ci = lax.broadcasted_iota(jnp.int32, (C, C), 1)
    lower = ri >= ci
    strict = ri > ci

    def conv(cur_ref, prev_ref, first_ref, col0):
        prev = jnp.where(first, first_ref[...], prev_ref[...])
        x = jnp.concatenate([prev, cur_ref[...]], 0)
        w = cw_ref[:, col0:col0 + A_QK]
        acc = x[SUBLANES:SUBLANES + C] * w[A_CONV - 1:A_CONV]
        for s in range(1, A_CONV):
            acc = acc + x[SUBLANES - s:SUBLANES - s + C] * w[A_CONV - 1 - s:A_CONV - s]
        return acc * jax.nn.sigmoid(acc)

    qx = conv(qc, qp, qf, 0)
    kx = conv(kc, kp, kf, A_QK)
    vx = conv(vc, vp, vf, 2 * A_QK)

    sm = sm_ref[...]
    beta_all = jnp.where(valid, jax.nn.sigmoid(sm), 0.0)
    g_all = jnp.where(valid, -jnp.exp(alog_ref[...]) * jax.nn.softplus(sm + dtb_ref[...]), 0.0)
    g_cum = _dot_exact_lhs(lower.astype(BF16), g_all)
    g_cum_t = g_cum.T
    e_g = jnp.exp(g_cum)
    g_last = g_cum[C - 1:C, :]
    e_rest = jnp.exp(g_last - g_cum)
    e_last = jnp.exp(g_last)

    heads = range(A_HEADS)
    hs = [slice(h * A_DK, (h + 1) * A_DK) for h in heads]
    col = lambda a, h: a[:, LANE_DECAY + h:LANE_DECAY + h + 1]
    qs = [qx[:, s] for s in hs]
    qs = [q * lax.rsqrt(jnp.sum(q * q, -1, keepdims=True) + EPS) * (A_DK ** -0.5) for q in qs]
    ks = [kx[:, s] for s in hs]
    ks = [jnp.where(valid, k * lax.rsqrt(jnp.sum(k * k, -1, keepdims=True) + EPS), 0.0) for k in ks]
    vs = [jnp.where(valid, vx[:, s], 0.0) for s in hs]
    betas = [beta_all[:, LANE_BETA + h:LANE_BETA + h + 1] for h in heads]
    gammas = [jnp.exp(jnp.where(lower, col(g_cum, h) - g_cum_t[LANE_DECAY + h:LANE_DECAY + h + 1, :], -jnp.inf))
              for h in heads]
    kbs = [k * b for k, b in zip(ks, betas)]
    a1s = [lax.dot_general(jnp.concatenate([kb, q], 0).astype(BF16), k.astype(BF16),
                           (((1,), (1,)), ((), ())), preferred_element_type=F32)
           for kb, q, k in zip(kbs, qs, ks)]
    ms = [jnp.where(strict, a1[:C] * gm, 0.0) for a1, gm in zip(a1s, gammas)]
    attns = [a1[C:] * gm for a1, gm in zip(a1s, gammas)]
    rhss = [jnp.concatenate([v * b, kb * col(e_g, h)], 1) for h, v, b, kb in zip(heads, vs, betas, kbs)]
    sols = [_dot3(t, r) for t, r in zip(_unit_lower_inverses(ms, ri, ci, C), rhss)]
    s_olds = [s_ref[0, h] for h in heads]
    wqs = [_bdot(jnp.concatenate([sol[:, A_DV:], q * col(e_g, h)], 0).astype(BF16), s.astype(BF16))
           for h, sol, q, s in zip(heads, sols, qs, s_olds)]
    v_news = [sol[:, :A_DV] - wq[:C] for sol, wq in zip(sols, wqs)]
    kg_ts = [(k * col(e_rest, h)).T for h, k in zip(heads, ks)]
    r2s = [_bdot(jnp.concatenate([at, kg], 0).astype(BF16), vn.astype(BF16))
           for at, kg, vn in zip(attns, kg_ts, v_news)]
    for h in heads:
        s_ref[0, h] = s_olds[h] * col(e_last, h) + r2s[h][C:]
        o = wqs[h][C:] + r2s[h][:C]
        on = (o * lax.rsqrt(jnp.mean(o * o, -1, keepdims=True) + EPS)) * anw_ref[...]
        zh = zc[:, hs[h]]
        ya_ref[:, hs[h]] = (on * (zh * jax.nn.sigmoid(zh))).astype(BF16)


def _delta_mixer(proj, small, first_prev, s0, conv_w, alog_row, dtb_row, anw_row, *, nb, nc, C, n_valid,
                 shared_first):
    rows = nb * nc * C
    assert proj.shape[0] == rows and C % SUBLANES == 0
    cpb = C // SUBLANES
    cq, ck, cv, cz = (COL_QKV // A_QK, COL_QKV // A_QK + 1, COL_QKV // A_QK + 2, COL_Z // A_V)
    fb = (lambda b: 0) if shared_first else (lambda b: b)
    cur = lambda col: pl.BlockSpec((C, A_QK), lambda b, c: (b * nc + c, col))
    prev = lambda col: pl.BlockSpec(
        (SUBLANES, A_QK), lambda b, c: (jnp.maximum((b * nc + c) * cpb - 1, 0), col))
    frst = lambda j: pl.BlockSpec((None, SUBLANES, A_QK), lambda b, c: (fb(b), 0, j))
    row1 = lambda n: pl.BlockSpec((1, n), lambda b, c: (0, 0))
    return pl.pallas_call(
        functools.partial(_delta_kernel, C=C, n_valid=n_valid),
        grid=(nb, nc),
        in_specs=[
            cur(cq), cur(ck), cur(cv), cur(cz),
            prev(cq), prev(ck), prev(cv),
            frst(0), frst(1), frst(2),
            pl.BlockSpec((C, SMALL_PAD), lambda b, c: (b * nc + c, 0)),
            pl.BlockSpec((A_CONV, A_CONV_DIM), lambda b, c: (0, 0)),
            row1(SMALL_PAD), row1(SMALL_PAD), row1(A_DV),
            pl.BlockSpec((1, A_HEADS, A_DK, A_DV), lambda b, c: (fb(b), 0, 0, 0)),
        ],
        out_specs=[
            pl.BlockSpec((C, A_V), lambda b, c: (b * nc + c, 0)),
            pl.BlockSpec((1, A_HEADS, A_DK, A_DV), lambda b, c: (b, 0, 0, 0)),
        ],
        out_shape=[
            jax.ShapeDtypeStruct((rows, A_V), BF16),
            jax.ShapeDtypeStruct((nb, A_HEADS, A_DK, A_DV), F32),
        ],
        compiler_params=pltpu.CompilerParams(
            dimension_semantics=("arbitrary", "arbitrary"), vmem_limit_bytes=VMEM_LIMIT),
        name="delta_mixer",
    )(proj, proj, proj, proj, proj, proj, proj, first_prev, first_prev, first_prev, small,
      conv_w, alog_row, dtb_row, anw_row, s0)


def _merge_kernel(ya_ref, yb_ref, ga_ref, gb_ref, wa_ref, wb_ref, o_ref):
    ua = _bdot(ya_ref[...].astype(BF16), wa_ref[...])
    ub = _bdot(yb_ref[...].astype(BF16), wb_ref[...])
    merged = jax.nn.sigmoid(ga_ref[...]) * ua + jax.nn.sigmoid(gb_ref[...]) * ub
    o_ref[...] = merged.astype(BF16)


def _merge(ya, yb, proj, wa, wb, tm):
    m = ya.shape[0]
    assert m % tm == 0
    return pl.pallas_call(
        _merge_kernel,
        grid=(m // tm,),
        in_specs=[
            pl.BlockSpec((tm, A_V), lambda i: (i, 0)),
            pl.BlockSpec((tm, B_Q), lambda i: (i, 0)),
            pl.BlockSpec((tm, D_MODEL), lambda i: (i, COL_GA // D_MODEL)),
            pl.BlockSpec((tm, D_MODEL), lambda i: (i, COL_GB // D_MODEL)),
            pl.BlockSpec((A_V, D_MODEL), lambda i: (0, 0)),
            pl.BlockSpec((B_Q, D_MODEL), lambda i: (0, 0)),
        ],
        out_specs=pl.BlockSpec((tm, D_MODEL), lambda i: (i, 0)),
        out_shape=jax.ShapeDtypeStruct((m, D_MODEL), BF16),
        compiler_params=pltpu.CompilerParams(
            dimension_semantics=("arbitrary",), vmem_limit_bytes=VMEM_LIMIT),
        name="merge",
    )(ya, yb, proj, proj, wa, wb)


LANE_SENTINEL = ROUTER_PAD - 1


def _lane_argmax(vals, eligible, lane):
    top = jnp.max(jnp.where(eligible, vals, -jnp.inf), -1, keepdims=True)
    idx = jnp.min(jnp.where(jnp.logical_and(eligible, vals == top), lane, LANE_SENTINEL), -1, keepdims=True)
    return top, idx


def _masked_softmax(logits, eligible):
    z = jnp.where(eligible, logits, -jnp.inf)
    e = jnp.exp(z - jnp.max(z, -1, keepdims=True))
    return e / jnp.sum(e, -1, keepdims=True)


def _outproj_kernel(m_ref, h_ref, wo_ref, g_ref, wr_ref, br_ref, cnt_in_ref,
                    h2_ref, xn_ref, gate_ref, route_ref, cnt_ref):
    tm = h_ref.shape[0]

    @pl.when(pl.program_id(0) == 0)
    def _():
        cnt_ref[...] = cnt_in_ref[...]

    h2 = h_ref[...] + _bdot(m_ref[...], wo_ref[...])
    h2_ref[...] = h2
    ms = jnp.mean(h2 * h2, axis=-1, keepdims=True)
    xn = ((h2 * lax.rsqrt(ms + EPS)) * g_ref[...]).astype(BF16)
    xn_ref[...] = xn
    lg = _bdot(xn, wr_ref[...]) + br_ref[...]

    lane = lax.broadcasted_iota(jnp.int32, (tm, ROUTER_PAD), 1)
    is_g = lane < N_GROUPS
    p_grp, grp = _lane_argmax(_masked_softmax(lg, is_g), is_g, lane)
    ex = lane - N_GROUPS
    is_e = jnp.logical_and(jnp.logical_and(ex >= 0, ex < N_EXPERTS), (ex >> 3) == grp)
    pe = _masked_softmax(lg, is_e)
    p1, i1 = _lane_argmax(pe, is_e, lane)
    rest = jnp.logical_and(is_e, lane != i1)
    p2, i2 = _lane_argmax(pe, rest, lane)
    den = p1 + p2
    gate_ref[...] = jnp.where(lane == 0, (p_grp * p1) / den, jnp.where(lane == 1, (p_grp * p2) / den, 0.0))

    e1 = i1 - N_GROUPS
    e2 = i2 - N_GROUPS
    oh1 = lane == e1
    oh2 = lane == e2
    ri = lax.broadcasted_iota(jnp.int32, (tm, tm), 0)
    ci = lax.broadcasted_iota(jnp.int32, (tm, tm), 1)
    below = (ri > ci).astype(BF16)
    f1 = oh1.astype(F32)
    f2 = oh2.astype(F32)
    tot1 = jnp.sum(f1, 0, keepdims=True)
    run = cnt_ref[...]
    before1 = run + _bdot(below, oh1.astype(BF16))
    before2 = run + tot1 + _bdot(below, oh2.astype(BF16))
    rank1 = jnp.sum(f1 * before1, -1, keepdims=True).astype(jnp.int32)
    rank2 = jnp.sum(f2 * before2, -1, keepdims=True).astype(jnp.int32)
    cnt_ref[...] = run + tot1 + jnp.sum(f2, 0, keepdims=True)
    route_ref[...] = jnp.where(lane == 0, e1, jnp.where(lane == 1, e2, jnp.where(
        lane == 2, rank1, jnp.where(lane == 3, rank2, 0))))


def _outproj(merged, h, wo, gain2, w_router, b_router, counts_in, tm):
    m = h.shape[0]
    assert m % tm == 0
    row = lambda n: pl.BlockSpec((1, n), lambda i: (0, 0))
    tile = lambda n: pl.BlockSpec((tm, n), lambda i: (i, 0))
    return pl.pallas_call(
        _outproj_kernel,
        grid=(m // tm,),
        in_specs=[
            tile(D_MODEL), tile(D_MODEL),
            pl.BlockSpec((D_MODEL, D_MODEL), lambda i: (0, 0)),
            row(D_MODEL),
            pl.BlockSpec((D_MODEL, ROUTER_PAD), lambda i: (0, 0)),
            row(ROUTER_PAD), row(ROUTER_PAD),
        ],
        out_specs=[tile(D_MODEL), tile(D_MODEL), tile(ROUTER_PAD), tile(ROUTER_PAD), row(ROUTER_PAD)],
        out_shape=[
            jax.ShapeDtypeStruct((m, D_MODEL), F32),
            jax.ShapeDtypeStruct((m, D_MODEL), BF16),
            jax.ShapeDtypeStruct((m, ROUTER_PAD), F32),
            jax.ShapeDtypeStruct((m, ROUTER_PAD), jnp.int32),
            jax.ShapeDtypeStruct((1, ROUTER_PAD), F32),
        ],
        compiler_params=pltpu.CompilerParams(
            dimension_semantics=("arbitrary",), vmem_limit_bytes=VMEM_LIMIT),
        name="outproj",
    )(merged, h, wo, gain2, w_router, b_router, counts_in)


def _moe_kernel(blk_e_ref, nused_ref, x_ref, wg_ref, wu_ref, wd_ref, o_ref, wgb, wub, wdb):
    i = pl.program_id(0)
    e = blk_e_ref[i]
    e_prev = blk_e_ref[jnp.maximum(i - 1, 0)]
    used = i < nused_ref[0]

    @pl.when(jnp.logical_and(used, jnp.logical_or(i == 0, e != e_prev)))
    def _():
        def cast_in(r, c):
            sl = pl.ds(pl.multiple_of(r * 256, 256), 256)
            wgb[sl, :] = wg_ref[sl, :].astype(BF16)
            wub[sl, :] = wu_ref[sl, :].astype(BF16)
            return c

        lax.fori_loop(0, D_MODEL // 256, cast_in, 0)

        def cast_out(r, c):
            sl = pl.ds(pl.multiple_of(r * 128, 128), 128)
            wdb[sl, :] = wd_ref[sl, :].astype(BF16)
            return c

        lax.fori_loop(0, D_EXPERT // 128, cast_out, 0)

    @pl.when(used)
    def _():
        x = x_ref[...]
        g = _bdot(x, wgb[...])
        u = _bdot(x, wub[...])
        hb = (g * jax.nn.sigmoid(g)) * u
        o_ref[...] = _bdot(hb.astype(BF16), wdb[...])

    @pl.when(jnp.logical_not(used))
    def _():
        o_ref[...] = jnp.zeros_like(o_ref)


def _moe_ffn(xs, blk_e, n_used, w_gate, w_up, w_down):
    p = xs.shape[0]
    nb = p // MOE_ROWS
    grid_spec = pltpu.PrefetchScalarGridSpec(
        num_scalar_prefetch=2,
        grid=(nb,),
        in_specs=[
            pl.BlockSpec((MOE_ROWS, D_MODEL), lambda i, be, nu: (i, 0)),
            pl.BlockSpec((None, D_MODEL, D_EXPERT), lambda i, be, nu: (be[i], 0, 0)),
            pl.BlockSpec((None, D_MODEL, D_EXPERT), lambda i, be, nu: (be[i], 0, 0)),
            pl.BlockSpec((None, D_EXPERT, D_MODEL), lambda i, be, nu: (be[i], 0, 0)),
        ],
        out_specs=pl.BlockSpec((MOE_ROWS, D_MODEL), lambda i, be, nu: (i, 0)),
        scratch_shapes=[
            pltpu.VMEM((D_MODEL, D_EXPERT), BF16),
            pltpu.VMEM((D_MODEL, D_EXPERT), BF16),
            pltpu.VMEM((D_EXPERT, D_MODEL), BF16),
        ],
    )
    return pl.pallas_call(
        _moe_kernel,
        grid_spec=grid_spec,
        out_shape=jax.ShapeDtypeStruct((p, D_MODEL), F32),
        compiler_params=pltpu.CompilerParams(
            dimension_semantics=("arbitrary",), vmem_limit_bytes=VMEM_LIMIT),
        name="moe_ffn",
    )(blk_e, n_used, xs, w_gate, w_up, w_down)


HALF_TILE = 64
LANES = 128


def _dot_exact_rhs(a, b01):
    a1 = a.astype(BF16)
    r1 = a - a1.astype(F32)
    a2 = r1.astype(BF16)
    a3 = (r1 - a2.astype(F32)).astype(BF16)
    return _bdot(a1, b01) + (_bdot(a2, b01) + _bdot(a3, b01))


def _group_rms_rope(x, gain, cos128, sin128, bd, bdt):
    r, w = x.shape
    ssq = _dot_exact_rhs(x * x, bd)
    rs = lax.rsqrt(ssq * (1.0 / B_HD) + EPS)
    y = (x * _dot_exact_rhs(rs, bdt)) * gain
    reps = w // LANES
    cosw = jnp.concatenate([cos128] * reps, 1)
    sinw = jnp.concatenate([sin128] * reps, 1)
    lane = lax.broadcasted_iota(jnp.int32, (r, w), 1)
    swapped = jnp.where((lane & (B_HD // 2)) == 0,
                        pltpu.roll(y, w - B_HD // 2, 1), pltpu.roll(y, B_HD // 2, 1))
    return y * cosw + swapped * sinw


def _kv_tiles(tile, odd, lo):
    rolled = pltpu.roll(tile, HALF_TILE, 1)
    dup = jnp.where(lo, rolled, tile) if odd else jnp.where(lo, tile, rolled)
    return dup.astype(BF16), jnp.where(lo, dup, 0.0).astype(BF16), jnp.where(lo, 0.0, dup).astype(BF16)


def _nt(a, b):
    return lax.dot_general(a, b, (((1,), (1,)), ((), ())), preferred_element_type=F32)


def _sink_softmax_pv(score_lists, value_lists, sinks):
    ms = []
    for scores, sink in zip(score_lists, sinks):
        m = jnp.max(scores[0], -1, keepdims=True)
        for s in scores[1:]:
            m = jnp.maximum(m, jnp.max(s, -1, keepdims=True))
        ms.append(jnp.maximum(m, sink))
    es = [[jnp.exp(s - m) for s in scores] for scores, m in zip(score_lists, ms)]
    dens = []
    for e_blocks, m, sink in zip(es, ms, sinks):
        den = jnp.exp(sink - m)
        for e in e_blocks:
            den = den + jnp.sum(e, -1, keepdims=True)
        dens.append(den)
    pvs = [[_bdot(e.astype(BF16), v) for e, v in zip(e_blocks, values)]
           for e_blocks, values in zip(es, value_lists)]
    outs = []
    for pv, den in zip(pvs, dens):
        acc = pv[0]
        for x in pv[1:]:
            acc = acc + x
        outs.append(acc / den)
    return outs


def _rope_small_kernel(q_ref, k_ref, cos_ref, sin_ref, gq_ref, gk_ref, bd_ref, bdt_ref, qo_ref, ko_ref):
    cos = cos_ref[...]
    sin = sin_ref[...]
    qo_ref[...] = _group_rms_rope(q_ref[...], gq_ref[...], cos, sin, bd_ref[...], bdt_ref[...])
    ko_ref[...] = _group_rms_rope(k_ref[...], gk_ref[...], cos, sin, bd_ref[:B_KV, :], bdt_ref[:, :B_KV])


def _rope_small(proj, cos, sin, gq, gk, bd, bdt):
    m = proj.shape[0]
    full = lambda shape: pl.BlockSpec(shape, lambda i: (0, 0))
    return pl.pallas_call(
        _rope_small_kernel,
        grid=(1,),
        in_specs=[
            pl.BlockSpec((m, B_Q), lambda i: (0, COL_QB // B_Q)),
            pl.BlockSpec((m, B_KV), lambda i: (0, COL_KB // B_KV)),
            full((m, LANES)), full((m, LANES)), full((1, B_Q)), full((1, B_KV)),
            full((B_Q, LANES)), full((LANES, B_Q)),
        ],
        out_specs=[full((m, B_Q)), full((m, B_KV))],
        out_shape=[jax.ShapeDtypeStruct((m, B_Q), F32), jax.ShapeDtypeStruct((m, B_KV), F32)],
        compiler_params=pltpu.CompilerParams(
            dimension_semantics=("arbitrary",), vmem_limit_bytes=VMEM_LIMIT),
        name="rope_small",
    )(proj, proj, cos, sin, gq, gk, bd, bdt)


def _swa_prompt_kernel(sinks_ref, q_ref, k_ref, v_ref, cos_ref, sin_ref, km_ref, vm_ref, gq_ref, gk_ref,
                       bd_ref, bdt_ref, y_ref, kc_ref, vc_ref, kprev, vloprev, vhiprev):
    n = pl.program_id(1)

    @pl.when(n == 0)
    def _():
        kprev[...] = jnp.zeros_like(kprev)
        vloprev[...] = jnp.zeros_like(vloprev)
        vhiprev[...] = jnp.zeros_like(vhiprev)

    cos = cos_ref[...]
    sin = sin_ref[...]
    q = _group_rms_rope(q_ref[...], gq_ref[...], cos, sin, bd_ref[...], bdt_ref[...])
    k = _group_rms_rope(k_ref[...], gk_ref[...], cos, sin, bd_ref[:B_KV, :], bdt_ref[:, :B_KV])
    v = v_ref[...]
    kc_ref[...] = k
    vc_ref[...] = v
    km = km_ref[...]
    vm = vm_ref[...]

    qi = lax.broadcasted_iota(jnp.int32, (WINDOW, WINDOW), 0)
    kj = lax.broadcasted_iota(jnp.int32, (WINDOW, WINDOW), 1)
    cur_ok = kj <= qi
    prev_ok = jnp.logical_and(kj > qi, n > 0)
    lo = lax.broadcasted_iota(jnp.int32, (WINDOW, LANES), 1) < HALF_TILE
    lo_m = lax.broadcasted_iota(jnp.int32, (N_META, LANES), 1) < HALF_TILE
    scale = B_HD ** -0.5

    kv = []
    for g in range(B_KV_HEADS):
        tl = slice((g // 2) * LANES, (g // 2 + 1) * LANES)
        k2c, _, _ = _kv_tiles(k[:, tl], g % 2, lo)
        _, vlo_c, vhi_c = _kv_tiles(v[:, tl], g % 2, lo)
        k2m, _, _ = _kv_tiles(km[:, tl], g % 2, lo_m)
        _, vlo_m, vhi_m = _kv_tiles(vm[:, tl], g % 2, lo_m)
        kv.append(((k2m, kprev[g], k2c), (vlo_m, vloprev[g], vlo_c), (vhi_m, vhiprev[g], vhi_c)))
    heads = range(B_HEADS)
    qms = []
    for h in heads:
        qt = q[:, (h // 2) * LANES:(h // 2 + 1) * LANES]
        qms.append((jnp.where(lo, qt, 0.0) if h % 2 == 0 else jnp.where(lo, 0.0, qt)).astype(BF16))
    raw = [[_nt(qms[h], kk) * scale for kk in kv[h // B_GROUP][0]] for h in heads]
    scores = [[sm, jnp.where(prev_ok, sp, -jnp.inf), jnp.where(cur_ok, sc, -jnp.inf)] for sm, sp, sc in raw]
    outs = _sink_softmax_pv(scores, [kv[h // B_GROUP][1 + h % 2] for h in heads], [sinks_ref[h] for h in heads])
    for t in range(B_HEADS // 2):
        y_ref[:, t * LANES:(t + 1) * LANES] = (outs[2 * t] + outs[2 * t + 1]).astype(BF16)
    for g in range(B_KV_HEADS):
        kprev[g] = kv[g][0][2]
        vloprev[g] = kv[g][1][2]
        vhiprev[g] = kv[g][2][2]


def _swa_prompt(proj, cos, sin, k_meta, v_meta, gq, gk, bd, bdt, sinks, nb, nblk):
    rows = nb * nblk * WINDOW
    assert proj.shape[0] == rows
    return pl.pallas_call(
        _swa_prompt_kernel,
        grid=(nb, nblk),
        in_specs=[
            pl.BlockSpec(memory_space=pltpu.SMEM),
            pl.BlockSpec((WINDOW, B_Q), lambda b, n: (b * nblk + n, COL_QB // B_Q)),
            pl.BlockSpec((WINDOW, B_KV), lambda b, n: (b * nblk + n, COL_KB // B_KV)),
            pl.BlockSpec((WINDOW, B_KV), lambda b, n: (b * nblk + n, COL_VB // B_KV)),
            pl.BlockSpec((WINDOW, LANES), lambda b, n: (n, 0)),
            pl.BlockSpec((WINDOW, LANES), lambda b, n: (n, 0)),
            pl.BlockSpec((N_META, B_KV), lambda b, n: (0, 0)),
            pl.BlockSpec((N_META, B_KV), lambda b, n: (0, 0)),
            pl.BlockSpec((1, B_Q), lambda b, n: (0, 0)),
            pl.BlockSpec((1, B_KV), lambda b, n: (0, 0)),
            pl.BlockSpec((B_Q, LANES), lambda b, n: (0, 0)),
            pl.BlockSpec((LANES, B_Q), lambda b, n: (0, 0)),
        ],
        out_specs=[
            pl.BlockSpec((WINDOW, B_Q), lambda b, n: (b * nblk + n, 0)),
            pl.BlockSpec((None, WINDOW, B_KV), lambda b, n: (b, 0, 0)),
            pl.BlockSpec((None, WINDOW, B_KV), lambda b, n: (b, 0, 0)),
        ],
        scratch_shapes=[pltpu.VMEM((B_KV_HEADS, WINDOW, LANES), BF16)] * 3,
        out_shape=[
            jax.ShapeDtypeStruct((rows, B_Q), BF16),
            jax.ShapeDtypeStruct((nb, WINDOW, B_KV), F32),
            jax.ShapeDtypeStruct((nb, WINDOW, B_KV), F32),
        ],
        compiler_params=pltpu.CompilerParams(
            dimension_semantics=("arbitrary", "arbitrary"), vmem_limit_bytes=VMEM_LIMIT),
        name="swa_prompt",
    )(sinks, proj, proj, proj, cos, sin, k_meta, v_meta, gq, gk, bd, bdt)


def _swa_sample_kernel(sinks_ref, q_ref, k_ref, v_ref, y_ref, *, n_keys, n_new):
    q = q_ref[...]
    k = k_ref[...]
    v = v_ref[...]
    rq = q.shape[0]
    t = lax.broadcasted_iota(jnp.int32, (rq, n_keys), 0)
    r = lax.broadcasted_iota(jnp.int32, (rq, n_keys), 1)
    wj = r - N_META
    win_pos = PAST_LEN - WINDOW + wj
    nm = r - N_META - WINDOW
    ok = ((r < N_META)
          | ((wj >= 0) & (wj < WINDOW) & (win_pos >= N_META) & (wj >= t + 1))
          | ((nm >= 0) & (nm <= t) & (nm > t - WINDOW) & (nm < n_new)))
    lo_q = lax.broadcasted_iota(jnp.int32, (rq, LANES), 1) < HALF_TILE
    lo_k = lax.broadcasted_iota(jnp.int32, (n_keys, LANES), 1) < HALF_TILE
    scale = B_HD ** -0.5
    kv = []
    for g in range(B_KV_HEADS):
        tl = slice((g // 2) * LANES, (g // 2 + 1) * LANES)
        k2, _, _ = _kv_tiles(k[:, tl], g % 2, lo_k)
        _, vlo, vhi = _kv_tiles(v[:, tl], g % 2, lo_k)
        kv.append((k2, vlo, vhi))
    heads = range(B_HEADS)
    qms = []
    for h in heads:
        qt = q[:, (h // 2) * LANES:(h // 2 + 1) * LANES]
        qms.append((jnp.where(lo_q, qt, 0.0) if h % 2 == 0 else jnp.where(lo_q, 0.0, qt)).astype(BF16))
    scores = [[jnp.where(ok, _nt(qms[h], kv[h // B_GROUP][0]) * scale, -jnp.inf)] for h in heads]
    outs = _sink_softmax_pv(scores, [[kv[h // B_GROUP][1 + h % 2]] for h in heads],
                            [sinks_ref[h] for h in heads])
    for t in range(B_HEADS // 2):
        y_ref[:, t * LANES:(t + 1) * LANES] = (outs[2 * t] + outs[2 * t + 1]).astype(BF16)


def _swa_sample(q8, kk, vv, sinks, n_new):
    nb, rq, _ = q8.shape
    n_keys = kk.shape[1]
    return pl.pallas_call(
        functools.partial(_swa_sample_kernel, n_keys=n_keys, n_new=n_new),
        grid=(nb,),
        in_specs=[
            pl.BlockSpec(memory_space=pltpu.SMEM),
            pl.BlockSpec((None, rq, B_Q), lambda b: (b, 0, 0)),
            pl.BlockSpec((None, n_keys, B_KV), lambda b: (b, 0, 0)),
            pl.BlockSpec((None, n_keys, B_KV), lambda b: (b, 0, 0)),
        ],
        out_specs=pl.BlockSpec((None, rq, B_Q), lambda b: (b, 0, 0)),
        out_shape=jax.ShapeDtypeStruct((nb, rq, B_Q), BF16),
        compiler_params=pltpu.CompilerParams(
            dimension_semantics=("arbitrary",), vmem_limit_bytes=VMEM_LIMIT),
        name="swa_sample",
    )(sinks, q8, kk, vv)


def _rope_tables(pos):
    half = B_HD // 2
    inv_freq = ROPE_THETA ** (-jnp.arange(half, dtype=F32) / half)
    ang = pos.astype(F32)[:, None] * inv_freq[None, :]
    cos, sin = jnp.cos(ang), jnp.sin(ang)
    return jnp.concatenate([cos, cos, cos, cos], 1), jnp.concatenate([-sin, sin, -sin, sin], 1)


def _hier_moe(xn2, gates, route, counts, w_gate, w_up, w_down):
    T = xn2.shape[0]
    gate = gates[:, :TOP_K]
    expert = route[:, :TOP_K]
    rank = route[:, TOP_K:2 * TOP_K]
    A = T * TOP_K
    nb = -(-A // MOE_ROWS) + N_EXPERTS
    P = nb * MOE_ROWS
    cnt = counts[0, :N_EXPERTS].astype(jnp.int32)
    padded = (cnt + MOE_ROWS - 1) // MOE_ROWS * MOE_ROWS
    pend = jnp.cumsum(padded)
    pstart = pend - padded
    dest = pstart[expert] + rank
    tok = jnp.broadcast_to(jnp.arange(T, dtype=jnp.int32)[:, None], (T, TOP_K))
    slot_tok = jnp.zeros((P,), jnp.int32).at[dest.reshape(-1)].set(tok.reshape(-1))
    xs = xn2[slot_tok]
    starts = jnp.arange(nb, dtype=jnp.int32) * MOE_ROWS
    blk_e = jnp.minimum(jnp.sum((pend[None, :] <= starts[:, None]).astype(jnp.int32), 1), N_EXPERTS - 1)
    n_used = (pend[-1:] // MOE_ROWS).astype(jnp.int32)
    ys = _moe_ffn(xs, blk_e, n_used, w_gate, w_up, w_down)
    return gate[:, 0:1] * ys[dest[:, 0]] + gate[:, 1:2] * ys[dest[:, 1]]


def _lane_row(vals, lane0):
    return jnp.zeros((1, SMALL_PAD), F32).at[0, lane0:lane0 + vals.shape[0]].set(vals)


def kernel(x_prompt, x_sample, state_delta, state_conv, cache_swa_k, cache_swa_v, meta_tokens,
           norm1_w, w_in, conv_w, a_log, dt_bias, a_norm_w, w_up_a, q_norm_w, k_norm_w, sinks,
           w_up_b, w_o, norm2_w, w_router_group, b_router_group, w_router_expert, b_router_expert,
           w_gate, w_up, w_down):
    Bp, Sp, _ = x_prompt.shape
    Bs, Ss, _ = x_sample.shape
    n_s = Bs * Ss
    l = 0
    w = w_in[l]
    offs = [0]
    for s in (A_CONV_DIM, A_HEADS, A_HEADS, A_V, B_Q, B_KV, B_KV, D_MODEL, D_MODEL):
        offs.append(offs[-1] + s)
    seg = lambda a: w[:, offs[a]:offs[a + 1]]
    w_main = jnp.concatenate([seg(7), seg(8), seg(0), seg(3), seg(4), seg(5), seg(6)], 1).astype(BF16)
    w_small = jnp.concatenate(
        [seg(1), seg(2), jnp.zeros((D_MODEL, SMALL_PAD - 2 * A_HEADS), F32)], 1).astype(BF16)
    wa = w_up_a[l].astype(BF16)
    wb = w_up_b[l].astype(BF16)
    wo = w_o[l].astype(BF16)
    w_router = jnp.concatenate(
        [w_router_group[l], w_router_expert[l],
         jnp.zeros((D_MODEL, ROUTER_PAD - N_GROUPS - N_EXPERTS), F32)], 1).astype(BF16)
    b_router = jnp.concatenate(
        [b_router_group[l], b_router_expert[l],
         jnp.zeros((ROUTER_PAD - N_GROUPS - N_EXPERTS,), F32)])[None, :]
    g1 = norm1_w[l][None, :]
    g2 = norm2_w[l][None, :]
    alog_row = _lane_row(a_log[l], LANE_DECAY)
    dtb_row = _lane_row(dt_bias[l], LANE_DECAY)
    anw_row = a_norm_w[l][None, :]

    xp = x_prompt.reshape(Bp * Sp, D_MODEL)
    xs = x_sample.reshape(n_s, D_MODEL)
    x_small = jnp.concatenate([xs, meta_tokens], 0)
    n_small = x_small.shape[0]

    proj_p, small_p = _inproj(xp, g1, w_main, w_small, tm=1024)
    proj_s, small_s = _inproj(x_small, g1, w_main, w_small, tm=n_small)

    dm = functools.partial(_delta_mixer, conv_w=conv_w[l], alog_row=alog_row, dtb_row=dtb_row, anw_row=anw_row)
    pad_rows = lambda a, n: jnp.concatenate([a, jnp.zeros((n - a.shape[0],) + a.shape[1:], a.dtype)], 0)
    meta_proj = pad_rows(proj_s[n_s:], A_CHUNK)
    meta_small = pad_rows(small_s[n_s:], A_CHUNK)
    zero_prev = jnp.zeros((1, SUBLANES, A_CONV_DIM), F32)
    zero_state = jnp.zeros((1, A_HEADS, A_DK, A_DV), F32)
    _, s_meta = dm(meta_proj, meta_small, zero_prev, zero_state, nb=1, nc=1, C=A_CHUNK, n_valid=N_META,
                   shared_first=True)
    meta_tail = proj_s[n_s + N_META - SUBLANES:, COL_QKV:COL_QKV + A_CONV_DIM][None]
    ya_p, sdelta_p = dm(proj_p, small_p, meta_tail, s_meta, nb=Bp, nc=Sp // A_CHUNK, C=A_CHUNK,
                        n_valid=A_CHUNK, shared_first=True)
    conv_p = proj_p.reshape(Bp, Sp, PROJ_MAIN)[:, Sp - (A_CONV - 1):, COL_QKV:COL_QKV + A_CONV_DIM]
    CS = SUBLANES
    samp_proj = jnp.pad(proj_s[:n_s].reshape(Bs, Ss, PROJ_MAIN), ((0, 0), (0, CS - Ss), (0, 0)))
    samp_small = jnp.pad(small_s[:n_s].reshape(Bs, Ss, SMALL_PAD), ((0, 0), (0, CS - Ss), (0, 0)))
    samp_prev = jnp.pad(state_conv[l], ((0, 0), (SUBLANES - (A_CONV - 1), 0), (0, 0)))
    ya_s8, sdelta_s = dm(samp_proj.reshape(Bs * CS, PROJ_MAIN), samp_small.reshape(Bs * CS, SMALL_PAD),
                         samp_prev, state_delta[l], nb=Bs, nc=1, C=CS, n_valid=Ss, shared_first=False)
    ya_s = ya_s8.reshape(Bs, CS, A_V)[:, :Ss].reshape(n_s, A_V)
    qkv_s = proj_s[:n_s, COL_QKV:COL_QKV + A_CONV_DIM].reshape(Bs, Ss, A_CONV_DIM)
    conv_s = jnp.concatenate([state_conv[l], qkv_s], 1)[:, -(A_CONV - 1):]

    gq = jnp.tile(q_norm_w[l], B_HEADS)[None]
    gk = jnp.tile(k_norm_w[l], B_KV_HEADS)[None]
    bd = ((jnp.arange(B_Q) // B_HD)[:, None] == jnp.arange(LANES)[None, :]).astype(BF16)
    bdt = bd.T
    pos_small = jnp.concatenate([PAST_LEN + jnp.arange(n_s, dtype=jnp.int32) % Ss,
                                 jnp.arange(N_META, dtype=jnp.int32)])
    cos_s, sin_s = _rope_tables(pos_small)
    q_rot_s, k_rot_s = _rope_small(proj_s, cos_s, sin_s, gq, gk, bd, bdt)
    k_meta = k_rot_s[n_s:]
    v_meta = proj_s[n_s:, COL_VB:COL_VB + B_KV]
    cos_p, sin_p = _rope_tables(N_META + jnp.arange(Sp, dtype=jnp.int32))
    yb_p, k_last, v_last = _swa_prompt(proj_p, cos_p, sin_p, k_meta, v_meta, gq, gk, bd, bdt, sinks[l],
                                       Bp, Sp // WINDOW)
    cache_shape = (N_META + WINDOW, B_KV_HEADS, B_HD)
    bcast_meta = lambda a: jnp.broadcast_to(a[None], (Bp, N_META, B_KV))
    swk_p = jnp.concatenate([bcast_meta(k_meta), k_last], 1).reshape((Bp,) + cache_shape)
    swv_p = jnp.concatenate([bcast_meta(v_meta), v_last], 1).reshape((Bp,) + cache_shape)
    n_cache = N_META + WINDOW
    key_pad = -(n_cache + Ss) % SUBLANES
    zpad = jnp.zeros((Bs, key_pad, B_KV), F32)
    kk = jnp.concatenate([cache_swa_k[l].reshape(Bs, n_cache, B_KV), k_rot_s[:n_s].reshape(Bs, Ss, B_KV), zpad], 1)
    vv = jnp.concatenate([cache_swa_v[l].reshape(Bs, n_cache, B_KV),
                          proj_s[:n_s, COL_VB:COL_VB + B_KV].reshape(Bs, Ss, B_KV), zpad], 1)
    q8 = jnp.pad(q_rot_s[:n_s].reshape(Bs, Ss, B_Q), ((0, 0), (0, SUBLANES - Ss), (0, 0)))
    yb_s = _swa_sample(q8, kk, vv, sinks[l], n_new=Ss)[:, :Ss].reshape(n_s, B_Q)
    new_cache = lambda t: jnp.concatenate(
        [t[:, :N_META], t[:, n_cache + Ss - WINDOW:n_cache + Ss]], 1).reshape((Bs,) + cache_shape)
    swk_s, swv_s = new_cache(kk), new_cache(vv)

    merged_p = _merge(ya_p, yb_p, proj_p, wa, wb, tm=256)
    merged_s = _merge(ya_s, yb_s, proj_s[:n_s], wa, wb, tm=256)
    cnt0 = jnp.zeros((1, ROUTER_PAD), F32)
    h2_p, xn2_p, gt_p, rt_p, cnt_p = _outproj(merged_p, xp, wo, g2, w_router, b_router, cnt0, tm=256)
    h2_s, xn2_s, gt_s, rt_s, cnt = _outproj(merged_s, xs, wo, g2, w_router, b_router, cnt_p, tm=256)

    cat = lambda a, b: jnp.concatenate([a, b], 0)
    h2 = cat(h2_p, h2_s)
    h3 = h2 + _hier_moe(cat(xn2_p, xn2_s), cat(gt_p, gt_s), cat(rt_p, rt_s), cnt,
                        w_gate[l], w_up[l], w_down[l])

    y_prompt = h3[:Bp * Sp].reshape(Bp, Sp, D_MODEL)
    y_sample = h3[Bp * Sp:].reshape(Bs, Ss, D_MODEL)
    return (y_prompt, y_sample, sdelta_p[None], conv_p[None], swk_p[None], swv_p[None],
            sdelta_s[None], conv_s[None], swk_s[None], swv_s[None])
```

```python
import functools

import jax
import jax.numpy as jnp
from jax import lax
from jax.experimental import pallas as pl
from jax.experimental.pallas import tpu as pltpu

F32 = jnp.float32
BF16 = jnp.bfloat16

D_MODEL = 2048
N_META = 16
A_HEADS = 8
A_DK = 128
A_DV = 128
A_CONV = 4
A_CHUNK = 64
A_QK = A_HEADS * A_DK
A_V = A_HEADS * A_DV
A_CONV_DIM = 2 * A_QK + A_V
B_HEADS = 16
B_KV_HEADS = 4
B_HD = 64
B_GROUP = B_HEADS // B_KV_HEADS
B_Q = B_HEADS * B_HD
B_KV = B_KV_HEADS * B_HD
WINDOW = 128
ROPE_THETA = 10000.0
PAST_LEN = 16384
N_GROUPS = 4
EXPERTS_PER_GROUP = 8
N_EXPERTS = N_GROUPS * EXPERTS_PER_GROUP
TOP_K = 2
D_EXPERT = 512
EPS = 1e-6

COL_GA = 0
COL_GB = COL_GA + D_MODEL
COL_QKV = COL_GB + D_MODEL
COL_Z = COL_QKV + A_CONV_DIM
COL_QB = COL_Z + A_V
COL_KB = COL_QB + B_Q
COL_VB = COL_KB + B_KV
PROJ_MAIN = COL_VB + B_KV
SMALL_PAD = 128
ROUTER_PAD = 128
LANE_BETA = 0
LANE_DECAY = A_HEADS

SUBLANES = 8
MOE_ROWS = 256
ROW_SLABS = 8
SLAB = D_MODEL // ROW_SLABS
VMEM_LIMIT = 56 * 1024 * 1024


def _row_chunk(tm):
    return 256 if tm % 256 == 0 else tm


def _bdot(a, b):
    return jnp.dot(a, b, preferred_element_type=F32)


def _store_slabs(ref3, val):
    for s in range(ROW_SLABS):
        ref3[:, s, :] = val[:, s * SLAB:(s + 1) * SLAB]


def _load_slabs(ref3):
    return jnp.concatenate([ref3[:, s, :] for s in range(ROW_SLABS)], 1)


def _inproj_kernel(x_ref, g_ref, w_ref, ws_ref, o_ref, os_ref, xn_ref, *, tm):
    rc = _row_chunk(tm)

    @pl.when(pl.program_id(1) == 0)
    def _():
        def norm_rows(sl):
            x = x_ref[sl, :]
            ms = jnp.mean(x * x, axis=-1, keepdims=True)
            xn_ref[sl, :] = ((x * lax.rsqrt(ms + EPS)) * g_ref[...]).astype(BF16)

        def body(r, c):
            norm_rows(pl.ds(pl.multiple_of(r * rc, rc), rc))
            return c

        if tm == rc:
            norm_rows(pl.ds(0, tm))
        else:
            lax.fori_loop(0, tm // rc, body, 0)
        os_ref[...] = _bdot(xn_ref[...], ws_ref[...])

    o_ref[...] = _bdot(xn_ref[...], w_ref[...])


def _inproj(x, gain, w_main, w_small, tm, tn=512):
    m = x.shape[0]
    assert m % tm == 0 and PROJ_MAIN % tn == 0
    return pl.pallas_call(
        functools.partial(_inproj_kernel, tm=tm),
        grid=(m // tm, PROJ_MAIN // tn),
        in_specs=[
            pl.BlockSpec((tm, D_MODEL), lambda i, j: (i, 0)),
            pl.BlockSpec((1, D_MODEL), lambda i, j: (0, 0)),
            pl.BlockSpec((D_MODEL, tn), lambda i, j: (0, j)),
            pl.BlockSpec((D_MODEL, SMALL_PAD), lambda i, j: (0, 0)),
        ],
        out_specs=[
            pl.BlockSpec((tm, tn), lambda i, j: (i, j)),
            pl.BlockSpec((tm, SMALL_PAD), lambda i, j: (i, 0)),
        ],
        out_shape=[
            jax.ShapeDtypeStruct((m, PROJ_MAIN), F32),
            jax.ShapeDtypeStruct((m, SMALL_PAD), F32),
        ],
        scratch_shapes=[pltpu.VMEM((tm, D_MODEL), BF16)],
        compiler_params=pltpu.CompilerParams(
            dimension_semantics=("arbitrary", "arbitrary"), vmem_limit_bytes=VMEM_LIMIT),
        name="inproj",
    )(x, gain, w_main, w_small)


def _split2(a):
    hi = a.astype(BF16)
    lo = (a - hi.astype(F32)).astype(BF16)
    return hi, lo


def _dot3(a, b):
    a_hi, a_lo = _split2(a)
    b_hi, b_lo = _split2(b)
    return _bdot(a_hi, b_hi) + (_bdot(a_hi, b_lo) + _bdot(a_lo, b_hi))


def _dot_exact_lhs(a_bf16, b):
    b1 = b.astype(BF16)
    r1 = b - b1.astype(F32)
    b2 = r1.astype(BF16)
    b3 = (r1 - b2.astype(F32)).astype(BF16)
    return _bdot(a_bf16, b1) + (_bdot(a_bf16, b2) + _bdot(a_bf16, b3))


def _unit_lower_inverses(ms, ri, ci, c):
    same = lambda n: (ri >> n) == (ci >> n)
    eye = (ri == ci).astype(F32)
    blk8 = same(3)
    n1 = [jnp.where(blk8, m, 0.0) for m in ms]
    n2 = [_dot3(a, a) for a in n1]
    ps = [eye - a for a in n1]
    ps = [p + _dot3(p, b) for p, b in zip(ps, n2)]
    n4 = [_dot3(b, b) for b in n2]
    ps = [p + _dot3(p, d) for p, d in zip(ps, n4)]
    lg = 3
    while (1 << lg) < c:
        sel = jnp.logical_and(same(lg + 1), jnp.logical_not(same(lg)))
        ts = [_dot3(p, jnp.where(sel, m, 0.0)) for p, m in zip(ps, ms)]
        ps = [p - _dot3(t, p) for p, t in zip(ps, ts)]
        lg += 1
    return ps


def _delta_kernel(qc, kc, vc, zc, qp, kp, vp, qf, kf, vf, sm_ref, cw_ref, alog_ref, dtb_ref, anw_ref, s0_ref,
                  ya_ref, s_ref, *, C, n_valid):
    first = pl.program_id(1) == 0

    @pl.when(first)
    def _():
        s_ref[...] = s0_ref[...]

    rows = lax.broadcasted_iota(jnp.int32, (C, 1), 0)
    valid = rows < n_valid
    ri = lax.broadcasted_iota(jnp.int32, (C, C), 0)
    ci = lax.broadcasted_iota(jnp.int32, (C, C), 1)
    lower = ri >= ci
    strict = ri > ci

    def conv(cur_ref, prev_ref, first_ref, col0):
        prev = jnp.where(first, first_ref[...], prev_ref[...])
        x = jnp.concatenate([prev, cur_ref[...]], 0)
        w = cw_ref[:, col0:col0 + A_QK]
        acc = x[SUBLANES:SUBLANES + C] * w[A_CONV - 1:A_CONV]
        for s in range(1, A_CONV):
            acc = acc + x[SUBLANES - s:SUBLANES - s + C] * w[A_CONV - 1 - s:A_CONV - s]
        return acc * jax.nn.sigmoid(acc)

    qx = conv(qc, qp, qf, 0)
    kx = conv(kc, kp, kf, A_QK)
    vx = conv(vc, vp, vf, 2 * A_QK)

    sm = sm_ref[...]
    beta_all = jnp.where(valid, jax.nn.sigmoid(sm), 0.0)
    g_all = jnp.where(valid, -jnp.exp(alog_ref[...]) * jax.nn.softplus(sm + dtb_ref[...]), 0.0)
    g_cum = _dot_exact_lhs(lower.astype(BF16), g_all)
    g_cum_t = g_cum.T
    e_g = jnp.exp(g_cum)
    g_last = g_cum[C - 1:C, :]
    e_rest = jnp.exp(g_last - g_cum)
    e_last = jnp.exp(g_last)

    heads = range(A_HEADS)
    hs = [slice(h * A_DK, (h + 1) * A_DK) for h in heads]
    col = lambda a, h: a[:, LANE_DECAY + h:LANE_DECAY + h + 1]
    qs = [qx[:, s] for s in hs]
    qs = [q * lax.rsqrt(jnp.sum(q * q, -1, keepdims=True) + EPS) * (A_DK ** -0.5) for q in qs]
    ks = [kx[:, s] for s in hs]
    ks = [jnp.where(valid, k * lax.rsqrt(jnp.sum(k * k, -1, keepdims=True) + EPS), 0.0) for k in ks]
    vs = [jnp.where(valid, vx[:, s], 0.0) for s in hs]
    betas = [beta_all[:, LANE_BETA + h:LANE_BETA + h + 1] for h in heads]
    gammas = [jnp.exp(jnp.where(lower, col(g_cum, h) - g_cum_t[LANE_DECAY + h:LANE_DECAY + h + 1, :], -jnp.inf))
              for h in heads]
    kbs = [k * b for k, b in zip(ks, betas)]
    a1s = [lax.dot_general(jnp.concatenate([kb, q], 0).astype(BF16), k.astype(BF16),
                           (((1,), (1,)), ((), ())), preferred_element_type=F32)
           for kb, q, k in zip(kbs, qs, ks)]
    ms = [jnp.where(strict, a1[:C] * gm, 0.0) for a1, gm in zip(a1s, gammas)]
    attns = [a1[C:] * gm for a1, gm in zip(a1s, gammas)]
    rhss = [jnp.concatenate([v * b, kb * col(e_g, h)], 1) for h, v, b, kb in zip(heads, vs, betas, kbs)]
    sols = [_dot3(t, r) for t, r in zip(_unit_lower_inverses(ms, ri, ci, C), rhss)]
    s_olds = [s_ref[0, h] for h in heads]
    wqs = [_bdot(jnp.concatenate([sol[:, A_DV:], q * col(e_g, h)], 0).astype(BF16), s.astype(BF16))
           for h, sol, q, s in zip(heads, sols, qs, s_olds)]
    v_news = [sol[:, :A_DV] - wq[:C] for sol, wq in zip(sols, wqs)]
    kg_ts = [(k * col(e_rest, h)).T for h, k in zip(heads, ks)]
    r2s = [_bdot(jnp.concatenate([at, kg], 0).astype(BF16), vn.astype(BF16))
           for at, kg, vn in zip(attns, kg_ts, v_news)]
    for h in heads:
        s_ref[0, h] = s_olds[h] * col(e_last, h) + r2s[h][C:]
        o = wqs[h][C:] + r2s[h][:C]
        on = (o * lax.rsqrt(jnp.mean(o * o, -1, keepdims=True) + EPS)) * anw_ref[...]
        zh = zc[:, hs[h]]
        ya_ref[:, hs[h]] = (on * (zh * jax.nn.sigmoid(zh))).astype(BF16)


def _delta_mixer(proj, small, first_prev, s0, conv_w, alog_row, dtb_row, anw_row, *, nb, nc, C, n_valid,
                 shared_first):
    rows = nb * nc * C
    assert proj.shape[0] == rows and C % SUBLANES == 0
    cpb = C // SUBLANES
    cq, ck, cv, cz = (COL_QKV // A_QK, COL_QKV // A_QK + 1, COL_QKV // A_QK + 2, COL_Z // A_V)
    fb = (lambda b: 0) if shared_first else (lambda b: b)
    cur = lambda col: pl.BlockSpec((C, A_QK), lambda b, c: (b * nc + c, col))
    prev = lambda col: pl.BlockSpec(
        (SUBLANES, A_QK), lambda b, c: (jnp.maximum((b * nc + c) * cpb - 1, 0), col))
    frst = lambda j: pl.BlockSpec((None, SUBLANES, A_QK), lambda b, c: (fb(b), 0, j))
    row1 = lambda n: pl.BlockSpec((1, n), lambda b, c: (0, 0))
    return pl.pallas_call(
        functools.partial(_delta_kernel, C=C, n_valid=n_valid),
        grid=(nb, nc),
        in_specs=[
            cur(cq), cur(ck), cur(cv), cur(cz),
            prev(cq), prev(ck), prev(cv),
            frst(0), frst(1), frst(2),
            pl.BlockSpec((C, SMALL_PAD), lambda b, c: (b * nc + c, 0)),
            pl.BlockSpec((A_CONV, A_CONV_DIM), lambda b, c: (0, 0)),
            row1(SMALL_PAD), row1(SMALL_PAD), row1(A_DV),
            pl.BlockSpec((1, A_HEADS, A_DK, A_DV), lambda b, c: (fb(b), 0, 0, 0)),
        ],
        out_specs=[
            pl.BlockSpec((C, A_V), lambda b, c: (b * nc + c, 0)),
            pl.BlockSpec((1, A_HEADS, A_DK, A_DV), lambda b, c: (b, 0, 0, 0)),
        ],
        out_shape=[
            jax.ShapeDtypeStruct((rows, A_V), BF16),
            jax.ShapeDtypeStruct((nb, A_HEADS, A_DK, A_DV), F32),
        ],
        compiler_params=pltpu.CompilerParams(
            dimension_semantics=("arbitrary", "arbitrary"), vmem_limit_bytes=VMEM_LIMIT),
        name="delta_mixer",
    )(proj, proj, proj, proj, proj, proj, proj, first_prev, first_prev, first_prev, small,
      conv_w, alog_row, dtb_row, anw_row, s0)


def _merge_kernel(ya_ref, yb_ref, ga_ref, gb_ref, wa_ref, wb_ref, o_ref):
    ua = _bdot(ya_ref[...].astype(BF16), wa_ref[...])
    ub = _bdot(yb_ref[...].astype(BF16), wb_ref[...])
    merged = jax.nn.sigmoid(ga_ref[...]) * ua + jax.nn.sigmoid(gb_ref[...]) * ub
    o_ref[...] = merged.astype(BF16)


def _merge(ya, yb, proj, wa, wb, tm):
    m = ya.shape[0]
    assert m % tm == 0
    return pl.pallas_call(
        _merge_kernel,
        grid=(m // tm,),
        in_specs=[
            pl.BlockSpec((tm, A_V), lambda i: (i, 0)),
            pl.BlockSpec((tm, B_Q), lambda i: (i, 0)),
            pl.BlockSpec((tm, D_MODEL), lambda i: (i, COL_GA // D_MODEL)),
            pl.BlockSpec((tm, D_MODEL), lambda i: (i, COL_GB // D_MODEL)),
            pl.BlockSpec((A_V, D_MODEL), lambda i: (0, 0)),
            pl.BlockSpec((B_Q, D_MODEL), lambda i: (0, 0)),
        ],
        out_specs=pl.BlockSpec((tm, D_MODEL), lambda i: (i, 0)),
        out_shape=jax.ShapeDtypeStruct((m, D_MODEL), BF16),
        compiler_params=pltpu.CompilerParams(
            dimension_semantics=("arbitrary",), vmem_limit_bytes=VMEM_LIMIT),
        name="merge",
    )(ya, yb, proj, proj, wa, wb)


LANE_SENTINEL = ROUTER_PAD - 1


def _lane_argmax(vals, eligible, lane):
    top = jnp.max(jnp.where(eligible, vals, -jnp.inf), -1, keepdims=True)
    idx = jnp.min(jnp.where(jnp.logical_and(eligible, vals == top), lane, LANE_SENTINEL), -1, keepdims=True)
    return top, idx


def _masked_softmax(logits, eligible):
    z = jnp.where(eligible, logits, -jnp.inf)
    e = jnp.exp(z - jnp.max(z, -1, keepdims=True))
    return e / jnp.sum(e, -1, keepdims=True)


def _outproj_kernel(m_ref, h_ref, wo_ref, g_ref, wr_ref, br_ref, cnt_in_ref,
                    h2_ref, xn_ref, gate_ref, route_ref, cnt_ref):
    tm = h_ref.shape[0]

    @pl.when(pl.program_id(0) == 0)
    def _():
        cnt_ref[...] = cnt_in_ref[...]

    h2 = h_ref[...] + _bdot(m_ref[...], wo_ref[...])
    h2_ref[...] = h2
    ms = jnp.mean(h2 * h2, axis=-1, keepdims=True)
    xn = (h2 * lax.rsqrt(ms + EPS)) * g_ref[...]
    _store_slabs(xn_ref, xn)
    lg = _bdot(xn.astype(BF16), wr_ref[...]) + br_ref[...]

    lane = lax.broadcasted_iota(jnp.int32, (tm, ROUTER_PAD), 1)
    is_g = lane < N_GROUPS
    p_grp, grp = _lane_argmax(_masked_softmax(lg, is_g), is_g, lane)
    ex = lane - N_GROUPS
    is_e = jnp.logical_and(jnp.logical_and(ex >= 0, ex < N_EXPERTS), (ex >> 3) == grp)
    pe = _masked_softmax(lg, is_e)
    p1, i1 = _lane_argmax(pe, is_e, lane)
    rest = jnp.logical_and(is_e, lane != i1)
    p2, i2 = _lane_argmax(pe, rest, lane)
    den = p1 + p2
    gate_ref[...] = jnp.where(lane == 0, (p_grp * p1) / den, jnp.where(lane == 1, (p_grp * p2) / den, 0.0))

    e1 = i1 - N_GROUPS
    e2 = i2 - N_GROUPS
    oh1 = lane == e1
    oh2 = lane == e2
    ri = lax.broadcasted_iota(jnp.int32, (tm, tm), 0)
    ci = lax.broadcasted_iota(jnp.int32, (tm, tm), 1)
    below = (ri > ci).astype(BF16)
    f1 = oh1.astype(F32)
    f2 = oh2.astype(F32)
    tot1 = jnp.sum(f1, 0, keepdims=True)
    run = cnt_ref[...]
    before1 = run + _bdot(below, oh1.astype(BF16))
    before2 = run + tot1 + _bdot(below, oh2.astype(BF16))
    rank1 = jnp.sum(f1 * before1, -1, keepdims=True).astype(jnp.int32)
    rank2 = jnp.sum(f2 * before2, -1, keepdims=True).astype(jnp.int32)
    cnt_ref[...] = run + tot1 + jnp.sum(f2, 0, keepdims=True)
    route_ref[...] = jnp.where(lane == 0, e1, jnp.where(lane == 1, e2, jnp.where(
        lane == 2, rank1, jnp.where(lane == 3, rank2, 0))))


def _outproj(merged, h, wo, gain2, w_router, b_router, counts_in, tm):
    m = h.shape[0]
    assert m % tm == 0
    row = lambda n: pl.BlockSpec((1, n), lambda i: (0, 0))
    tile = lambda n: pl.BlockSpec((tm, n), lambda i: (i, 0))
    return pl.pallas_call(
        _outproj_kernel,
        grid=(m // tm,),
        in_specs=[
            tile(D_MODEL), tile(D_MODEL),
            pl.BlockSpec((D_MODEL, D_MODEL), lambda i: (0, 0)),
            row(D_MODEL),
            pl.BlockSpec((D_MODEL, ROUTER_PAD), lambda i: (0, 0)),
            row(ROUTER_PAD), row(ROUTER_PAD),
        ],
        out_specs=[tile(D_MODEL), pl.BlockSpec((tm, ROW_SLABS, SLAB), lambda i: (i, 0, 0)),
                   tile(ROUTER_PAD), tile(ROUTER_PAD), row(ROUTER_PAD)],
        out_shape=[
            jax.ShapeDtypeStruct((m, D_MODEL), F32),
            jax.ShapeDtypeStruct((m, ROW_SLABS, SLAB), F32),
            jax.ShapeDtypeStruct((m, ROUTER_PAD), F32),
            jax.ShapeDtypeStruct((m, ROUTER_PAD), jnp.int32),
            jax.ShapeDtypeStruct((1, ROUTER_PAD), F32),
        ],
        compiler_params=pltpu.CompilerParams(
            dimension_semantics=("arbitrary",), vmem_limit_bytes=VMEM_LIMIT),
        name="outproj",
    )(merged, h, wo, gain2, w_router, b_router, counts_in)


def _moe_kernel(blk_e_ref, nused_ref, x_ref, wg_ref, wu_ref, wd_ref, o_ref, wgb, wub, wdb):
    i = pl.program_id(0)
    e = blk_e_ref[i]
    e_prev = blk_e_ref[jnp.maximum(i - 1, 0)]
    used = i < nused_ref[0]

    @pl.when(jnp.logical_and(used, jnp.logical_or(i == 0, e != e_prev)))
    def _():
        def cast_in(r, c):
            sl = pl.ds(pl.multiple_of(r * 256, 256), 256)
            wgb[sl, :] = wg_ref[sl, :].astype(BF16)
            wub[sl, :] = wu_ref[sl, :].astype(BF16)
            return c

        lax.fori_loop(0, D_MODEL // 256, cast_in, 0)

        def cast_out(r, c):
            sl = pl.ds(pl.multiple_of(r * 128, 128), 128)
            wdb[sl, :] = wd_ref[sl, :].astype(BF16)
            return c

        lax.fori_loop(0, D_EXPERT // 128, cast_out, 0)

    @pl.when(used)
    def _():
        x = _load_slabs(x_ref).astype(BF16)
        g = _bdot(x, wgb[...])
        u = _bdot(x, wub[...])
        hb = (g * jax.nn.sigmoid(g)) * u
        _store_slabs(o_ref, _bdot(hb.astype(BF16), wdb[...]))

    @pl.when(jnp.logical_not(used))
    def _():
        o_ref[...] = jnp.zeros_like(o_ref)


def _moe_ffn(xs, blk_e, n_used, w_gate, w_up, w_down):
    p = xs.shape[0]
    nb = p // MOE_ROWS
    grid_spec = pltpu.PrefetchScalarGridSpec(
        num_scalar_prefetch=2,
        grid=(nb,),
        in_specs=[
            pl.BlockSpec((MOE_ROWS, ROW_SLABS, SLAB), lambda i, be, nu: (jnp.minimum(i, nu[0] - 1), 0, 0)),
            pl.BlockSpec((None, D_MODEL, D_EXPERT), lambda i, be, nu: (be[i], 0, 0)),
            pl.BlockSpec((None, D_MODEL, D_EXPERT), lambda i, be, nu: (be[i], 0, 0)),
            pl.BlockSpec((None, D_EXPERT, D_MODEL), lambda i, be, nu: (be[i], 0, 0)),
        ],
        out_specs=pl.BlockSpec((MOE_ROWS, ROW_SLABS, SLAB), lambda i, be, nu: (i, 0, 0)),
        scratch_shapes=[
            pltpu.VMEM((D_MODEL, D_EXPERT), BF16),
            pltpu.VMEM((D_MODEL, D_EXPERT), BF16),
            pltpu.VMEM((D_EXPERT, D_MODEL), BF16),
        ],
    )
    return pl.pallas_call(
        _moe_kernel,
        grid_spec=grid_spec,
        out_shape=jax.ShapeDtypeStruct((p, ROW_SLABS, SLAB), F32),
        compiler_params=pltpu.CompilerParams(
            dimension_semantics=("arbitrary",), vmem_limit_bytes=VMEM_LIMIT),
        name="moe_ffn",
    )(blk_e, n_used, xs, w_gate, w_up, w_down)


HALF_TILE = 64
LANES = 128


def _dot_exact_rhs(a, b01):
    a1 = a.astype(BF16)
    r1 = a - a1.astype(F32)
    a2 = r1.astype(BF16)
    a3 = (r1 - a2.astype(F32)).astype(BF16)
    return _bdot(a1, b01) + (_bdot(a2, b01) + _bdot(a3, b01))


def _group_rms_rope(x, gain, cos128, sin128, bd, bdt):
    r, w = x.shape
    ssq = _dot_exact_rhs(x * x, bd)
    rs = lax.rsqrt(ssq * (1.0 / B_HD) + EPS)
    y = (x * _dot_exact_rhs(rs, bdt)) * gain
    reps = w // LANES
    cosw = jnp.concatenate([cos128] * reps, 1)
    sinw = jnp.concatenate([sin128] * reps, 1)
    lane = lax.broadcasted_iota(jnp.int32, (r, w), 1)
    swapped = jnp.where((lane & (B_HD // 2)) == 0,
                        pltpu.roll(y, w - B_HD // 2, 1), pltpu.roll(y, B_HD // 2, 1))
    return y * cosw + swapped * sinw


def _kv_tiles(tile, odd, lo):
    rolled = pltpu.roll(tile, HALF_TILE, 1)
    dup = jnp.where(lo, rolled, tile) if odd else jnp.where(lo, tile, rolled)
    return dup.astype(BF16), jnp.where(lo, dup, 0.0).astype(BF16), jnp.where(lo, 0.0, dup).astype(BF16)


def _nt(a, b):
    return lax.dot_general(a, b, (((1,), (1,)), ((), ())), preferred_element_type=F32)


def _sink_softmax_pv(score_lists, value_lists, sinks):
    ms = []
    for scores, sink in zip(score_lists, sinks):
        m = jnp.max(scores[0], -1, keepdims=True)
        for s in scores[1:]:
            m = jnp.maximum(m, jnp.max(s, -1, keepdims=True))
        ms.append(jnp.maximum(m, sink))
    es = [[jnp.exp(s - m) for s in scores] for scores, m in zip(score_lists, ms)]
    dens = []
    for e_blocks, m, sink in zip(es, ms, sinks):
        den = jnp.exp(sink - m)
        for e in e_blocks:
            den = den + jnp.sum(e, -1, keepdims=True)
        dens.append(den)
    pvs = [[_bdot(e.astype(BF16), v) for e, v in zip(e_blocks, values)]
           for e_blocks, values in zip(es, value_lists)]
    outs = []
    for pv, den in zip(pvs, dens):
        acc = pv[0]
        for x in pv[1:]:
            acc = acc + x
        outs.append(acc / den)
    return outs


def _rope_small_kernel(q_ref, k_ref, cos_ref, sin_ref, gq_ref, gk_ref, bd_ref, bdt_ref, qo_ref, ko_ref):
    cos = cos_ref[...]
    sin = sin_ref[...]
    qo_ref[...] = _group_rms_rope(q_ref[...], gq_ref[...], cos, sin, bd_ref[...], bdt_ref[...])
    ko_ref[...] = _group_rms_rope(k_ref[...], gk_ref[...], cos, sin, bd_ref[:B_KV, :], bdt_ref[:, :B_KV])


def _rope_small(proj, cos, sin, gq, gk, bd, bdt):
    m = proj.shape[0]
    full = lambda shape: pl.BlockSpec(shape, lambda i: (0, 0))
    return pl.pallas_call(
        _rope_small_kernel,
        grid=(1,),
        in_specs=[
            pl.BlockSpec((m, B_Q), lambda i: (0, COL_QB // B_Q)),
            pl.BlockSpec((m, B_KV), lambda i: (0, COL_KB // B_KV)),
            full((m, LANES)), full((m, LANES)), full((1, B_Q)), full((1, B_KV)),
            full((B_Q, LANES)), full((LANES, B_Q)),
        ],
        out_specs=[full((m, B_Q)), full((m, B_KV))],
        out_shape=[jax.ShapeDtypeStruct((m, B_Q), F32), jax.ShapeDtypeStruct((m, B_KV), F32)],
        compiler_params=pltpu.CompilerParams(
            dimension_semantics=("arbitrary",), vmem_limit_bytes=VMEM_LIMIT),
        name="rope_small",
    )(proj, proj, cos, sin, gq, gk, bd, bdt)


def _swa_prompt_kernel(sinks_ref, q_ref, k_ref, v_ref, cos_ref, sin_ref, km_ref, vm_ref, gq_ref, gk_ref,
                       bd_ref, bdt_ref, y_ref, kc_ref, vc_ref, kprev, vloprev, vhiprev):
    n = pl.program_id(1)

    @pl.when(n == 0)
    def _():
        kprev[...] = jnp.zeros_like(kprev)
        vloprev[...] = jnp.zeros_like(vloprev)
        vhiprev[...] = jnp.zeros_like(vhiprev)

    cos = cos_ref[...]
    sin = sin_ref[...]
    q = _group_rms_rope(q_ref[...], gq_ref[...], cos, sin, bd_ref[...], bdt_ref[...])
    k = _group_rms_rope(k_ref[...], gk_ref[...], cos, sin, bd_ref[:B_KV, :], bdt_ref[:, :B_KV])
    v = v_ref[...]
    kc_ref[...] = k
    vc_ref[...] = v
    km = km_ref[...]
    vm = vm_ref[...]

    qi = lax.broadcasted_iota(jnp.int32, (WINDOW, WINDOW), 0)
    kj = lax.broadcasted_iota(jnp.int32, (WINDOW, WINDOW), 1)
    cur_ok = kj <= qi
    prev_ok = jnp.logical_and(kj > qi, n > 0)
    lo = lax.broadcasted_iota(jnp.int32, (WINDOW, LANES), 1) < HALF_TILE
    lo_m = lax.broadcasted_iota(jnp.int32, (N_META, LANES), 1) < HALF_TILE
    scale = B_HD ** -0.5

    kv = []
    for g in range(B_KV_HEADS):
        tl = slice((g // 2) * LANES, (g // 2 + 1) * LANES)
        k2c, _, _ = _kv_tiles(k[:, tl], g % 2, lo)
        _, vlo_c, vhi_c = _kv_tiles(v[:, tl], g % 2, lo)
        k2m, _, _ = _kv_tiles(km[:, tl], g % 2, lo_m)
        _, vlo_m, vhi_m = _kv_tiles(vm[:, tl], g % 2, lo_m)
        kv.append(((k2m, kprev[g], k2c), (vlo_m, vloprev[g], vlo_c), (vhi_m, vhiprev[g], vhi_c)))
    heads = range(B_HEADS)
    qms = []
    for h in heads:
        qt = q[:, (h // 2) * LANES:(h // 2 + 1) * LANES]
        qms.append((jnp.where(lo, qt, 0.0) if h % 2 == 0 else jnp.where(lo, 0.0, qt)).astype(BF16))
    raw = [[_nt(qms[h], kk) * scale for kk in kv[h // B_GROUP][0]] for h in heads]
    scores = [[sm, jnp.where(prev_ok, sp, -jnp.inf), jnp.where(cur_ok, sc, -jnp.inf)] for sm, sp, sc in raw]
    outs = _sink_softmax_pv(scores, [kv[h // B_GROUP][1 + h % 2] for h in heads], [sinks_ref[h] for h in heads])
    for t in range(B_HEADS // 2):
        y_ref[:, t * LANES:(t + 1) * LANES] = (outs[2 * t] + outs[2 * t + 1]).astype(BF16)
    for g in range(B_KV_HEADS):
        kprev[g] = kv[g][0][2]
        vloprev[g] = kv[g][1][2]
        vhiprev[g] = kv[g][2][2]


def _swa_prompt(proj, cos, sin, k_meta, v_meta, gq, gk, bd, bdt, sinks, nb, nblk):
    rows = nb * nblk * WINDOW
    assert proj.shape[0] == rows
    return pl.pallas_call(
        _swa_prompt_kernel,
        grid=(nb, nblk),
        in_specs=[
            pl.BlockSpec(memory_space=pltpu.SMEM),
            pl.BlockSpec((WINDOW, B_Q), lambda b, n: (b * nblk + n, COL_QB // B_Q)),
            pl.BlockSpec((WINDOW, B_KV), lambda b, n: (b * nblk + n, COL_KB // B_KV)),
            pl.BlockSpec((WINDOW, B_KV), lambda b, n: (b * nblk + n, COL_VB // B_KV)),
            pl.BlockSpec((WINDOW, LANES), lambda b, n: (n, 0)),
            pl.BlockSpec((WINDOW, LANES), lambda b, n: (n, 0)),
            pl.BlockSpec((N_META, B_KV), lambda b, n: (0, 0)),
            pl.BlockSpec((N_META, B_KV), lambda b, n: (0, 0)),
            pl.BlockSpec((1, B_Q), lambda b, n: (0, 0)),
            pl.BlockSpec((1, B_KV), lambda b, n: (0, 0)),
            pl.BlockSpec((B_Q, LANES), lambda b, n: (0, 0)),
            pl.BlockSpec((LANES, B_Q), lambda b, n: (0, 0)),
        ],
        out_specs=[
            pl.BlockSpec((WINDOW, B_Q), lambda b, n: (b * nblk + n, 0)),
            pl.BlockSpec((None, WINDOW, B_KV), lambda b, n: (b, 0, 0)),
            pl.BlockSpec((None, WINDOW, B_KV), lambda b, n: (b, 0, 0)),
        ],
        scratch_shapes=[pltpu.VMEM((B_KV_HEADS, WINDOW, LANES), BF16)] * 3,
        out_shape=[
            jax.ShapeDtypeStruct((rows, B_Q), BF16),
            jax.ShapeDtypeStruct((nb, WINDOW, B_KV), F32),
            jax.ShapeDtypeStruct((nb, WINDOW, B_KV), F32),
        ],
        compiler_params=pltpu.CompilerParams(
            dimension_semantics=("arbitrary", "arbitrary"), vmem_limit_bytes=VMEM_LIMIT),
        name="swa_prompt",
    )(sinks, proj, proj, proj, cos, sin, k_meta, v_meta, gq, gk, bd, bdt)


def _swa_sample_kernel(sinks_ref, q_ref, k_ref, v_ref, y_ref, *, n_keys, n_new):
    q = q_ref[...]
    k = k_ref[...]
    v = v_ref[...]
    rq = q.shape[0]
    t = lax.broadcasted_iota(jnp.int32, (rq, n_keys), 0)
    r = lax.broadcasted_iota(jnp.int32, (rq, n_keys), 1)
    wj = r - N_META
    win_pos = PAST_LEN - WINDOW + wj
    nm = r - N_META - WINDOW
    ok = ((r < N_META)
          | ((wj >= 0) & (wj < WINDOW) & (win_pos >= N_META) & (wj >= t + 1))
          | ((nm >= 0) & (nm <= t) & (nm > t - WINDOW) & (nm < n_new)))
    lo_q = lax.broadcasted_iota(jnp.int32, (rq, LANES), 1) < HALF_TILE
    lo_k = lax.broadcasted_iota(jnp.int32, (n_keys, LANES), 1) < HALF_TILE
    scale = B_HD ** -0.5
    kv = []
    for g in range(B_KV_HEADS):
        tl = slice((g // 2) * LANES, (g // 2 + 1) * LANES)
        k2, _, _ = _kv_tiles(k[:, tl], g % 2, lo_k)
        _, vlo, vhi = _kv_tiles(v[:, tl], g % 2, lo_k)
        kv.append((k2, vlo, vhi))
    heads = range(B_HEADS)
    qms = []
    for h in heads:
        qt = q[:, (h // 2) * LANES:(h // 2 + 1) * LANES]
        qms.append((jnp.where(lo_q, qt, 0.0) if h % 2 == 0 else jnp.where(lo_q, 0.0, qt)).astype(BF16))
    scores = [[jnp.where(ok, _nt(qms[h], kv[h // B_GROUP][0]) * scale, -jnp.inf)] for h in heads]
    outs = _sink_softmax_pv(scores, [[kv[h // B_GROUP][1 + h % 2]] for h in heads],
                            [sinks_ref[h] for h in heads])
    for t in range(B_HEADS // 2):
        y_ref[:, t * LANES:(t + 1) * LANES] = (outs[2 * t] + outs[2 * t + 1]).astype(BF16)


def _swa_sample(q8, kk, vv, sinks, n_new):
    nb, rq, _ = q8.shape
    n_keys = kk.shape[1]
    return pl.pallas_call(
        functools.partial(_swa_sample_kernel, n_keys=n_keys, n_new=n_new),
        grid=(nb,),
        in_specs=[
            pl.BlockSpec(memory_space=pltpu.SMEM),
            pl.BlockSpec((None, rq, B_Q), lambda b: (b, 0, 0)),
            pl.BlockSpec((None, n_keys, B_KV), lambda b: (b, 0, 0)),
            pl.BlockSpec((None, n_keys, B_KV), lambda b: (b, 0, 0)),
        ],
        out_specs=pl.BlockSpec((None, rq, B_Q), lambda b: (b, 0, 0)),
        out_shape=jax.ShapeDtypeStruct((nb, rq, B_Q), BF16),
        compiler_params=pltpu.CompilerParams(
            dimension_semantics=("arbitrary",), vmem_limit_bytes=VMEM_LIMIT),
        name="swa_sample",
    )(sinks, q8, kk, vv)


def _rope_tables(pos):
    half = B_HD // 2
    inv_freq = ROPE_THETA ** (-jnp.arange(half, dtype=F32) / half)
    ang = pos.astype(F32)[:, None] * inv_freq[None, :]
    cos, sin = jnp.cos(ang), jnp.sin(ang)
    return jnp.concatenate([cos, cos, cos, cos], 1), jnp.concatenate([-sin, sin, -sin, sin], 1)


ROUTE_ROWS = 256
DMA_UNROLL = 8


def _row(ref, r):
    return ref.at[pl.ds(r, 1)]


def _dispatch_kernel(pend_ref, dest_ref, x_hbm, xs_hbm, zeros_vmem, sem_zero, sem, *, nt, nb):
    i = pl.program_id(0)
    tm = ROUTE_ROWS
    step_rows = lambda: pltpu.make_async_copy(x_hbm.at[pl.ds(0, tm)], xs_hbm.at[pl.ds(0, tm)], sem)

    @pl.when(i == 0)
    def _():
        zeros_vmem[...] = jnp.zeros_like(zeros_vmem)
        fill = lambda row0: pltpu.make_async_copy(zeros_vmem, xs_hbm.at[pl.ds(row0, MOE_ROWS)], sem_zero)
        for e in range(N_EXPERTS):
            fill(jnp.maximum(pend_ref[e] - MOE_ROWS, 0)).start()
        for e in range(N_EXPERTS):
            fill(0).wait()
        n_used = pend_ref[N_EXPERTS - 1] // MOE_ROWS

        def fill_tail(b, c):
            fill(b * MOE_ROWS).start()
            fill(0).wait()
            return c

        lax.fori_loop(n_used, nb, fill_tail, 0)

    def body(r, c):
        src = _row(x_hbm, i * tm + r)
        pltpu.make_async_copy(src, _row(xs_hbm, dest_ref[0, 2 * r]), sem).start()
        pltpu.make_async_copy(src, _row(xs_hbm, dest_ref[0, 2 * r + 1]), sem).start()
        return c

    lax.fori_loop(0, tm, body, 0, unroll=DMA_UNROLL)

    @pl.when(i > 0)
    def _():
        for _k in range(TOP_K):
            step_rows().wait()

    @pl.when(i == nt - 1)
    def _():
        for _k in range(TOP_K):
            step_rows().wait()


def _dispatch(x, dest, pend, n_slots):
    t = x.shape[0]
    assert t % ROUTE_ROWS == 0 and MOE_ROWS == ROUTE_ROWS
    nt = t // ROUTE_ROWS
    grid_spec = pltpu.PrefetchScalarGridSpec(
        num_scalar_prefetch=1,
        grid=(nt,),
        in_specs=[
            pl.BlockSpec((None, 1, TOP_K * ROUTE_ROWS), lambda i, pe: (i, 0, 0), memory_space=pltpu.SMEM),
            pl.BlockSpec(memory_space=pl.ANY),
        ],
        out_specs=pl.BlockSpec(memory_space=pl.ANY),
        scratch_shapes=[
            pltpu.VMEM((MOE_ROWS, ROW_SLABS, SLAB), F32),
            pltpu.SemaphoreType.DMA(()),
            pltpu.SemaphoreType.DMA(()),
        ],
    )
    return pl.pallas_call(
        functools.partial(_dispatch_kernel, nt=nt, nb=n_slots // MOE_ROWS),
        grid_spec=grid_spec,
        out_shape=jax.ShapeDtypeStruct((n_slots, ROW_SLABS, SLAB), F32),
        compiler_params=pltpu.CompilerParams(
            dimension_semantics=("arbitrary",), vmem_limit_bytes=VMEM_LIMIT, has_side_effects=True),
        name="moe_dispatch",
    )(pend, dest.reshape(nt, 1, TOP_K * ROUTE_ROWS), x)


def _combine_kernel(dest_ref, dest_next_ref, gate_ref, h_ref, ys_hbm, o_ref, buf, sem, *, nt):
    i = pl.program_id(0)
    tm = ROUTE_ROWS
    slot = i % 2

    def issue(dref, s):
        def body(r, c):
            for k in range(TOP_K):
                pltpu.make_async_copy(_row(ys_hbm, dref[0, TOP_K * r + k]), buf.at[s, k, pl.ds(r, 1)],
                                      sem.at[s]).start()
            return c

        lax.fori_loop(0, tm, body, 0, unroll=DMA_UNROLL)

    @pl.when(i == 0)
    def _():
        issue(dest_ref, 0)

    @pl.when(i < nt - 1)
    def _():
        issue(dest_next_ref, 1 - slot)

    for k in range(TOP_K):
        pltpu.make_async_copy(ys_hbm.at[pl.ds(0, tm)], buf.at[slot, k], sem.at[slot]).wait()
    g = gate_ref[...]
    for s in range(ROW_SLABS):
        cols = slice(s * SLAB, (s + 1) * SLAB)
        o_ref[:, cols] = h_ref[:, cols] + (g[:, 0:1] * buf[slot, 0, :, s, :] + g[:, 1:2] * buf[slot, 1, :, s, :])


def _combine(h2, gates, dest, ys):
    t = h2.shape[0]
    assert t % ROUTE_ROWS == 0
    nt = t // ROUTE_ROWS
    dest3 = dest.reshape(nt, 1, TOP_K * ROUTE_ROWS)
    idx = lambda f: pl.BlockSpec((None, 1, TOP_K * ROUTE_ROWS), f, memory_space=pltpu.SMEM)
    return pl.pallas_call(
        functools.partial(_combine_kernel, nt=nt),
        grid=(nt,),
        in_specs=[
            idx(lambda i: (i, 0, 0)),
            idx(lambda i: (jnp.minimum(i + 1, nt - 1), 0, 0)),
            pl.BlockSpec((ROUTE_ROWS, ROUTER_PAD), lambda i: (i, 0)),
            pl.BlockSpec((ROUTE_ROWS, D_MODEL), lambda i: (i, 0)),
            pl.BlockSpec(memory_space=pl.ANY),
        ],
        out_specs=pl.BlockSpec((ROUTE_ROWS, D_MODEL), lambda i: (i, 0)),
        out_shape=jax.ShapeDtypeStruct((t, D_MODEL), F32),
        scratch_shapes=[
            pltpu.VMEM((2, TOP_K, ROUTE_ROWS, ROW_SLABS, SLAB), F32),
            pltpu.SemaphoreType.DMA((2,)),
        ],
        compiler_params=pltpu.CompilerParams(
            dimension_semantics=("arbitrary",), vmem_limit_bytes=VMEM_LIMIT),
        name="moe_combine",
    )(dest3, dest3, gates, h2, ys)


def _moe_plan(route, counts):
    T = route.shape[0]
    expert = route[:, :TOP_K]
    rank = route[:, TOP_K:2 * TOP_K]
    nb = -(-(T * TOP_K) // MOE_ROWS) + N_EXPERTS
    cnt = counts[0, :N_EXPERTS].astype(jnp.int32)
    padded = (cnt + MOE_ROWS - 1) // MOE_ROWS * MOE_ROWS
    pend = jnp.cumsum(padded).astype(jnp.int32)
    dest = (pend - padded)[expert] + rank
    starts = jnp.arange(nb, dtype=jnp.int32) * MOE_ROWS
    blk_e = jnp.minimum(jnp.sum((pend[None, :] <= starts[:, None]).astype(jnp.int32), 1), N_EXPERTS - 1)
    n_used = pend[-1:] // MOE_ROWS
    return dest, pend, blk_e, n_used, nb * MOE_ROWS


def _lane_row(vals, lane0):
    return jnp.zeros((1, SMALL_PAD), F32).at[0, lane0:lane0 + vals.shape[0]].set(vals)


def kernel(x_prompt, x_sample, state_delta, state_conv, cache_swa_k, cache_swa_v, meta_tokens,
           norm1_w, w_in, conv_w, a_log, dt_bias, a_norm_w, w_up_a, q_norm_w, k_norm_w, sinks,
           w_up_b, w_o, norm2_w, w_router_group, b_router_group, w_router_expert, b_router_expert,
           w_gate, w_up, w_down):
    Bp, Sp, _ = x_prompt.shape
    Bs, Ss, _ = x_sample.shape
    n_s = Bs * Ss
    l = 0
    w = w_in[l]
    offs = [0]
    for s in (A_CONV_DIM, A_HEADS, A_HEADS, A_V, B_Q, B_KV, B_KV, D_MODEL, D_MODEL):
        offs.append(offs[-1] + s)
    seg = lambda a: w[:, offs[a]:offs[a + 1]]
    w_main = jnp.concatenate([seg(7), seg(8), seg(0), seg(3), seg(4), seg(5), seg(6)], 1).astype(BF16)
    w_small = jnp.concatenate(
        [seg(1), seg(2), jnp.zeros((D_MODEL, SMALL_PAD - 2 * A_HEADS), F32)], 1).astype(BF16)
    wa = w_up_a[l].astype(BF16)
    wb = w_up_b[l].astype(BF16)
    wo = w_o[l].astype(BF16)
    w_router = jnp.concatenate(
        [w_router_group[l], w_router_expert[l],
         jnp.zeros((D_MODEL, ROUTER_PAD - N_GROUPS - N_EXPERTS), F32)], 1).astype(BF16)
    b_router = jnp.concatenate(
        [b_router_group[l], b_router_expert[l],
         jnp.zeros((ROUTER_PAD - N_GROUPS - N_EXPERTS,), F32)])[None, :]
    g1 = norm1_w[l][None, :]
    g2 = norm2_w[l][None, :]
    alog_row = _lane_row(a_log[l], LANE_DECAY)
    dtb_row = _lane_row(dt_bias[l], LANE_DECAY)
    anw_row = a_norm_w[l][None, :]

    xp = x_prompt.reshape(Bp * Sp, D_MODEL)
    xs = x_sample.reshape(n_s, D_MODEL)
    x_small = jnp.concatenate([xs, meta_tokens], 0)
    n_small = x_small.shape[0]

    proj_p, small_p = _inproj(xp, g1, w_main, w_small, tm=1024)
    proj_s, small_s = _inproj(x_small, g1, w_main, w_small, tm=n_small)

    dm = functools.partial(_delta_mixer, conv_w=conv_w[l], alog_row=alog_row, dtb_row=dtb_row, anw_row=anw_row)
    pad_rows = lambda a, n: jnp.concatenate([a, jnp.zeros((n - a.shape[0],) + a.shape[1:], a.dtype)], 0)
    meta_proj = pad_rows(proj_s[n_s:], A_CHUNK)
    meta_small = pad_rows(small_s[n_s:], A_CHUNK)
    zero_prev = jnp.zeros((1, SUBLANES, A_CONV_DIM), F32)
    zero_state = jnp.zeros((1, A_HEADS, A_DK, A_DV), F32)
    _, s_meta = dm(meta_proj, meta_small, zero_prev, zero_state, nb=1, nc=1, C=A_CHUNK, n_valid=N_META,
                   shared_first=True)
    meta_tail = proj_s[n_s + N_META - SUBLANES:, COL_QKV:COL_QKV + A_CONV_DIM][None]
    ya_p, sdelta_p = dm(proj_p, small_p, meta_tail, s_meta, nb=Bp, nc=Sp // A_CHUNK, C=A_CHUNK,
                        n_valid=A_CHUNK, shared_first=True)
    conv_p = proj_p.reshape(Bp, Sp, PROJ_MAIN)[:, Sp - (A_CONV - 1):, COL_QKV:COL_QKV + A_CONV_DIM]
    CS = SUBLANES
    samp_proj = jnp.pad(proj_s[:n_s].reshape(Bs, Ss, PROJ_MAIN), ((0, 0), (0, CS - Ss), (0, 0)))
    samp_small = jnp.pad(small_s[:n_s].reshape(Bs, Ss, SMALL_PAD), ((0, 0), (0, CS - Ss), (0, 0)))
    samp_prev = jnp.pad(state_conv[l], ((0, 0), (SUBLANES - (A_CONV - 1), 0), (0, 0)))
    ya_s8, sdelta_s = dm(samp_proj.reshape(Bs * CS, PROJ_MAIN), samp_small.reshape(Bs * CS, SMALL_PAD),
                         samp_prev, state_delta[l], nb=Bs, nc=1, C=CS, n_valid=Ss, shared_first=False)
    ya_s = ya_s8.reshape(Bs, CS, A_V)[:, :Ss].reshape(n_s, A_V)
    qkv_s = proj_s[:n_s, COL_QKV:COL_QKV + A_CONV_DIM].reshape(Bs, Ss, A_CONV_DIM)
    conv_s = jnp.concatenate([state_conv[l], qkv_s], 1)[:, -(A_CONV - 1):]

    gq = jnp.tile(q_norm_w[l], B_HEADS)[None]
    gk = jnp.tile(k_norm_w[l], B_KV_HEADS)[None]
    bd = ((jnp.arange(B_Q) // B_HD)[:, None] == jnp.arange(LANES)[None, :]).astype(BF16)
    bdt = bd.T
    pos_small = jnp.concatenate([PAST_LEN + jnp.arange(n_s, dtype=jnp.int32) % Ss,
                                 jnp.arange(N_META, dtype=jnp.int32)])
    cos_s, sin_s = _rope_tables(pos_small)
    q_rot_s, k_rot_s = _rope_small(proj_s, cos_s, sin_s, gq, gk, bd, bdt)
    k_meta = k_rot_s[n_s:]
    v_meta = proj_s[n_s:, COL_VB:COL_VB + B_KV]
    cos_p, sin_p = _rope_tables(N_META + jnp.arange(Sp, dtype=jnp.int32))
    yb_p, k_last, v_last = _swa_prompt(proj_p, cos_p, sin_p, k_meta, v_meta, gq, gk, bd, bdt, sinks[l],
                                       Bp, Sp // WINDOW)
    cache_shape = (N_META + WINDOW, B_KV_HEADS, B_HD)
    bcast_meta = lambda a: jnp.broadcast_to(a[None], (Bp, N_META, B_KV))
    swk_p = jnp.concatenate([bcast_meta(k_meta), k_last], 1).reshape((Bp,) + cache_shape)
    swv_p = jnp.concatenate([bcast_meta(v_meta), v_last], 1).reshape((Bp,) + cache_shape)
    n_cache = N_META + WINDOW
    key_pad = -(n_cache + Ss) % SUBLANES
    zpad = jnp.zeros((Bs, key_pad, B_KV), F32)
    kk = jnp.concatenate([cache_swa_k[l].reshape(Bs, n_cache, B_KV), k_rot_s[:n_s].reshape(Bs, Ss, B_KV), zpad], 1)
    vv = jnp.concatenate([cache_swa_v[l].reshape(Bs, n_cache, B_KV),
                          proj_s[:n_s, COL_VB:COL_VB + B_KV].reshape(Bs, Ss, B_KV), zpad], 1)
    q8 = jnp.pad(q_rot_s[:n_s].reshape(Bs, Ss, B_Q), ((0, 0), (0, SUBLANES - Ss), (0, 0)))
    yb_s = _swa_sample(q8, kk, vv, sinks[l], n_new=Ss)[:, :Ss].reshape(n_s, B_Q)
    new_cache = lambda t: jnp.concatenate(
        [t[:, :N_META], t[:, n_cache + Ss - WINDOW:n_cache + Ss]], 1).reshape((Bs,) + cache_shape)
    swk_s, swv_s = new_cache(kk), new_cache(vv)

    merged_p = _merge(ya_p, yb_p, proj_p, wa, wb, tm=256)
    merged_s = _merge(ya_s, yb_s, proj_s[:n_s], wa, wb, tm=256)
    cnt0 = jnp.zeros((1, ROUTER_PAD), F32)
    h2_p, xn2_p, gt_p, rt_p, cnt_p = _outproj(merged_p, xp, wo, g2, w_router, b_router, cnt0, tm=256)
    h2_s, xn2_s, gt_s, rt_s, cnt = _outproj(merged_s, xs, wo, g2, w_router, b_router, cnt_p, tm=256)

    cat = lambda a, b: jnp.concatenate([a, b], 0)
    n_p = Bp * Sp
    dest, pend, blk_e, n_used, n_slots = _moe_plan(cat(rt_p, rt_s), cnt)
    xs_slots = _dispatch(cat(xn2_p, xn2_s), dest, pend, n_slots)
    ys = _moe_ffn(xs_slots, blk_e, n_used, w_gate[l], w_up[l], w_down[l])
    y_prompt = _combine(h2_p, gt_p, dest[:n_p], ys).reshape(Bp, Sp, D_MODEL)
    y_sample = _combine(h2_s, gt_s, dest[n_p:], ys).reshape(Bs, Ss, D_MODEL)
    return (y_prompt, y_sample, sdelta_p[None], conv_p[None], swk_p[None], swv_p[None],
            sdelta_s[None], conv_s[None], swk_s[None], swv_s[None])
```

```python
import functools

import jax
import jax.numpy as jnp
from jax import lax
from jax.experimental import pallas as pl
from jax.experimental.pallas import tpu as pltpu

F32 = jnp.float32
BF16 = jnp.bfloat16

D_MODEL = 2048
N_META = 16
A_HEADS = 8
A_DK = 128
A_DV = 128
A_CONV = 4
A_CHUNK = 64
A_QK = A_HEADS * A_DK
A_V = A_HEADS * A_DV
A_CONV_DIM = 2 * A_QK + A_V
B_HEADS = 16
B_KV_HEADS = 4
B_HD = 64
B_GROUP = B_HEADS // B_KV_HEADS
B_Q = B_HEADS * B_HD
B_KV = B_KV_HEADS * B_HD
WINDOW = 128
ROPE_THETA = 10000.0
PAST_LEN = 16384
N_GROUPS = 4
EXPERTS_PER_GROUP = 8
N_EXPERTS = N_GROUPS * EXPERTS_PER_GROUP
TOP_K = 2
D_EXPERT = 512
EPS = 1e-6

COL_GA = 0
COL_GB = COL_GA + D_MODEL
COL_QKV = COL_GB + D_MODEL
COL_Z = COL_QKV + A_CONV_DIM
COL_QB = COL_Z + A_V
COL_KB = COL_QB + B_Q
COL_VB = COL_KB + B_KV
PROJ_MAIN = COL_VB + B_KV
SMALL_PAD = 128
ROUTER_PAD = 128
LANE_BETA = 0
LANE_DECAY = A_HEADS

SUBLANES = 8
MOE_ROWS = 256
ROW_SLABS = 8
SLAB = D_MODEL // ROW_SLABS
VMEM_LIMIT = 56 * 1024 * 1024


def _row_chunk(tm):
    return 256 if tm % 256 == 0 else tm


def _bdot(a, b):
    return jnp.dot(a, b, preferred_element_type=F32)


def _store_slabs(ref3, val):
    for s in range(ROW_SLABS):
        ref3[:, s, :] = val[:, s * SLAB:(s + 1) * SLAB]


def _load_slabs(ref3):
    return jnp.concatenate([ref3[:, s, :] for s in range(ROW_SLABS)], 1)


def _inproj_kernel(x_ref, g_ref, w_ref, ws_ref, o_ref, os_ref, xn_ref, *, tm):
    rc = _row_chunk(tm)

    @pl.when(pl.program_id(1) == 0)
    def _():
        def norm_rows(sl):
            x = x_ref[sl, :]
            ms = jnp.mean(x * x, axis=-1, keepdims=True)
            xn_ref[sl, :] = ((x * lax.rsqrt(ms + EPS)) * g_ref[...]).astype(BF16)

        def body(r, c):
            norm_rows(pl.ds(pl.multiple_of(r * rc, rc), rc))
            return c

        if tm == rc:
            norm_rows(pl.ds(0, tm))
        else:
            lax.fori_loop(0, tm // rc, body, 0)
        os_ref[...] = _bdot(xn_ref[...], ws_ref[...])

    o_ref[...] = _bdot(xn_ref[...], w_ref[...])


def _inproj(x, gain, w_main, w_small, tm, tn=512):
    m = x.shape[0]
    assert m % tm == 0 and PROJ_MAIN % tn == 0
    return pl.pallas_call(
        functools.partial(_inproj_kernel, tm=tm),
        grid=(m // tm, PROJ_MAIN // tn),
        in_specs=[
            pl.BlockSpec((tm, D_MODEL), lambda i, j: (i, 0)),
            pl.BlockSpec((1, D_MODEL), lambda i, j: (0, 0)),
            pl.BlockSpec((D_MODEL, tn), lambda i, j: (0, j)),
            pl.BlockSpec((D_MODEL, SMALL_PAD), lambda i, j: (0, 0)),
        ],
        out_specs=[
            pl.BlockSpec((tm, tn), lambda i, j: (i, j)),
            pl.BlockSpec((tm, SMALL_PAD), lambda i, j: (i, 0)),
        ],
        out_shape=[
            jax.ShapeDtypeStruct((m, PROJ_MAIN), F32),
            jax.ShapeDtypeStruct((m, SMALL_PAD), F32),
        ],
        scratch_shapes=[pltpu.VMEM((tm, D_MODEL), BF16)],
        compiler_params=pltpu.CompilerParams(
            dimension_semantics=("arbitrary", "arbitrary"), vmem_limit_bytes=VMEM_LIMIT),
        name="inproj",
    )(x, gain, w_main, w_small)


def _split2(a):
    hi = a.astype(BF16)
    lo = (a - hi.astype(F32)).astype(BF16)
    return hi, lo


def _dot3(a, b):
    a_hi, a_lo = _split2(a)
    b_hi, b_lo = _split2(b)
    return _bdot(a_hi, b_hi) + (_bdot(a_hi, b_lo) + _bdot(a_lo, b_hi))


def _dot_exact_lhs(a_bf16, b):
    b1 = b.astype(BF16)
    r1 = b - b1.astype(F32)
    b2 = r1.astype(BF16)
    b3 = (r1 - b2.astype(F32)).astype(BF16)
    return _bdot(a_bf16, b1) + (_bdot(a_bf16, b2) + _bdot(a_bf16, b3))


def _unit_lower_inverses(ms, ri, ci, c):
    same = lambda n: (ri >> n) == (ci >> n)
    eye = (ri == ci).astype(F32)
    blk8 = same(3)
    n1 = [jnp.where(blk8, m, 0.0) for m in ms]
    n2 = [_dot3(a, a) for a in n1]
    ps = [eye - a for a in n1]
    ps = [p + _dot3(p, b) for p, b in zip(ps, n2)]
    n4 = [_dot3(b, b) for b in n2]
    ps = [p + _dot3(p, d) for p, d in zip(ps, n4)]
    lg = 3
    while (1 << lg) < c:
        sel = jnp.logical_and(same(lg + 1), jnp.logical_not(same(lg)))
        ts = [_dot3(p, jnp.where(sel, m, 0.0)) for p, m in zip(ps, ms)]
        ps = [p - _dot3(t, p) for p, t in zip(ps, ts)]
        lg += 1
    return ps


def _delta_kernel(qc, kc, vc, zc, qp, kp, vp, qf, kf, vf, sm_ref, cw_ref, alog_ref, dtb_ref, anw_ref, s0_ref,
                  ya_ref, s_ref, *, C, n_valid):
    first = pl.program_id(1) == 0

    @pl.when(first)
    def _():
        s_ref[...] = s0_ref[...]

    rows = lax.broadcasted_iota(jnp.int32, (C, 1), 0)
    valid = rows < n_valid
    ri = lax.broadcasted_iota(jnp.int32, (C, C), 0)
    ci = lax.broadcasted_iota(jnp.int32, (C, C), 1)
    lower = ri >= ci
    strict = ri > ci

    def conv(cur_ref, prev_ref, first_ref, col0):
        prev = jnp.where(first, first_ref[...], prev_ref[...])
        x = jnp.concatenate([prev, cur_ref[...]], 0)
        w = cw_ref[:, col0:col0 + A_QK]
        acc = x[SUBLANES:SUBLANES + C] * w[A_CONV - 1:A_CONV]
        for s in range(1, A_CONV):
            acc = acc + x[SUBLANES - s:SUBLANES - s + C] * w[A_CONV - 1 - s:A_CONV - s]
        return acc * jax.nn.sigmoid(acc)

    qx = conv(qc, qp, qf, 0)
    kx = conv(kc, kp, kf, A_QK)
    vx = conv(vc, vp, vf, 2 * A_QK)

    sm = sm_ref[...]
    beta_all = jnp.where(valid, jax.nn.sigmoid(sm), 0.0)
    g_all = jnp.where(valid, -jnp.exp(alog_ref[...]) * jax.nn.softplus(sm + dtb_ref[...]), 0.0)
    g_cum = _dot_exact_lhs(lower.astype(BF16), g_all)
    g_cum_t = g_cum.T
    e_g = jnp.exp(g_cum)
    g_last = g_cum[C - 1:C, :]
    e_rest = jnp.exp(g_last - g_cum)
    e_last = jnp.exp(g_last)

    heads = range(A_HEADS)
    hs = [slice(h * A_DK, (h + 1) * A_DK) for h in heads]
    col = lambda a, h: a[:, LANE_DECAY + h:LANE_DECAY + h + 1]
    qs = [qx[:, s] for s in hs]
    qs = [q * lax.rsqrt(jnp.sum(q * q, -1, keepdims=True) + EPS) * (A_DK ** -0.5) for q in qs]
    ks = [kx[:, s] for s in hs]
    ks = [jnp.where(valid, k * lax.rsqrt(jnp.sum(k * k, -1, keepdims=True) + EPS), 0.0) for k in ks]
    vs = [jnp.where(valid, vx[:, s], 0.0) for s in hs]
    betas = [beta_all[:, LANE_BETA + h:LANE_BETA + h + 1] for h in heads]
    gammas = [jnp.exp(jnp.where(lower, col(g_cum, h) - g_cum_t[LANE_DECAY + h:LANE_DECAY + h + 1, :], -jnp.inf))
              for h in heads]
    kbs = [k * b for k, b in zip(ks, betas)]
    a1s = [lax.dot_general(jnp.concatenate([kb, q], 0).astype(BF16), k.astype(BF16),
                           (((1,), (1,)), ((), ())), preferred_element_type=F32)
           for kb, q, k in zip(kbs, qs, ks)]
    ms = [jnp.where(strict, a1[:C] * gm, 0.0) for a1, gm in zip(a1s, gammas)]
    attns = [a1[C:] * gm for a1, gm in zip(a1s, gammas)]
    rhss = [jnp.concatenate([v * b, kb * col(e_g, h)], 1) for h, v, b, kb in zip(heads, vs, betas, kbs)]
    sols = [_dot3(t, r) for t, r in zip(_unit_lower_inverses(ms, ri, ci, C), rhss)]
    s_olds = [s_ref[0, h] for h in heads]
    wqs = [_bdot(jnp.concatenate([sol[:, A_DV:], q * col(e_g, h)], 0).astype(BF16), s.astype(BF16))
           for h, sol, q, s in zip(heads, sols, qs, s_olds)]
    v_news = [sol[:, :A_DV] - wq[:C] for sol, wq in zip(sols, wqs)]
    kg_ts = [(k * col(e_rest, h)).T for h, k in zip(heads, ks)]
    r2s = [_bdot(jnp.concatenate([at, kg], 0).astype(BF16), vn.astype(BF16))
           for at, kg, vn in zip(attns, kg_ts, v_news)]
    for h in heads:
        s_ref[0, h] = s_olds[h] * col(e_last, h) + r2s[h][C:]
        o = wqs[h][C:] + r2s[h][:C]
        on = (o * lax.rsqrt(jnp.mean(o * o, -1, keepdims=True) + EPS)) * anw_ref[...]
        zh = zc[:, hs[h]]
        ya_ref[:, hs[h]] = (on * (zh * jax.nn.sigmoid(zh))).astype(BF16)


def _delta_mixer(proj, small, first_prev, s0, conv_w, alog_row, dtb_row, anw_row, *, nb, nc, C, n_valid,
                 shared_first):
    rows = nb * nc * C
    assert proj.shape[0] == rows and C % SUBLANES == 0
    cpb = C // SUBLANES
    cq, ck, cv, cz = (COL_QKV // A_QK, COL_QKV // A_QK + 1, COL_QKV // A_QK + 2, COL_Z // A_V)
    fb = (lambda b: 0) if shared_first else (lambda b: b)
    cur = lambda col: pl.BlockSpec((C, A_QK), lambda b, c: (b * nc + c, col))
    prev = lambda col: pl.BlockSpec(
        (SUBLANES, A_QK), lambda b, c: (jnp.maximum((b * nc + c) * cpb - 1, 0), col))
    frst = lambda j: pl.BlockSpec((None, SUBLANES, A_QK), lambda b, c: (fb(b), 0, j))
    row1 = lambda n: pl.BlockSpec((1, n), lambda b, c: (0, 0))
    return pl.pallas_call(
        functools.partial(_delta_kernel, C=C, n_valid=n_valid),
        grid=(nb, nc),
        in_specs=[
            cur(cq), cur(ck), cur(cv), cur(cz),
            prev(cq), prev(ck), prev(cv),
            frst(0), frst(1), frst(2),
            pl.BlockSpec((C, SMALL_PAD), lambda b, c: (b * nc + c, 0)),
            pl.BlockSpec((A_CONV, A_CONV_DIM), lambda b, c: (0, 0)),
            row1(SMALL_PAD), row1(SMALL_PAD), row1(A_DV),
            pl.BlockSpec((1, A_HEADS, A_DK, A_DV), lambda b, c: (fb(b), 0, 0, 0)),
        ],
        out_specs=[
            pl.BlockSpec((C, A_V), lambda b, c: (b * nc + c, 0)),
            pl.BlockSpec((1, A_HEADS, A_DK, A_DV), lambda b, c: (b, 0, 0, 0)),
        ],
        out_shape=[
            jax.ShapeDtypeStruct((rows, A_V), BF16),
            jax.ShapeDtypeStruct((nb, A_HEADS, A_DK, A_DV), F32),
        ],
        compiler_params=pltpu.CompilerParams(
            dimension_semantics=("arbitrary", "arbitrary"), vmem_limit_bytes=VMEM_LIMIT),
        name="delta_mixer",
    )(proj, proj, proj, proj, proj, proj, proj, first_prev, first_prev, first_prev, small,
      conv_w, alog_row, dtb_row, anw_row, s0)


def _merge_kernel(ya_ref, yb_ref, ga_ref, gb_ref, wa_ref, wb_ref, o_ref):
    ua = _bdot(ya_ref[...].astype(BF16), wa_ref[...])
    ub = _bdot(yb_ref[...].astype(BF16), wb_ref[...])
    merged = jax.nn.sigmoid(ga_ref[...]) * ua + jax.nn.sigmoid(gb_ref[...]) * ub
    o_ref[...] = merged.astype(BF16)


def _merge(ya, yb, proj, wa, wb, tm):
    m = ya.shape[0]
    assert m % tm == 0
    return pl.pallas_call(
        _merge_kernel,
        grid=(m // tm,),
        in_specs=[
            pl.BlockSpec((tm, A_V), lambda i: (i, 0)),
            pl.BlockSpec((tm, B_Q), lambda i: (i, 0)),
            pl.BlockSpec((tm, D_MODEL), lambda i: (i, COL_GA // D_MODEL)),
            pl.BlockSpec((tm, D_MODEL), lambda i: (i, COL_GB // D_MODEL)),
            pl.BlockSpec((A_V, D_MODEL), lambda i: (0, 0)),
            pl.BlockSpec((B_Q, D_MODEL), lambda i: (0, 0)),
        ],
        out_specs=pl.BlockSpec((tm, D_MODEL), lambda i: (i, 0)),
        out_shape=jax.ShapeDtypeStruct((m, D_MODEL), BF16),
        compiler_params=pltpu.CompilerParams(
            dimension_semantics=("arbitrary",), vmem_limit_bytes=VMEM_LIMIT),
        name="merge",
    )(ya, yb, proj, proj, wa, wb)


LANE_SENTINEL = ROUTER_PAD - 1


def _lane_argmax(vals, eligible, lane):
    top = jnp.max(jnp.where(eligible, vals, -jnp.inf), -1, keepdims=True)
    idx = jnp.min(jnp.where(jnp.logical_and(eligible, vals == top), lane, LANE_SENTINEL), -1, keepdims=True)
    return top, idx


def _masked_softmax(logits, eligible):
    z = jnp.where(eligible, logits, -jnp.inf)
    e = jnp.exp(z - jnp.max(z, -1, keepdims=True))
    return e / jnp.sum(e, -1, keepdims=True)


def _outproj_kernel(m_ref, h_ref, wo_ref, g_ref, wr_ref, br_ref, cnt_in_ref,
                    h2_ref, xn_ref, gate_ref, route_ref, cnt_ref):
    tm = h_ref.shape[0]

    @pl.when(pl.program_id(0) == 0)
    def _():
        cnt_ref[...] = cnt_in_ref[...]

    h2 = h_ref[...] + _bdot(m_ref[...], wo_ref[...])
    h2_ref[...] = h2
    ms = jnp.mean(h2 * h2, axis=-1, keepdims=True)
    xn = (h2 * lax.rsqrt(ms + EPS)) * g_ref[...]
    _store_slabs(xn_ref, xn)
    lg = _bdot(xn.astype(BF16), wr_ref[...]) + br_ref[...]

    lane = lax.broadcasted_iota(jnp.int32, (tm, ROUTER_PAD), 1)
    is_g = lane < N_GROUPS
    p_grp, grp = _lane_argmax(_masked_softmax(lg, is_g), is_g, lane)
    ex = lane - N_GROUPS
    is_e = jnp.logical_and(jnp.logical_and(ex >= 0, ex < N_EXPERTS), (ex >> 3) == grp)
    pe = _masked_softmax(lg, is_e)
    p1, i1 = _lane_argmax(pe, is_e, lane)
    rest = jnp.logical_and(is_e, lane != i1)
    p2, i2 = _lane_argmax(pe, rest, lane)
    den = p1 + p2
    gate_ref[...] = jnp.where(lane == 0, (p_grp * p1) / den, jnp.where(lane == 1, (p_grp * p2) / den, 0.0))

    e1 = i1 - N_GROUPS
    e2 = i2 - N_GROUPS
    oh1 = lane == e1
    oh2 = lane == e2
    ri = lax.broadcasted_iota(jnp.int32, (tm, tm), 0)
    ci = lax.broadcasted_iota(jnp.int32, (tm, tm), 1)
    below = (ri > ci).astype(BF16)
    f1 = oh1.astype(F32)
    f2 = oh2.astype(F32)
    tot1 = jnp.sum(f1, 0, keepdims=True)
    run = cnt_ref[...]
    before1 = run + _bdot(below, oh1.astype(BF16))
    before2 = run + tot1 + _bdot(below, oh2.astype(BF16))
    rank1 = jnp.sum(f1 * before1, -1, keepdims=True).astype(jnp.int32)
    rank2 = jnp.sum(f2 * before2, -1, keepdims=True).astype(jnp.int32)
    cnt_ref[...] = run + tot1 + jnp.sum(f2, 0, keepdims=True)
    route_ref[...] = jnp.where(lane == 0, e1, jnp.where(lane == 1, e2, jnp.where(
        lane == 2, rank1, jnp.where(lane == 3, rank2, 0))))


def _outproj(merged, h, wo, gain2, w_router, b_router, counts_in, tm):
    m = h.shape[0]
    assert m % tm == 0
    row = lambda n: pl.BlockSpec((1, n), lambda i: (0, 0))
    tile = lambda n: pl.BlockSpec((tm, n), lambda i: (i, 0))
    return pl.pallas_call(
        _outproj_kernel,
        grid=(m // tm,),
        in_specs=[
            tile(D_MODEL), tile(D_MODEL),
            pl.BlockSpec((D_MODEL, D_MODEL), lambda i: (0, 0)),
            row(D_MODEL),
            pl.BlockSpec((D_MODEL, ROUTER_PAD), lambda i: (0, 0)),
            row(ROUTER_PAD), row(ROUTER_PAD),
        ],
        out_specs=[tile(D_MODEL), pl.BlockSpec((tm, ROW_SLABS, SLAB), lambda i: (i, 0, 0)),
                   tile(ROUTER_PAD), tile(ROUTER_PAD), row(ROUTER_PAD)],
        out_shape=[
            jax.ShapeDtypeStruct((m, D_MODEL), F32),
            jax.ShapeDtypeStruct((m, ROW_SLABS, SLAB), F32),
            jax.ShapeDtypeStruct((m, ROUTER_PAD), F32),
            jax.ShapeDtypeStruct((m, ROUTER_PAD), jnp.int32),
            jax.ShapeDtypeStruct((1, ROUTER_PAD), F32),
        ],
        compiler_params=pltpu.CompilerParams(
            dimension_semantics=("arbitrary",), vmem_limit_bytes=VMEM_LIMIT),
        name="outproj",
    )(merged, h, wo, gain2, w_router, b_router, counts_in)


def _moe_kernel(blk_e_ref, nused_ref, x_ref, wg_ref, wu_ref, wd_ref, o_ref, wgb, wub, wdb):
    i = pl.program_id(0)
    e = blk_e_ref[i]
    e_prev = blk_e_ref[jnp.maximum(i - 1, 0)]
    used = i < nused_ref[0]

    @pl.when(jnp.logical_and(used, jnp.logical_or(i == 0, e != e_prev)))
    def _():
        def cast_in(r, c):
            sl = pl.ds(pl.multiple_of(r * 256, 256), 256)
            wgb[sl, :] = wg_ref[sl, :].astype(BF16)
            wub[sl, :] = wu_ref[sl, :].astype(BF16)
            return c

        lax.fori_loop(0, D_MODEL // 256, cast_in, 0)

        def cast_out(r, c):
            sl = pl.ds(pl.multiple_of(r * 128, 128), 128)
            wdb[sl, :] = wd_ref[sl, :].astype(BF16)
            return c

        lax.fori_loop(0, D_EXPERT // 128, cast_out, 0)

    @pl.when(used)
    def _():
        x = _load_slabs(x_ref).astype(BF16)
        g = _bdot(x, wgb[...])
        u = _bdot(x, wub[...])
        hb = (g * jax.nn.sigmoid(g)) * u
        _store_slabs(o_ref, _bdot(hb.astype(BF16), wdb[...]))

    @pl.when(jnp.logical_not(used))
    def _():
        o_ref[...] = jnp.zeros_like(o_ref)


def _moe_ffn(xs, blk_e, n_used, w_gate, w_up, w_down):
    p = xs.shape[0]
    nb = p // MOE_ROWS
    grid_spec = pltpu.PrefetchScalarGridSpec(
        num_scalar_prefetch=2,
        grid=(nb,),
        in_specs=[
            pl.BlockSpec((MOE_ROWS, ROW_SLABS, SLAB), lambda i, be, nu: (jnp.minimum(i, nu[0] - 1), 0, 0)),
            pl.BlockSpec((None, D_MODEL, D_EXPERT), lambda i, be, nu: (be[i], 0, 0)),
            pl.BlockSpec((None, D_MODEL, D_EXPERT), lambda i, be, nu: (be[i], 0, 0)),
            pl.BlockSpec((None, D_EXPERT, D_MODEL), lambda i, be, nu: (be[i], 0, 0)),
        ],
        out_specs=pl.BlockSpec((MOE_ROWS, ROW_SLABS, SLAB), lambda i, be, nu: (i, 0, 0)),
        scratch_shapes=[
            pltpu.VMEM((D_MODEL, D_EXPERT), BF16),
            pltpu.VMEM((D_MODEL, D_EXPERT), BF16),
            pltpu.VMEM((D_EXPERT, D_MODEL), BF16),
        ],
    )
    return pl.pallas_call(
        _moe_kernel,
        grid_spec=grid_spec,
        out_shape=jax.ShapeDtypeStruct((p, ROW_SLABS, SLAB), F32),
        compiler_params=pltpu.CompilerParams(
            dimension_semantics=("arbitrary",), vmem_limit_bytes=VMEM_LIMIT),
        name="moe_ffn",
    )(blk_e, n_used, xs, w_gate, w_up, w_down)


HALF_TILE = 64
LANES = 128


def _dot_exact_rhs(a, b01):
    a1 = a.astype(BF16)
    r1 = a - a1.astype(F32)
    a2 = r1.astype(BF16)
    a3 = (r1 - a2.astype(F32)).astype(BF16)
    return _bdot(a1, b01) + (_bdot(a2, b01) + _bdot(a3, b01))


def _group_rms_rope(x, gain, cos128, sin128, bd, bdt):
    r, w = x.shape
    ssq = _dot_exact_rhs(x * x, bd)
    rs = lax.rsqrt(ssq * (1.0 / B_HD) + EPS)
    y = (x * _dot_exact_rhs(rs, bdt)) * gain
    reps = w // LANES
    cosw = jnp.concatenate([cos128] * reps, 1)
    sinw = jnp.concatenate([sin128] * reps, 1)
    lane = lax.broadcasted_iota(jnp.int32, (r, w), 1)
    swapped = jnp.where((lane & (B_HD // 2)) == 0,
                        pltpu.roll(y, w - B_HD // 2, 1), pltpu.roll(y, B_HD // 2, 1))
    return y * cosw + swapped * sinw


def _kv_tiles(tile, odd, lo):
    rolled = pltpu.roll(tile, HALF_TILE, 1)
    dup = jnp.where(lo, rolled, tile) if odd else jnp.where(lo, tile, rolled)
    return dup.astype(BF16), jnp.where(lo, dup, 0.0).astype(BF16), jnp.where(lo, 0.0, dup).astype(BF16)


def _nt(a, b):
    return lax.dot_general(a, b, (((1,), (1,)), ((), ())), preferred_element_type=F32)


def _sink_softmax_pv(score_lists, value_lists, sinks):
    ms = []
    for scores, sink in zip(score_lists, sinks):
        m = jnp.max(scores[0], -1, keepdims=True)
        for s in scores[1:]:
            m = jnp.maximum(m, jnp.max(s, -1, keepdims=True))
        ms.append(jnp.maximum(m, sink))
    es = [[jnp.exp(s - m) for s in scores] for scores, m in zip(score_lists, ms)]
    dens = []
    for e_blocks, m, sink in zip(es, ms, sinks):
        den = jnp.exp(sink - m)
        for e in e_blocks:
            den = den + jnp.sum(e, -1, keepdims=True)
        dens.append(den)
    pvs = [[_bdot(e.astype(BF16), v) for e, v in zip(e_blocks, values)]
           for e_blocks, values in zip(es, value_lists)]
    outs = []
    for pv, den in zip(pvs, dens):
        acc = pv[0]
        for x in pv[1:]:
            acc = acc + x
        outs.append(acc / den)
    return outs


def _rope_small_kernel(q_ref, k_ref, cos_ref, sin_ref, gq_ref, gk_ref, bd_ref, bdt_ref, qo_ref, ko_ref):
    cos = cos_ref[...]
    sin = sin_ref[...]
    qo_ref[...] = _group_rms_rope(q_ref[...], gq_ref[...], cos, sin, bd_ref[...], bdt_ref[...])
    ko_ref[...] = _group_rms_rope(k_ref[...], gk_ref[...], cos, sin, bd_ref[:B_KV, :], bdt_ref[:, :B_KV])


def _rope_small(proj, cos, sin, gq, gk, bd, bdt):
    m = proj.shape[0]
    full = lambda shape: pl.BlockSpec(shape, lambda i: (0, 0))
    return pl.pallas_call(
        _rope_small_kernel,
        grid=(1,),
        in_specs=[
            pl.BlockSpec((m, B_Q), lambda i: (0, COL_QB // B_Q)),
            pl.BlockSpec((m, B_KV), lambda i: (0, COL_KB // B_KV)),
            full((m, LANES)), full((m, LANES)), full((1, B_Q)), full((1, B_KV)),
            full((B_Q, LANES)), full((LANES, B_Q)),
        ],
        out_specs=[full((m, B_Q)), full((m, B_KV))],
        out_shape=[jax.ShapeDtypeStruct((m, B_Q), F32), jax.ShapeDtypeStruct((m, B_KV), F32)],
        compiler_params=pltpu.CompilerParams(
            dimension_semantics=("arbitrary",), vmem_limit_bytes=VMEM_LIMIT),
        name="rope_small",
    )(proj, proj, cos, sin, gq, gk, bd, bdt)


def _swa_prompt_kernel(sinks_ref, q_ref, k_ref, v_ref, cos_ref, sin_ref, km_ref, vm_ref, gq_ref, gk_ref,
                       bd_ref, bdt_ref, y_ref, kc_ref, vc_ref, kprev, vloprev, vhiprev):
    n = pl.program_id(1)

    @pl.when(n == 0)
    def _():
        kprev[...] = jnp.zeros_like(kprev)
        vloprev[...] = jnp.zeros_like(vloprev)
        vhiprev[...] = jnp.zeros_like(vhiprev)

    cos = cos_ref[...]
    sin = sin_ref[...]
    q = _group_rms_rope(q_ref[...], gq_ref[...], cos, sin, bd_ref[...], bdt_ref[...])
    k = _group_rms_rope(k_ref[...], gk_ref[...], cos, sin, bd_ref[:B_KV, :], bdt_ref[:, :B_KV])
    v = v_ref[...]
    kc_ref[...] = k
    vc_ref[...] = v
    km = km_ref[...]
    vm = vm_ref[...]

    qi = lax.broadcasted_iota(jnp.int32, (WINDOW, WINDOW), 0)
    kj = lax.broadcasted_iota(jnp.int32, (WINDOW, WINDOW), 1)
    cur_ok = kj <= qi
    prev_ok = jnp.logical_and(kj > qi, n > 0)
    lo = lax.broadcasted_iota(jnp.int32, (WINDOW, LANES), 1) < HALF_TILE
    lo_m = lax.broadcasted_iota(jnp.int32, (N_META, LANES), 1) < HALF_TILE
    scale = B_HD ** -0.5

    kv = []
    for g in range(B_KV_HEADS):
        tl = slice((g // 2) * LANES, (g // 2 + 1) * LANES)
        k2c, _, _ = _kv_tiles(k[:, tl], g % 2, lo)
        _, vlo_c, vhi_c = _kv_tiles(v[:, tl], g % 2, lo)
        k2m, _, _ = _kv_tiles(km[:, tl], g % 2, lo_m)
        _, vlo_m, vhi_m = _kv_tiles(vm[:, tl], g % 2, lo_m)
        kv.append(((k2m, kprev[g], k2c), (vlo_m, vloprev[g], vlo_c), (vhi_m, vhiprev[g], vhi_c)))
    heads = range(B_HEADS)
    qms = []
    for h in heads:
        qt = q[:, (h // 2) * LANES:(h // 2 + 1) * LANES]
        qms.append((jnp.where(lo, qt, 0.0) if h % 2 == 0 else jnp.where(lo, 0.0, qt)).astype(BF16))
    raw = [[_nt(qms[h], kk) * scale for kk in kv[h // B_GROUP][0]] for h in heads]
    scores = [[sm, jnp.where(prev_ok, sp, -jnp.inf), jnp.where(cur_ok, sc, -jnp.inf)] for sm, sp, sc in raw]
    outs = _sink_softmax_pv(scores, [kv[h // B_GROUP][1 + h % 2] for h in heads], [sinks_ref[h] for h in heads])
    for t in range(B_HEADS // 2):
        y_ref[:, t * LANES:(t + 1) * LANES] = (outs[2 * t] + outs[2 * t + 1]).astype(BF16)
    for g in range(B_KV_HEADS):
        kprev[g] = kv[g][0][2]
        vloprev[g] = kv[g][1][2]
        vhiprev[g] = kv[g][2][2]


def _swa_prompt(proj, cos, sin, k_meta, v_meta, gq, gk, bd, bdt, sinks, nb, nblk):
    rows = nb * nblk * WINDOW
    assert proj.shape[0] == rows
    return pl.pallas_call(
        _swa_prompt_kernel,
        grid=(nb, nblk),
        in_specs=[
            pl.BlockSpec(memory_space=pltpu.SMEM),
            pl.BlockSpec((WINDOW, B_Q), lambda b, n: (b * nblk + n, COL_QB // B_Q)),
            pl.BlockSpec((WINDOW, B_KV), lambda b, n: (b * nblk + n, COL_KB // B_KV)),
            pl.BlockSpec((WINDOW, B_KV), lambda b, n: (b * nblk + n, COL_VB // B_KV)),
            pl.BlockSpec((WINDOW, LANES), lambda b, n: (n, 0)),
            pl.BlockSpec((WINDOW, LANES), lambda b, n: (n, 0)),
            pl.BlockSpec((N_META, B_KV), lambda b, n: (0, 0)),
            pl.BlockSpec((N_META, B_KV), lambda b, n: (0, 0)),
            pl.BlockSpec((1, B_Q), lambda b, n: (0, 0)),
            pl.BlockSpec((1, B_KV), lambda b, n: (0, 0)),
            pl.BlockSpec((B_Q, LANES), lambda b, n: (0, 0)),
            pl.BlockSpec((LANES, B_Q), lambda b, n: (0, 0)),
        ],
        out_specs=[
            pl.BlockSpec((WINDOW, B_Q), lambda b, n: (b * nblk + n, 0)),
            pl.BlockSpec((None, WINDOW, B_KV), lambda b, n: (b, 0, 0)),
            pl.BlockSpec((None, WINDOW, B_KV), lambda b, n: (b, 0, 0)),
        ],
        scratch_shapes=[pltpu.VMEM((B_KV_HEADS, WINDOW, LANES), BF16)] * 3,
        out_shape=[
            jax.ShapeDtypeStruct((rows, B_Q), BF16),
            jax.ShapeDtypeStruct((nb, WINDOW, B_KV), F32),
            jax.ShapeDtypeStruct((nb, WINDOW, B_KV), F32),
        ],
        compiler_params=pltpu.CompilerParams(
            dimension_semantics=("arbitrary", "arbitrary"), vmem_limit_bytes=VMEM_LIMIT),
        name="swa_prompt",
    )(sinks, proj, proj, proj, cos, sin, k_meta, v_meta, gq, gk, bd, bdt)


def _swa_sample_kernel(sinks_ref, q_ref, k_ref, v_ref, y_ref, *, n_keys, n_new):
    q = q_ref[...]
    k = k_ref[...]
    v = v_ref[...]
    rq = q.shape[0]
    t = lax.broadcasted_iota(jnp.int32, (rq, n_keys), 0)
    r = lax.broadcasted_iota(jnp.int32, (rq, n_keys), 1)
    wj = r - N_META
    win_pos = PAST_LEN - WINDOW + wj
    nm = r - N_META - WINDOW
    ok = ((r < N_META)
          | ((wj >= 0) & (wj < WINDOW) & (win_pos >= N_META) & (wj >= t + 1))
          | ((nm >= 0) & (nm <= t) & (nm > t - WINDOW) & (nm < n_new)))
    lo_q = lax.broadcasted_iota(jnp.int32, (rq, LANES), 1) < HALF_TILE
    lo_k = lax.broadcasted_iota(jnp.int32, (n_keys, LANES), 1) < HALF_TILE
    scale = B_HD ** -0.5
    kv = []
    for g in range(B_KV_HEADS):
        tl = slice((g // 2) * LANES, (g // 2 + 1) * LANES)
        k2, _, _ = _kv_tiles(k[:, tl], g % 2, lo_k)
        _, vlo, vhi = _kv_tiles(v[:, tl], g % 2, lo_k)
        kv.append((k2, vlo, vhi))
    heads = range(B_HEADS)
    qms = []
    for h in heads:
        qt = q[:, (h // 2) * LANES:(h // 2 + 1) * LANES]
        qms.append((jnp.where(lo_q, qt, 0.0) if h % 2 == 0 else jnp.where(lo_q, 0.0, qt)).astype(BF16))
    scores = [[jnp.where(ok, _nt(qms[h], kv[h // B_GROUP][0]) * scale, -jnp.inf)] for h in heads]
    outs = _sink_softmax_pv(scores, [[kv[h // B_GROUP][1 + h % 2]] for h in heads],
                            [sinks_ref[h] for h in heads])
    for t in range(B_HEADS // 2):
        y_ref[:, t * LANES:(t + 1) * LANES] = (outs[2 * t] + outs[2 * t + 1]).astype(BF16)


def _swa_sample(q8, kk, vv, sinks, n_new):
    nb, rq, _ = q8.shape
    n_keys = kk.shape[1]
    return pl.pallas_call(
        functools.partial(_swa_sample_kernel, n_keys=n_keys, n_new=n_new),
        grid=(nb,),
        in_specs=[
            pl.BlockSpec(memory_space=pltpu.SMEM),
            pl.BlockSpec((None, rq, B_Q), lambda b: (b, 0, 0)),
            pl.BlockSpec((None, n_keys, B_KV), lambda b: (b, 0, 0)),
            pl.BlockSpec((None, n_keys, B_KV), lambda b: (b, 0, 0)),
        ],
        out_specs=pl.BlockSpec((None, rq, B_Q), lambda b: (b, 0, 0)),
        out_shape=jax.ShapeDtypeStruct((nb, rq, B_Q), BF16),
        compiler_params=pltpu.CompilerParams(
            dimension_semantics=("arbitrary",), vmem_limit_bytes=VMEM_LIMIT),
        name="swa_sample",
    )(sinks, q8, kk, vv)


def _rope_tables(pos):
    half = B_HD // 2
    inv_freq = ROPE_THETA ** (-jnp.arange(half, dtype=F32) / half)
    ang = pos.astype(F32)[:, None] * inv_freq[None, :]
    cos, sin = jnp.cos(ang), jnp.sin(ang)
    return jnp.concatenate([cos, cos, cos, cos], 1), jnp.concatenate([-sin, sin, -sin, sin], 1)


ROUTE_ROWS = 256
DMA_UNROLL = 8


def _row(ref, r):
    return ref.at[pl.ds(r, 1)]


def _dispatch_kernel(pend_ref, dest_ref, x_ref, xs_hbm, zeros_vmem, ring, sem_zero, sem, *, nt, nb):
    i = pl.program_id(0)
    tm = ROUTE_ROWS
    slot = i % 2

    def drain(s):
        for _k in range(TOP_K):
            pltpu.make_async_copy(ring.at[s], xs_hbm.at[pl.ds(0, tm)], sem.at[s]).wait()

    @pl.when(i == 0)
    def _():
        zeros_vmem[...] = jnp.zeros_like(zeros_vmem)
        fill = lambda row0: pltpu.make_async_copy(zeros_vmem, xs_hbm.at[pl.ds(row0, MOE_ROWS)], sem_zero)
        for e in range(N_EXPERTS):
            fill(jnp.maximum(pend_ref[e] - MOE_ROWS, 0)).start()
        for e in range(N_EXPERTS):
            fill(0).wait()
        n_used = pend_ref[N_EXPERTS - 1] // MOE_ROWS

        def fill_tail(b, c):
            fill(b * MOE_ROWS).start()
            fill(0).wait()
            return c

        lax.fori_loop(n_used, nb, fill_tail, 0)

    @pl.when(i >= 2)
    def _():
        drain(slot)

    ring[slot] = x_ref[...]

    def body(r, c):
        src = ring.at[slot, pl.ds(r, 1)]
        for k in range(TOP_K):
            pltpu.make_async_copy(src, _row(xs_hbm, dest_ref[0, TOP_K * r + k]), sem.at[slot]).start()
        return c

    lax.fori_loop(0, tm, body, 0, unroll=DMA_UNROLL)

    @pl.when(i == nt - 1)
    def _():
        if nt >= 2:
            drain(1 - slot)
        drain(slot)


def _dispatch(x, dest, pend, n_slots):
    t = x.shape[0]
    assert t % ROUTE_ROWS == 0 and MOE_ROWS == ROUTE_ROWS
    nt = t // ROUTE_ROWS
    slab_rows = (ROUTE_ROWS, ROW_SLABS, SLAB)
    grid_spec = pltpu.PrefetchScalarGridSpec(
        num_scalar_prefetch=1,
        grid=(nt,),
        in_specs=[
            pl.BlockSpec((None, 1, TOP_K * ROUTE_ROWS), lambda i, pe: (i, 0, 0), memory_space=pltpu.SMEM),
            pl.BlockSpec(slab_rows, lambda i, pe: (i, 0, 0)),
        ],
        out_specs=pl.BlockSpec(memory_space=pl.ANY),
        scratch_shapes=[
            pltpu.VMEM(slab_rows, F32),
            pltpu.VMEM((2,) + slab_rows, F32),
            pltpu.SemaphoreType.DMA(()),
            pltpu.SemaphoreType.DMA((2,)),
        ],
    )
    return pl.pallas_call(
        functools.partial(_dispatch_kernel, nt=nt, nb=n_slots // MOE_ROWS),
        grid_spec=grid_spec,
        out_shape=jax.ShapeDtypeStruct((n_slots, ROW_SLABS, SLAB), F32),
        compiler_params=pltpu.CompilerParams(
            dimension_semantics=("arbitrary",), vmem_limit_bytes=VMEM_LIMIT, has_side_effects=True),
        name="moe_dispatch",
    )(pend, dest.reshape(nt, 1, TOP_K * ROUTE_ROWS), x)


def _combine_kernel(dest_ref, dest_next_ref, gate_ref, h_ref, ys_hbm, o_ref, buf, sem, *, nt):
    i = pl.program_id(0)
    tm = ROUTE_ROWS
    slot = i % 2

    def issue(dref, s):
        def body(r, c):
            for k in range(TOP_K):
                pltpu.make_async_copy(_row(ys_hbm, dref[0, TOP_K * r + k]), buf.at[s, k, pl.ds(r, 1)],
                                      sem.at[s]).start()
            return c

        lax.fori_loop(0, tm, body, 0, unroll=DMA_UNROLL)

    @pl.when(i == 0)
    def _():
        issue(dest_ref, 0)

    @pl.when(i < nt - 1)
    def _():
        issue(dest_next_ref, 1 - slot)

    for k in range(TOP_K):
        pltpu.make_async_copy(ys_hbm.at[pl.ds(0, tm)], buf.at[slot, k], sem.at[slot]).wait()
    g = gate_ref[...]
    for s in range(ROW_SLABS):
        cols = slice(s * SLAB, (s + 1) * SLAB)
        o_ref[:, cols] = h_ref[:, cols] + (g[:, 0:1] * buf[slot, 0, :, s, :] + g[:, 1:2] * buf[slot, 1, :, s, :])


def _combine(h2, gates, dest, ys):
    t = h2.shape[0]
    assert t % ROUTE_ROWS == 0
    nt = t // ROUTE_ROWS
    dest3 = dest.reshape(nt, 1, TOP_K * ROUTE_ROWS)
    idx = lambda f: pl.BlockSpec((None, 1, TOP_K * ROUTE_ROWS), f, memory_space=pltpu.SMEM)
    return pl.pallas_call(
        functools.partial(_combine_kernel, nt=nt),
        grid=(nt,),
        in_specs=[
            idx(lambda i: (i, 0, 0)),
            idx(lambda i: (jnp.minimum(i + 1, nt - 1), 0, 0)),
            pl.BlockSpec((ROUTE_ROWS, ROUTER_PAD), lambda i: (i, 0)),
            pl.BlockSpec((ROUTE_ROWS, D_MODEL), lambda i: (i, 0)),
            pl.BlockSpec(memory_space=pl.ANY),
        ],
        out_specs=pl.BlockSpec((ROUTE_ROWS, D_MODEL), lambda i: (i, 0)),
        out_shape=jax.ShapeDtypeStruct((t, D_MODEL), F32),
        scratch_shapes=[
            pltpu.VMEM((2, TOP_K, ROUTE_ROWS, ROW_SLABS, SLAB), F32),
            pltpu.SemaphoreType.DMA((2,)),
        ],
        compiler_params=pltpu.CompilerParams(
            dimension_semantics=("arbitrary",), vmem_limit_bytes=VMEM_LIMIT),
        name="moe_combine",
    )(dest3, dest3, gates, h2, ys)


def _moe_plan(route, counts):
    T = route.shape[0]
    expert = route[:, :TOP_K]
    rank = route[:, TOP_K:2 * TOP_K]
    nb = -(-(T * TOP_K) // MOE_ROWS) + N_EXPERTS
    cnt = counts[0, :N_EXPERTS].astype(jnp.int32)
    padded = (cnt + MOE_ROWS - 1) // MOE_ROWS * MOE_ROWS
    pend = jnp.cumsum(padded).astype(jnp.int32)
    dest = (pend - padded)[expert] + rank
    starts = jnp.arange(nb, dtype=jnp.int32) * MOE_ROWS
    blk_e = jnp.minimum(jnp.sum((pend[None, :] <= starts[:, None]).astype(jnp.int32), 1), N_EXPERTS - 1)
    n_used = pend[-1:] // MOE_ROWS
    return dest, pend, blk_e, n_used, nb * MOE_ROWS


def _lane_row(vals, lane0):
    return jnp.zeros((1, SMALL_PAD), F32).at[0, lane0:lane0 + vals.shape[0]].set(vals)


def kernel(x_prompt, x_sample, state_delta, state_conv, cache_swa_k, cache_swa_v, meta_tokens,
           norm1_w, w_in, conv_w, a_log, dt_bias, a_norm_w, w_up_a, q_norm_w, k_norm_w, sinks,
           w_up_b, w_o, norm2_w, w_router_group, b_router_group, w_router_expert, b_router_expert,
           w_gate, w_up, w_down):
    Bp, Sp, _ = x_prompt.shape
    Bs, Ss, _ = x_sample.shape
    n_s = Bs * Ss
    l = 0
    w = w_in[l]
    offs = [0]
    for s in (A_CONV_DIM, A_HEADS, A_HEADS, A_V, B_Q, B_KV, B_KV, D_MODEL, D_MODEL):
        offs.append(offs[-1] + s)
    seg = lambda a: w[:, offs[a]:offs[a + 1]]
    w_main = jnp.concatenate([seg(7), seg(8), seg(0), seg(3), seg(4), seg(5), seg(6)], 1).astype(BF16)
    w_small = jnp.concatenate(
        [seg(1), seg(2), jnp.zeros((D_MODEL, SMALL_PAD - 2 * A_HEADS), F32)], 1).astype(BF16)
    wa = w_up_a[l].astype(BF16)
    wb = w_up_b[l].astype(BF16)
    wo = w_o[l].astype(BF16)
    w_router = jnp.concatenate(
        [w_router_group[l], w_router_expert[l],
         jnp.zeros((D_MODEL, ROUTER_PAD - N_GROUPS - N_EXPERTS), F32)], 1).astype(BF16)
    b_router = jnp.concatenate(
        [b_router_group[l], b_router_expert[l],
         jnp.zeros((ROUTER_PAD - N_GROUPS - N_EXPERTS,), F32)])[None, :]
    g1 = norm1_w[l][None, :]
    g2 = norm2_w[l][None, :]
    alog_row = _lane_row(a_log[l], LANE_DECAY)
    dtb_row = _lane_row(dt_bias[l], LANE_DECAY)
    anw_row = a_norm_w[l][None, :]

    xp = x_prompt.reshape(Bp * Sp, D_MODEL)
    xs = x_sample.reshape(n_s, D_MODEL)
    x_small = jnp.concatenate([xs, meta_tokens], 0)
    n_small = x_small.shape[0]

    proj_p, small_p = _inproj(xp, g1, w_main, w_small, tm=1024)
    proj_s, small_s = _inproj(x_small, g1, w_main, w_small, tm=n_small)

    dm = functools.partial(_delta_mixer, conv_w=conv_w[l], alog_row=alog_row, dtb_row=dtb_row, anw_row=anw_row)
    pad_rows = lambda a, n: jnp.concatenate([a, jnp.zeros((n - a.shape[0],) + a.shape[1:], a.dtype)], 0)
    meta_proj = pad_rows(proj_s[n_s:], A_CHUNK)
    meta_small = pad_rows(small_s[n_s:], A_CHUNK)
    zero_prev = jnp.zeros((1, SUBLANES, A_CONV_DIM), F32)
    zero_state = jnp.zeros((1, A_HEADS, A_DK, A_DV), F32)
    _, s_meta = dm(meta_proj, meta_small, zero_prev, zero_state, nb=1, nc=1, C=A_CHUNK, n_valid=N_META,
                   shared_first=True)
    meta_tail = proj_s[n_s + N_META - SUBLANES:, COL_QKV:COL_QKV + A_CONV_DIM][None]
    ya_p, sdelta_p = dm(proj_p, small_p, meta_tail, s_meta, nb=Bp, nc=Sp // A_CHUNK, C=A_CHUNK,
                        n_valid=A_CHUNK, shared_first=True)
    conv_p = proj_p.reshape(Bp, Sp, PROJ_MAIN)[:, Sp - (A_CONV - 1):, COL_QKV:COL_QKV + A_CONV_DIM]
    CS = SUBLANES
    samp_proj = jnp.pad(proj_s[:n_s].reshape(Bs, Ss, PROJ_MAIN), ((0, 0), (0, CS - Ss), (0, 0)))
    samp_small = jnp.pad(small_s[:n_s].reshape(Bs, Ss, SMALL_PAD), ((0, 0), (0, CS - Ss), (0, 0)))
    samp_prev = jnp.pad(state_conv[l], ((0, 0), (SUBLANES - (A_CONV - 1), 0), (0, 0)))
    ya_s8, sdelta_s = dm(samp_proj.reshape(Bs * CS, PROJ_MAIN), samp_small.reshape(Bs * CS, SMALL_PAD),
                         samp_prev, state_delta[l], nb=Bs, nc=1, C=CS, n_valid=Ss, shared_first=False)
    ya_s = ya_s8.reshape(Bs, CS, A_V)[:, :Ss].reshape(n_s, A_V)
    qkv_s = proj_s[:n_s, COL_QKV:COL_QKV + A_CONV_DIM].reshape(Bs, Ss, A_CONV_DIM)
    conv_s = jnp.concatenate([state_conv[l], qkv_s], 1)[:, -(A_CONV - 1):]

    gq = jnp.tile(q_norm_w[l], B_HEADS)[None]
    gk = jnp.tile(k_norm_w[l], B_KV_HEADS)[None]
    bd = ((jnp.arange(B_Q) // B_HD)[:, None] == jnp.arange(LANES)[None, :]).astype(BF16)
    bdt = bd.T
    pos_small = jnp.concatenate([PAST_LEN + jnp.arange(n_s, dtype=jnp.int32) % Ss,
                                 jnp.arange(N_META, dtype=jnp.int32)])
    cos_s, sin_s = _rope_tables(pos_small)
    q_rot_s, k_rot_s = _rope_small(proj_s, cos_s, sin_s, gq, gk, bd, bdt)
    k_meta = k_rot_s[n_s:]
    v_meta = proj_s[n_s:, COL_VB:COL_VB + B_KV]
    cos_p, sin_p = _rope_tables(N_META + jnp.arange(Sp, dtype=jnp.int32))
    yb_p, k_last, v_last = _swa_prompt(proj_p, cos_p, sin_p, k_meta, v_meta, gq, gk, bd, bdt, sinks[l],
                                       Bp, Sp // WINDOW)
    cache_shape = (N_META + WINDOW, B_KV_HEADS, B_HD)
    bcast_meta = lambda a: jnp.broadcast_to(a[None], (Bp, N_META, B_KV))
    swk_p = jnp.concatenate([bcast_meta(k_meta), k_last], 1).reshape((Bp,) + cache_shape)
    swv_p = jnp.concatenate([bcast_meta(v_meta), v_last], 1).reshape((Bp,) + cache_shape)
    n_cache = N_META + WINDOW
    key_pad = -(n_cache + Ss) % SUBLANES
    zpad = jnp.zeros((Bs, key_pad, B_KV), F32)
    kk = jnp.concatenate([cache_swa_k[l].reshape(Bs, n_cache, B_KV), k_rot_s[:n_s].reshape(Bs, Ss, B_KV), zpad], 1)
    vv = jnp.concatenate([cache_swa_v[l].reshape(Bs, n_cache, B_KV),
                          proj_s[:n_s, COL_VB:COL_VB + B_KV].reshape(Bs, Ss, B_KV), zpad], 1)
    q8 = jnp.pad(q_rot_s[:n_s].reshape(Bs, Ss, B_Q), ((0, 0), (0, SUBLANES - Ss), (0, 0)))
    yb_s = _swa_sample(q8, kk, vv, sinks[l], n_new=Ss)[:, :Ss].reshape(n_s, B_Q)
    new_cache = lambda t: jnp.concatenate(
        [t[:, :N_META], t[:, n_cache + Ss - WINDOW:n_cache + Ss]], 1).reshape((Bs,) + cache_shape)
    swk_s, swv_s = new_cache(kk), new_cache(vv)

    merged_p = _merge(ya_p, yb_p, proj_p, wa, wb, tm=256)
    merged_s = _merge(ya_s, yb_s, proj_s[:n_s], wa, wb, tm=256)
    cnt0 = jnp.zeros((1, ROUTER_PAD), F32)
    h2_p, xn2_p, gt_p, rt_p, cnt_p = _outproj(merged_p, xp, wo, g2, w_router, b_router, cnt0, tm=256)
    h2_s, xn2_s, gt_s, rt_s, cnt = _outproj(merged_s, xs, wo, g2, w_router, b_router, cnt_p, tm=256)

    cat = lambda a, b: jnp.concatenate([a, b], 0)
    n_p = Bp * Sp
    dest, pend, blk_e, n_used, n_slots = _moe_plan(cat(rt_p, rt_s), cnt)
    xs_slots = _dispatch(cat(xn2_p, xn2_s), dest, pend, n_slots)
    ys = _moe_ffn(xs_slots, blk_e, n_used, w_gate[l], w_up[l], w_down[l])
    y_prompt = _combine(h2_p, gt_p, dest[:n_p], ys).reshape(Bp, Sp, D_MODEL)
    y_sample = _combine(h2_s, gt_s, dest[n_p:], ys).reshape(Bs, Ss, D_MODEL)
    return (y_prompt, y_sample, sdelta_p[None], conv_p[None], swk_p[None], swv_p[None],
            sdelta_s[None], conv_s[None], swk_s[None], swv_s[None])
```

```python
import functools

import jax
import jax.numpy as jnp
from jax import lax
from jax.experimental import pallas as pl
from jax.experimental.pallas import tpu as pltpu

F32 = jnp.float32
BF16 = jnp.bfloat16

D_MODEL = 2048
N_META = 16
A_HEADS = 8
A_DK = 128
A_DV = 128
A_CONV = 4
A_CHUNK = 64
A_QK = A_HEADS * A_DK
A_V = A_HEADS * A_DV
A_CONV_DIM = 2 * A_QK + A_V
B_HEADS = 16
B_KV_HEADS = 4
B_HD = 64
B_GROUP = B_HEADS // B_KV_HEADS
B_Q = B_HEADS * B_HD
B_KV = B_KV_HEADS * B_HD
WINDOW = 128
ROPE_THETA = 10000.0
PAST_LEN = 16384
N_GROUPS = 4
EXPERTS_PER_GROUP = 8
N_EXPERTS = N_GROUPS * EXPERTS_PER_GROUP
TOP_K = 2
D_EXPERT = 512
EPS = 1e-6

COL_GA = 0
COL_GB = COL_GA + D_MODEL
COL_QKV = COL_GB + D_MODEL
COL_Z = COL_QKV + A_CONV_DIM
COL_QB = COL_Z + A_V
COL_KB = COL_QB + B_Q
COL_VB = COL_KB + B_KV
PROJ_MAIN = COL_VB + B_KV
SMALL_PAD = 128
ROUTER_PAD = 128
LANE_BETA = 0
LANE_DECAY = A_HEADS

SUBLANES = 8
MOE_ROWS = 256
ROW_SLABS = 8
SLAB = D_MODEL // ROW_SLABS
VMEM_LIMIT = 56 * 1024 * 1024


def _row_chunk(tm):
    return 256 if tm % 256 == 0 else tm


def _bdot(a, b):
    return jnp.dot(a, b, preferred_element_type=F32)


def _store_slabs(ref3, val):
    for s in range(ROW_SLABS):
        ref3[:, s, :] = val[:, s * SLAB:(s + 1) * SLAB]


def _load_slabs(ref3):
    return jnp.concatenate([ref3[:, s, :] for s in range(ROW_SLABS)], 1)


def _inproj_kernel(x_ref, g_ref, w_ref, ws_ref, o_ref, os_ref, xn_ref, *, tm):
    rc = _row_chunk(tm)

    @pl.when(pl.program_id(1) == 0)
    def _():
        def norm_rows(sl):
            x = x_ref[sl, :]
            ms = jnp.mean(x * x, axis=-1, keepdims=True)
            xn_ref[sl, :] = ((x * lax.rsqrt(ms + EPS)) * g_ref[...]).astype(BF16)

        def body(r, c):
            norm_rows(pl.ds(pl.multiple_of(r * rc, rc), rc))
            return c

        if tm == rc:
            norm_rows(pl.ds(0, tm))
        else:
            lax.fori_loop(0, tm // rc, body, 0)
        os_ref[...] = _bdot(xn_ref[...], ws_ref[...])

    o_ref[...] = _bdot(xn_ref[...], w_ref[...])


def _inproj(x, gain, w_main, w_small, tm, tn=512):
    m = x.shape[0]
    assert m % tm == 0 and PROJ_MAIN % tn == 0
    return pl.pallas_call(
        functools.partial(_inproj_kernel, tm=tm),
        grid=(m // tm, PROJ_MAIN // tn),
        in_specs=[
            pl.BlockSpec((tm, D_MODEL), lambda i, j: (i, 0)),
            pl.BlockSpec((1, D_MODEL), lambda i, j: (0, 0)),
            pl.BlockSpec((D_MODEL, tn), lambda i, j: (0, j)),
            pl.BlockSpec((D_MODEL, SMALL_PAD), lambda i, j: (0, 0)),
        ],
        out_specs=[
            pl.BlockSpec((tm, tn), lambda i, j: (i, j)),
            pl.BlockSpec((tm, SMALL_PAD), lambda i, j: (i, 0)),
        ],
        out_shape=[
            jax.ShapeDtypeStruct((m, PROJ_MAIN), F32),
            jax.ShapeDtypeStruct((m, SMALL_PAD), F32),
        ],
        scratch_shapes=[pltpu.VMEM((tm, D_MODEL), BF16)],
        compiler_params=pltpu.CompilerParams(
            dimension_semantics=("arbitrary", "arbitrary"), vmem_limit_bytes=VMEM_LIMIT),
        name="inproj",
    )(x, gain, w_main, w_small)


def _dot_exact_lhs(a_bf16, b):
    b1 = b.astype(BF16)
    r1 = b - b1.astype(F32)
    b2 = r1.astype(BF16)
    b3 = (r1 - b2.astype(F32)).astype(BF16)
    return _bdot(a_bf16, b1) + (_bdot(a_bf16, b2) + _bdot(a_bf16, b3))


def _unit_lower_inverse_offsets(ms, ri, ci, c):
    same = lambda n: (ri >> n) == (ci >> n)
    d = lambda a, b: _bdot(a.astype(BF16), b.astype(BF16))
    blk8 = same(3)
    n1 = [jnp.where(blk8, m, 0.0) for m in ms]
    n2 = [d(a, a) for a in n1]
    n3 = [d(a, b) for a, b in zip(n1, n2)]
    n4 = [d(b, b) for b in n2]
    qs = [(b - a) - t for a, b, t in zip(n1, n2, n3)]
    qs = [(q + f) + d(q, f) for q, f in zip(qs, n4)]
    lg = 3
    while (1 << lg) < c:
        sel = jnp.logical_and(same(lg + 1), jnp.logical_not(same(lg)))
        offs = [jnp.where(sel, m, 0.0) for m in ms]
        ts = [o + d(q, o) for q, o in zip(qs, offs)]
        qs = [(q - t) - d(t, q) for q, t in zip(qs, ts)]
        lg += 1
    return qs


def _delta_kernel(qc, kc, vc, zc, qp, kp, vp, qf, kf, vf, sm_ref, cw_ref, alog_ref, dtb_ref, anw_ref, s0_ref,
                  ya_ref, s_ref, *, C, n_valid):
    first = pl.program_id(1) == 0

    @pl.when(first)
    def _():
        s_ref[...] = s0_ref[...]

    rows = lax.broadcasted_iota(jnp.int32, (C, 1), 0)
    valid = rows < n_valid
    ri = lax.broadcasted_iota(jnp.int32, (C, C), 0)
    ci = lax.broadcasted_iota(jnp.int32, (C, C), 1)
    lower = ri >= ci
    strict = ri > ci

    def conv(cur_ref, prev_ref, first_ref, col0):
        prev = jnp.where(first, first_ref[...], prev_ref[...])
        x = jnp.concatenate([prev, cur_ref[...]], 0)
        w = cw_ref[:, col0:col0 + A_QK]
        acc = x[SUBLANES:SUBLANES + C] * w[A_CONV - 1:A_CONV]
        for s in range(1, A_CONV):
            acc = acc + x[SUBLANES - s:SUBLANES - s + C] * w[A_CONV - 1 - s:A_CONV - s]
        return acc * jax.nn.sigmoid(acc)

    qx = conv(qc, qp, qf, 0)
    kx = conv(kc, kp, kf, A_QK)
    vx = conv(vc, vp, vf, 2 * A_QK)

    sm = sm_ref[...]
    beta_all = jnp.where(valid, jax.nn.sigmoid(sm), 0.0)
    g_all = jnp.where(valid, -jnp.exp(alog_ref[...]) * jax.nn.softplus(sm + dtb_ref[...]), 0.0)
    g_cum = _dot_exact_lhs(lower.astype(BF16), g_all)
    g_cum_t = g_cum.T
    e_g = jnp.exp(g_cum)
    g_last = g_cum[C - 1:C, :]
    e_rest = jnp.exp(g_last - g_cum)
    e_last = jnp.exp(g_last)

    heads = range(A_HEADS)
    hs = [slice(h * A_DK, (h + 1) * A_DK) for h in heads]
    col = lambda a, h: a[:, LANE_DECAY + h:LANE_DECAY + h + 1]
    qs = [qx[:, s] for s in hs]
    qs = [q * lax.rsqrt(jnp.sum(q * q, -1, keepdims=True) + EPS) * (A_DK ** -0.5) for q in qs]
    ks = [kx[:, s] for s in hs]
    ks = [jnp.where(valid, k * lax.rsqrt(jnp.sum(k * k, -1, keepdims=True) + EPS), 0.0) for k in ks]
    vs = [jnp.where(valid, vx[:, s], 0.0) for s in hs]
    betas = [beta_all[:, LANE_BETA + h:LANE_BETA + h + 1] for h in heads]
    gammas = [jnp.exp(jnp.where(lower, col(g_cum, h) - g_cum_t[LANE_DECAY + h:LANE_DECAY + h + 1, :], -jnp.inf))
              for h in heads]
    kbs = [k * b for k, b in zip(ks, betas)]
    a1s = [lax.dot_general(jnp.concatenate([kb, q], 0).astype(BF16), k.astype(BF16),
                           (((1,), (1,)), ((), ())), preferred_element_type=F32)
           for kb, q, k in zip(kbs, qs, ks)]
    ms = [jnp.where(strict, a1[:C] * gm, 0.0) for a1, gm in zip(a1s, gammas)]
    attns = [a1[C:] * gm for a1, gm in zip(a1s, gammas)]
    rhss = [jnp.concatenate([v * b, kb * col(e_g, h)], 1) for h, v, b, kb in zip(heads, vs, betas, kbs)]
    sols = [r + _bdot(q.astype(BF16), r.astype(BF16))
            for q, r in zip(_unit_lower_inverse_offsets(ms, ri, ci, C), rhss)]
    s_olds = [s_ref[0, h] for h in heads]
    wqs = [_bdot(jnp.concatenate([sol[:, A_DV:], q * col(e_g, h)], 0).astype(BF16), s.astype(BF16))
           for h, sol, q, s in zip(heads, sols, qs, s_olds)]
    v_news = [sol[:, :A_DV] - wq[:C] for sol, wq in zip(sols, wqs)]
    kg_ts = [(k * col(e_rest, h)).T for h, k in zip(heads, ks)]
    r2s = [_bdot(jnp.concatenate([at, kg], 0).astype(BF16), vn.astype(BF16))
           for at, kg, vn in zip(attns, kg_ts, v_news)]
    for h in heads:
        s_ref[0, h] = s_olds[h] * col(e_last, h) + r2s[h][C:]
        o = wqs[h][C:] + r2s[h][:C]
        on = (o * lax.rsqrt(jnp.mean(o * o, -1, keepdims=True) + EPS)) * anw_ref[...]
        zh = zc[:, hs[h]]
        ya_ref[:, hs[h]] = (on * (zh * jax.nn.sigmoid(zh))).astype(BF16)


def _delta_mixer(proj, small, first_prev, s0, conv_w, alog_row, dtb_row, anw_row, *, nb, nc, C, n_valid,
                 shared_first):
    rows = nb * nc * C
    assert proj.shape[0] == rows and C % SUBLANES == 0
    cpb = C // SUBLANES
    cq, ck, cv, cz = (COL_QKV // A_QK, COL_QKV // A_QK + 1, COL_QKV // A_QK + 2, COL_Z // A_V)
    fb = (lambda b: 0) if shared_first else (lambda b: b)
    cur = lambda col: pl.BlockSpec((C, A_QK), lambda b, c: (b * nc + c, col))
    prev = lambda col: pl.BlockSpec(
        (SUBLANES, A_QK), lambda b, c: (jnp.maximum((b * nc + c) * cpb - 1, 0), col))
    frst = lambda j: pl.BlockSpec((None, SUBLANES, A_QK), lambda b, c: (fb(b), 0, j))
    row1 = lambda n: pl.BlockSpec((1, n), lambda b, c: (0, 0))
    return pl.pallas_call(
        functools.partial(_delta_kernel, C=C, n_valid=n_valid),
        grid=(nb, nc),
        in_specs=[
            cur(cq), cur(ck), cur(cv), cur(cz),
            prev(cq), prev(ck), prev(cv),
            frst(0), frst(1), frst(2),
            pl.BlockSpec((C, SMALL_PAD), lambda b, c: (b * nc + c, 0)),
            pl.BlockSpec((A_CONV, A_CONV_DIM), lambda b, c: (0, 0)),
            row1(SMALL_PAD), row1(SMALL_PAD), row1(A_DV),
            pl.BlockSpec((1, A_HEADS, A_DK, A_DV), lambda b, c: (fb(b), 0, 0, 0)),
        ],
        out_specs=[
            pl.BlockSpec((C, A_V), lambda b, c: (b * nc + c, 0)),
            pl.BlockSpec((1, A_HEADS, A_DK, A_DV), lambda b, c: (b, 0, 0, 0)),
        ],
        out_shape=[
            jax.ShapeDtypeStruct((rows, A_V), BF16),
            jax.ShapeDtypeStruct((nb, A_HEADS, A_DK, A_DV), F32),
        ],
        compiler_params=pltpu.CompilerParams(
            dimension_semantics=("arbitrary", "arbitrary"), vmem_limit_bytes=VMEM_LIMIT),
        name="delta_mixer",
    )(proj, proj, proj, proj, proj, proj, proj, first_prev, first_prev, first_prev, small,
      conv_w, alog_row, dtb_row, anw_row, s0)


def _merge_kernel(ya_ref, yb_ref, ga_ref, gb_ref, wa_ref, wb_ref, o_ref):
    ua = _bdot(ya_ref[...].astype(BF16), wa_ref[...])
    ub = _bdot(yb_ref[...].astype(BF16), wb_ref[...])
    merged = jax.nn.sigmoid(ga_ref[...]) * ua + jax.nn.sigmoid(gb_ref[...]) * ub
    o_ref[...] = merged.astype(BF16)


def _merge(ya, yb, proj, wa, wb, tm):
    m = ya.shape[0]
    assert m % tm == 0
    return pl.pallas_call(
        _merge_kernel,
        grid=(m // tm,),
        in_specs=[
            pl.BlockSpec((tm, A_V), lambda i: (i, 0)),
            pl.BlockSpec((tm, B_Q), lambda i: (i, 0)),
            pl.BlockSpec((tm, D_MODEL), lambda i: (i, COL_GA // D_MODEL)),
            pl.BlockSpec((tm, D_MODEL), lambda i: (i, COL_GB // D_MODEL)),
            pl.BlockSpec((A_V, D_MODEL), lambda i: (0, 0)),
            pl.BlockSpec((B_Q, D_MODEL), lambda i: (0, 0)),
        ],
        out_specs=pl.BlockSpec((tm, D_MODEL), lambda i: (i, 0)),
        out_shape=jax.ShapeDtypeStruct((m, D_MODEL), BF16),
        compiler_params=pltpu.CompilerParams(
            dimension_semantics=("arbitrary",), vmem_limit_bytes=VMEM_LIMIT),
        name="merge",
    )(ya, yb, proj, proj, wa, wb)


LANE_SENTINEL = ROUTER_PAD - 1


def _lane_argmax(vals, eligible, lane):
    top = jnp.max(jnp.where(eligible, vals, -jnp.inf), -1, keepdims=True)
    idx = jnp.min(jnp.where(jnp.logical_and(eligible, vals == top), lane, LANE_SENTINEL), -1, keepdims=True)
    return top, idx


def _masked_softmax(logits, eligible):
    z = jnp.where(eligible, logits, -jnp.inf)
    e = jnp.exp(z - jnp.max(z, -1, keepdims=True))
    return e / jnp.sum(e, -1, keepdims=True)


def _outproj_kernel(m_ref, h_ref, wo_ref, g_ref, wr_ref, br_ref, cnt_in_ref,
                    h2_ref, xn_ref, gate_ref, route_ref, cnt_ref):
    tm = h_ref.shape[0]

    @pl.when(pl.program_id(0) == 0)
    def _():
        cnt_ref[...] = cnt_in_ref[...]

    h2 = h_ref[...] + _bdot(m_ref[...], wo_ref[...])
    h2_ref[...] = h2
    ms = jnp.mean(h2 * h2, axis=-1, keepdims=True)
    xn = (h2 * lax.rsqrt(ms + EPS)) * g_ref[...]
    _store_slabs(xn_ref, xn)
    lg = _bdot(xn.astype(BF16), wr_ref[...]) + br_ref[...]

    lane = lax.broadcasted_iota(jnp.int32, (tm, ROUTER_PAD), 1)
    is_g = lane < N_GROUPS
    p_grp, grp = _lane_argmax(_masked_softmax(lg, is_g), is_g, lane)
    ex = lane - N_GROUPS
    is_e = jnp.logical_and(jnp.logical_and(ex >= 0, ex < N_EXPERTS), (ex >> 3) == grp)
    pe = _masked_softmax(lg, is_e)
    p1, i1 = _lane_argmax(pe, is_e, lane)
    rest = jnp.logical_and(is_e, lane != i1)
    p2, i2 = _lane_argmax(pe, rest, lane)
    den = p1 + p2
    gate_ref[...] = jnp.where(lane == 0, (p_grp * p1) / den, jnp.where(lane == 1, (p_grp * p2) / den, 0.0))

    e1 = i1 - N_GROUPS
    e2 = i2 - N_GROUPS
    oh1 = lane == e1
    oh2 = lane == e2
    ri = lax.broadcasted_iota(jnp.int32, (tm, tm), 0)
    ci = lax.broadcasted_iota(jnp.int32, (tm, tm), 1)
    below = (ri > ci).astype(BF16)
    f1 = oh1.astype(F32)
    f2 = oh2.astype(F32)
    tot1 = jnp.sum(f1, 0, keepdims=True)
    run = cnt_ref[...]
    before1 = run + _bdot(below, oh1.astype(BF16))
    before2 = run + tot1 + _bdot(below, oh2.astype(BF16))
    rank1 = jnp.sum(f1 * before1, -1, keepdims=True).astype(jnp.int32)
    rank2 = jnp.sum(f2 * before2, -1, keepdims=True).astype(jnp.int32)
    cnt_ref[...] = run + tot1 + jnp.sum(f2, 0, keepdims=True)
    route_ref[...] = jnp.where(lane == 0, e1, jnp.where(lane == 1, e2, jnp.where(
        lane == 2, rank1, jnp.where(lane == 3, rank2, 0))))


def _outproj(merged, h, wo, gain2, w_router, b_router, counts_in, tm):
    m = h.shape[0]
    assert m % tm == 0
    row = lambda n: pl.BlockSpec((1, n), lambda i: (0, 0))
    tile = lambda n: pl.BlockSpec((tm, n), lambda i: (i, 0))
    return pl.pallas_call(
        _outproj_kernel,
        grid=(m // tm,),
        in_specs=[
            tile(D_MODEL), tile(D_MODEL),
            pl.BlockSpec((D_MODEL, D_MODEL), lambda i: (0, 0)),
            row(D_MODEL),
            pl.BlockSpec((D_MODEL, ROUTER_PAD), lambda i: (0, 0)),
            row(ROUTER_PAD), row(ROUTER_PAD),
        ],
        out_specs=[tile(D_MODEL), pl.BlockSpec((tm, ROW_SLABS, SLAB), lambda i: (i, 0, 0)),
                   tile(ROUTER_PAD), tile(ROUTER_PAD), row(ROUTER_PAD)],
        out_shape=[
            jax.ShapeDtypeStruct((m, D_MODEL), F32),
            jax.ShapeDtypeStruct((m, ROW_SLABS, SLAB), F32),
            jax.ShapeDtypeStruct((m, ROUTER_PAD), F32),
            jax.ShapeDtypeStruct((m, ROUTER_PAD), jnp.int32),
            jax.ShapeDtypeStruct((1, ROUTER_PAD), F32),
        ],
        compiler_params=pltpu.CompilerParams(
            dimension_semantics=("arbitrary",), vmem_limit_bytes=VMEM_LIMIT),
        name="outproj",
    )(merged, h, wo, gain2, w_router, b_router, counts_in)


def _moe_kernel(blk_e_ref, nused_ref, x_ref, wg_ref, wu_ref, wd_ref, o_ref, wgb, wub, wdb):
    i = pl.program_id(0)
    e = blk_e_ref[i]
    e_prev = blk_e_ref[jnp.maximum(i - 1, 0)]
    used = i < nused_ref[0]

    @pl.when(jnp.logical_and(used, jnp.logical_or(i == 0, e != e_prev)))
    def _():
        def cast_in(r, c):
            sl = pl.ds(pl.multiple_of(r * 256, 256), 256)
            wgb[sl, :] = wg_ref[sl, :].astype(BF16)
            wub[sl, :] = wu_ref[sl, :].astype(BF16)
            return c

        lax.fori_loop(0, D_MODEL // 256, cast_in, 0)

        def cast_out(r, c):
            sl = pl.ds(pl.multiple_of(r * 128, 128), 128)
            wdb[sl, :] = wd_ref[sl, :].astype(BF16)
            return c

        lax.fori_loop(0, D_EXPERT // 128, cast_out, 0)

    @pl.when(used)
    def _():
        x = _load_slabs(x_ref).astype(BF16)
        g = _bdot(x, wgb[...])
        u = _bdot(x, wub[...])
        hb = (g * jax.nn.sigmoid(g)) * u
        _store_slabs(o_ref, _bdot(hb.astype(BF16), wdb[...]))

    @pl.when(jnp.logical_not(used))
    def _():
        o_ref[...] = jnp.zeros_like(o_ref)


def _moe_ffn(xs, blk_e, n_used, w_gate, w_up, w_down):
    p = xs.shape[0]
    nb = p // MOE_ROWS
    grid_spec = pltpu.PrefetchScalarGridSpec(
        num_scalar_prefetch=2,
        grid=(nb,),
        in_specs=[
            pl.BlockSpec((MOE_ROWS, ROW_SLABS, SLAB), lambda i, be, nu: (jnp.minimum(i, nu[0] - 1), 0, 0)),
            pl.BlockSpec((None, D_MODEL, D_EXPERT), lambda i, be, nu: (be[i], 0, 0)),
            pl.BlockSpec((None, D_MODEL, D_EXPERT), lambda i, be, nu: (be[i], 0, 0)),
            pl.BlockSpec((None, D_EXPERT, D_MODEL), lambda i, be, nu: (be[i], 0, 0)),
        ],
        out_specs=pl.BlockSpec((MOE_ROWS, ROW_SLABS, SLAB), lambda i, be, nu: (i, 0, 0)),
        scratch_shapes=[
            pltpu.VMEM((D_MODEL, D_EXPERT), BF16),
            pltpu.VMEM((D_MODEL, D_EXPERT), BF16),
            pltpu.VMEM((D_EXPERT, D_MODEL), BF16),
        ],
    )
    return pl.pallas_call(
        _moe_kernel,
        grid_spec=grid_spec,
        out_shape=jax.ShapeDtypeStruct((p, ROW_SLABS, SLAB), F32),
        compiler_params=pltpu.CompilerParams(
            dimension_semantics=("arbitrary",), vmem_limit_bytes=VMEM_LIMIT),
        name="moe_ffn",
    )(blk_e, n_used, xs, w_gate, w_up, w_down)


HALF_TILE = 64
LANES = 128


def _dot_exact_rhs(a, b01):
    a1 = a.astype(BF16)
    r1 = a - a1.astype(F32)
    a2 = r1.astype(BF16)
    a3 = (r1 - a2.astype(F32)).astype(BF16)
    return _bdot(a1, b01) + (_bdot(a2, b01) + _bdot(a3, b01))


def _group_rms_rope(x, gain, cos128, sin128, bd, bdt):
    r, w = x.shape
    ssq = _dot_exact_rhs(x * x, bd)
    rs = lax.rsqrt(ssq * (1.0 / B_HD) + EPS)
    y = (x * _dot_exact_rhs(rs, bdt)) * gain
    reps = w // LANES
    cosw = jnp.concatenate([cos128] * reps, 1)
    sinw = jnp.concatenate([sin128] * reps, 1)
    lane = lax.broadcasted_iota(jnp.int32, (r, w), 1)
    swapped = jnp.where((lane & (B_HD // 2)) == 0,
                        pltpu.roll(y, w - B_HD // 2, 1), pltpu.roll(y, B_HD // 2, 1))
    return y * cosw + swapped * sinw


def _kv_tiles(tile, odd, lo):
    rolled = pltpu.roll(tile, HALF_TILE, 1)
    dup = jnp.where(lo, rolled, tile) if odd else jnp.where(lo, tile, rolled)
    return dup.astype(BF16), jnp.where(lo, dup, 0.0).astype(BF16), jnp.where(lo, 0.0, dup).astype(BF16)


def _nt(a, b):
    return lax.dot_general(a, b, (((1,), (1,)), ((), ())), preferred_element_type=F32)


def _sink_softmax_pv(score_lists, value_lists, sinks):
    ms = []
    for scores, sink in zip(score_lists, sinks):
        m = jnp.max(scores[0], -1, keepdims=True)
        for s in scores[1:]:
            m = jnp.maximum(m, jnp.max(s, -1, keepdims=True))
        ms.append(jnp.maximum(m, sink))
    es = [[jnp.exp(s - m) for s in scores] for scores, m in zip(score_lists, ms)]
    dens = []
    for e_blocks, m, sink in zip(es, ms, sinks):
        den = jnp.exp(sink - m)
        for e in e_blocks:
            den = den + jnp.sum(e, -1, keepdims=True)
        dens.append(den)
    pvs = [[_bdot(e.astype(BF16), v) for e, v in zip(e_blocks, values)]
           for e_blocks, values in zip(es, value_lists)]
    outs = []
    for pv, den in zip(pvs, dens):
        acc = pv[0]
        for x in pv[1:]:
            acc = acc + x
        outs.append(acc / den)
    return outs


def _rope_small_kernel(q_ref, k_ref, cos_ref, sin_ref, gq_ref, gk_ref, bd_ref, bdt_ref, qo_ref, ko_ref):
    cos = cos_ref[...]
    sin = sin_ref[...]
    qo_ref[...] = _group_rms_rope(q_ref[...], gq_ref[...], cos, sin, bd_ref[...], bdt_ref[...])
    ko_ref[...] = _group_rms_rope(k_ref[...], gk_ref[...], cos, sin, bd_ref[:B_KV, :], bdt_ref[:, :B_KV])


def _rope_small(proj, cos, sin, gq, gk, bd, bdt):
    m = proj.shape[0]
    full = lambda shape: pl.BlockSpec(shape, lambda i: (0, 0))
    return pl.pallas_call(
        _rope_small_kernel,
        grid=(1,),
        in_specs=[
            pl.BlockSpec((m, B_Q), lambda i: (0, COL_QB // B_Q)),
            pl.BlockSpec((m, B_KV), lambda i: (0, COL_KB // B_KV)),
            full((m, LANES)), full((m, LANES)), full((1, B_Q)), full((1, B_KV)),
            full((B_Q, LANES)), full((LANES, B_Q)),
        ],
        out_specs=[full((m, B_Q)), full((m, B_KV))],
        out_shape=[jax.ShapeDtypeStruct((m, B_Q), F32), jax.ShapeDtypeStruct((m, B_KV), F32)],
        compiler_params=pltpu.CompilerParams(
            dimension_semantics=("arbitrary",), vmem_limit_bytes=VMEM_LIMIT),
        name="rope_small",
    )(proj, proj, cos, sin, gq, gk, bd, bdt)


def _swa_prompt_kernel(sinks_ref, q_ref, k_ref, v_ref, cos_ref, sin_ref, km_ref, vm_ref, gq_ref, gk_ref,
                       bd_ref, bdt_ref, y_ref, kc_ref, vc_ref, kprev, vloprev, vhiprev):
    n = pl.program_id(1)

    @pl.when(n == 0)
    def _():
        kprev[...] = jnp.zeros_like(kprev)
        vloprev[...] = jnp.zeros_like(vloprev)
        vhiprev[...] = jnp.zeros_like(vhiprev)

    cos = cos_ref[...]
    sin = sin_ref[...]
    q = _group_rms_rope(q_ref[...], gq_ref[...], cos, sin, bd_ref[...], bdt_ref[...])
    k = _group_rms_rope(k_ref[...], gk_ref[...], cos, sin, bd_ref[:B_KV, :], bdt_ref[:, :B_KV])
    v = v_ref[...]
    kc_ref[...] = k
    vc_ref[...] = v
    km = km_ref[...]
    vm = vm_ref[...]

    qi = lax.broadcasted_iota(jnp.int32, (WINDOW, WINDOW), 0)
    kj = lax.broadcasted_iota(jnp.int32, (WINDOW, WINDOW), 1)
    cur_ok = kj <= qi
    prev_ok = jnp.logical_and(kj > qi, n > 0)
    lo = lax.broadcasted_iota(jnp.int32, (WINDOW, LANES), 1) < HALF_TILE
    lo_m = lax.broadcasted_iota(jnp.int32, (N_META, LANES), 1) < HALF_TILE
    scale = B_HD ** -0.5

    kv = []
    for g in range(B_KV_HEADS):
        tl = slice((g // 2) * LANES, (g // 2 + 1) * LANES)
        k2c, _, _ = _kv_tiles(k[:, tl], g % 2, lo)
        _, vlo_c, vhi_c = _kv_tiles(v[:, tl], g % 2, lo)
        k2m, _, _ = _kv_tiles(km[:, tl], g % 2, lo_m)
        _, vlo_m, vhi_m = _kv_tiles(vm[:, tl], g % 2, lo_m)
        kv.append(((k2m, kprev[g], k2c), (vlo_m, vloprev[g], vlo_c), (vhi_m, vhiprev[g], vhi_c)))
    heads = range(B_HEADS)
    qms = []
    for h in heads:
        qt = q[:, (h // 2) * LANES:(h // 2 + 1) * LANES]
        qms.append((jnp.where(lo, qt, 0.0) if h % 2 == 0 else jnp.where(lo, 0.0, qt)).astype(BF16))
    raw = [[_nt(qms[h], kk) * scale for kk in kv[h // B_GROUP][0]] for h in heads]
    scores = [[sm, jnp.where(prev_ok, sp, -jnp.inf), jnp.where(cur_ok, sc, -jnp.inf)] for sm, sp, sc in raw]
    outs = _sink_softmax_pv(scores, [kv[h // B_GROUP][1 + h % 2] for h in heads], [sinks_ref[h] for h in heads])
    for t in range(B_HEADS // 2):
        y_ref[:, t * LANES:(t + 1) * LANES] = (outs[2 * t] + outs[2 * t + 1]).astype(BF16)
    for g in range(B_KV_HEADS):
        kprev[g] = kv[g][0][2]
        vloprev[g] = kv[g][1][2]
        vhiprev[g] = kv[g][2][2]


def _swa_prompt(proj, cos, sin, k_meta, v_meta, gq, gk, bd, bdt, sinks, nb, nblk):
    rows = nb * nblk * WINDOW
    assert proj.shape[0] == rows
    return pl.pallas_call(
        _swa_prompt_kernel,
        grid=(nb, nblk),
        in_specs=[
            pl.BlockSpec(memory_space=pltpu.SMEM),
            pl.BlockSpec((WINDOW, B_Q), lambda b, n: (b * nblk + n, COL_QB // B_Q)),
            pl.BlockSpec((WINDOW, B_KV), lambda b, n: (b * nblk + n, COL_KB // B_KV)),
            pl.BlockSpec((WINDOW, B_KV), lambda b, n: (b * nblk + n, COL_VB // B_KV)),
            pl.BlockSpec((WINDOW, LANES), lambda b, n: (n, 0)),
            pl.BlockSpec((WINDOW, LANES), lambda b, n: (n, 0)),
            pl.BlockSpec((N_META, B_KV), lambda b, n: (0, 0)),
            pl.BlockSpec((N_META, B_KV), lambda b, n: (0, 0)),
            pl.BlockSpec((1, B_Q), lambda b, n: (0, 0)),
            pl.BlockSpec((1, B_KV), lambda b, n: (0, 0)),
            pl.BlockSpec((B_Q, LANES), lambda b, n: (0, 0)),
            pl.BlockSpec((LANES, B_Q), lambda b, n: (0, 0)),
        ],
        out_specs=[
            pl.BlockSpec((WINDOW, B_Q), lambda b, n: (b * nblk + n, 0)),
            pl.BlockSpec((None, WINDOW, B_KV), lambda b, n: (b, 0, 0)),
            pl.BlockSpec((None, WINDOW, B_KV), lambda b, n: (b, 0, 0)),
        ],
        scratch_shapes=[pltpu.VMEM((B_KV_HEADS, WINDOW, LANES), BF16)] * 3,
        out_shape=[
            jax.ShapeDtypeStruct((rows, B_Q), BF16),
            jax.ShapeDtypeStruct((nb, WINDOW, B_KV), F32),
            jax.ShapeDtypeStruct((nb, WINDOW, B_KV), F32),
        ],
        compiler_params=pltpu.CompilerParams(
            dimension_semantics=("arbitrary", "arbitrary"), vmem_limit_bytes=VMEM_LIMIT),
        name="swa_prompt",
    )(sinks, proj, proj, proj, cos, sin, k_meta, v_meta, gq, gk, bd, bdt)


def _swa_sample_kernel(sinks_ref, q_ref, k_ref, v_ref, y_ref, *, n_keys, n_new):
    q = q_ref[...]
    k = k_ref[...]
    v = v_ref[...]
    rq = q.shape[0]
    t = lax.broadcasted_iota(jnp.int32, (rq, n_keys), 0)
    r = lax.broadcasted_iota(jnp.int32, (rq, n_keys), 1)
    wj = r - N_META
    win_pos = PAST_LEN - WINDOW + wj
    nm = r - N_META - WINDOW
    ok = ((r < N_META)
          | ((wj >= 0) & (wj < WINDOW) & (win_pos >= N_META) & (wj >= t + 1))
          | ((nm >= 0) & (nm <= t) & (nm > t - WINDOW) & (nm < n_new)))
    lo_q = lax.broadcasted_iota(jnp.int32, (rq, LANES), 1) < HALF_TILE
    lo_k = lax.broadcasted_iota(jnp.int32, (n_keys, LANES), 1) < HALF_TILE
    scale = B_HD ** -0.5
    kv = []
    for g in range(B_KV_HEADS):
        tl = slice((g // 2) * LANES, (g // 2 + 1) * LANES)
        k2, _, _ = _kv_tiles(k[:, tl], g % 2, lo_k)
        _, vlo, vhi = _kv_tiles(v[:, tl], g % 2, lo_k)
        kv.append((k2, vlo, vhi))
    heads = range(B_HEADS)
    qms = []
    for h in heads:
        qt = q[:, (h // 2) * LANES:(h // 2 + 1) * LANES]
        qms.append((jnp.where(lo_q, qt, 0.0) if h % 2 == 0 else jnp.where(lo_q, 0.0, qt)).astype(BF16))
    scores = [[jnp.where(ok, _nt(qms[h], kv[h // B_GROUP][0]) * scale, -jnp.inf)] for h in heads]
    outs = _sink_softmax_pv(scores, [[kv[h // B_GROUP][1 + h % 2]] for h in heads],
                            [sinks_ref[h] for h in heads])
    for t in range(B_HEADS // 2):
        y_ref[:, t * LANES:(t + 1) * LANES] = (outs[2 * t] + outs[2 * t + 1]).astype(BF16)


def _swa_sample(q8, kk, vv, sinks, n_new):
    nb, rq, _ = q8.shape
    n_keys = kk.shape[1]
    return pl.pallas_call(
        functools.partial(_swa_sample_kernel, n_keys=n_keys, n_new=n_new),
        grid=(nb,),
        in_specs=[
            pl.BlockSpec(memory_space=pltpu.SMEM),
            pl.BlockSpec((None, rq, B_Q), lambda b: (b, 0, 0)),
            pl.BlockSpec((None, n_keys, B_KV), lambda b: (b, 0, 0)),
            pl.BlockSpec((None, n_keys, B_KV), lambda b: (b, 0, 0)),
        ],
        out_specs=pl.BlockSpec((None, rq, B_Q), lambda b: (b, 0, 0)),
        out_shape=jax.ShapeDtypeStruct((nb, rq, B_Q), BF16),
        compiler_params=pltpu.CompilerParams(
            dimension_semantics=("arbitrary",), vmem_limit_bytes=VMEM_LIMIT),
        name="swa_sample",
    )(sinks, q8, kk, vv)


def _rope_tables(pos):
    half = B_HD // 2
    inv_freq = ROPE_THETA ** (-jnp.arange(half, dtype=F32) / half)
    ang = pos.astype(F32)[:, None] * inv_freq[None, :]
    cos, sin = jnp.cos(ang), jnp.sin(ang)
    return jnp.concatenate([cos, cos, cos, cos], 1), jnp.concatenate([-sin, sin, -sin, sin], 1)


ROUTE_ROWS = 256
DMA_UNROLL = 8


def _row(ref, r):
    return ref.at[pl.ds(r, 1)]


def _dispatch_kernel(pend_ref, dest_ref, xa_ref, xb_ref, xs_hbm, zeros_vmem, ring, sem_zero, sem, *, nta, nt, nb):
    i = pl.program_id(0)
    tm = ROUTE_ROWS
    slot = i % 2

    def drain(s):
        for _k in range(TOP_K):
            pltpu.make_async_copy(ring.at[s], xs_hbm.at[pl.ds(0, tm)], sem.at[s]).wait()

    @pl.when(i == 0)
    def _():
        zeros_vmem[...] = jnp.zeros_like(zeros_vmem)
        fill = lambda row0: pltpu.make_async_copy(zeros_vmem, xs_hbm.at[pl.ds(row0, MOE_ROWS)], sem_zero)
        for e in range(N_EXPERTS):
            fill(jnp.maximum(pend_ref[e] - MOE_ROWS, 0)).start()
        for e in range(N_EXPERTS):
            fill(0).wait()
        n_used = pend_ref[N_EXPERTS - 1] // MOE_ROWS

        def fill_tail(b, c):
            fill(b * MOE_ROWS).start()
            fill(0).wait()
            return c

        lax.fori_loop(n_used, nb, fill_tail, 0)

    @pl.when(i >= 2)
    def _():
        drain(slot)

    @pl.when(i < nta)
    def _():
        ring[slot] = xa_ref[...]

    @pl.when(i >= nta)
    def _():
        ring[slot] = xb_ref[...]

    def body(r, c):
        src = ring.at[slot, pl.ds(r, 1)]
        for k in range(TOP_K):
            pltpu.make_async_copy(src, _row(xs_hbm, dest_ref[0, TOP_K * r + k]), sem.at[slot]).start(priority=k)
        return c

    lax.fori_loop(0, tm, body, 0, unroll=DMA_UNROLL)

    @pl.when(i == nt - 1)
    def _():
        if nt >= 2:
            drain(1 - slot)
        drain(slot)


def _dispatch(xa, xb, dest, pend, n_slots):
    assert xa.shape[0] % ROUTE_ROWS == 0 and xb.shape[0] % ROUTE_ROWS == 0 and MOE_ROWS == ROUTE_ROWS
    nta = xa.shape[0] // ROUTE_ROWS
    nt = nta + xb.shape[0] // ROUTE_ROWS
    slab_rows = (ROUTE_ROWS, ROW_SLABS, SLAB)
    grid_spec = pltpu.PrefetchScalarGridSpec(
        num_scalar_prefetch=1,
        grid=(nt,),
        in_specs=[
            pl.BlockSpec((None, 1, TOP_K * ROUTE_ROWS), lambda i, pe: (i, 0, 0), memory_space=pltpu.SMEM),
            pl.BlockSpec(slab_rows, lambda i, pe: (jnp.minimum(i, nta - 1), 0, 0)),
            pl.BlockSpec(slab_rows, lambda i, pe: (jnp.maximum(i - nta, 0), 0, 0)),
        ],
        out_specs=pl.BlockSpec(memory_space=pl.ANY),
        scratch_shapes=[
            pltpu.VMEM(slab_rows, F32),
            pltpu.VMEM((2,) + slab_rows, F32),
            pltpu.SemaphoreType.DMA(()),
            pltpu.SemaphoreType.DMA((2,)),
        ],
    )
    return pl.pallas_call(
        functools.partial(_dispatch_kernel, nta=nta, nt=nt, nb=n_slots // MOE_ROWS),
        grid_spec=grid_spec,
        out_shape=jax.ShapeDtypeStruct((n_slots, ROW_SLABS, SLAB), F32),
        compiler_params=pltpu.CompilerParams(
            dimension_semantics=("arbitrary",), vmem_limit_bytes=VMEM_LIMIT, has_side_effects=True),
        name="moe_dispatch",
    )(pend, dest.reshape(nt, 1, TOP_K * ROUTE_ROWS), xa, xb)


def _combine_kernel(dest_ref, dest_next_ref, gate_ref, h_ref, ys_hbm, o_ref, buf, sem, *, nt):
    i = pl.program_id(0)
    tm = ROUTE_ROWS
    slot = i % 2

    def issue(dref, s):
        def body(r, c):
            for k in range(TOP_K):
                pltpu.make_async_copy(_row(ys_hbm, dref[0, TOP_K * r + k]), buf.at[s, k, pl.ds(r, 1)],
                                      sem.at[s]).start(priority=k)
            return c

        lax.fori_loop(0, tm, body, 0, unroll=DMA_UNROLL)

    @pl.when(i == 0)
    def _():
        issue(dest_ref, 0)

    @pl.when(i < nt - 1)
    def _():
        issue(dest_next_ref, 1 - slot)

    for k in range(TOP_K):
        pltpu.make_async_copy(ys_hbm.at[pl.ds(0, tm)], buf.at[slot, k], sem.at[slot]).wait()
    g = gate_ref[...]
    for s in range(ROW_SLABS):
        cols = slice(s * SLAB, (s + 1) * SLAB)
        o_ref[:, cols] = h_ref[:, cols] + (g[:, 0:1] * buf[slot, 0, :, s, :] + g[:, 1:2] * buf[slot, 1, :, s, :])


def _combine(h2, gates, dest, ys):
    t = h2.shape[0]
    assert t % ROUTE_ROWS == 0
    nt = t // ROUTE_ROWS
    dest3 = dest.reshape(nt, 1, TOP_K * ROUTE_ROWS)
    idx = lambda f: pl.BlockSpec((None, 1, TOP_K * ROUTE_ROWS), f, memory_space=pltpu.SMEM)
    return pl.pallas_call(
        functools.partial(_combine_kernel, nt=nt),
        grid=(nt,),
        in_specs=[
            idx(lambda i: (i, 0, 0)),
            idx(lambda i: (jnp.minimum(i + 1, nt - 1), 0, 0)),
            pl.BlockSpec((ROUTE_ROWS, ROUTER_PAD), lambda i: (i, 0)),
            pl.BlockSpec((ROUTE_ROWS, D_MODEL), lambda i: (i, 0)),
            pl.BlockSpec(memory_space=pl.ANY),
        ],
        out_specs=pl.BlockSpec((ROUTE_ROWS, D_MODEL), lambda i: (i, 0)),
        out_shape=jax.ShapeDtypeStruct((t, D_MODEL), F32),
        scratch_shapes=[
            pltpu.VMEM((2, TOP_K, ROUTE_ROWS, ROW_SLABS, SLAB), F32),
            pltpu.SemaphoreType.DMA((2,)),
        ],
        compiler_params=pltpu.CompilerParams(
            dimension_semantics=("arbitrary",), vmem_limit_bytes=VMEM_LIMIT),
        name="moe_combine",
    )(dest3, dest3, gates, h2, ys)


def _moe_plan(route, counts):
    T = route.shape[0]
    expert = route[:, :TOP_K]
    rank = route[:, TOP_K:2 * TOP_K]
    nb = -(-(T * TOP_K) // MOE_ROWS) + N_EXPERTS
    cnt = counts[0, :N_EXPERTS].astype(jnp.int32)
    padded = (cnt + MOE_ROWS - 1) // MOE_ROWS * MOE_ROWS
    pend = jnp.cumsum(padded).astype(jnp.int32)
    dest = (pend - padded)[expert] + rank
    starts = jnp.arange(nb, dtype=jnp.int32) * MOE_ROWS
    blk_e = jnp.minimum(jnp.sum((pend[None, :] <= starts[:, None]).astype(jnp.int32), 1), N_EXPERTS - 1)
    n_used = pend[-1:] // MOE_ROWS
    return dest, pend, blk_e, n_used, nb * MOE_ROWS


def _lane_row(vals, lane0):
    return jnp.zeros((1, SMALL_PAD), F32).at[0, lane0:lane0 + vals.shape[0]].set(vals)


def kernel(x_prompt, x_sample, state_delta, state_conv, cache_swa_k, cache_swa_v, meta_tokens,
           norm1_w, w_in, conv_w, a_log, dt_bias, a_norm_w, w_up_a, q_norm_w, k_norm_w, sinks,
           w_up_b, w_o, norm2_w, w_router_group, b_router_group, w_router_expert, b_router_expert,
           w_gate, w_up, w_down):
    Bp, Sp, _ = x_prompt.shape
    Bs, Ss, _ = x_sample.shape
    n_s = Bs * Ss
    l = 0
    w = w_in[l]
    offs = [0]
    for s in (A_CONV_DIM, A_HEADS, A_HEADS, A_V, B_Q, B_KV, B_KV, D_MODEL, D_MODEL):
        offs.append(offs[-1] + s)
    seg = lambda a: w[:, offs[a]:offs[a + 1]]
    w_main = jnp.concatenate([seg(7), seg(8), seg(0), seg(3), seg(4), seg(5), seg(6)], 1).astype(BF16)
    w_small = jnp.concatenate(
        [seg(1), seg(2), jnp.zeros((D_MODEL, SMALL_PAD - 2 * A_HEADS), F32)], 1).astype(BF16)
    wa = w_up_a[l].astype(BF16)
    wb = w_up_b[l].astype(BF16)
    wo = w_o[l].astype(BF16)
    w_router = jnp.concatenate(
        [w_router_group[l], w_router_expert[l],
         jnp.zeros((D_MODEL, ROUTER_PAD - N_GROUPS - N_EXPERTS), F32)], 1).astype(BF16)
    b_router = jnp.concatenate(
        [b_router_group[l], b_router_expert[l],
         jnp.zeros((ROUTER_PAD - N_GROUPS - N_EXPERTS,), F32)])[None, :]
    g1 = norm1_w[l][None, :]
    g2 = norm2_w[l][None, :]
    alog_row = _lane_row(a_log[l], LANE_DECAY)
    dtb_row = _lane_row(dt_bias[l], LANE_DECAY)
    anw_row = a_norm_w[l][None, :]

    xp = x_prompt.reshape(Bp * Sp, D_MODEL)
    xs = x_sample.reshape(n_s, D_MODEL)
    x_small = jnp.concatenate([xs, meta_tokens], 0)
    n_small = x_small.shape[0]

    proj_p, small_p = _inproj(xp, g1, w_main, w_small, tm=1024)
    proj_s, small_s = _inproj(x_small, g1, w_main, w_small, tm=n_small)

    dm = functools.partial(_delta_mixer, conv_w=conv_w[l], alog_row=alog_row, dtb_row=dtb_row, anw_row=anw_row)
    pad_rows = lambda a, n: jnp.concatenate([a, jnp.zeros((n - a.shape[0],) + a.shape[1:], a.dtype)], 0)
    meta_proj = pad_rows(proj_s[n_s:], A_CHUNK)
    meta_small = pad_rows(small_s[n_s:], A_CHUNK)
    zero_prev = jnp.zeros((1, SUBLANES, A_CONV_DIM), F32)
    zero_state = jnp.zeros((1, A_HEADS, A_DK, A_DV), F32)
    _, s_meta = dm(meta_proj, meta_small, zero_prev, zero_state, nb=1, nc=1, C=A_CHUNK, n_valid=N_META,
                   shared_first=True)
    meta_tail = proj_s[n_s + N_META - SUBLANES:, COL_QKV:COL_QKV + A_CONV_DIM][None]
    ya_p, sdelta_p = dm(proj_p, small_p, meta_tail, s_meta, nb=Bp, nc=Sp // A_CHUNK, C=A_CHUNK,
                        n_valid=A_CHUNK, shared_first=True)
    conv_p = proj_p.reshape(Bp, Sp, PROJ_MAIN)[:, Sp - (A_CONV - 1):, COL_QKV:COL_QKV + A_CONV_DIM]
    CS = SUBLANES
    samp_proj = jnp.pad(proj_s[:n_s].reshape(Bs, Ss, PROJ_MAIN), ((0, 0), (0, CS - Ss), (0, 0)))
    samp_small = jnp.pad(small_s[:n_s].reshape(Bs, Ss, SMALL_PAD), ((0, 0), (0, CS - Ss), (0, 0)))
    samp_prev = jnp.pad(state_conv[l], ((0, 0), (SUBLANES - (A_CONV - 1), 0), (0, 0)))
    ya_s8, sdelta_s = dm(samp_proj.reshape(Bs * CS, PROJ_MAIN), samp_small.reshape(Bs * CS, SMALL_PAD),
                         samp_prev, state_delta[l], nb=Bs, nc=1, C=CS, n_valid=Ss, shared_first=False)
    ya_s = ya_s8.reshape(Bs, CS, A_V)[:, :Ss].reshape(n_s, A_V)
    qkv_s = proj_s[:n_s, COL_QKV:COL_QKV + A_CONV_DIM].reshape(Bs, Ss, A_CONV_DIM)
    conv_s = jnp.concatenate([state_conv[l], qkv_s], 1)[:, -(A_CONV - 1):]

    gq = jnp.tile(q_norm_w[l], B_HEADS)[None]
    gk = jnp.tile(k_norm_w[l], B_KV_HEADS)[None]
    bd = ((jnp.arange(B_Q) // B_HD)[:, None] == jnp.arange(LANES)[None, :]).astype(BF16)
    bdt = bd.T
    pos_small = jnp.concatenate([PAST_LEN + jnp.arange(n_s, dtype=jnp.int32) % Ss,
                                 jnp.arange(N_META, dtype=jnp.int32)])
    cos_s, sin_s = _rope_tables(pos_small)
    q_rot_s, k_rot_s = _rope_small(proj_s, cos_s, sin_s, gq, gk, bd, bdt)
    k_meta = k_rot_s[n_s:]
    v_meta = proj_s[n_s:, COL_VB:COL_VB + B_KV]
    cos_p, sin_p = _rope_tables(N_META + jnp.arange(Sp, dtype=jnp.int32))
    yb_p, k_last, v_last = _swa_prompt(proj_p, cos_p, sin_p, k_meta, v_meta, gq, gk, bd, bdt, sinks[l],
                                       Bp, Sp // WINDOW)
    cache_shape = (N_META + WINDOW, B_KV_HEADS, B_HD)
    bcast_meta = lambda a: jnp.broadcast_to(a[None], (Bp, N_META, B_KV))
    swk_p = jnp.concatenate([bcast_meta(k_meta), k_last], 1).reshape((Bp,) + cache_shape)
    swv_p = jnp.concatenate([bcast_meta(v_meta), v_last], 1).reshape((Bp,) + cache_shape)
    n_cache = N_META + WINDOW
    key_pad = -(n_cache + Ss) % SUBLANES
    zpad = jnp.zeros((Bs, key_pad, B_KV), F32)
    kk = jnp.concatenate([cache_swa_k[l].reshape(Bs, n_cache, B_KV), k_rot_s[:n_s].reshape(Bs, Ss, B_KV), zpad], 1)
    vv = jnp.concatenate([cache_swa_v[l].reshape(Bs, n_cache, B_KV),
                          proj_s[:n_s, COL_VB:COL_VB + B_KV].reshape(Bs, Ss, B_KV), zpad], 1)
    q8 = jnp.pad(q_rot_s[:n_s].reshape(Bs, Ss, B_Q), ((0, 0), (0, SUBLANES - Ss), (0, 0)))
    yb_s = _swa_sample(q8, kk, vv, sinks[l], n_new=Ss)[:, :Ss].reshape(n_s, B_Q)
    new_cache = lambda t: jnp.concatenate(
        [t[:, :N_META], t[:, n_cache + Ss - WINDOW:n_cache + Ss]], 1).reshape((Bs,) + cache_shape)
    swk_s, swv_s = new_cache(kk), new_cache(vv)

    merged_p = _merge(ya_p, yb_p, proj_p, wa, wb, tm=512)
    merged_s = _merge(ya_s, yb_s, proj_s[:n_s], wa, wb, tm=256)
    cnt0 = jnp.zeros((1, ROUTER_PAD), F32)
    h2_p, xn2_p, gt_p, rt_p, cnt_p = _outproj(merged_p, xp, wo, g2, w_router, b_router, cnt0, tm=256)
    h2_s, xn2_s, gt_s, rt_s, cnt = _outproj(merged_s, xs, wo, g2, w_router, b_router, cnt_p, tm=256)

    cat = lambda a, b: jnp.concatenate([a, b], 0)
    n_p = Bp * Sp
    dest, pend, blk_e, n_used, n_slots = _moe_plan(cat(rt_p, rt_s), cnt)
    xs_slots = _dispatch(xn2_p, xn2_s, dest, pend, n_slots)
    ys = _moe_ffn(xs_slots, blk_e, n_used, w_gate[l], w_up[l], w_down[l])
    y_prompt = _combine(h2_p, gt_p, dest[:n_p], ys).reshape(Bp, Sp, D_MODEL)
    y_sample = _combine(h2_s, gt_s, dest[n_p:], ys).reshape(Bs, Ss, D_MODEL)
    return (y_prompt, y_sample, sdelta_p[None], conv_p[None], swk_p[None], swv_p[None],
            sdelta_s[None], conv_s[None], swk_s[None], swv_s[None])
```

```python
import functools

import jax
import jax.numpy as jnp
from jax import lax
from jax.experimental import pallas as pl
from jax.experimental.pallas import tpu as pltpu

F32 = jnp.float32
BF16 = jnp.bfloat16

D_MODEL = 2048
N_META = 16
A_HEADS = 8
A_DK = 128
A_DV = 128
A_CONV = 4
A_CHUNK = 64
A_QK = A_HEADS * A_DK
A_V = A_HEADS * A_DV
A_CONV_DIM = 2 * A_QK + A_V
B_HEADS = 16
B_KV_HEADS = 4
B_HD = 64
B_GROUP = B_HEADS // B_KV_HEADS
B_Q = B_HEADS * B_HD
B_KV = B_KV_HEADS * B_HD
WINDOW = 128
ROPE_THETA = 10000.0
PAST_LEN = 16384
N_GROUPS = 4
EXPERTS_PER_GROUP = 8
N_EXPERTS = N_GROUPS * EXPERTS_PER_GROUP
TOP_K = 2
D_EXPERT = 512
EPS = 1e-6

COL_GA = 0
COL_GB = COL_GA + D_MODEL
COL_QKV = COL_GB + D_MODEL
COL_Z = COL_QKV + A_CONV_DIM
COL_QB = COL_Z + A_V
COL_KB = COL_QB + B_Q
COL_VB = COL_KB + B_KV
PROJ_MAIN = COL_VB + B_KV
SMALL_PAD = 128
ROUTER_PAD = 128
LANE_BETA = 0
LANE_DECAY = A_HEADS

SUBLANES = 8
MOE_ROWS = 256
DELTA_CHUNKS_PER_STEP = 2
ROW_SLABS = 8
SLAB = D_MODEL // ROW_SLABS
VMEM_LIMIT = 56 * 1024 * 1024


def _row_chunk(tm):
    return 256 if tm % 256 == 0 else tm


def _bdot(a, b):
    return jnp.dot(a, b, preferred_element_type=F32)


def _store_slabs(ref3, val):
    for s in range(ROW_SLABS):
        ref3[:, s, :] = val[:, s * SLAB:(s + 1) * SLAB]


def _load_slabs(ref3):
    return jnp.concatenate([ref3[:, s, :] for s in range(ROW_SLABS)], 1)


def _inproj_kernel(x_ref, g_ref, w_ref, ws_ref, o_ref, os_ref, xn_ref, *, tm):
    rc = _row_chunk(tm)

    @pl.when(pl.program_id(1) == 0)
    def _():
        def norm_rows(sl):
            x = x_ref[sl, :]
            ms = jnp.mean(x * x, axis=-1, keepdims=True)
            xn_ref[sl, :] = ((x * lax.rsqrt(ms + EPS)) * g_ref[...]).astype(BF16)

        def body(r, c):
            norm_rows(pl.ds(pl.multiple_of(r * rc, rc), rc))
            return c

        if tm == rc:
            norm_rows(pl.ds(0, tm))
        else:
            lax.fori_loop(0, tm // rc, body, 0)
        os_ref[...] = _bdot(xn_ref[...], ws_ref[...])

    o_ref[...] = _bdot(xn_ref[...], w_ref[...])


def _inproj(x, gain, w_main, w_small, tm, tn=512):
    m = x.shape[0]
    assert m % tm == 0 and PROJ_MAIN % tn == 0
    return pl.pallas_call(
        functools.partial(_inproj_kernel, tm=tm),
        grid=(m // tm, PROJ_MAIN // tn),
        in_specs=[
            pl.BlockSpec((tm, D_MODEL), lambda i, j: (i, 0)),
            pl.BlockSpec((1, D_MODEL), lambda i, j: (0, 0)),
            pl.BlockSpec((D_MODEL, tn), lambda i, j: (0, j)),
            pl.BlockSpec((D_MODEL, SMALL_PAD), lambda i, j: (0, 0)),
        ],
        out_specs=[
            pl.BlockSpec((tm, tn), lambda i, j: (i, j)),
            pl.BlockSpec((tm, SMALL_PAD), lambda i, j: (i, 0)),
        ],
        out_shape=[
            jax.ShapeDtypeStruct((m, PROJ_MAIN), F32),
            jax.ShapeDtypeStruct((m, SMALL_PAD), F32),
        ],
        scratch_shapes=[pltpu.VMEM((tm, D_MODEL), BF16)],
        compiler_params=pltpu.CompilerParams(
            dimension_semantics=("arbitrary", "arbitrary"), vmem_limit_bytes=VMEM_LIMIT),
        name="inproj",
    )(x, gain, w_main, w_small)


def _dot_exact_lhs(a_bf16, b):
    b1 = b.astype(BF16)
    r1 = b - b1.astype(F32)
    b2 = r1.astype(BF16)
    b3 = (r1 - b2.astype(F32)).astype(BF16)
    return _bdot(a_bf16, b1) + (_bdot(a_bf16, b2) + _bdot(a_bf16, b3))


def _unit_lower_inverse_offsets(ms, ri, ci, c):
    same = lambda n: (ri >> n) == (ci >> n)
    d = lambda a, b: _bdot(a.astype(BF16), b.astype(BF16))
    blk8 = same(3)
    n1 = [jnp.where(blk8, m, 0.0) for m in ms]
    n2 = [d(a, a) for a in n1]
    n3 = [d(a, b) for a, b in zip(n1, n2)]
    n4 = [d(b, b) for b in n2]
    qs = [(b - a) - t for a, b, t in zip(n1, n2, n3)]
    qs = [(q + f) + d(q, f) for q, f in zip(qs, n4)]
    lg = 3
    while (1 << lg) < c:
        sel = jnp.logical_and(same(lg + 1), jnp.logical_not(same(lg)))
        offs = [jnp.where(sel, m, 0.0) for m in ms]
        ts = [o + d(q, o) for q, o in zip(qs, offs)]
        qs = [(q - t) - d(t, q) for q, t in zip(qs, ts)]
        lg += 1
    return qs


def _delta_kernel(qc, kc, vc, zc, qp, kp, vp, qf, kf, vf, sm_ref, cw_ref, alog_ref, dtb_ref, anw_ref, s0_ref,
                  ya_ref, s_ref, *, C, n_valid, nsub):
    R = nsub * C
    first = pl.program_id(1) == 0

    @pl.when(first)
    def _():
        s_ref[...] = s0_ref[...]

    rows = lax.broadcasted_iota(jnp.int32, (R, 1), 0)
    valid = rows < n_valid if nsub == 1 else None
    keep = (lambda a: jnp.where(valid, a, 0.0)) if nsub == 1 else (lambda a: a)
    ri = lax.broadcasted_iota(jnp.int32, (C, C), 0)
    ci = lax.broadcasted_iota(jnp.int32, (C, C), 1)
    lower = ri >= ci
    strict = ri > ci

    def conv(cur_ref, prev_ref, first_ref, col0):
        prev = jnp.where(first, first_ref[...], prev_ref[...])
        x = jnp.concatenate([prev, cur_ref[...]], 0)
        w = cw_ref[:, col0:col0 + A_QK]
        acc = x[SUBLANES:SUBLANES + R] * w[A_CONV - 1:A_CONV]
        for s in range(1, A_CONV):
            acc = acc + x[SUBLANES - s:SUBLANES - s + R] * w[A_CONV - 1 - s:A_CONV - s]
        return acc * jax.nn.sigmoid(acc)

    qx = conv(qc, qp, qf, 0)
    kx = conv(kc, kp, kf, A_QK)
    vx = conv(vc, vp, vf, 2 * A_QK)

    sm = sm_ref[...]
    beta_all = keep(jax.nn.sigmoid(sm))
    g_all = keep(-jnp.exp(alog_ref[...]) * jax.nn.softplus(sm + dtb_ref[...]))
    rr = lax.broadcasted_iota(jnp.int32, (R, R), 0)
    rc = lax.broadcasted_iota(jnp.int32, (R, R), 1)
    sh = C.bit_length() - 1
    same_chunk_lower = jnp.logical_and(rr >= rc, (rr >> sh) == (rc >> sh))
    g_cum = _dot_exact_lhs(same_chunk_lower.astype(BF16), g_all)
    g_cum_t = g_cum.T
    e_g = jnp.exp(g_cum)

    heads = range(A_HEADS)
    subs = range(nsub)
    items = [(s, h) for s in subs for h in heads]
    hs = [slice(h * A_DK, (h + 1) * A_DK) for h in heads]
    rs = [slice(s * C, (s + 1) * C) for s in subs]
    ld = lambda h: slice(LANE_DECAY + h, LANE_DECAY + h + 1)
    qs = [qx[rs[s], hs[h]] for s, h in items]
    qs = [q * lax.rsqrt(jnp.sum(q * q, -1, keepdims=True) + EPS) * (A_DK ** -0.5) for q in qs]
    ks = [kx[rs[s], hs[h]] for s, h in items]
    ks = [keep(k * lax.rsqrt(jnp.sum(k * k, -1, keepdims=True) + EPS)) for k in ks]
    vs = [keep(vx[rs[s], hs[h]]) for s, h in items]
    betas = [beta_all[rs[s], LANE_BETA + h:LANE_BETA + h + 1] for s, h in items]
    egs = [e_g[rs[s], ld(h)] for s, h in items]
    gammas = [jnp.exp(jnp.where(lower, g_cum[rs[s], ld(h)] - g_cum_t[ld(h), rs[s]], -jnp.inf))
              for s, h in items]
    kbs = [k * b for k, b in zip(ks, betas)]
    a1s = [lax.dot_general(jnp.concatenate([kb, q], 0).astype(BF16), k.astype(BF16),
                           (((1,), (1,)), ((), ())), preferred_element_type=F32)
           for kb, q, k in zip(kbs, qs, ks)]
    ms = [jnp.where(strict, a1[:C] * gm, 0.0) for a1, gm in zip(a1s, gammas)]
    attns = [a1[C:] * gm for a1, gm in zip(a1s, gammas)]
    rhss = [jnp.concatenate([v * b, kb * eg], 1) for v, b, kb, eg in zip(vs, betas, kbs, egs)]
    sols = [r + _bdot(q.astype(BF16), r.astype(BF16))
            for q, r in zip(_unit_lower_inverse_offsets(ms, ri, ci, C), rhss)]
    wq_lhs = [jnp.concatenate([sol[:, A_DV:], q * eg], 0).astype(BF16) for sol, q, eg in zip(sols, qs, egs)]
    g_lasts = [g_cum[(s + 1) * C - 1:(s + 1) * C, :] for s in subs]
    kg_ts = [(k * jnp.exp(g_lasts[s][:, ld(h)] - g_cum[rs[s], ld(h)])).T
             for (s, h), k in zip(items, ks)]
    states = [s_ref[0, h] for h in heads]
    for s in subs:
        at = lambda xs: xs[s * A_HEADS:(s + 1) * A_HEADS]
        wqs = [_bdot(l, st.astype(BF16)) for l, st in zip(at(wq_lhs), states)]
        v_news = [sol[:, :A_DV] - wq[:C] for sol, wq in zip(at(sols), wqs)]
        r2s = [_bdot(jnp.concatenate([a, kg], 0).astype(BF16), vn.astype(BF16))
               for a, kg, vn in zip(at(attns), at(kg_ts), v_news)]
        e_last = jnp.exp(g_lasts[s])
        states = [st * e_last[:, ld(h)] + r2[C:] for h, st, r2 in zip(heads, states, r2s)]
        for h in heads:
            o = wqs[h][C:] + r2s[h][:C]
            on = (o * lax.rsqrt(jnp.mean(o * o, -1, keepdims=True) + EPS)) * anw_ref[...]
            zh = zc[rs[s], hs[h]]
            ya_ref[rs[s], hs[h]] = (on * (zh * jax.nn.sigmoid(zh))).astype(BF16)
    for h in heads:
        s_ref[0, h] = states[h]


def _delta_mixer(proj, small, first_prev, s0, conv_w, alog_row, dtb_row, anw_row, *, nb, nc, C, n_valid,
                 shared_first, nsub=1):
    rows = nb * nc * C
    assert proj.shape[0] == rows and C % SUBLANES == 0 and nc % nsub == 0 and C & (C - 1) == 0
    R = nsub * C
    steps = nc // nsub
    cpb = R // SUBLANES
    cq, ck, cv, cz = (COL_QKV // A_QK, COL_QKV // A_QK + 1, COL_QKV // A_QK + 2, COL_Z // A_V)
    fb = (lambda b: 0) if shared_first else (lambda b: b)
    cur = lambda col: pl.BlockSpec((R, A_QK), lambda b, c: (b * steps + c, col))
    prev = lambda col: pl.BlockSpec(
        (SUBLANES, A_QK), lambda b, c: (jnp.maximum((b * steps + c) * cpb - 1, 0), col))
    frst = lambda j: pl.BlockSpec((None, SUBLANES, A_QK), lambda b, c: (fb(b), 0, j))
    row1 = lambda n: pl.BlockSpec((1, n), lambda b, c: (0, 0))
    return pl.pallas_call(
        functools.partial(_delta_kernel, C=C, n_valid=n_valid, nsub=nsub),
        grid=(nb, steps),
        in_specs=[
            cur(cq), cur(ck), cur(cv), cur(cz),
            prev(cq), prev(ck), prev(cv),
            frst(0), frst(1), frst(2),
            pl.BlockSpec((R, SMALL_PAD), lambda b, c: (b * steps + c, 0)),
            pl.BlockSpec((A_CONV, A_CONV_DIM), lambda b, c: (0, 0)),
            row1(SMALL_PAD), row1(SMALL_PAD), row1(A_DV),
            pl.BlockSpec((1, A_HEADS, A_DK, A_DV), lambda b, c: (fb(b), 0, 0, 0)),
        ],
        out_specs=[
            pl.BlockSpec((R, A_V), lambda b, c: (b * steps + c, 0)),
            pl.BlockSpec((1, A_HEADS, A_DK, A_DV), lambda b, c: (b, 0, 0, 0)),
        ],
        out_shape=[
            jax.ShapeDtypeStruct((rows, A_V), BF16),
            jax.ShapeDtypeStruct((nb, A_HEADS, A_DK, A_DV), F32),
        ],
        compiler_params=pltpu.CompilerParams(
            dimension_semantics=("arbitrary", "arbitrary"), vmem_limit_bytes=VMEM_LIMIT),
        name="delta_mixer",
    )(proj, proj, proj, proj, proj, proj, proj, first_prev, first_prev, first_prev, small,
      conv_w, alog_row, dtb_row, anw_row, s0)


def _merge_kernel(ya_ref, yb_ref, ga_ref, gb_ref, wa_ref, wb_ref, o_ref):
    ua = _bdot(ya_ref[...].astype(BF16), wa_ref[...])
    ub = _bdot(yb_ref[...].astype(BF16), wb_ref[...])
    merged = jax.nn.sigmoid(ga_ref[...]) * ua + jax.nn.sigmoid(gb_ref[...]) * ub
    o_ref[...] = merged.astype(BF16)


def _merge(ya, yb, proj, wa, wb, tm):
    m = ya.shape[0]
    assert m % tm == 0
    return pl.pallas_call(
        _merge_kernel,
        grid=(m // tm,),
        in_specs=[
            pl.BlockSpec((tm, A_V), lambda i: (i, 0)),
            pl.BlockSpec((tm, B_Q), lambda i: (i, 0)),
            pl.BlockSpec((tm, D_MODEL), lambda i: (i, COL_GA // D_MODEL)),
            pl.BlockSpec((tm, D_MODEL), lambda i: (i, COL_GB // D_MODEL)),
            pl.BlockSpec((A_V, D_MODEL), lambda i: (0, 0)),
            pl.BlockSpec((B_Q, D_MODEL), lambda i: (0, 0)),
        ],
        out_specs=pl.BlockSpec((tm, D_MODEL), lambda i: (i, 0)),
        out_shape=jax.ShapeDtypeStruct((m, D_MODEL), BF16),
        compiler_params=pltpu.CompilerParams(
            dimension_semantics=("arbitrary",), vmem_limit_bytes=VMEM_LIMIT),
        name="merge",
    )(ya, yb, proj, proj, wa, wb)


LANE_SENTINEL = ROUTER_PAD - 1


def _lane_argmax(vals, eligible, lane):
    top = jnp.max(jnp.where(eligible, vals, -jnp.inf), -1, keepdims=True)
    idx = jnp.min(jnp.where(jnp.logical_and(eligible, vals == top), lane, LANE_SENTINEL), -1, keepdims=True)
    return top, idx


def _masked_softmax(logits, eligible):
    z = jnp.where(eligible, logits, -jnp.inf)
    e = jnp.exp(z - jnp.max(z, -1, keepdims=True))
    return e / jnp.sum(e, -1, keepdims=True)


def _outproj_kernel(m_ref, h_ref, wo_ref, g_ref, wr_ref, br_ref, cnt_in_ref,
                    h2_ref, xn_ref, gate_ref, route_ref, cnt_ref):
    tm = h_ref.shape[0]

    @pl.when(pl.program_id(0) == 0)
    def _():
        cnt_ref[...] = cnt_in_ref[...]

    h2 = h_ref[...] + _bdot(m_ref[...], wo_ref[...])
    h2_ref[...] = h2
    ms = jnp.mean(h2 * h2, axis=-1, keepdims=True)
    xn = (h2 * lax.rsqrt(ms + EPS)) * g_ref[...]
    _store_slabs(xn_ref, xn)
    lg = _bdot(xn.astype(BF16), wr_ref[...]) + br_ref[...]

    lane = lax.broadcasted_iota(jnp.int32, (tm, ROUTER_PAD), 1)
    is_g = lane < N_GROUPS
    p_grp, grp = _lane_argmax(_masked_softmax(lg, is_g), is_g, lane)
    ex = lane - N_GROUPS
    is_e = jnp.logical_and(jnp.logical_and(ex >= 0, ex < N_EXPERTS), (ex >> 3) == grp)
    pe = _masked_softmax(lg, is_e)
    p1, i1 = _lane_argmax(pe, is_e, lane)
    rest = jnp.logical_and(is_e, lane != i1)
    p2, i2 = _lane_argmax(pe, rest, lane)
    den = p1 + p2
    gate_ref[...] = jnp.where(lane == 0, (p_grp * p1) / den, jnp.where(lane == 1, (p_grp * p2) / den, 0.0))

    e1 = i1 - N_GROUPS
    e2 = i2 - N_GROUPS
    oh1 = lane == e1
    oh2 = lane == e2
    ri = lax.broadcasted_iota(jnp.int32, (tm, tm), 0)
    ci = lax.broadcasted_iota(jnp.int32, (tm, tm), 1)
    below = (ri > ci).astype(BF16)
    f1 = oh1.astype(F32)
    f2 = oh2.astype(F32)
    tot1 = jnp.sum(f1, 0, keepdims=True)
    run = cnt_ref[...]
    before1 = run + _bdot(below, oh1.astype(BF16))
    before2 = run + tot1 + _bdot(below, oh2.astype(BF16))
    rank1 = jnp.sum(f1 * before1, -1, keepdims=True).astype(jnp.int32)
    rank2 = jnp.sum(f2 * before2, -1, keepdims=True).astype(jnp.int32)
    cnt_ref[...] = run + tot1 + jnp.sum(f2, 0, keepdims=True)
    route_ref[...] = jnp.where(lane == 0, e1, jnp.where(lane == 1, e2, jnp.where(
        lane == 2, rank1, jnp.where(lane == 3, rank2, 0))))


def _outproj(merged, h, wo, gain2, w_router, b_router, counts_in, tm):
    m = h.shape[0]
    assert m % tm == 0
    row = lambda n: pl.BlockSpec((1, n), lambda i: (0, 0))
    tile = lambda n: pl.BlockSpec((tm, n), lambda i: (i, 0))
    return pl.pallas_call(
        _outproj_kernel,
        grid=(m // tm,),
        in_specs=[
            tile(D_MODEL), tile(D_MODEL),
            pl.BlockSpec((D_MODEL, D_MODEL), lambda i: (0, 0)),
            row(D_MODEL),
            pl.BlockSpec((D_MODEL, ROUTER_PAD), lambda i: (0, 0)),
            row(ROUTER_PAD), row(ROUTER_PAD),
        ],
        out_specs=[tile(D_MODEL), pl.BlockSpec((tm, ROW_SLABS, SLAB), lambda i: (i, 0, 0)),
                   tile(ROUTER_PAD), tile(ROUTER_PAD), row(ROUTER_PAD)],
        out_shape=[
            jax.ShapeDtypeStruct((m, D_MODEL), F32),
            jax.ShapeDtypeStruct((m, ROW_SLABS, SLAB), F32),
            jax.ShapeDtypeStruct((m, ROUTER_PAD), F32),
            jax.ShapeDtypeStruct((m, ROUTER_PAD), jnp.int32),
            jax.ShapeDtypeStruct((1, ROUTER_PAD), F32),
        ],
        compiler_params=pltpu.CompilerParams(
            dimension_semantics=("arbitrary",), vmem_limit_bytes=VMEM_LIMIT),
        name="outproj",
    )(merged, h, wo, gain2, w_router, b_router, counts_in)


def _moe_kernel(blk_e_ref, nused_ref, x_ref, wg_ref, wu_ref, wd_ref, o_ref, wgb, wub, wdb):
    i = pl.program_id(0)
    e = blk_e_ref[i]
    e_prev = blk_e_ref[jnp.maximum(i - 1, 0)]
    used = i < nused_ref[0]

    @pl.when(jnp.logical_and(used, jnp.logical_or(i == 0, e != e_prev)))
    def _():
        def cast_in(r, c):
            sl = pl.ds(pl.multiple_of(r * 256, 256), 256)
            wgb[sl, :] = wg_ref[sl, :].astype(BF16)
            wub[sl, :] = wu_ref[sl, :].astype(BF16)
            return c

        lax.fori_loop(0, D_MODEL // 256, cast_in, 0)

        def cast_out(r, c):
            sl = pl.ds(pl.multiple_of(r * 128, 128), 128)
            wdb[sl, :] = wd_ref[sl, :].astype(BF16)
            return c

        lax.fori_loop(0, D_EXPERT // 128, cast_out, 0)

    @pl.when(used)
    def _():
        x = _load_slabs(x_ref).astype(BF16)
        g = _bdot(x, wgb[...])
        u = _bdot(x, wub[...])
        hb = (g * jax.nn.sigmoid(g)) * u
        _store_slabs(o_ref, _bdot(hb.astype(BF16), wdb[...]))

    @pl.when(jnp.logical_not(used))
    def _():
        o_ref[...] = jnp.zeros_like(o_ref)


def _moe_ffn(xs, blk_e, n_used, w_gate, w_up, w_down):
    p = xs.shape[0]
    nb = p // MOE_ROWS
    grid_spec = pltpu.PrefetchScalarGridSpec(
        num_scalar_prefetch=2,
        grid=(nb,),
        in_specs=[
            pl.BlockSpec((MOE_ROWS, ROW_SLABS, SLAB), lambda i, be, nu: (jnp.minimum(i, nu[0] - 1), 0, 0)),
            pl.BlockSpec((None, D_MODEL, D_EXPERT), lambda i, be, nu: (be[i], 0, 0)),
            pl.BlockSpec((None, D_MODEL, D_EXPERT), lambda i, be, nu: (be[i], 0, 0)),
            pl.BlockSpec((None, D_EXPERT, D_MODEL), lambda i, be, nu: (be[i], 0, 0)),
        ],
        out_specs=pl.BlockSpec((MOE_ROWS, ROW_SLABS, SLAB), lambda i, be, nu: (i, 0, 0)),
        scratch_shapes=[
            pltpu.VMEM((D_MODEL, D_EXPERT), BF16),
            pltpu.VMEM((D_MODEL, D_EXPERT), BF16),
            pltpu.VMEM((D_EXPERT, D_MODEL), BF16),
        ],
    )
    return pl.pallas_call(
        _moe_kernel,
        grid_spec=grid_spec,
        out_shape=jax.ShapeDtypeStruct((p, ROW_SLABS, SLAB), F32),
        compiler_params=pltpu.CompilerParams(
            dimension_semantics=("arbitrary",), vmem_limit_bytes=VMEM_LIMIT),
        name="moe_ffn",
    )(blk_e, n_used, xs, w_gate, w_up, w_down)


HALF_TILE = 64
SWA_HEADS_PER_STAGE = 16
LANES = 128


def _dot_exact_rhs(a, b01):
    a1 = a.astype(BF16)
    r1 = a - a1.astype(F32)
    a2 = r1.astype(BF16)
    a3 = (r1 - a2.astype(F32)).astype(BF16)
    return _bdot(a1, b01) + (_bdot(a2, b01) + _bdot(a3, b01))


def _group_rms_rope(x, gain, cos128, sin128, bd, bdt):
    r, w = x.shape
    ssq = _dot_exact_rhs(x * x, bd)
    rs = lax.rsqrt(ssq * (1.0 / B_HD) + EPS)
    y = (x * _dot_exact_rhs(rs, bdt)) * gain
    reps = w // LANES
    cosw = jnp.concatenate([cos128] * reps, 1)
    sinw = jnp.concatenate([sin128] * reps, 1)
    lane = lax.broadcasted_iota(jnp.int32, (r, w), 1)
    swapped = jnp.where((lane & (B_HD // 2)) == 0,
                        pltpu.roll(y, w - B_HD // 2, 1), pltpu.roll(y, B_HD // 2, 1))
    return y * cosw + swapped * sinw


def _kv_tiles(tile, odd, lo):
    rolled = pltpu.roll(tile, HALF_TILE, 1)
    dup = jnp.where(lo, rolled, tile) if odd else jnp.where(lo, tile, rolled)
    return dup.astype(BF16), jnp.where(lo, dup, 0.0).astype(BF16), jnp.where(lo, 0.0, dup).astype(BF16)


def _nt(a, b):
    return lax.dot_general(a, b, (((1,), (1,)), ((), ())), preferred_element_type=F32)


def _sink_softmax_pv(score_lists, value_lists, sinks):
    ms = []
    for scores, sink in zip(score_lists, sinks):
        m = jnp.max(scores[0], -1, keepdims=True)
        for s in scores[1:]:
            m = jnp.maximum(m, jnp.max(s, -1, keepdims=True))
        ms.append(jnp.maximum(m, sink))
    es = [[jnp.exp(s - m) for s in scores] for scores, m in zip(score_lists, ms)]
    dens = []
    for e_blocks, m, sink in zip(es, ms, sinks):
        den = jnp.exp(sink - m)
        for e in e_blocks:
            den = den + jnp.sum(e, -1, keepdims=True)
        dens.append(den)
    pvs = [[_bdot(e.astype(BF16), v) for e, v in zip(e_blocks, values)]
           for e_blocks, values in zip(es, value_lists)]
    outs = []
    for pv, den in zip(pvs, dens):
        acc = pv[0]
        for x in pv[1:]:
            acc = acc + x
        outs.append(acc / den)
    return outs


def _rope_small_kernel(q_ref, k_ref, cos_ref, sin_ref, gq_ref, gk_ref, bd_ref, bdt_ref, qo_ref, ko_ref):
    cos = cos_ref[...]
    sin = sin_ref[...]
    qo_ref[...] = _group_rms_rope(q_ref[...], gq_ref[...], cos, sin, bd_ref[...], bdt_ref[...])
    ko_ref[...] = _group_rms_rope(k_ref[...], gk_ref[...], cos, sin, bd_ref[:B_KV, :], bdt_ref[:, :B_KV])


def _rope_small(proj, cos, sin, gq, gk, bd, bdt):
    m = proj.shape[0]
    full = lambda shape: pl.BlockSpec(shape, lambda i: (0, 0))
    return pl.pallas_call(
        _rope_small_kernel,
        grid=(1,),
        in_specs=[
            pl.BlockSpec((m, B_Q), lambda i: (0, COL_QB // B_Q)),
            pl.BlockSpec((m, B_KV), lambda i: (0, COL_KB // B_KV)),
            full((m, LANES)), full((m, LANES)), full((1, B_Q)), full((1, B_KV)),
            full((B_Q, LANES)), full((LANES, B_Q)),
        ],
        out_specs=[full((m, B_Q)), full((m, B_KV))],
        out_shape=[jax.ShapeDtypeStruct((m, B_Q), F32), jax.ShapeDtypeStruct((m, B_KV), F32)],
        compiler_params=pltpu.CompilerParams(
            dimension_semantics=("arbitrary",), vmem_limit_bytes=VMEM_LIMIT),
        name="rope_small",
    )(proj, proj, cos, sin, gq, gk, bd, bdt)


def _swa_prompt_kernel(sinks_ref, q_ref, k_ref, v_ref, cos_ref, sin_ref, km_ref, vm_ref, gq_ref, gk_ref,
                       bd_ref, bdt_ref, y_ref, kc_ref, vc_ref, kprev, vloprev, vhiprev):
    n = pl.program_id(1)

    @pl.when(n == 0)
    def _():
        kprev[...] = jnp.zeros_like(kprev)
        vloprev[...] = jnp.zeros_like(vloprev)
        vhiprev[...] = jnp.zeros_like(vhiprev)

    cos = cos_ref[...]
    sin = sin_ref[...]
    q = _group_rms_rope(q_ref[...], gq_ref[...], cos, sin, bd_ref[...], bdt_ref[...])
    k = _group_rms_rope(k_ref[...], gk_ref[...], cos, sin, bd_ref[:B_KV, :], bdt_ref[:, :B_KV])
    v = v_ref[...]
    kc_ref[...] = k
    vc_ref[...] = v
    km = km_ref[...]
    vm = vm_ref[...]

    qi = lax.broadcasted_iota(jnp.int32, (WINDOW, WINDOW), 0)
    kj = lax.broadcasted_iota(jnp.int32, (WINDOW, WINDOW), 1)
    cur_ok = kj <= qi
    prev_ok = jnp.logical_and(kj > qi, n > 0)
    lo = lax.broadcasted_iota(jnp.int32, (WINDOW, LANES), 1) < HALF_TILE
    lo_m = lax.broadcasted_iota(jnp.int32, (N_META, LANES), 1) < HALF_TILE
    scale = B_HD ** -0.5

    kv = []
    for g in range(B_KV_HEADS):
        tl = slice((g // 2) * LANES, (g // 2 + 1) * LANES)
        k2c, _, _ = _kv_tiles(k[:, tl], g % 2, lo)
        _, vlo_c, vhi_c = _kv_tiles(v[:, tl], g % 2, lo)
        k2m, _, _ = _kv_tiles(km[:, tl], g % 2, lo_m)
        _, vlo_m, vhi_m = _kv_tiles(vm[:, tl], g % 2, lo_m)
        kv.append(((k2m, kprev[g], k2c), (vlo_m, vloprev[g], vlo_c), (vhi_m, vhiprev[g], vhi_c)))
    heads = range(B_HEADS)
    qms = []
    for h in heads:
        qt = q[:, (h // 2) * LANES:(h // 2 + 1) * LANES]
        qms.append((jnp.where(lo, qt, 0.0) if h % 2 == 0 else jnp.where(lo, 0.0, qt)).astype(BF16))
    for h0 in range(0, B_HEADS, SWA_HEADS_PER_STAGE):
        hg = range(h0, h0 + SWA_HEADS_PER_STAGE)
        raw = [[_nt(qms[h], kk) * scale for kk in kv[h // B_GROUP][0]] for h in hg]
        scores = [[sm, jnp.where(prev_ok, sp, -jnp.inf), jnp.where(cur_ok, sc, -jnp.inf)] for sm, sp, sc in raw]
        outs = _sink_softmax_pv(scores, [kv[h // B_GROUP][1 + h % 2] for h in hg], [sinks_ref[h] for h in hg])
        for t in range(SWA_HEADS_PER_STAGE // 2):
            y_ref[:, (h0 // 2 + t) * LANES:(h0 // 2 + t + 1) * LANES] = (outs[2 * t] + outs[2 * t + 1]).astype(BF16)
    for g in range(B_KV_HEADS):
        kprev[g] = kv[g][0][2]
        vloprev[g] = kv[g][1][2]
        vhiprev[g] = kv[g][2][2]


def _swa_prompt(proj, cos, sin, k_meta, v_meta, gq, gk, bd, bdt, sinks, nb, nblk):
    rows = nb * nblk * WINDOW
    assert proj.shape[0] == rows
    return pl.pallas_call(
        _swa_prompt_kernel,
        grid=(nb, nblk),
        in_specs=[
            pl.BlockSpec(memory_space=pltpu.SMEM),
            pl.BlockSpec((WINDOW, B_Q), lambda b, n: (b * nblk + n, COL_QB // B_Q)),
            pl.BlockSpec((WINDOW, B_KV), lambda b, n: (b * nblk + n, COL_KB // B_KV)),
            pl.BlockSpec((WINDOW, B_KV), lambda b, n: (b * nblk + n, COL_VB // B_KV)),
            pl.BlockSpec((WINDOW, LANES), lambda b, n: (n, 0)),
            pl.BlockSpec((WINDOW, LANES), lambda b, n: (n, 0)),
            pl.BlockSpec((N_META, B_KV), lambda b, n: (0, 0)),
            pl.BlockSpec((N_META, B_KV), lambda b, n: (0, 0)),
            pl.BlockSpec((1, B_Q), lambda b, n: (0, 0)),
            pl.BlockSpec((1, B_KV), lambda b, n: (0, 0)),
            pl.BlockSpec((B_Q, LANES), lambda b, n: (0, 0)),
            pl.BlockSpec((LANES, B_Q), lambda b, n: (0, 0)),
        ],
        out_specs=[
            pl.BlockSpec((WINDOW, B_Q), lambda b, n: (b * nblk + n, 0)),
            pl.BlockSpec((None, WINDOW, B_KV), lambda b, n: (b, 0, 0)),
            pl.BlockSpec((None, WINDOW, B_KV), lambda b, n: (b, 0, 0)),
        ],
        scratch_shapes=[pltpu.VMEM((B_KV_HEADS, WINDOW, LANES), BF16)] * 3,
        out_shape=[
            jax.ShapeDtypeStruct((rows, B_Q), BF16),
            jax.ShapeDtypeStruct((nb, WINDOW, B_KV), F32),
            jax.ShapeDtypeStruct((nb, WINDOW, B_KV), F32),
        ],
        compiler_params=pltpu.CompilerParams(
            dimension_semantics=("arbitrary", "arbitrary"), vmem_limit_bytes=VMEM_LIMIT),
        name="swa_prompt",
    )(sinks, proj, proj, proj, cos, sin, k_meta, v_meta, gq, gk, bd, bdt)


def _swa_sample_kernel(sinks_ref, q_ref, k_ref, v_ref, y_ref, *, n_keys, n_new):
    q = q_ref[...]
    k = k_ref[...]
    v = v_ref[...]
    rq = q.shape[0]
    t = lax.broadcasted_iota(jnp.int32, (rq, n_keys), 0)
    r = lax.broadcasted_iota(jnp.int32, (rq, n_keys), 1)
    wj = r - N_META
    win_pos = PAST_LEN - WINDOW + wj
    nm = r - N_META - WINDOW
    ok = ((r < N_META)
          | ((wj >= 0) & (wj < WINDOW) & (win_pos >= N_META) & (wj >= t + 1))
          | ((nm >= 0) & (nm <= t) & (nm > t - WINDOW) & (nm < n_new)))
    lo_q = lax.broadcasted_iota(jnp.int32, (rq, LANES), 1) < HALF_TILE
    lo_k = lax.broadcasted_iota(jnp.int32, (n_keys, LANES), 1) < HALF_TILE
    scale = B_HD ** -0.5
    kv = []
    for g in range(B_KV_HEADS):
        tl = slice((g // 2) * LANES, (g // 2 + 1) * LANES)
        k2, _, _ = _kv_tiles(k[:, tl], g % 2, lo_k)
        _, vlo, vhi = _kv_tiles(v[:, tl], g % 2, lo_k)
        kv.append((k2, vlo, vhi))
    heads = range(B_HEADS)
    qms = []
    for h in heads:
        qt = q[:, (h // 2) * LANES:(h // 2 + 1) * LANES]
        qms.append((jnp.where(lo_q, qt, 0.0) if h % 2 == 0 else jnp.where(lo_q, 0.0, qt)).astype(BF16))
    scores = [[jnp.where(ok, _nt(qms[h], kv[h // B_GROUP][0]) * scale, -jnp.inf)] for h in heads]
    outs = _sink_softmax_pv(scores, [[kv[h // B_GROUP][1 + h % 2]] for h in heads],
                            [sinks_ref[h] for h in heads])
    for t in range(B_HEADS // 2):
        y_ref[:, t * LANES:(t + 1) * LANES] = (outs[2 * t] + outs[2 * t + 1]).astype(BF16)


def _swa_sample(q8, kk, vv, sinks, n_new):
    nb, rq, _ = q8.shape
    n_keys = kk.shape[1]
    return pl.pallas_call(
        functools.partial(_swa_sample_kernel, n_keys=n_keys, n_new=n_new),
        grid=(nb,),
        in_specs=[
            pl.BlockSpec(memory_space=pltpu.SMEM),
            pl.BlockSpec((None, rq, B_Q), lambda b: (b, 0, 0)),
            pl.BlockSpec((None, n_keys, B_KV), lambda b: (b, 0, 0)),
            pl.BlockSpec((None, n_keys, B_KV), lambda b: (b, 0, 0)),
        ],
        out_specs=pl.BlockSpec((None, rq, B_Q), lambda b: (b, 0, 0)),
        out_shape=jax.ShapeDtypeStruct((nb, rq, B_Q), BF16),
        compiler_params=pltpu.CompilerParams(
            dimension_semantics=("arbitrary",), vmem_limit_bytes=VMEM_LIMIT),
        name="swa_sample",
    )(sinks, q8, kk, vv)


def _rope_tables(pos):
    half = B_HD // 2
    inv_freq = ROPE_THETA ** (-jnp.arange(half, dtype=F32) / half)
    ang = pos.astype(F32)[:, None] * inv_freq[None, :]
    cos, sin = jnp.cos(ang), jnp.sin(ang)
    return jnp.concatenate([cos, cos, cos, cos], 1), jnp.concatenate([-sin, sin, -sin, sin], 1)


ROUTE_ROWS = 256
DMA_UNROLL = 8


def _row(ref, r):
    return ref.at[pl.ds(r, 1)]


def _dispatch_kernel(pend_ref, dest_ref, xa_ref, xb_ref, xs_hbm, zeros_vmem, ring, sem_zero, sem, *, nta, nt, nb):
    i = pl.program_id(0)
    tm = ROUTE_ROWS
    slot = i % 2

    def drain(s):
        for _k in range(TOP_K):
            pltpu.make_async_copy(ring.at[s], xs_hbm.at[pl.ds(0, tm)], sem.at[s]).wait()

    @pl.when(i == 0)
    def _():
        zeros_vmem[...] = jnp.zeros_like(zeros_vmem)
        fill = lambda row0: pltpu.make_async_copy(zeros_vmem, xs_hbm.at[pl.ds(row0, MOE_ROWS)], sem_zero)
        for e in range(N_EXPERTS):
            fill(jnp.maximum(pend_ref[e] - MOE_ROWS, 0)).start()
        for e in range(N_EXPERTS):
            fill(0).wait()
        n_used = pend_ref[N_EXPERTS - 1] // MOE_ROWS

        def fill_tail(b, c):
            fill(b * MOE_ROWS).start()
            fill(0).wait()
            return c

        lax.fori_loop(n_used, nb, fill_tail, 0)

    @pl.when(i >= 2)
    def _():
        drain(slot)

    @pl.when(i < nta)
    def _():
        ring[slot] = xa_ref[...]

    @pl.when(i >= nta)
    def _():
        ring[slot] = xb_ref[...]

    def body(r, c):
        src = ring.at[slot, pl.ds(r, 1)]
        for k in range(TOP_K):
            pltpu.make_async_copy(src, _row(xs_hbm, dest_ref[0, TOP_K * r + k]), sem.at[slot]).start(priority=k)
        return c

    lax.fori_loop(0, tm, body, 0, unroll=DMA_UNROLL)

    @pl.when(i == nt - 1)
    def _():
        if nt >= 2:
            drain(1 - slot)
        drain(slot)


def _dispatch(xa, xb, dest, pend, n_slots):
    assert xa.shape[0] % ROUTE_ROWS == 0 and xb.shape[0] % ROUTE_ROWS == 0 and MOE_ROWS == ROUTE_ROWS
    nta = xa.shape[0] // ROUTE_ROWS
    nt = nta + xb.shape[0] // ROUTE_ROWS
    slab_rows = (ROUTE_ROWS, ROW_SLABS, SLAB)
    grid_spec = pltpu.PrefetchScalarGridSpec(
        num_scalar_prefetch=1,
        grid=(nt,),
        in_specs=[
            pl.BlockSpec((None, 1, TOP_K * ROUTE_ROWS), lambda i, pe: (i, 0, 0), memory_space=pltpu.SMEM),
            pl.BlockSpec(slab_rows, lambda i, pe: (jnp.minimum(i, nta - 1), 0, 0)),
            pl.BlockSpec(slab_rows, lambda i, pe: (jnp.maximum(i - nta, 0), 0, 0)),
        ],
        out_specs=pl.BlockSpec(memory_space=pl.ANY),
        scratch_shapes=[
            pltpu.VMEM(slab_rows, F32),
            pltpu.VMEM((2,) + slab_rows, F32),
            pltpu.SemaphoreType.DMA(()),
            pltpu.SemaphoreType.DMA((2,)),
        ],
    )
    return pl.pallas_call(
        functools.partial(_dispatch_kernel, nta=nta, nt=nt, nb=n_slots // MOE_ROWS),
        grid_spec=grid_spec,
        out_shape=jax.ShapeDtypeStruct((n_slots, ROW_SLABS, SLAB), F32),
        compiler_params=pltpu.CompilerParams(
            dimension_semantics=("arbitrary",), vmem_limit_bytes=VMEM_LIMIT, has_side_effects=True),
        name="moe_dispatch",
    )(pend, dest.reshape(nt, 1, TOP_K * ROUTE_ROWS), xa, xb)


def _combine_kernel(dest_ref, dest_next_ref, gate_ref, h_ref, ys_hbm, o_ref, buf, sem, *, nt):
    i = pl.program_id(0)
    tm = ROUTE_ROWS
    slot = i % 2

    def issue(dref, s):
        def body(r, c):
            for k in range(TOP_K):
                pltpu.make_async_copy(_row(ys_hbm, dref[0, TOP_K * r + k]), buf.at[s, k, pl.ds(r, 1)],
                                      sem.at[s]).start(priority=k)
            return c

        lax.fori_loop(0, tm, body, 0, unroll=DMA_UNROLL)

    @pl.when(i == 0)
    def _():
        issue(dest_ref, 0)

    @pl.when(i < nt - 1)
    def _():
        issue(dest_next_ref, 1 - slot)

    for k in range(TOP_K):
        pltpu.make_async_copy(ys_hbm.at[pl.ds(0, tm)], buf.at[slot, k], sem.at[slot]).wait()
    g = gate_ref[...]
    for s in range(ROW_SLABS):
        cols = slice(s * SLAB, (s + 1) * SLAB)
        o_ref[:, cols] = h_ref[:, cols] + (g[:, 0:1] * buf[slot, 0, :, s, :] + g[:, 1:2] * buf[slot, 1, :, s, :])


def _combine(h2, gates, dest, ys):
    t = h2.shape[0]
    assert t % ROUTE_ROWS == 0
    nt = t // ROUTE_ROWS
    dest3 = dest.reshape(nt, 1, TOP_K * ROUTE_ROWS)
    idx = lambda f: pl.BlockSpec((None, 1, TOP_K * ROUTE_ROWS), f, memory_space=pltpu.SMEM)
    return pl.pallas_call(
        functools.partial(_combine_kernel, nt=nt),
        grid=(nt,),
        in_specs=[
            idx(lambda i: (i, 0, 0)),
            idx(lambda i: (jnp.minimum(i + 1, nt - 1), 0, 0)),
            pl.BlockSpec((ROUTE_ROWS, ROUTER_PAD), lambda i: (i, 0)),
            pl.BlockSpec((ROUTE_ROWS, D_MODEL), lambda i: (i, 0)),
            pl.BlockSpec(memory_space=pl.ANY),
        ],
        out_specs=pl.BlockSpec((ROUTE_ROWS, D_MODEL), lambda i: (i, 0)),
        out_shape=jax.ShapeDtypeStruct((t, D_MODEL), F32),
        scratch_shapes=[
            pltpu.VMEM((2, TOP_K, ROUTE_ROWS, ROW_SLABS, SLAB), F32),
            pltpu.SemaphoreType.DMA((2,)),
        ],
        compiler_params=pltpu.CompilerParams(
            dimension_semantics=("arbitrary",), vmem_limit_bytes=VMEM_LIMIT),
        name="moe_combine",
    )(dest3, dest3, gates, h2, ys)


def _moe_plan(route, counts):
    T = route.shape[0]
    expert = route[:, :TOP_K]
    rank = route[:, TOP_K:2 * TOP_K]
    nb = -(-(T * TOP_K) // MOE_ROWS) + N_EXPERTS
    cnt = counts[0, :N_EXPERTS].astype(jnp.int32)
    padded = (cnt + MOE_ROWS - 1) // MOE_ROWS * MOE_ROWS
    pend = jnp.cumsum(padded).astype(jnp.int32)
    dest = (pend - padded)[expert] + rank
    starts = jnp.arange(nb, dtype=jnp.int32) * MOE_ROWS
    blk_e = jnp.minimum(jnp.sum((pend[None, :] <= starts[:, None]).astype(jnp.int32), 1), N_EXPERTS - 1)
    n_used = pend[-1:] // MOE_ROWS
    return dest, pend, blk_e, n_used, nb * MOE_ROWS


def _lane_row(vals, lane0):
    return jnp.zeros((1, SMALL_PAD), F32).at[0, lane0:lane0 + vals.shape[0]].set(vals)


def kernel(x_prompt, x_sample, state_delta, state_conv, cache_swa_k, cache_swa_v, meta_tokens,
           norm1_w, w_in, conv_w, a_log, dt_bias, a_norm_w, w_up_a, q_norm_w, k_norm_w, sinks,
           w_up_b, w_o, norm2_w, w_router_group, b_router_group, w_router_expert, b_router_expert,
           w_gate, w_up, w_down):
    Bp, Sp, _ = x_prompt.shape
    Bs, Ss, _ = x_sample.shape
    n_s = Bs * Ss
    l = 0
    w = w_in[l]
    offs = [0]
    for s in (A_CONV_DIM, A_HEADS, A_HEADS, A_V, B_Q, B_KV, B_KV, D_MODEL, D_MODEL):
        offs.append(offs[-1] + s)
    seg = lambda a: w[:, offs[a]:offs[a + 1]]
    w_main = jnp.concatenate([seg(7), seg(8), seg(0), seg(3), seg(4), seg(5), seg(6)], 1).astype(BF16)
    w_small = jnp.concatenate(
        [seg(1), seg(2), jnp.zeros((D_MODEL, SMALL_PAD - 2 * A_HEADS), F32)], 1).astype(BF16)
    wa = w_up_a[l].astype(BF16)
    wb = w_up_b[l].astype(BF16)
    wo = w_o[l].astype(BF16)
    w_router = jnp.concatenate(
        [w_router_group[l], w_router_expert[l],
         jnp.zeros((D_MODEL, ROUTER_PAD - N_GROUPS - N_EXPERTS), F32)], 1).astype(BF16)
    b_router = jnp.concatenate(
        [b_router_group[l], b_router_expert[l],
         jnp.zeros((ROUTER_PAD - N_GROUPS - N_EXPERTS,), F32)])[None, :]
    g1 = norm1_w[l][None, :]
    g2 = norm2_w[l][None, :]
    alog_row = _lane_row(a_log[l], LANE_DECAY)
    dtb_row = _lane_row(dt_bias[l], LANE_DECAY)
    anw_row = a_norm_w[l][None, :]

    xp = x_prompt.reshape(Bp * Sp, D_MODEL)
    xs = x_sample.reshape(n_s, D_MODEL)
    x_small = jnp.concatenate([xs, meta_tokens], 0)
    n_small = x_small.shape[0]

    proj_p, small_p = _inproj(xp, g1, w_main, w_small, tm=1024)
    proj_s, small_s = _inproj(x_small, g1, w_main, w_small, tm=n_small)

    dm = functools.partial(_delta_mixer, conv_w=conv_w[l], alog_row=alog_row, dtb_row=dtb_row, anw_row=anw_row)
    pad_rows = lambda a, n: jnp.concatenate([a, jnp.zeros((n - a.shape[0],) + a.shape[1:], a.dtype)], 0)
    meta_proj = pad_rows(proj_s[n_s:], A_CHUNK)
    meta_small = pad_rows(small_s[n_s:], A_CHUNK)
    zero_prev = jnp.zeros((1, SUBLANES, A_CONV_DIM), F32)
    zero_state = jnp.zeros((1, A_HEADS, A_DK, A_DV), F32)
    _, s_meta = dm(meta_proj, meta_small, zero_prev, zero_state, nb=1, nc=1, C=A_CHUNK, n_valid=N_META,
                   shared_first=True)
    meta_tail = proj_s[n_s + N_META - SUBLANES:, COL_QKV:COL_QKV + A_CONV_DIM][None]
    ya_p, sdelta_p = dm(proj_p, small_p, meta_tail, s_meta, nb=Bp, nc=Sp // A_CHUNK, C=A_CHUNK,
                        n_valid=A_CHUNK, shared_first=True, nsub=DELTA_CHUNKS_PER_STEP)
    conv_p = proj_p.reshape(Bp, Sp, PROJ_MAIN)[:, Sp - (A_CONV - 1):, COL_QKV:COL_QKV + A_CONV_DIM]
    CS = SUBLANES
    samp_proj = jnp.pad(proj_s[:n_s].reshape(Bs, Ss, PROJ_MAIN), ((0, 0), (0, CS - Ss), (0, 0)))
    samp_small = jnp.pad(small_s[:n_s].reshape(Bs, Ss, SMALL_PAD), ((0, 0), (0, CS - Ss), (0, 0)))
    samp_prev = jnp.pad(state_conv[l], ((0, 0), (SUBLANES - (A_CONV - 1), 0), (0, 0)))
    ya_s8, sdelta_s = dm(samp_proj.reshape(Bs * CS, PROJ_MAIN), samp_small.reshape(Bs * CS, SMALL_PAD),
                         samp_prev, state_delta[l], nb=Bs, nc=1, C=CS, n_valid=Ss, shared_first=False)
    ya_s = ya_s8.reshape(Bs, CS, A_V)[:, :Ss].reshape(n_s, A_V)
    qkv_s = proj_s[:n_s, COL_QKV:COL_QKV + A_CONV_DIM].reshape(Bs, Ss, A_CONV_DIM)
    conv_s = jnp.concatenate([state_conv[l], qkv_s], 1)[:, -(A_CONV - 1):]

    gq = jnp.tile(q_norm_w[l], B_HEADS)[None]
    gk = jnp.tile(k_norm_w[l], B_KV_HEADS)[None]
    bd = ((jnp.arange(B_Q) // B_HD)[:, None] == jnp.arange(LANES)[None, :]).astype(BF16)
    bdt = bd.T
    pos_small = jnp.concatenate([PAST_LEN + jnp.arange(n_s, dtype=jnp.int32) % Ss,
                                 jnp.arange(N_META, dtype=jnp.int32)])
    cos_s, sin_s = _rope_tables(pos_small)
    q_rot_s, k_rot_s = _rope_small(proj_s, cos_s, sin_s, gq, gk, bd, bdt)
    k_meta = k_rot_s[n_s:]
    v_meta = proj_s[n_s:, COL_VB:COL_VB + B_KV]
    cos_p, sin_p = _rope_tables(N_META + jnp.arange(Sp, dtype=jnp.int32))
    yb_p, k_last, v_last = _swa_prompt(proj_p, cos_p, sin_p, k_meta, v_meta, gq, gk, bd, bdt, sinks[l],
                                       Bp, Sp // WINDOW)
    cache_shape = (N_META + WINDOW, B_KV_HEADS, B_HD)
    bcast_meta = lambda a: jnp.broadcast_to(a[None], (Bp, N_META, B_KV))
    swk_p = jnp.concatenate([bcast_meta(k_meta), k_last], 1).reshape((Bp,) + cache_shape)
    swv_p = jnp.concatenate([bcast_meta(v_meta), v_last], 1).reshape((Bp,) + cache_shape)
    n_cache = N_META + WINDOW
    key_pad = -(n_cache + Ss) % SUBLANES
    zpad = jnp.zeros((Bs, key_pad, B_KV), F32)
    kk = jnp.concatenate([cache_swa_k[l].reshape(Bs, n_cache, B_KV), k_rot_s[:n_s].reshape(Bs, Ss, B_KV), zpad], 1)
    vv = jnp.concatenate([cache_swa_v[l].reshape(Bs, n_cache, B_KV),
                          proj_s[:n_s, COL_VB:COL_VB + B_KV].reshape(Bs, Ss, B_KV), zpad], 1)
    q8 = jnp.pad(q_rot_s[:n_s].reshape(Bs, Ss, B_Q), ((0, 0), (0, SUBLANES - Ss), (0, 0)))
    yb_s = _swa_sample(q8, kk, vv, sinks[l], n_new=Ss)[:, :Ss].reshape(n_s, B_Q)
    new_cache = lambda t: jnp.concatenate(
        [t[:, :N_META], t[:, n_cache + Ss - WINDOW:n_cache + Ss]], 1).reshape((Bs,) + cache_shape)
    swk_s, swv_s = new_cache(kk), new_cache(vv)

    merged_p = _merge(ya_p, yb_p, proj_p, wa, wb, tm=512)
    merged_s = _merge(ya_s, yb_s, proj_s[:n_s], wa, wb, tm=256)
    cnt0 = jnp.zeros((1, ROUTER_PAD), F32)
    h2_p, xn2_p, gt_p, rt_p, cnt_p = _outproj(merged_p, xp, wo, g2, w_router, b_router, cnt0, tm=256)
    h2_s, xn2_s, gt_s, rt_s, cnt = _outproj(merged_s, xs, wo, g2, w_router, b_router, cnt_p, tm=256)

    cat = lambda a, b: jnp.concatenate([a, b], 0)
    n_p = Bp * Sp
    dest, pend, blk_e, n_used, n_slots = _moe_plan(cat(rt_p, rt_s), cnt)
    xs_slots = _dispatch(xn2_p, xn2_s, dest, pend, n_slots)
    ys = _moe_ffn(xs_slots, blk_e, n_used, w_gate[l], w_up[l], w_down[l])
    y_prompt = _combine(h2_p, gt_p, dest[:n_p], ys).reshape(Bp, Sp, D_MODEL)
    y_sample = _combine(h2_s, gt_s, dest[n_p:], ys).reshape(Bs, Ss, D_MODEL)
    return (y_prompt, y_sample, sdelta_p[None], conv_p[None], swk_p[None], swv_p[None],
            sdelta_s[None], conv_s[None], swk_s[None], swv_s[None])
```

```python
import functools

import jax
import jax.numpy as jnp
from jax import lax
from jax.experimental import pallas as pl
from jax.experimental.pallas import tpu as pltpu

F32 = jnp.float32
BF16 = jnp.bfloat16

D_MODEL = 2048
N_META = 16
A_HEADS = 8
A_DK = 128
A_DV = 128
A_CONV = 4
A_CHUNK = 64
A_QK = A_HEADS * A_DK
A_V = A_HEADS * A_DV
A_CONV_DIM = 2 * A_QK + A_V
B_HEADS = 16
B_KV_HEADS = 4
B_HD = 64
B_GROUP = B_HEADS // B_KV_HEADS
B_Q = B_HEADS * B_HD
B_KV = B_KV_HEADS * B_HD
WINDOW = 128
ROPE_THETA = 10000.0
PAST_LEN = 16384
N_GROUPS = 4
EXPERTS_PER_GROUP = 8
N_EXPERTS = N_GROUPS * EXPERTS_PER_GROUP
TOP_K = 2
D_EXPERT = 512
EPS = 1e-6

COL_GA = 0
COL_GB = COL_GA + D_MODEL
COL_QKV = COL_GB + D_MODEL
COL_Z = COL_QKV + A_CONV_DIM
COL_QB = COL_Z + A_V
COL_KB = COL_QB + B_Q
COL_VB = COL_KB + B_KV
PROJ_MAIN = COL_VB + B_KV
SMALL_PAD = 128
ROUTER_PAD = 128
LANE_BETA = 0
LANE_DECAY = A_HEADS

SUBLANES = 8
MOE_ROWS = 256
DELTA_CHUNKS_PER_STEP = 2
SAMPLE_SEQS_PER_STEP = 8
ROW_SLABS = 8
SLAB = D_MODEL // ROW_SLABS
VMEM_LIMIT = 56 * 1024 * 1024


def _row_chunk(tm):
    return 256 if tm % 256 == 0 else tm


def _bdot(a, b):
    return jnp.dot(a, b, preferred_element_type=F32)


def _store_slabs(ref3, val):
    for s in range(ROW_SLABS):
        ref3[:, s, :] = val[:, s * SLAB:(s + 1) * SLAB]


def _load_slabs(ref3):
    return jnp.concatenate([ref3[:, s, :] for s in range(ROW_SLABS)], 1)


def _inproj_kernel(x_ref, g_ref, w_ref, ws_ref, o_ref, os_ref, xn_ref, *, tm):
    rc = _row_chunk(tm)

    @pl.when(pl.program_id(1) == 0)
    def _():
        def norm_rows(sl):
            x = x_ref[sl, :]
            ms = jnp.mean(x * x, axis=-1, keepdims=True)
            xn_ref[sl, :] = ((x * lax.rsqrt(ms + EPS)) * g_ref[...]).astype(BF16)

        def body(r, c):
            norm_rows(pl.ds(pl.multiple_of(r * rc, rc), rc))
            return c

        if tm == rc:
            norm_rows(pl.ds(0, tm))
        else:
            lax.fori_loop(0, tm // rc, body, 0)
        os_ref[...] = _bdot(xn_ref[...], ws_ref[...])

    o_ref[...] = _bdot(xn_ref[...], w_ref[...])


def _inproj(x, gain, w_main, w_small, tm, tn=512):
    m = x.shape[0]
    assert m % tm == 0 and PROJ_MAIN % tn == 0
    return pl.pallas_call(
        functools.partial(_inproj_kernel, tm=tm),
        grid=(m // tm, PROJ_MAIN // tn),
        in_specs=[
            pl.BlockSpec((tm, D_MODEL), lambda i, j: (i, 0)),
            pl.BlockSpec((1, D_MODEL), lambda i, j: (0, 0)),
            pl.BlockSpec((D_MODEL, tn), lambda i, j: (0, j)),
            pl.BlockSpec((D_MODEL, SMALL_PAD), lambda i, j: (0, 0)),
        ],
        out_specs=[
            pl.BlockSpec((tm, tn), lambda i, j: (i, j)),
            pl.BlockSpec((tm, SMALL_PAD), lambda i, j: (i, 0)),
        ],
        out_shape=[
            jax.ShapeDtypeStruct((m, PROJ_MAIN), F32),
            jax.ShapeDtypeStruct((m, SMALL_PAD), F32),
        ],
        scratch_shapes=[pltpu.VMEM((tm, D_MODEL), BF16)],
        compiler_params=pltpu.CompilerParams(
            dimension_semantics=("arbitrary", "arbitrary"), vmem_limit_bytes=VMEM_LIMIT),
        name="inproj",
    )(x, gain, w_main, w_small)


def _dot_exact_lhs(a_bf16, b):
    b1 = b.astype(BF16)
    r1 = b - b1.astype(F32)
    b2 = r1.astype(BF16)
    b3 = (r1 - b2.astype(F32)).astype(BF16)
    return _bdot(a_bf16, b1) + (_bdot(a_bf16, b2) + _bdot(a_bf16, b3))


def _unit_lower_inverse_offsets(ms, ri, ci, c):
    same = lambda n: (ri >> n) == (ci >> n)
    d = lambda a, b: _bdot(a.astype(BF16), b.astype(BF16))
    blk8 = same(3)
    n1 = [jnp.where(blk8, m, 0.0) for m in ms]
    n2 = [d(a, a) for a in n1]
    n3 = [d(a, b) for a, b in zip(n1, n2)]
    n4 = [d(b, b) for b in n2]
    qs = [(b - a) - t for a, b, t in zip(n1, n2, n3)]
    qs = [(q + f) + d(q, f) for q, f in zip(qs, n4)]
    lg = 3
    while (1 << lg) < c:
        sel = jnp.logical_and(same(lg + 1), jnp.logical_not(same(lg)))
        offs = [jnp.where(sel, m, 0.0) for m in ms]
        ts = [o + d(q, o) for q, o in zip(qs, offs)]
        qs = [(q - t) - d(t, q) for q, t in zip(qs, ts)]
        lg += 1
    return qs


def _delta_kernel(qc, kc, vc, zc, qp, kp, vp, qf, kf, vf, sm_ref, cw_ref, alog_ref, dtb_ref, anw_ref, s0_ref,
                  ya_ref, s_ref, *, C, n_valid, nsub, chained):
    R = nsub * C
    first = pl.program_id(1) == 0

    @pl.when(first)
    def _():
        s_ref[...] = s0_ref[...]

    def keep(a):
        if n_valid == C:
            return a
        rows = lax.broadcasted_iota(jnp.int32, (a.shape[0], 1), 0)
        return jnp.where((rows & (C - 1)) < n_valid, a, 0.0)

    ri = lax.broadcasted_iota(jnp.int32, (C, C), 0)
    ci = lax.broadcasted_iota(jnp.int32, (C, C), 1)
    lower = ri >= ci
    strict = ri > ci

    def conv_rows(prev, cur, col0):
        n = cur.shape[0]
        x = jnp.concatenate([prev, cur], 0)
        w = cw_ref[:, col0:col0 + A_QK]
        acc = x[SUBLANES:SUBLANES + n] * w[A_CONV - 1:A_CONV]
        for s in range(1, A_CONV):
            acc = acc + x[SUBLANES - s:SUBLANES - s + n] * w[A_CONV - 1 - s:A_CONV - s]
        return acc * jax.nn.sigmoid(acc)

    def conv(cur_ref, prev_ref, first_ref, col0):
        if chained:
            return conv_rows(jnp.where(first, first_ref[0], prev_ref[...]), cur_ref[...], col0)
        return jnp.concatenate(
            [conv_rows(first_ref[s], cur_ref[s * C:(s + 1) * C, :], col0) for s in range(nsub)], 0)

    qx = conv(qc, qp, qf, 0)
    kx = conv(kc, kp, kf, A_QK)
    vx = conv(vc, vp, vf, 2 * A_QK)

    sm = sm_ref[...]
    beta_all = keep(jax.nn.sigmoid(sm))
    g_all = keep(-jnp.exp(alog_ref[...]) * jax.nn.softplus(sm + dtb_ref[...]))
    rr = lax.broadcasted_iota(jnp.int32, (R, R), 0)
    rc = lax.broadcasted_iota(jnp.int32, (R, R), 1)
    sh = C.bit_length() - 1
    same_chunk_lower = jnp.logical_and(rr >= rc, (rr >> sh) == (rc >> sh))
    g_cum = _dot_exact_lhs(same_chunk_lower.astype(BF16), g_all)
    g_cum_t = g_cum.T
    e_g = jnp.exp(g_cum)

    heads = range(A_HEADS)
    subs = range(nsub)
    items = [(s, h) for s in subs for h in heads]
    hs = [slice(h * A_DK, (h + 1) * A_DK) for h in heads]
    rs = [slice(s * C, (s + 1) * C) for s in subs]
    ld = lambda h: slice(LANE_DECAY + h, LANE_DECAY + h + 1)
    qs = [qx[rs[s], hs[h]] for s, h in items]
    qs = [q * lax.rsqrt(jnp.sum(q * q, -1, keepdims=True) + EPS) * (A_DK ** -0.5) for q in qs]
    ks = [kx[rs[s], hs[h]] for s, h in items]
    ks = [keep(k * lax.rsqrt(jnp.sum(k * k, -1, keepdims=True) + EPS)) for k in ks]
    vs = [keep(vx[rs[s], hs[h]]) for s, h in items]
    betas = [beta_all[rs[s], LANE_BETA + h:LANE_BETA + h + 1] for s, h in items]
    egs = [e_g[rs[s], ld(h)] for s, h in items]
    gammas = [jnp.exp(jnp.where(lower, g_cum[rs[s], ld(h)] - g_cum_t[ld(h), rs[s]], -jnp.inf))
              for s, h in items]
    kbs = [k * b for k, b in zip(ks, betas)]
    a1s = [lax.dot_general(jnp.concatenate([kb, q], 0).astype(BF16), k.astype(BF16),
                           (((1,), (1,)), ((), ())), preferred_element_type=F32)
           for kb, q, k in zip(kbs, qs, ks)]
    ms = [jnp.where(strict, a1[:C] * gm, 0.0) for a1, gm in zip(a1s, gammas)]
    attns = [a1[C:] * gm for a1, gm in zip(a1s, gammas)]
    rhss = [jnp.concatenate([v * b, kb * eg], 1) for v, b, kb, eg in zip(vs, betas, kbs, egs)]
    sols = [r + _bdot(q.astype(BF16), r.astype(BF16))
            for q, r in zip(_unit_lower_inverse_offsets(ms, ri, ci, C), rhss)]
    wq_lhs = [jnp.concatenate([sol[:, A_DV:], q * eg], 0).astype(BF16) for sol, q, eg in zip(sols, qs, egs)]
    g_lasts = [g_cum[(s + 1) * C - 1:(s + 1) * C, :] for s in subs]
    kg_ts = [(k * jnp.exp(g_lasts[s][:, ld(h)] - g_cum[rs[s], ld(h)])).T
             for (s, h), k in zip(items, ks)]
    states = [s_ref[0, h] for h in heads]
    for s in subs:
        if not chained:
            states = [s_ref[s, h] for h in heads]
        at = lambda xs: xs[s * A_HEADS:(s + 1) * A_HEADS]
        wqs = [_bdot(l, st.astype(BF16)) for l, st in zip(at(wq_lhs), states)]
        v_news = [sol[:, :A_DV] - wq[:C] for sol, wq in zip(at(sols), wqs)]
        r2s = [_bdot(jnp.concatenate([a, kg], 0).astype(BF16), vn.astype(BF16))
               for a, kg, vn in zip(at(attns), at(kg_ts), v_news)]
        e_last = jnp.exp(g_lasts[s])
        states = [st * e_last[:, ld(h)] + r2[C:] for h, st, r2 in zip(heads, states, r2s)]
        for h in heads:
            o = wqs[h][C:] + r2s[h][:C]
            on = (o * lax.rsqrt(jnp.mean(o * o, -1, keepdims=True) + EPS)) * anw_ref[...]
            zh = zc[rs[s], hs[h]]
            ya_ref[rs[s], hs[h]] = (on * (zh * jax.nn.sigmoid(zh))).astype(BF16)
        if not chained:
            for h in heads:
                s_ref[s, h] = states[h]
    if chained:
        for h in heads:
            s_ref[0, h] = states[h]


def _delta_mixer(proj, small, first_prev, s0, conv_w, alog_row, dtb_row, anw_row, *, nb, nc, C, n_valid,
                 shared_first, nsub=1, chained=True):
    rows = nb * nc * C
    assert proj.shape[0] == rows and C % SUBLANES == 0 and C & (C - 1) == 0
    R = nsub * C
    if chained:
        assert nc % nsub == 0
        nseq, steps = 1, nc // nsub
    else:
        assert nc == 1 and nb % nsub == 0 and not shared_first
        nseq, steps, nb = nsub, 1, nb // nsub
    cpb = R // SUBLANES
    cq, ck, cv, cz = (COL_QKV // A_QK, COL_QKV // A_QK + 1, COL_QKV // A_QK + 2, COL_Z // A_V)
    fb = (lambda b: 0) if shared_first else (lambda b: b)
    cur = lambda col: pl.BlockSpec((R, A_QK), lambda b, c: (b * steps + c, col))
    prev = lambda col: pl.BlockSpec(
        (SUBLANES, A_QK), lambda b, c: (jnp.maximum((b * steps + c) * cpb - 1, 0), col))
    frst = lambda j: pl.BlockSpec((nseq, SUBLANES, A_QK), lambda b, c: (fb(b), 0, j))
    row1 = lambda n: pl.BlockSpec((1, n), lambda b, c: (0, 0))
    return pl.pallas_call(
        functools.partial(_delta_kernel, C=C, n_valid=n_valid, nsub=nsub, chained=chained),
        grid=(nb, steps),
        in_specs=[
            cur(cq), cur(ck), cur(cv), cur(cz),
            prev(cq), prev(ck), prev(cv),
            frst(0), frst(1), frst(2),
            pl.BlockSpec((R, SMALL_PAD), lambda b, c: (b * steps + c, 0)),
            pl.BlockSpec((A_CONV, A_CONV_DIM), lambda b, c: (0, 0)),
            row1(SMALL_PAD), row1(SMALL_PAD), row1(A_DV),
            pl.BlockSpec((nseq, A_HEADS, A_DK, A_DV), lambda b, c: (fb(b), 0, 0, 0)),
        ],
        out_specs=[
            pl.BlockSpec((R, A_V), lambda b, c: (b * steps + c, 0)),
            pl.BlockSpec((nseq, A_HEADS, A_DK, A_DV), lambda b, c: (b, 0, 0, 0)),
        ],
        out_shape=[
            jax.ShapeDtypeStruct((rows, A_V), BF16),
            jax.ShapeDtypeStruct((nb * nseq, A_HEADS, A_DK, A_DV), F32),
        ],
        compiler_params=pltpu.CompilerParams(
            dimension_semantics=("arbitrary", "arbitrary"), vmem_limit_bytes=VMEM_LIMIT),
        name="delta_mixer",
    )(proj, proj, proj, proj, proj, proj, proj, first_prev, first_prev, first_prev, small,
      conv_w, alog_row, dtb_row, anw_row, s0)


def _merge_kernel(ya_ref, yb_ref, ga_ref, gb_ref, wa_ref, wb_ref, o_ref):
    ua = _bdot(ya_ref[...].astype(BF16), wa_ref[...])
    ub = _bdot(yb_ref[...].astype(BF16), wb_ref[...])
    merged = jax.nn.sigmoid(ga_ref[...]) * ua + jax.nn.sigmoid(gb_ref[...]) * ub
    o_ref[...] = merged.astype(BF16)


def _merge(ya, yb, proj, wa, wb, tm):
    m = ya.shape[0]
    assert m % tm == 0
    return pl.pallas_call(
        _merge_kernel,
        grid=(m // tm,),
        in_specs=[
            pl.BlockSpec((tm, A_V), lambda i: (i, 0)),
            pl.BlockSpec((tm, B_Q), lambda i: (i, 0)),
            pl.BlockSpec((tm, D_MODEL), lambda i: (i, COL_GA // D_MODEL)),
            pl.BlockSpec((tm, D_MODEL), lambda i: (i, COL_GB // D_MODEL)),
            pl.BlockSpec((A_V, D_MODEL), lambda i: (0, 0)),
            pl.BlockSpec((B_Q, D_MODEL), lambda i: (0, 0)),
        ],
        out_specs=pl.BlockSpec((tm, D_MODEL), lambda i: (i, 0)),
        out_shape=jax.ShapeDtypeStruct((m, D_MODEL), BF16),
        compiler_params=pltpu.CompilerParams(
            dimension_semantics=("arbitrary",), vmem_limit_bytes=VMEM_LIMIT),
        name="merge",
    )(ya, yb, proj, proj, wa, wb)


LANE_SENTINEL = ROUTER_PAD - 1


def _lane_argmax(vals, eligible, lane):
    top = jnp.max(jnp.where(eligible, vals, -jnp.inf), -1, keepdims=True)
    idx = jnp.min(jnp.where(jnp.logical_and(eligible, vals == top), lane, LANE_SENTINEL), -1, keepdims=True)
    return top, idx


def _masked_softmax(logits, eligible):
    z = jnp.where(eligible, logits, -jnp.inf)
    e = jnp.exp(z - jnp.max(z, -1, keepdims=True))
    return e / jnp.sum(e, -1, keepdims=True)


def _outproj_kernel(m_ref, h_ref, wo_ref, g_ref, wr_ref, br_ref, cnt_in_ref,
                    h2_ref, xn_ref, gate_ref, route_ref, cnt_ref):
    tm = h_ref.shape[0]

    @pl.when(pl.program_id(0) == 0)
    def _():
        cnt_ref[...] = cnt_in_ref[...]

    h2 = h_ref[...] + _bdot(m_ref[...], wo_ref[...])
    h2_ref[...] = h2
    ms = jnp.mean(h2 * h2, axis=-1, keepdims=True)
    xn = (h2 * lax.rsqrt(ms + EPS)) * g_ref[...]
    _store_slabs(xn_ref, xn)
    lg = _bdot(xn.astype(BF16), wr_ref[...]) + br_ref[...]

    lane = lax.broadcasted_iota(jnp.int32, (tm, ROUTER_PAD), 1)
    is_g = lane < N_GROUPS
    p_grp, grp = _lane_argmax(_masked_softmax(lg, is_g), is_g, lane)
    ex = lane - N_GROUPS
    is_e = jnp.logical_and(jnp.logical_and(ex >= 0, ex < N_EXPERTS), (ex >> 3) == grp)
    pe = _masked_softmax(lg, is_e)
    p1, i1 = _lane_argmax(pe, is_e, lane)
    rest = jnp.logical_and(is_e, lane != i1)
    p2, i2 = _lane_argmax(pe, rest, lane)
    den = p1 + p2
    gate_ref[...] = jnp.where(lane == 0, (p_grp * p1) / den, jnp.where(lane == 1, (p_grp * p2) / den, 0.0))

    e1 = i1 - N_GROUPS
    e2 = i2 - N_GROUPS
    oh1 = lane == e1
    oh2 = lane == e2
    ri = lax.broadcasted_iota(jnp.int32, (tm, tm), 0)
    ci = lax.broadcasted_iota(jnp.int32, (tm, tm), 1)
    below = (ri > ci).astype(BF16)
    f1 = oh1.astype(F32)
    f2 = oh2.astype(F32)
    tot1 = jnp.sum(f1, 0, keepdims=True)
    run = cnt_ref[...]
    before1 = run + _bdot(below, oh1.astype(BF16))
    before2 = run + tot1 + _bdot(below, oh2.astype(BF16))
    rank1 = jnp.sum(f1 * before1, -1, keepdims=True).astype(jnp.int32)
    rank2 = jnp.sum(f2 * before2, -1, keepdims=True).astype(jnp.int32)
    cnt_ref[...] = run + tot1 + jnp.sum(f2, 0, keepdims=True)
    route_ref[...] = jnp.where(lane == 0, e1, jnp.where(lane == 1, e2, jnp.where(
        lane == 2, rank1, jnp.where(lane == 3, rank2, 0))))


def _outproj(merged, h, wo, gain2, w_router, b_router, counts_in, tm):
    m = h.shape[0]
    assert m % tm == 0
    row = lambda n: pl.BlockSpec((1, n), lambda i: (0, 0))
    tile = lambda n: pl.BlockSpec((tm, n), lambda i: (i, 0))
    return pl.pallas_call(
        _outproj_kernel,
        grid=(m // tm,),
        in_specs=[
            tile(D_MODEL), tile(D_MODEL),
            pl.BlockSpec((D_MODEL, D_MODEL), lambda i: (0, 0)),
            row(D_MODEL),
            pl.BlockSpec((D_MODEL, ROUTER_PAD), lambda i: (0, 0)),
            row(ROUTER_PAD), row(ROUTER_PAD),
        ],
        out_specs=[tile(D_MODEL), pl.BlockSpec((tm, ROW_SLABS, SLAB), lambda i: (i, 0, 0)),
                   tile(ROUTER_PAD), tile(ROUTER_PAD), row(ROUTER_PAD)],
        out_shape=[
            jax.ShapeDtypeStruct((m, D_MODEL), F32),
            jax.ShapeDtypeStruct((m, ROW_SLABS, SLAB), F32),
            jax.ShapeDtypeStruct((m, ROUTER_PAD), F32),
            jax.ShapeDtypeStruct((m, ROUTER_PAD), jnp.int32),
            jax.ShapeDtypeStruct((1, ROUTER_PAD), F32),
        ],
        compiler_params=pltpu.CompilerParams(
            dimension_semantics=("arbitrary",), vmem_limit_bytes=VMEM_LIMIT),
        name="outproj",
    )(merged, h, wo, gain2, w_router, b_router, counts_in)


def _moe_kernel(blk_e_ref, nused_ref, x_ref, wg_ref, wu_ref, wd_ref, o_ref, wgb, wub, wdb):
    i = pl.program_id(0)
    e = blk_e_ref[i]
    e_prev = blk_e_ref[jnp.maximum(i - 1, 0)]
    used = i < nused_ref[0]

    @pl.when(jnp.logical_and(used, jnp.logical_or(i == 0, e != e_prev)))
    def _():
        def cast_in(r, c):
            sl = pl.ds(pl.multiple_of(r * 256, 256), 256)
            wgb[sl, :] = wg_ref[sl, :].astype(BF16)
            wub[sl, :] = wu_ref[sl, :].astype(BF16)
            return c

        lax.fori_loop(0, D_MODEL // 256, cast_in, 0)

        def cast_out(r, c):
            sl = pl.ds(pl.multiple_of(r * 128, 128), 128)
            wdb[sl, :] = wd_ref[sl, :].astype(BF16)
            return c

        lax.fori_loop(0, D_EXPERT // 128, cast_out, 0)

    @pl.when(used)
    def _():
        x = _load_slabs(x_ref).astype(BF16)
        g = _bdot(x, wgb[...])
        u = _bdot(x, wub[...])
        hb = (g * jax.nn.sigmoid(g)) * u
        _store_slabs(o_ref, _bdot(hb.astype(BF16), wdb[...]))

    @pl.when(jnp.logical_not(used))
    def _():
        o_ref[...] = jnp.zeros_like(o_ref)


def _moe_ffn(xs, blk_e, n_used, w_gate, w_up, w_down):
    p = xs.shape[0]
    nb = p // MOE_ROWS
    grid_spec = pltpu.PrefetchScalarGridSpec(
        num_scalar_prefetch=2,
        grid=(nb,),
        in_specs=[
            pl.BlockSpec((MOE_ROWS, ROW_SLABS, SLAB), lambda i, be, nu: (jnp.minimum(i, nu[0] - 1), 0, 0)),
            pl.BlockSpec((None, D_MODEL, D_EXPERT), lambda i, be, nu: (be[i], 0, 0)),
            pl.BlockSpec((None, D_MODEL, D_EXPERT), lambda i, be, nu: (be[i], 0, 0)),
            pl.BlockSpec((None, D_EXPERT, D_MODEL), lambda i, be, nu: (be[i], 0, 0)),
        ],
        out_specs=pl.BlockSpec((MOE_ROWS, ROW_SLABS, SLAB), lambda i, be, nu: (i, 0, 0)),
        scratch_shapes=[
            pltpu.VMEM((D_MODEL, D_EXPERT), BF16),
            pltpu.VMEM((D_MODEL, D_EXPERT), BF16),
            pltpu.VMEM((D_EXPERT, D_MODEL), BF16),
        ],
    )
    return pl.pallas_call(
        _moe_kernel,
        grid_spec=grid_spec,
        out_shape=jax.ShapeDtypeStruct((p, ROW_SLABS, SLAB), F32),
        compiler_params=pltpu.CompilerParams(
            dimension_semantics=("arbitrary",), vmem_limit_bytes=VMEM_LIMIT),
        name="moe_ffn",
    )(blk_e, n_used, xs, w_gate, w_up, w_down)


HALF_TILE = 64
SWA_HEADS_PER_STAGE = 16
LANES = 128


def _dot_exact_rhs(a, b01):
    a1 = a.astype(BF16)
    r1 = a - a1.astype(F32)
    a2 = r1.astype(BF16)
    a3 = (r1 - a2.astype(F32)).astype(BF16)
    return _bdot(a1, b01) + (_bdot(a2, b01) + _bdot(a3, b01))


def _group_rms_rope(x, gain, cos128, sin128, bd, bdt):
    r, w = x.shape
    ssq = _dot_exact_rhs(x * x, bd)
    rs = lax.rsqrt(ssq * (1.0 / B_HD) + EPS)
    y = (x * _dot_exact_rhs(rs, bdt)) * gain
    reps = w // LANES
    cosw = jnp.concatenate([cos128] * reps, 1)
    sinw = jnp.concatenate([sin128] * reps, 1)
    lane = lax.broadcasted_iota(jnp.int32, (r, w), 1)
    swapped = jnp.where((lane & (B_HD // 2)) == 0,
                        pltpu.roll(y, w - B_HD // 2, 1), pltpu.roll(y, B_HD // 2, 1))
    return y * cosw + swapped * sinw


def _kv_tiles(tile, odd, lo):
    rolled = pltpu.roll(tile, HALF_TILE, 1)
    dup = jnp.where(lo, rolled, tile) if odd else jnp.where(lo, tile, rolled)
    return dup.astype(BF16), jnp.where(lo, dup, 0.0).astype(BF16), jnp.where(lo, 0.0, dup).astype(BF16)


def _nt(a, b):
    return lax.dot_general(a, b, (((1,), (1,)), ((), ())), preferred_element_type=F32)


def _sink_softmax_pv(score_lists, value_lists, sinks):
    ms = []
    for scores, sink in zip(score_lists, sinks):
        m = jnp.max(scores[0], -1, keepdims=True)
        for s in scores[1:]:
            m = jnp.maximum(m, jnp.max(s, -1, keepdims=True))
        ms.append(jnp.maximum(m, sink))
    es = [[jnp.exp(s - m) for s in scores] for scores, m in zip(score_lists, ms)]
    dens = []
    for e_blocks, m, sink in zip(es, ms, sinks):
        den = jnp.exp(sink - m)
        for e in e_blocks:
            den = den + jnp.sum(e, -1, keepdims=True)
        dens.append(den)
    pvs = [[_bdot(e.astype(BF16), v) for e, v in zip(e_blocks, values)]
           for e_blocks, values in zip(es, value_lists)]
    outs = []
    for pv, den in zip(pvs, dens):
        acc = pv[0]
        for x in pv[1:]:
            acc = acc + x
        outs.append(acc / den)
    return outs


def _rope_small_kernel(q_ref, k_ref, cos_ref, sin_ref, gq_ref, gk_ref, bd_ref, bdt_ref, qo_ref, ko_ref):
    cos = cos_ref[...]
    sin = sin_ref[...]
    qo_ref[...] = _group_rms_rope(q_ref[...], gq_ref[...], cos, sin, bd_ref[...], bdt_ref[...])
    ko_ref[...] = _group_rms_rope(k_ref[...], gk_ref[...], cos, sin, bd_ref[:B_KV, :], bdt_ref[:, :B_KV])


def _rope_small(proj, cos, sin, gq, gk, bd, bdt):
    m = proj.shape[0]
    full = lambda shape: pl.BlockSpec(shape, lambda i: (0, 0))
    return pl.pallas_call(
        _rope_small_kernel,
        grid=(1,),
        in_specs=[
            pl.BlockSpec((m, B_Q), lambda i: (0, COL_QB // B_Q)),
            pl.BlockSpec((m, B_KV), lambda i: (0, COL_KB // B_KV)),
            full((m, LANES)), full((m, LANES)), full((1, B_Q)), full((1, B_KV)),
            full((B_Q, LANES)), full((LANES, B_Q)),
        ],
        out_specs=[full((m, B_Q)), full((m, B_KV))],
        out_shape=[jax.ShapeDtypeStruct((m, B_Q), F32), jax.ShapeDtypeStruct((m, B_KV), F32)],
        compiler_params=pltpu.CompilerParams(
            dimension_semantics=("arbitrary",), vmem_limit_bytes=VMEM_LIMIT),
        name="rope_small",
    )(proj, proj, cos, sin, gq, gk, bd, bdt)


def _swa_prompt_kernel(sinks_ref, q_ref, k_ref, v_ref, cos_ref, sin_ref, km_ref, vm_ref, gq_ref, gk_ref,
                       bd_ref, bdt_ref, y_ref, kc_ref, vc_ref, kprev, vloprev, vhiprev):
    n = pl.program_id(1)

    @pl.when(n == 0)
    def _():
        kprev[...] = jnp.zeros_like(kprev)
        vloprev[...] = jnp.zeros_like(vloprev)
        vhiprev[...] = jnp.zeros_like(vhiprev)

    cos = cos_ref[...]
    sin = sin_ref[...]
    q = _group_rms_rope(q_ref[...], gq_ref[...], cos, sin, bd_ref[...], bdt_ref[...])
    k = _group_rms_rope(k_ref[...], gk_ref[...], cos, sin, bd_ref[:B_KV, :], bdt_ref[:, :B_KV])
    v = v_ref[...]
    kc_ref[...] = k
    vc_ref[...] = v
    km = km_ref[...]
    vm = vm_ref[...]

    qi = lax.broadcasted_iota(jnp.int32, (WINDOW, WINDOW), 0)
    kj = lax.broadcasted_iota(jnp.int32, (WINDOW, WINDOW), 1)
    cur_ok = kj <= qi
    prev_ok = jnp.logical_and(kj > qi, n > 0)
    lo = lax.broadcasted_iota(jnp.int32, (WINDOW, LANES), 1) < HALF_TILE
    lo_m = lax.broadcasted_iota(jnp.int32, (N_META, LANES), 1) < HALF_TILE
    scale = B_HD ** -0.5

    kv = []
    for g in range(B_KV_HEADS):
        tl = slice((g // 2) * LANES, (g // 2 + 1) * LANES)
        k2c, _, _ = _kv_tiles(k[:, tl], g % 2, lo)
        _, vlo_c, vhi_c = _kv_tiles(v[:, tl], g % 2, lo)
        k2m, _, _ = _kv_tiles(km[:, tl], g % 2, lo_m)
        _, vlo_m, vhi_m = _kv_tiles(vm[:, tl], g % 2, lo_m)
        kv.append(((k2m, kprev[g], k2c), (vlo_m, vloprev[g], vlo_c), (vhi_m, vhiprev[g], vhi_c)))
    heads = range(B_HEADS)
    qms = []
    for h in heads:
        qt = q[:, (h // 2) * LANES:(h // 2 + 1) * LANES]
        qms.append((jnp.where(lo, qt, 0.0) if h % 2 == 0 else jnp.where(lo, 0.0, qt)).astype(BF16))
    for h0 in range(0, B_HEADS, SWA_HEADS_PER_STAGE):
        hg = range(h0, h0 + SWA_HEADS_PER_STAGE)
        raw = [[_nt(qms[h], kk) * scale for kk in kv[h // B_GROUP][0]] for h in hg]
        scores = [[sm, jnp.where(prev_ok, sp, -jnp.inf), jnp.where(cur_ok, sc, -jnp.inf)] for sm, sp, sc in raw]
        outs = _sink_softmax_pv(scores, [kv[h // B_GROUP][1 + h % 2] for h in hg], [sinks_ref[h] for h in hg])
        for t in range(SWA_HEADS_PER_STAGE // 2):
            y_ref[:, (h0 // 2 + t) * LANES:(h0 // 2 + t + 1) * LANES] = (outs[2 * t] + outs[2 * t + 1]).astype(BF16)
    for g in range(B_KV_HEADS):
        kprev[g] = kv[g][0][2]
        vloprev[g] = kv[g][1][2]
        vhiprev[g] = kv[g][2][2]


def _swa_prompt(proj, cos, sin, k_meta, v_meta, gq, gk, bd, bdt, sinks, nb, nblk):
    rows = nb * nblk * WINDOW
    assert proj.shape[0] == rows
    return pl.pallas_call(
        _swa_prompt_kernel,
        grid=(nb, nblk),
        in_specs=[
            pl.BlockSpec(memory_space=pltpu.SMEM),
            pl.BlockSpec((WINDOW, B_Q), lambda b, n: (b * nblk + n, COL_QB // B_Q)),
            pl.BlockSpec((WINDOW, B_KV), lambda b, n: (b * nblk + n, COL_KB // B_KV)),
            pl.BlockSpec((WINDOW, B_KV), lambda b, n: (b * nblk + n, COL_VB // B_KV)),
            pl.BlockSpec((WINDOW, LANES), lambda b, n: (n, 0)),
            pl.BlockSpec((WINDOW, LANES), lambda b, n: (n, 0)),
            pl.BlockSpec((N_META, B_KV), lambda b, n: (0, 0)),
            pl.BlockSpec((N_META, B_KV), lambda b, n: (0, 0)),
            pl.BlockSpec((1, B_Q), lambda b, n: (0, 0)),
            pl.BlockSpec((1, B_KV), lambda b, n: (0, 0)),
            pl.BlockSpec((B_Q, LANES), lambda b, n: (0, 0)),
            pl.BlockSpec((LANES, B_Q), lambda b, n: (0, 0)),
        ],
        out_specs=[
            pl.BlockSpec((WINDOW, B_Q), lambda b, n: (b * nblk + n, 0)),
            pl.BlockSpec((None, WINDOW, B_KV), lambda b, n: (b, 0, 0)),
            pl.BlockSpec((None, WINDOW, B_KV), lambda b, n: (b, 0, 0)),
        ],
        scratch_shapes=[pltpu.VMEM((B_KV_HEADS, WINDOW, LANES), BF16)] * 3,
        out_shape=[
            jax.ShapeDtypeStruct((rows, B_Q), BF16),
            jax.ShapeDtypeStruct((nb, WINDOW, B_KV), F32),
            jax.ShapeDtypeStruct((nb, WINDOW, B_KV), F32),
        ],
        compiler_params=pltpu.CompilerParams(
            dimension_semantics=("arbitrary", "arbitrary"), vmem_limit_bytes=VMEM_LIMIT),
        name="swa_prompt",
    )(sinks, proj, proj, proj, cos, sin, k_meta, v_meta, gq, gk, bd, bdt)


def _swa_sample_kernel(sinks_ref, q_ref, k_ref, v_ref, y_ref, *, n_keys, n_new):
    nseq, rq, _ = q_ref.shape
    t = lax.broadcasted_iota(jnp.int32, (rq, n_keys), 0)
    r = lax.broadcasted_iota(jnp.int32, (rq, n_keys), 1)
    wj = r - N_META
    win_pos = PAST_LEN - WINDOW + wj
    nm = r - N_META - WINDOW
    ok = ((r < N_META)
          | ((wj >= 0) & (wj < WINDOW) & (win_pos >= N_META) & (wj >= t + 1))
          | ((nm >= 0) & (nm <= t) & (nm > t - WINDOW) & (nm < n_new)))
    lo_q = lax.broadcasted_iota(jnp.int32, (rq, LANES), 1) < HALF_TILE
    lo_k = lax.broadcasted_iota(jnp.int32, (n_keys, LANES), 1) < HALF_TILE
    scale = B_HD ** -0.5
    heads = range(B_HEADS)
    scores, values = [], []
    for s in range(nseq):
        q = q_ref[s]
        k = k_ref[s]
        v = v_ref[s]
        kv = []
        for g in range(B_KV_HEADS):
            tl = slice((g // 2) * LANES, (g // 2 + 1) * LANES)
            k2, _, _ = _kv_tiles(k[:, tl], g % 2, lo_k)
            _, vlo, vhi = _kv_tiles(v[:, tl], g % 2, lo_k)
            kv.append((k2, vlo, vhi))
        for h in heads:
            qt = q[:, (h // 2) * LANES:(h // 2 + 1) * LANES]
            qm = (jnp.where(lo_q, qt, 0.0) if h % 2 == 0 else jnp.where(lo_q, 0.0, qt)).astype(BF16)
            scores.append([jnp.where(ok, _nt(qm, kv[h // B_GROUP][0]) * scale, -jnp.inf)])
            values.append([kv[h // B_GROUP][1 + h % 2]])
    outs = _sink_softmax_pv(scores, values, [sinks_ref[h] for _s in range(nseq) for h in heads])
    for s in range(nseq):
        for t2 in range(B_HEADS // 2):
            pair = outs[s * B_HEADS + 2 * t2] + outs[s * B_HEADS + 2 * t2 + 1]
            y_ref[s, :, t2 * LANES:(t2 + 1) * LANES] = pair.astype(BF16)


def _swa_sample(q8, kk, vv, sinks, n_new):
    nb, rq, _ = q8.shape
    n_keys = kk.shape[1]
    nseq = SAMPLE_SEQS_PER_STEP
    assert nb % nseq == 0
    return pl.pallas_call(
        functools.partial(_swa_sample_kernel, n_keys=n_keys, n_new=n_new),
        grid=(nb // nseq,),
        in_specs=[
            pl.BlockSpec(memory_space=pltpu.SMEM),
            pl.BlockSpec((nseq, rq, B_Q), lambda b: (b, 0, 0)),
            pl.BlockSpec((nseq, n_keys, B_KV), lambda b: (b, 0, 0)),
            pl.BlockSpec((nseq, n_keys, B_KV), lambda b: (b, 0, 0)),
        ],
        out_specs=pl.BlockSpec((nseq, rq, B_Q), lambda b: (b, 0, 0)),
        out_shape=jax.ShapeDtypeStruct((nb, rq, B_Q), BF16),
        compiler_params=pltpu.CompilerParams(
            dimension_semantics=("arbitrary",), vmem_limit_bytes=VMEM_LIMIT),
        name="swa_sample",
    )(sinks, q8, kk, vv)


def _rope_tables(pos):
    half = B_HD // 2
    inv_freq = ROPE_THETA ** (-jnp.arange(half, dtype=F32) / half)
    ang = pos.astype(F32)[:, None] * inv_freq[None, :]
    cos, sin = jnp.cos(ang), jnp.sin(ang)
    return jnp.concatenate([cos, cos, cos, cos], 1), jnp.concatenate([-sin, sin, -sin, sin], 1)


ROUTE_ROWS = 256
DMA_UNROLL = 8


def _row(ref, r):
    return ref.at[pl.ds(r, 1)]


def _dispatch_kernel(pend_ref, dest_ref, xa_ref, xb_ref, xs_hbm, zeros_vmem, ring, sem_zero, sem, *, nta, nt, nb):
    i = pl.program_id(0)
    tm = ROUTE_ROWS
    slot = i % 2

    def drain(s):
        for _k in range(TOP_K):
            pltpu.make_async_copy(ring.at[s], xs_hbm.at[pl.ds(0, tm)], sem.at[s]).wait()

    @pl.when(i == 0)
    def _():
        zeros_vmem[...] = jnp.zeros_like(zeros_vmem)
        fill = lambda row0: pltpu.make_async_copy(zeros_vmem, xs_hbm.at[pl.ds(row0, MOE_ROWS)], sem_zero)
        for e in range(N_EXPERTS):
            fill(jnp.maximum(pend_ref[e] - MOE_ROWS, 0)).start()
        for e in range(N_EXPERTS):
            fill(0).wait()
        n_used = pend_ref[N_EXPERTS - 1] // MOE_ROWS

        def fill_tail(b, c):
            fill(b * MOE_ROWS).start()
            fill(0).wait()
            return c

        lax.fori_loop(n_used, nb, fill_tail, 0)

    @pl.when(i >= 2)
    def _():
        drain(slot)

    @pl.when(i < nta)
    def _():
        ring[slot] = xa_ref[...]

    @pl.when(i >= nta)
    def _():
        ring[slot] = xb_ref[...]

    def body(r, c):
        src = ring.at[slot, pl.ds(r, 1)]
        for k in range(TOP_K):
            pltpu.make_async_copy(src, _row(xs_hbm, dest_ref[0, TOP_K * r + k]), sem.at[slot]).start(priority=k)
        return c

    lax.fori_loop(0, tm, body, 0, unroll=DMA_UNROLL)

    @pl.when(i == nt - 1)
    def _():
        if nt >= 2:
            drain(1 - slot)
        drain(slot)


def _dispatch(xa, xb, dest, pend, n_slots):
    assert xa.shape[0] % ROUTE_ROWS == 0 and xb.shape[0] % ROUTE_ROWS == 0 and MOE_ROWS == ROUTE_ROWS
    nta = xa.shape[0] // ROUTE_ROWS
    nt = nta + xb.shape[0] // ROUTE_ROWS
    slab_rows = (ROUTE_ROWS, ROW_SLABS, SLAB)
    grid_spec = pltpu.PrefetchScalarGridSpec(
        num_scalar_prefetch=1,
        grid=(nt,),
        in_specs=[
            pl.BlockSpec((None, 1, TOP_K * ROUTE_ROWS), lambda i, pe: (i, 0, 0), memory_space=pltpu.SMEM),
            pl.BlockSpec(slab_rows, lambda i, pe: (jnp.minimum(i, nta - 1), 0, 0)),
            pl.BlockSpec(slab_rows, lambda i, pe: (jnp.maximum(i - nta, 0), 0, 0)),
        ],
        out_specs=pl.BlockSpec(memory_space=pl.ANY),
        scratch_shapes=[
            pltpu.VMEM(slab_rows, F32),
            pltpu.VMEM((2,) + slab_rows, F32),
            pltpu.SemaphoreType.DMA(()),
            pltpu.SemaphoreType.DMA((2,)),
        ],
    )
    return pl.pallas_call(
        functools.partial(_dispatch_kernel, nta=nta, nt=nt, nb=n_slots // MOE_ROWS),
        grid_spec=grid_spec,
        out_shape=jax.ShapeDtypeStruct((n_slots, ROW_SLABS, SLAB), F32),
        compiler_params=pltpu.CompilerParams(
            dimension_semantics=("arbitrary",), vmem_limit_bytes=VMEM_LIMIT, has_side_effects=True),
        name="moe_dispatch",
    )(pend, dest.reshape(nt, 1, TOP_K * ROUTE_ROWS), xa, xb)


def _combine_kernel(dest_ref, dest_next_ref, gate_ref, h_ref, ys_hbm, o_ref, buf, sem, *, nt):
    i = pl.program_id(0)
    tm = ROUTE_ROWS
    slot = i % 2

    def issue(dref, s):
        def body(r, c):
            for k in range(TOP_K):
                pltpu.make_async_copy(_row(ys_hbm, dref[0, TOP_K * r + k]), buf.at[s, k, pl.ds(r, 1)],
                                      sem.at[s]).start(priority=k)
            return c

        lax.fori_loop(0, tm, body, 0, unroll=DMA_UNROLL)

    @pl.when(i == 0)
    def _():
        issue(dest_ref, 0)

    @pl.when(i < nt - 1)
    def _():
        issue(dest_next_ref, 1 - slot)

    for k in range(TOP_K):
        pltpu.make_async_copy(ys_hbm.at[pl.ds(0, tm)], buf.at[slot, k], sem.at[slot]).wait()
    g = gate_ref[...]
    for s in range(ROW_SLABS):
        cols = slice(s * SLAB, (s + 1) * SLAB)
        o_ref[:, cols] = h_ref[:, cols] + (g[:, 0:1] * buf[slot, 0, :, s, :] + g[:, 1:2] * buf[slot, 1, :, s, :])


def _combine(h2, gates, dest, ys):
    t = h2.shape[0]
    assert t % ROUTE_ROWS == 0
    nt = t // ROUTE_ROWS
    dest3 = dest.reshape(nt, 1, TOP_K * ROUTE_ROWS)
    idx = lambda f: pl.BlockSpec((None, 1, TOP_K * ROUTE_ROWS), f, memory_space=pltpu.SMEM)
    return pl.pallas_call(
        functools.partial(_combine_kernel, nt=nt),
        grid=(nt,),
        in_specs=[
            idx(lambda i: (i, 0, 0)),
            idx(lambda i: (jnp.minimum(i + 1, nt - 1), 0, 0)),
            pl.BlockSpec((ROUTE_ROWS, ROUTER_PAD), lambda i: (i, 0)),
            pl.BlockSpec((ROUTE_ROWS, D_MODEL), lambda i: (i, 0)),
            pl.BlockSpec(memory_space=pl.ANY),
        ],
        out_specs=pl.BlockSpec((ROUTE_ROWS, D_MODEL), lambda i: (i, 0)),
        out_shape=jax.ShapeDtypeStruct((t, D_MODEL), F32),
        scratch_shapes=[
            pltpu.VMEM((2, TOP_K, ROUTE_ROWS, ROW_SLABS, SLAB), F32),
            pltpu.SemaphoreType.DMA((2,)),
        ],
        compiler_params=pltpu.CompilerParams(
            dimension_semantics=("arbitrary",), vmem_limit_bytes=VMEM_LIMIT),
        name="moe_combine",
    )(dest3, dest3, gates, h2, ys)


def _moe_plan(route, counts):
    T = route.shape[0]
    expert = route[:, :TOP_K]
    rank = route[:, TOP_K:2 * TOP_K]
    nb = -(-(T * TOP_K) // MOE_ROWS) + N_EXPERTS
    cnt = counts[0, :N_EXPERTS].astype(jnp.int32)
    padded = (cnt + MOE_ROWS - 1) // MOE_ROWS * MOE_ROWS
    pend = jnp.cumsum(padded).astype(jnp.int32)
    dest = (pend - padded)[expert] + rank
    starts = jnp.arange(nb, dtype=jnp.int32) * MOE_ROWS
    blk_e = jnp.minimum(jnp.sum((pend[None, :] <= starts[:, None]).astype(jnp.int32), 1), N_EXPERTS - 1)
    n_used = pend[-1:] // MOE_ROWS
    return dest, pend, blk_e, n_used, nb * MOE_ROWS


def _lane_row(vals, lane0):
    return jnp.zeros((1, SMALL_PAD), F32).at[0, lane0:lane0 + vals.shape[0]].set(vals)


def kernel(x_prompt, x_sample, state_delta, state_conv, cache_swa_k, cache_swa_v, meta_tokens,
           norm1_w, w_in, conv_w, a_log, dt_bias, a_norm_w, w_up_a, q_norm_w, k_norm_w, sinks,
           w_up_b, w_o, norm2_w, w_router_group, b_router_group, w_router_expert, b_router_expert,
           w_gate, w_up, w_down):
    Bp, Sp, _ = x_prompt.shape
    Bs, Ss, _ = x_sample.shape
    n_s = Bs * Ss
    l = 0
    w = w_in[l]
    offs = [0]
    for s in (A_CONV_DIM, A_HEADS, A_HEADS, A_V, B_Q, B_KV, B_KV, D_MODEL, D_MODEL):
        offs.append(offs[-1] + s)
    seg = lambda a: w[:, offs[a]:offs[a + 1]]
    w_main = jnp.concatenate([seg(7), seg(8), seg(0), seg(3), seg(4), seg(5), seg(6)], 1).astype(BF16)
    w_small = jnp.concatenate(
        [seg(1), seg(2), jnp.zeros((D_MODEL, SMALL_PAD - 2 * A_HEADS), F32)], 1).astype(BF16)
    wa = w_up_a[l].astype(BF16)
    wb = w_up_b[l].astype(BF16)
    wo = w_o[l].astype(BF16)
    w_router = jnp.concatenate(
        [w_router_group[l], w_router_expert[l],
         jnp.zeros((D_MODEL, ROUTER_PAD - N_GROUPS - N_EXPERTS), F32)], 1).astype(BF16)
    b_router = jnp.concatenate(
        [b_router_group[l], b_router_expert[l],
         jnp.zeros((ROUTER_PAD - N_GROUPS - N_EXPERTS,), F32)])[None, :]
    g1 = norm1_w[l][None, :]
    g2 = norm2_w[l][None, :]
    alog_row = _lane_row(a_log[l], LANE_DECAY)
    dtb_row = _lane_row(dt_bias[l], LANE_DECAY)
    anw_row = a_norm_w[l][None, :]

    xp = x_prompt.reshape(Bp * Sp, D_MODEL)
    xs = x_sample.reshape(n_s, D_MODEL)
    x_small = jnp.concatenate([xs, meta_tokens], 0)
    n_small = x_small.shape[0]

    proj_p, small_p = _inproj(xp, g1, w_main, w_small, tm=1024)
    proj_s, small_s = _inproj(x_small, g1, w_main, w_small, tm=n_small)

    dm = functools.partial(_delta_mixer, conv_w=conv_w[l], alog_row=alog_row, dtb_row=dtb_row, anw_row=anw_row)
    pad_rows = lambda a, n: jnp.concatenate([a, jnp.zeros((n - a.shape[0],) + a.shape[1:], a.dtype)], 0)
    meta_proj = pad_rows(proj_s[n_s:], A_CHUNK)
    meta_small = pad_rows(small_s[n_s:], A_CHUNK)
    zero_prev = jnp.zeros((1, SUBLANES, A_CONV_DIM), F32)
    zero_state = jnp.zeros((1, A_HEADS, A_DK, A_DV), F32)
    _, s_meta = dm(meta_proj, meta_small, zero_prev, zero_state, nb=1, nc=1, C=A_CHUNK, n_valid=N_META,
                   shared_first=True)
    meta_tail = proj_s[n_s + N_META - SUBLANES:, COL_QKV:COL_QKV + A_CONV_DIM][None]
    ya_p, sdelta_p = dm(proj_p, small_p, meta_tail, s_meta, nb=Bp, nc=Sp // A_CHUNK, C=A_CHUNK,
                        n_valid=A_CHUNK, shared_first=True, nsub=DELTA_CHUNKS_PER_STEP)
    conv_p = proj_p.reshape(Bp, Sp, PROJ_MAIN)[:, Sp - (A_CONV - 1):, COL_QKV:COL_QKV + A_CONV_DIM]
    CS = SUBLANES
    samp_proj = jnp.pad(proj_s[:n_s].reshape(Bs, Ss, PROJ_MAIN), ((0, 0), (0, CS - Ss), (0, 0)))
    samp_small = jnp.pad(small_s[:n_s].reshape(Bs, Ss, SMALL_PAD), ((0, 0), (0, CS - Ss), (0, 0)))
    samp_prev = jnp.pad(state_conv[l], ((0, 0), (SUBLANES - (A_CONV - 1), 0), (0, 0)))
    ya_s8, sdelta_s = dm(samp_proj.reshape(Bs * CS, PROJ_MAIN), samp_small.reshape(Bs * CS, SMALL_PAD),
                         samp_prev, state_delta[l], nb=Bs, nc=1, C=CS, n_valid=Ss, shared_first=False,
                         nsub=SAMPLE_SEQS_PER_STEP, chained=False)
    ya_s = ya_s8.reshape(Bs, CS, A_V)[:, :Ss].reshape(n_s, A_V)
    qkv_s = proj_s[:n_s, COL_QKV:COL_QKV + A_CONV_DIM].reshape(Bs, Ss, A_CONV_DIM)
    conv_s = jnp.concatenate([state_conv[l], qkv_s], 1)[:, -(A_CONV - 1):]

    gq = jnp.tile(q_norm_w[l], B_HEADS)[None]
    gk = jnp.tile(k_norm_w[l], B_KV_HEADS)[None]
    bd = ((jnp.arange(B_Q) // B_HD)[:, None] == jnp.arange(LANES)[None, :]).astype(BF16)
    bdt = bd.T
    pos_small = jnp.concatenate([PAST_LEN + jnp.arange(n_s, dtype=jnp.int32) % Ss,
                                 jnp.arange(N_META, dtype=jnp.int32)])
    cos_s, sin_s = _rope_tables(pos_small)
    q_rot_s, k_rot_s = _rope_small(proj_s, cos_s, sin_s, gq, gk, bd, bdt)
    k_meta = k_rot_s[n_s:]
    v_meta = proj_s[n_s:, COL_VB:COL_VB + B_KV]
    cos_p, sin_p = _rope_tables(N_META + jnp.arange(Sp, dtype=jnp.int32))
    yb_p, k_last, v_last = _swa_prompt(proj_p, cos_p, sin_p, k_meta, v_meta, gq, gk, bd, bdt, sinks[l],
                                       Bp, Sp // WINDOW)
    cache_shape = (N_META + WINDOW, B_KV_HEADS, B_HD)
    bcast_meta = lambda a: jnp.broadcast_to(a[None], (Bp, N_META, B_KV))
    swk_p = jnp.concatenate([bcast_meta(k_meta), k_last], 1).reshape((Bp,) + cache_shape)
    swv_p = jnp.concatenate([bcast_meta(v_meta), v_last], 1).reshape((Bp,) + cache_shape)
    n_cache = N_META + WINDOW
    key_pad = -(n_cache + Ss) % SUBLANES
    zpad = jnp.zeros((Bs, key_pad, B_KV), F32)
    kk = jnp.concatenate([cache_swa_k[l].reshape(Bs, n_cache, B_KV), k_rot_s[:n_s].reshape(Bs, Ss, B_KV), zpad], 1)
    vv = jnp.concatenate([cache_swa_v[l].reshape(Bs, n_cache, B_KV),
                          proj_s[:n_s, COL_VB:COL_VB + B_KV].reshape(Bs, Ss, B_KV), zpad], 1)
    q8 = jnp.pad(q_rot_s[:n_s].reshape(Bs, Ss, B_Q), ((0, 0), (0, SUBLANES - Ss), (0, 0)))
    yb_s = _swa_sample(q8, kk, vv, sinks[l], n_new=Ss)[:, :Ss].reshape(n_s, B_Q)
    new_cache = lambda t: jnp.concatenate(
        [t[:, :N_META], t[:, n_cache + Ss - WINDOW:n_cache + Ss]], 1).reshape((Bs,) + cache_shape)
    swk_s, swv_s = new_cache(kk), new_cache(vv)

    merged_p = _merge(ya_p, yb_p, proj_p, wa, wb, tm=512)
    merged_s = _merge(ya_s, yb_s, proj_s[:n_s], wa, wb, tm=256)
    cnt0 = jnp.zeros((1, ROUTER_PAD), F32)
    h2_p, xn2_p, gt_p, rt_p, cnt_p = _outproj(merged_p, xp, wo, g2, w_router, b_router, cnt0, tm=256)
    h2_s, xn2_s, gt_s, rt_s, cnt = _outproj(merged_s, xs, wo, g2, w_router, b_router, cnt_p, tm=256)

    cat = lambda a, b: jnp.concatenate([a, b], 0)
    n_p = Bp * Sp
    dest, pend, blk_e, n_used, n_slots = _moe_plan(cat(rt_p, rt_s), cnt)
    xs_slots = _dispatch(xn2_p, xn2_s, dest, pend, n_slots)
    ys = _moe_ffn(xs_slots, blk_e, n_used, w_gate[l], w_up[l], w_down[l])
    y_prompt = _combine(h2_p, gt_p, dest[:n_p], ys).reshape(Bp, Sp, D_MODEL)
    y_sample = _combine(h2_s, gt_s, dest[n_p:], ys).reshape(Bs, Ss, D_MODEL)
    return (y_prompt, y_sample, sdelta_p[None], conv_p[None], swk_p[None], swv_p[None],
            sdelta_s[None], conv_s[None], swk_s[None], swv_s[None])
```

```python
import functools

import jax
import jax.numpy as jnp
from jax import lax
from jax.experimental import pallas as pl
from jax.experimental.pallas import tpu as pltpu

F32 = jnp.float32
BF16 = jnp.bfloat16

D_MODEL = 2048
N_META = 16
A_HEADS = 8
A_DK = 128
A_DV = 128
A_CONV = 4
A_CHUNK = 64
A_QK = A_HEADS * A_DK
A_V = A_HEADS * A_DV
A_CONV_DIM = 2 * A_QK + A_V
B_HEADS = 16
B_KV_HEADS = 4
B_HD = 64
B_GROUP = B_HEADS // B_KV_HEADS
B_Q = B_HEADS * B_HD
B_KV = B_KV_HEADS * B_HD
WINDOW = 128
ROPE_THETA = 10000.0
PAST_LEN = 16384
N_GROUPS = 4
EXPERTS_PER_GROUP = 8
N_EXPERTS = N_GROUPS * EXPERTS_PER_GROUP
TOP_K = 2
D_EXPERT = 512
EPS = 1e-6

COL_GA = 0
COL_GB = COL_GA + D_MODEL
COL_QKV = COL_GB + D_MODEL
COL_Z = COL_QKV + A_CONV_DIM
COL_QB = COL_Z + A_V
COL_KB = COL_QB + B_Q
COL_VB = COL_KB + B_KV
PROJ_MAIN = COL_VB + B_KV
SMALL_PAD = 128
ROUTER_PAD = 128
LANE_BETA = 0
LANE_DECAY = A_HEADS

SUBLANES = 8
MOE_ROWS = 256
DELTA_CHUNKS_PER_STEP = 2
SAMPLE_SEQS_PER_STEP = 8
ROW_SLABS = 8
SLAB = D_MODEL // ROW_SLABS
VMEM_LIMIT = 56 * 1024 * 1024


def _row_chunk(tm):
    return 256 if tm % 256 == 0 else tm


def _bdot(a, b):
    return jnp.dot(a, b, preferred_element_type=F32)


def _store_slabs(ref3, val):
    for s in range(ROW_SLABS):
        ref3[:, s, :] = val[:, s * SLAB:(s + 1) * SLAB]


def _load_slabs(ref3):
    return jnp.concatenate([ref3[:, s, :] for s in range(ROW_SLABS)], 1)


def _inproj_kernel(x_ref, g_ref, w_ref, ws_ref, o_ref, os_ref, xn_ref, *, tm):
    rc = _row_chunk(tm)

    @pl.when(pl.program_id(1) == 0)
    def _():
        def norm_rows(sl):
            x = x_ref[sl, :]
            ms = jnp.mean(x * x, axis=-1, keepdims=True)
            xn_ref[sl, :] = ((x * lax.rsqrt(ms + EPS)) * g_ref[...]).astype(BF16)

        def body(r, c):
            norm_rows(pl.ds(pl.multiple_of(r * rc, rc), rc))
            return c

        if tm == rc:
            norm_rows(pl.ds(0, tm))
        else:
            lax.fori_loop(0, tm // rc, body, 0)
        os_ref[...] = _bdot(xn_ref[...], ws_ref[...])

    o_ref[...] = _bdot(xn_ref[...], w_ref[...])


def _inproj(x, gain, w_main, w_small, tm, tn=512):
    m = x.shape[0]
    assert m % tm == 0 and PROJ_MAIN % tn == 0
    return pl.pallas_call(
        functools.partial(_inproj_kernel, tm=tm),
        grid=(m // tm, PROJ_MAIN // tn),
        in_specs=[
            pl.BlockSpec((tm, D_MODEL), lambda i, j: (i, 0)),
            pl.BlockSpec((1, D_MODEL), lambda i, j: (0, 0)),
            pl.BlockSpec((D_MODEL, tn), lambda i, j: (0, j)),
            pl.BlockSpec((D_MODEL, SMALL_PAD), lambda i, j: (0, 0)),
        ],
        out_specs=[
            pl.BlockSpec((tm, tn), lambda i, j: (i, j)),
            pl.BlockSpec((tm, SMALL_PAD), lambda i, j: (i, 0)),
        ],
        out_shape=[
            jax.ShapeDtypeStruct((m, PROJ_MAIN), F32),
            jax.ShapeDtypeStruct((m, SMALL_PAD), F32),
        ],
        scratch_shapes=[pltpu.VMEM((tm, D_MODEL), BF16)],
        compiler_params=pltpu.CompilerParams(
            dimension_semantics=("arbitrary", "arbitrary"), vmem_limit_bytes=VMEM_LIMIT),
        name="inproj",
    )(x, gain, w_main, w_small)


def _dot_exact_lhs(a_bf16, b):
    b1 = b.astype(BF16)
    r1 = b - b1.astype(F32)
    b2 = r1.astype(BF16)
    b3 = (r1 - b2.astype(F32)).astype(BF16)
    return _bdot(a_bf16, b1) + (_bdot(a_bf16, b2) + _bdot(a_bf16, b3))


def _unit_lower_inverse_offsets(ms, ri, ci, c):
    same = lambda n: (ri >> n) == (ci >> n)
    d = lambda a, b: _bdot(a.astype(BF16), b.astype(BF16))
    blk8 = same(3)
    n1 = [jnp.where(blk8, m, 0.0) for m in ms]
    n2 = [d(a, a) for a in n1]
    n3 = [d(a, b) for a, b in zip(n1, n2)]
    n4 = [d(b, b) for b in n2]
    qs = [(b - a) - t for a, b, t in zip(n1, n2, n3)]
    qs = [(q + f) + d(q, f) for q, f in zip(qs, n4)]
    lg = 3
    while (1 << lg) < c:
        sel = jnp.logical_and(same(lg + 1), jnp.logical_not(same(lg)))
        offs = [jnp.where(sel, m, 0.0) for m in ms]
        ts = [o + d(q, o) for q, o in zip(qs, offs)]
        qs = [(q - t) - d(t, q) for q, t in zip(qs, ts)]
        lg += 1
    return qs


def _delta_kernel(qc, kc, vc, zc, qp, kp, vp, qf, kf, vf, sm_ref, cw_ref, alog_ref, dtb_ref, anw_ref, s0_ref,
                  ya_ref, s_ref, *, C, n_valid, nsub, chained):
    R = nsub * C
    first = pl.program_id(1) == 0

    @pl.when(first)
    def _():
        s_ref[...] = s0_ref[...]

    def keep(a):
        if n_valid == C:
            return a
        rows = lax.broadcasted_iota(jnp.int32, (a.shape[0], 1), 0)
        return jnp.where((rows & (C - 1)) < n_valid, a, 0.0)

    ri = lax.broadcasted_iota(jnp.int32, (C, C), 0)
    ci = lax.broadcasted_iota(jnp.int32, (C, C), 1)
    lower = ri >= ci
    strict = ri > ci

    def conv_rows(prev, cur, col0):
        n = cur.shape[0]
        x = jnp.concatenate([prev, cur], 0)
        w = cw_ref[:, col0:col0 + A_QK]
        acc = x[SUBLANES:SUBLANES + n] * w[A_CONV - 1:A_CONV]
        for s in range(1, A_CONV):
            acc = acc + x[SUBLANES - s:SUBLANES - s + n] * w[A_CONV - 1 - s:A_CONV - s]
        return acc * jax.nn.sigmoid(acc)

    def conv(cur_ref, prev_ref, first_ref, col0):
        if chained:
            return conv_rows(jnp.where(first, first_ref[0], prev_ref[...]), cur_ref[...], col0)
        return jnp.concatenate(
            [conv_rows(first_ref[s], cur_ref[s * C:(s + 1) * C, :], col0) for s in range(nsub)], 0)

    qx = conv(qc, qp, qf, 0)
    kx = conv(kc, kp, kf, A_QK)
    vx = conv(vc, vp, vf, 2 * A_QK)

    sm = sm_ref[...]
    beta_all = keep(jax.nn.sigmoid(sm))
    g_all = keep(-jnp.exp(alog_ref[...]) * jax.nn.softplus(sm + dtb_ref[...]))
    rr = lax.broadcasted_iota(jnp.int32, (R, R), 0)
    rc = lax.broadcasted_iota(jnp.int32, (R, R), 1)
    sh = C.bit_length() - 1
    same_chunk_lower = jnp.logical_and(rr >= rc, (rr >> sh) == (rc >> sh))
    g_cum = _dot_exact_lhs(same_chunk_lower.astype(BF16), g_all)
    g_cum_t = g_cum.T
    e_g = jnp.exp(g_cum)

    heads = range(A_HEADS)
    subs = range(nsub)
    items = [(s, h) for s in subs for h in heads]
    hs = [slice(h * A_DK, (h + 1) * A_DK) for h in heads]
    rs = [slice(s * C, (s + 1) * C) for s in subs]
    ld = lambda h: slice(LANE_DECAY + h, LANE_DECAY + h + 1)
    qs = [qx[rs[s], hs[h]] for s, h in items]
    qs = [q * lax.rsqrt(jnp.sum(q * q, -1, keepdims=True) + EPS) * (A_DK ** -0.5) for q in qs]
    ks = [kx[rs[s], hs[h]] for s, h in items]
    ks = [keep(k * lax.rsqrt(jnp.sum(k * k, -1, keepdims=True) + EPS)) for k in ks]
    vs = [keep(vx[rs[s], hs[h]]) for s, h in items]
    betas = [beta_all[rs[s], LANE_BETA + h:LANE_BETA + h + 1] for s, h in items]
    egs = [e_g[rs[s], ld(h)] for s, h in items]
    gammas = [jnp.exp(jnp.where(lower, g_cum[rs[s], ld(h)] - g_cum_t[ld(h), rs[s]], -jnp.inf))
              for s, h in items]
    kbs = [k * b for k, b in zip(ks, betas)]
    a1s = [lax.dot_general(jnp.concatenate([kb, q], 0).astype(BF16), k.astype(BF16),
                           (((1,), (1,)), ((), ())), preferred_element_type=F32)
           for kb, q, k in zip(kbs, qs, ks)]
    ms = [jnp.where(strict, a1[:C] * gm, 0.0) for a1, gm in zip(a1s, gammas)]
    attns = [a1[C:] * gm for a1, gm in zip(a1s, gammas)]
    rhss = [jnp.concatenate([v * b, kb * eg], 1) for v, b, kb, eg in zip(vs, betas, kbs, egs)]
    sols = [r + _bdot(q.astype(BF16), r.astype(BF16))
            for q, r in zip(_unit_lower_inverse_offsets(ms, ri, ci, C), rhss)]
    wq_lhs = [jnp.concatenate([sol[:, A_DV:], q * eg], 0).astype(BF16) for sol, q, eg in zip(sols, qs, egs)]
    g_lasts = [g_cum[(s + 1) * C - 1:(s + 1) * C, :] for s in subs]
    kg_ts = [(k * jnp.exp(g_lasts[s][:, ld(h)] - g_cum[rs[s], ld(h)])).T
             for (s, h), k in zip(items, ks)]
    states = [s_ref[0, h] for h in heads]
    for s in subs:
        if not chained:
            states = [s_ref[s, h] for h in heads]
        at = lambda xs: xs[s * A_HEADS:(s + 1) * A_HEADS]
        wqs = [_bdot(l, st.astype(BF16)) for l, st in zip(at(wq_lhs), states)]
        v_news = [sol[:, :A_DV] - wq[:C] for sol, wq in zip(at(sols), wqs)]
        r2s = [_bdot(jnp.concatenate([a, kg], 0).astype(BF16), vn.astype(BF16))
               for a, kg, vn in zip(at(attns), at(kg_ts), v_news)]
        e_last = jnp.exp(g_lasts[s])
        states = [st * e_last[:, ld(h)] + r2[C:] for h, st, r2 in zip(heads, states, r2s)]
        for h in heads:
            o = wqs[h][C:] + r2s[h][:C]
            on = (o * lax.rsqrt(jnp.mean(o * o, -1, keepdims=True) + EPS)) * anw_ref[...]
            zh = zc[rs[s], hs[h]]
            ya_ref[rs[s], hs[h]] = (on * (zh * jax.nn.sigmoid(zh))).astype(BF16)
        if not chained:
            for h in heads:
                s_ref[s, h] = states[h]
    if chained:
        for h in heads:
            s_ref[0, h] = states[h]


def _delta_mixer(proj, small, first_prev, s0, conv_w, alog_row, dtb_row, anw_row, *, nb, nc, C, n_valid,
                 shared_first, nsub=1, chained=True):
    rows = nb * nc * C
    assert proj.shape[0] == rows and C % SUBLANES == 0 and C & (C - 1) == 0
    R = nsub * C
    if chained:
        assert nc % nsub == 0
        nseq, steps = 1, nc // nsub
    else:
        assert nc == 1 and nb % nsub == 0 and not shared_first
        nseq, steps, nb = nsub, 1, nb // nsub
    cpb = R // SUBLANES
    cq, ck, cv, cz = (COL_QKV // A_QK, COL_QKV // A_QK + 1, COL_QKV // A_QK + 2, COL_Z // A_V)
    fb = (lambda b: 0) if shared_first else (lambda b: b)
    cur = lambda col: pl.BlockSpec((R, A_QK), lambda b, c: (b * steps + c, col))
    prev = lambda col: pl.BlockSpec(
        (SUBLANES, A_QK), lambda b, c: (jnp.maximum((b * steps + c) * cpb - 1, 0), col))
    frst = lambda j: pl.BlockSpec((nseq, SUBLANES, A_QK), lambda b, c: (fb(b), 0, j))
    row1 = lambda n: pl.BlockSpec((1, n), lambda b, c: (0, 0))
    return pl.pallas_call(
        functools.partial(_delta_kernel, C=C, n_valid=n_valid, nsub=nsub, chained=chained),
        grid=(nb, steps),
        in_specs=[
            cur(cq), cur(ck), cur(cv), cur(cz),
            prev(cq), prev(ck), prev(cv),
            frst(0), frst(1), frst(2),
            pl.BlockSpec((R, SMALL_PAD), lambda b, c: (b * steps + c, 0)),
            pl.BlockSpec((A_CONV, A_CONV_DIM), lambda b, c: (0, 0)),
            row1(SMALL_PAD), row1(SMALL_PAD), row1(A_DV),
            pl.BlockSpec((nseq, A_HEADS, A_DK, A_DV), lambda b, c: (fb(b), 0, 0, 0)),
        ],
        out_specs=[
            pl.BlockSpec((R, A_V), lambda b, c: (b * steps + c, 0)),
            pl.BlockSpec((nseq, A_HEADS, A_DK, A_DV), lambda b, c: (b, 0, 0, 0)),
        ],
        out_shape=[
            jax.ShapeDtypeStruct((rows, A_V), BF16),
            jax.ShapeDtypeStruct((nb * nseq, A_HEADS, A_DK, A_DV), F32),
        ],
        compiler_params=pltpu.CompilerParams(
            dimension_semantics=("arbitrary", "arbitrary"), vmem_limit_bytes=VMEM_LIMIT),
        name="delta_mixer",
    )(proj, proj, proj, proj, proj, proj, proj, first_prev, first_prev, first_prev, small,
      conv_w, alog_row, dtb_row, anw_row, s0)


def _merge_kernel(ya_ref, yb_ref, ga_ref, gb_ref, wa_ref, wb_ref, o_ref):
    ua = _bdot(ya_ref[...].astype(BF16), wa_ref[...])
    ub = _bdot(yb_ref[...].astype(BF16), wb_ref[...])
    merged = jax.nn.sigmoid(ga_ref[...]) * ua + jax.nn.sigmoid(gb_ref[...]) * ub
    o_ref[...] = merged.astype(BF16)


def _merge(ya, yb, proj, wa, wb, tm):
    m = ya.shape[0]
    assert m % tm == 0
    return pl.pallas_call(
        _merge_kernel,
        grid=(m // tm,),
        in_specs=[
            pl.BlockSpec((tm, A_V), lambda i: (i, 0)),
            pl.BlockSpec((tm, B_Q), lambda i: (i, 0)),
            pl.BlockSpec((tm, D_MODEL), lambda i: (i, COL_GA // D_MODEL)),
            pl.BlockSpec((tm, D_MODEL), lambda i: (i, COL_GB // D_MODEL)),
            pl.BlockSpec((A_V, D_MODEL), lambda i: (0, 0)),
            pl.BlockSpec((B_Q, D_MODEL), lambda i: (0, 0)),
        ],
        out_specs=pl.BlockSpec((tm, D_MODEL), lambda i: (i, 0)),
        out_shape=jax.ShapeDtypeStruct((m, D_MODEL), BF16),
        compiler_params=pltpu.CompilerParams(
            dimension_semantics=("arbitrary",), vmem_limit_bytes=VMEM_LIMIT),
        name="merge",
    )(ya, yb, proj, proj, wa, wb)


LANE_SENTINEL = ROUTER_PAD - 1


def _lane_argmax(vals, eligible, lane):
    top = jnp.max(jnp.where(eligible, vals, -jnp.inf), -1, keepdims=True)
    idx = jnp.min(jnp.where(jnp.logical_and(eligible, vals == top), lane, LANE_SENTINEL), -1, keepdims=True)
    return top, idx


def _masked_softmax(logits, eligible):
    z = jnp.where(eligible, logits, -jnp.inf)
    e = jnp.exp(z - jnp.max(z, -1, keepdims=True))
    return e / jnp.sum(e, -1, keepdims=True)


def _outproj_kernel(m_ref, h_ref, wo_ref, g_ref, wr_ref, br_ref, cnt_in_ref,
                    h2_ref, xn_ref, gate_ref, route_ref, cnt_ref):
    tm = h_ref.shape[0]

    @pl.when(pl.program_id(0) == 0)
    def _():
        cnt_ref[...] = cnt_in_ref[...]

    h2 = h_ref[...] + _bdot(m_ref[...], wo_ref[...])
    h2_ref[...] = h2
    ms = jnp.mean(h2 * h2, axis=-1, keepdims=True)
    xn = (h2 * lax.rsqrt(ms + EPS)) * g_ref[...]
    _store_slabs(xn_ref, xn)
    lg = _bdot(xn.astype(BF16), wr_ref[...]) + br_ref[...]

    lane = lax.broadcasted_iota(jnp.int32, (tm, ROUTER_PAD), 1)
    is_g = lane < N_GROUPS
    p_grp, grp = _lane_argmax(_masked_softmax(lg, is_g), is_g, lane)
    ex = lane - N_GROUPS
    is_e = jnp.logical_and(jnp.logical_and(ex >= 0, ex < N_EXPERTS), (ex >> 3) == grp)
    pe = _masked_softmax(lg, is_e)
    p1, i1 = _lane_argmax(pe, is_e, lane)
    rest = jnp.logical_and(is_e, lane != i1)
    p2, i2 = _lane_argmax(pe, rest, lane)
    den = p1 + p2
    gate_ref[...] = jnp.where(lane == 0, (p_grp * p1) / den, jnp.where(lane == 1, (p_grp * p2) / den, 0.0))

    e1 = i1 - N_GROUPS
    e2 = i2 - N_GROUPS
    oh1 = lane == e1
    oh2 = lane == e2
    ri = lax.broadcasted_iota(jnp.int32, (tm, tm), 0)
    ci = lax.broadcasted_iota(jnp.int32, (tm, tm), 1)
    below = (ri > ci).astype(BF16)
    f1 = oh1.astype(F32)
    f2 = oh2.astype(F32)
    tot1 = jnp.sum(f1, 0, keepdims=True)
    run = cnt_ref[...]
    before1 = run + _bdot(below, oh1.astype(BF16))
    before2 = run + tot1 + _bdot(below, oh2.astype(BF16))
    rank1 = jnp.sum(f1 * before1, -1, keepdims=True).astype(jnp.int32)
    rank2 = jnp.sum(f2 * before2, -1, keepdims=True).astype(jnp.int32)
    cnt_ref[...] = run + tot1 + jnp.sum(f2, 0, keepdims=True)
    route_ref[...] = jnp.where(lane == 0, e1, jnp.where(lane == 1, e2, jnp.where(
        lane == 2, rank1, jnp.where(lane == 3, rank2, 0))))


def _outproj(merged, h, wo, gain2, w_router, b_router, counts_in, tm):
    m = h.shape[0]
    assert m % tm == 0
    row = lambda n: pl.BlockSpec((1, n), lambda i: (0, 0))
    tile = lambda n: pl.BlockSpec((tm, n), lambda i: (i, 0))
    return pl.pallas_call(
        _outproj_kernel,
        grid=(m // tm,),
        in_specs=[
            tile(D_MODEL), tile(D_MODEL),
            pl.BlockSpec((D_MODEL, D_MODEL), lambda i: (0, 0)),
            row(D_MODEL),
            pl.BlockSpec((D_MODEL, ROUTER_PAD), lambda i: (0, 0)),
            row(ROUTER_PAD), row(ROUTER_PAD),
        ],
        out_specs=[tile(D_MODEL), pl.BlockSpec((tm, ROW_SLABS, SLAB), lambda i: (i, 0, 0)),
                   tile(ROUTER_PAD), tile(ROUTER_PAD), row(ROUTER_PAD)],
        out_shape=[
            jax.ShapeDtypeStruct((m, D_MODEL), F32),
            jax.ShapeDtypeStruct((m, ROW_SLABS, SLAB), F32),
            jax.ShapeDtypeStruct((m, ROUTER_PAD), F32),
            jax.ShapeDtypeStruct((m, ROUTER_PAD), jnp.int32),
            jax.ShapeDtypeStruct((1, ROUTER_PAD), F32),
        ],
        compiler_params=pltpu.CompilerParams(
            dimension_semantics=("arbitrary",), vmem_limit_bytes=VMEM_LIMIT),
        name="outproj",
    )(merged, h, wo, gain2, w_router, b_router, counts_in)


def _moe_kernel(blk_e_ref, nused_ref, nxt_e_ref, slot_ref, x_ref, wg_hbm, wu_hbm, wd_hbm, o_ref,
                wg32, wu32, wd32, wgb, wub, wdb, sem):
    i = pl.program_id(0)
    e = blk_e_ref[i]
    e_prev = blk_e_ref[jnp.maximum(i - 1, 0)]
    used = i < nused_ref[0]

    def copies(expert, slot):
        return (pltpu.make_async_copy(wg_hbm.at[expert], wg32.at[slot], sem.at[slot]),
                pltpu.make_async_copy(wu_hbm.at[expert], wu32.at[slot], sem.at[slot]),
                pltpu.make_async_copy(wd_hbm.at[expert], wd32.at[slot], sem.at[slot]))

    @pl.when(i == 0)
    def _():
        for c in copies(e, 0):
            c.start()

    @pl.when(jnp.logical_and(used, jnp.logical_or(i == 0, e != e_prev)))
    def _():
        slot = slot_ref[i]
        for c in copies(e, slot):
            c.wait()
        nxt = nxt_e_ref[i]

        @pl.when(nxt != e)
        def _():
            for c in copies(nxt, 1 - slot):
                c.start()

        def cast_in(r, c):
            sl = pl.ds(pl.multiple_of(r * 256, 256), 256)
            wgb[sl, :] = wg32[slot, sl, :].astype(BF16)
            wub[sl, :] = wu32[slot, sl, :].astype(BF16)
            return c

        lax.fori_loop(0, D_MODEL // 256, cast_in, 0)

        def cast_out(r, c):
            sl = pl.ds(pl.multiple_of(r * 128, 128), 128)
            wdb[sl, :] = wd32[slot, sl, :].astype(BF16)
            return c

        lax.fori_loop(0, D_EXPERT // 128, cast_out, 0)

    @pl.when(used)
    def _():
        x = _load_slabs(x_ref).astype(BF16)
        g = _bdot(x, wgb[...])
        u = _bdot(x, wub[...])
        hb = (g * jax.nn.sigmoid(g)) * u
        _store_slabs(o_ref, _bdot(hb.astype(BF16), wdb[...]))

    @pl.when(jnp.logical_not(used))
    def _():
        o_ref[...] = jnp.zeros_like(o_ref)


def _moe_ffn(xs, blk_e, n_used, nxt_e, slot, w_gate, w_up, w_down):
    p = xs.shape[0]
    nb = p // MOE_ROWS
    grid_spec = pltpu.PrefetchScalarGridSpec(
        num_scalar_prefetch=4,
        grid=(nb,),
        in_specs=[
            pl.BlockSpec((MOE_ROWS, ROW_SLABS, SLAB), lambda i, be, nu, nx, sl: (jnp.minimum(i, nu[0] - 1), 0, 0)),
            pl.BlockSpec(memory_space=pl.ANY),
            pl.BlockSpec(memory_space=pl.ANY),
            pl.BlockSpec(memory_space=pl.ANY),
        ],
        out_specs=pl.BlockSpec((MOE_ROWS, ROW_SLABS, SLAB), lambda i, be, nu, nx, sl: (i, 0, 0)),
        scratch_shapes=[
            pltpu.VMEM((2, D_MODEL, D_EXPERT), F32),
            pltpu.VMEM((2, D_MODEL, D_EXPERT), F32),
            pltpu.VMEM((2, D_EXPERT, D_MODEL), F32),
            pltpu.VMEM((D_MODEL, D_EXPERT), BF16),
            pltpu.VMEM((D_MODEL, D_EXPERT), BF16),
            pltpu.VMEM((D_EXPERT, D_MODEL), BF16),
            pltpu.SemaphoreType.DMA((2,)),
        ],
    )
    return pl.pallas_call(
        _moe_kernel,
        grid_spec=grid_spec,
        out_shape=jax.ShapeDtypeStruct((p, ROW_SLABS, SLAB), F32),
        compiler_params=pltpu.CompilerParams(
            dimension_semantics=("arbitrary",), vmem_limit_bytes=VMEM_LIMIT),
        name="moe_ffn",
    )(blk_e, n_used, nxt_e, slot, xs, w_gate, w_up, w_down)


HALF_TILE = 64
SWA_HEADS_PER_STAGE = 16
LANES = 128


def _dot_exact_rhs(a, b01):
    a1 = a.astype(BF16)
    r1 = a - a1.astype(F32)
    a2 = r1.astype(BF16)
    a3 = (r1 - a2.astype(F32)).astype(BF16)
    return _bdot(a1, b01) + (_bdot(a2, b01) + _bdot(a3, b01))


def _group_rms_rope(x, gain, cos128, sin128, bd, bdt):
    r, w = x.shape
    ssq = _dot_exact_rhs(x * x, bd)
    rs = lax.rsqrt(ssq * (1.0 / B_HD) + EPS)
    y = (x * _dot_exact_rhs(rs, bdt)) * gain
    reps = w // LANES
    cosw = jnp.concatenate([cos128] * reps, 1)
    sinw = jnp.concatenate([sin128] * reps, 1)
    lane = lax.broadcasted_iota(jnp.int32, (r, w), 1)
    swapped = jnp.where((lane & (B_HD // 2)) == 0,
                        pltpu.roll(y, w - B_HD // 2, 1), pltpu.roll(y, B_HD // 2, 1))
    return y * cosw + swapped * sinw


def _kv_tiles(tile, odd, lo):
    rolled = pltpu.roll(tile, HALF_TILE, 1)
    dup = jnp.where(lo, rolled, tile) if odd else jnp.where(lo, tile, rolled)
    return dup.astype(BF16), jnp.where(lo, dup, 0.0).astype(BF16), jnp.where(lo, 0.0, dup).astype(BF16)


def _nt(a, b):
    return lax.dot_general(a, b, (((1,), (1,)), ((), ())), preferred_element_type=F32)


def _sink_softmax_pv(score_lists, value_lists, sinks):
    ms = []
    for scores, sink in zip(score_lists, sinks):
        m = jnp.max(scores[0], -1, keepdims=True)
        for s in scores[1:]:
            m = jnp.maximum(m, jnp.max(s, -1, keepdims=True))
        ms.append(jnp.maximum(m, sink))
    es = [[jnp.exp(s - m) for s in scores] for scores, m in zip(score_lists, ms)]
    dens = []
    for e_blocks, m, sink in zip(es, ms, sinks):
        den = jnp.exp(sink - m)
        for e in e_blocks:
            den = den + jnp.sum(e, -1, keepdims=True)
        dens.append(den)
    pvs = [[_bdot(e.astype(BF16), v) for e, v in zip(e_blocks, values)]
           for e_blocks, values in zip(es, value_lists)]
    outs = []
    for pv, den in zip(pvs, dens):
        acc = pv[0]
        for x in pv[1:]:
            acc = acc + x
        outs.append(acc / den)
    return outs


def _rope_small_kernel(q_ref, k_ref, cos_ref, sin_ref, gq_ref, gk_ref, bd_ref, bdt_ref, qo_ref, ko_ref):
    cos = cos_ref[...]
    sin = sin_ref[...]
    qo_ref[...] = _group_rms_rope(q_ref[...], gq_ref[...], cos, sin, bd_ref[...], bdt_ref[...])
    ko_ref[...] = _group_rms_rope(k_ref[...], gk_ref[...], cos, sin, bd_ref[:B_KV, :], bdt_ref[:, :B_KV])


def _rope_small(proj, cos, sin, gq, gk, bd, bdt):
    m = proj.shape[0]
    full = lambda shape: pl.BlockSpec(shape, lambda i: (0, 0))
    return pl.pallas_call(
        _rope_small_kernel,
        grid=(1,),
        in_specs=[
            pl.BlockSpec((m, B_Q), lambda i: (0, COL_QB // B_Q)),
            pl.BlockSpec((m, B_KV), lambda i: (0, COL_KB // B_KV)),
            full((m, LANES)), full((m, LANES)), full((1, B_Q)), full((1, B_KV)),
            full((B_Q, LANES)), full((LANES, B_Q)),
        ],
        out_specs=[full((m, B_Q)), full((m, B_KV))],
        out_shape=[jax.ShapeDtypeStruct((m, B_Q), F32), jax.ShapeDtypeStruct((m, B_KV), F32)],
        compiler_params=pltpu.CompilerParams(
            dimension_semantics=("arbitrary",), vmem_limit_bytes=VMEM_LIMIT),
        name="rope_small",
    )(proj, proj, cos, sin, gq, gk, bd, bdt)


def _swa_prompt_kernel(sinks_ref, q_ref, k_ref, v_ref, cos_ref, sin_ref, km_ref, vm_ref, gq_ref, gk_ref,
                       bd_ref, bdt_ref, y_ref, kc_ref, vc_ref, kprev, vloprev, vhiprev):
    n = pl.program_id(1)

    @pl.when(n == 0)
    def _():
        kprev[...] = jnp.zeros_like(kprev)
        vloprev[...] = jnp.zeros_like(vloprev)
        vhiprev[...] = jnp.zeros_like(vhiprev)

    cos = cos_ref[...]
    sin = sin_ref[...]
    q = _group_rms_rope(q_ref[...], gq_ref[...], cos, sin, bd_ref[...], bdt_ref[...])
    k = _group_rms_rope(k_ref[...], gk_ref[...], cos, sin, bd_ref[:B_KV, :], bdt_ref[:, :B_KV])
    v = v_ref[...]
    kc_ref[...] = k
    vc_ref[...] = v
    km = km_ref[...]
    vm = vm_ref[...]

    qi = lax.broadcasted_iota(jnp.int32, (WINDOW, WINDOW), 0)
    kj = lax.broadcasted_iota(jnp.int32, (WINDOW, WINDOW), 1)
    cur_ok = kj <= qi
    prev_ok = jnp.logical_and(kj > qi, n > 0)
    lo = lax.broadcasted_iota(jnp.int32, (WINDOW, LANES), 1) < HALF_TILE
    lo_m = lax.broadcasted_iota(jnp.int32, (N_META, LANES), 1) < HALF_TILE
    scale = B_HD ** -0.5

    kv = []
    for g in range(B_KV_HEADS):
        tl = slice((g // 2) * LANES, (g // 2 + 1) * LANES)
        k2c, _, _ = _kv_tiles(k[:, tl], g % 2, lo)
        _, vlo_c, vhi_c = _kv_tiles(v[:, tl], g % 2, lo)
        k2m, _, _ = _kv_tiles(km[:, tl], g % 2, lo_m)
        _, vlo_m, vhi_m = _kv_tiles(vm[:, tl], g % 2, lo_m)
        kv.append(((k2m, kprev[g], k2c), (vlo_m, vloprev[g], vlo_c), (vhi_m, vhiprev[g], vhi_c)))
    heads = range(B_HEADS)
    qms = []
    for h in heads:
        qt = q[:, (h // 2) * LANES:(h // 2 + 1) * LANES]
        qms.append((jnp.where(lo, qt, 0.0) if h % 2 == 0 else jnp.where(lo, 0.0, qt)).astype(BF16))
    for h0 in range(0, B_HEADS, SWA_HEADS_PER_STAGE):
        hg = range(h0, h0 + SWA_HEADS_PER_STAGE)
        raw = [[_nt(qms[h], kk) * scale for kk in kv[h // B_GROUP][0]] for h in hg]
        scores = [[sm, jnp.where(prev_ok, sp, -jnp.inf), jnp.where(cur_ok, sc, -jnp.inf)] for sm, sp, sc in raw]
        outs = _sink_softmax_pv(scores, [kv[h // B_GROUP][1 + h % 2] for h in hg], [sinks_ref[h] for h in hg])
        for t in range(SWA_HEADS_PER_STAGE // 2):
            y_ref[:, (h0 // 2 + t) * LANES:(h0 // 2 + t + 1) * LANES] = (outs[2 * t] + outs[2 * t + 1]).astype(BF16)
    for g in range(B_KV_HEADS):
        kprev[g] = kv[g][0][2]
        vloprev[g] = kv[g][1][2]
        vhiprev[g] = kv[g][2][2]


def _swa_prompt(proj, cos, sin, k_meta, v_meta, gq, gk, bd, bdt, sinks, nb, nblk):
    rows = nb * nblk * WINDOW
    assert proj.shape[0] == rows
    return pl.pallas_call(
        _swa_prompt_kernel,
        grid=(nb, nblk),
        in_specs=[
            pl.BlockSpec(memory_space=pltpu.SMEM),
            pl.BlockSpec((WINDOW, B_Q), lambda b, n: (b * nblk + n, COL_QB // B_Q)),
            pl.BlockSpec((WINDOW, B_KV), lambda b, n: (b * nblk + n, COL_KB // B_KV)),
            pl.BlockSpec((WINDOW, B_KV), lambda b, n: (b * nblk + n, COL_VB // B_KV)),
            pl.BlockSpec((WINDOW, LANES), lambda b, n: (n, 0)),
            pl.BlockSpec((WINDOW, LANES), lambda b, n: (n, 0)),
            pl.BlockSpec((N_META, B_KV), lambda b, n: (0, 0)),
            pl.BlockSpec((N_META, B_KV), lambda b, n: (0, 0)),
            pl.BlockSpec((1, B_Q), lambda b, n: (0, 0)),
            pl.BlockSpec((1, B_KV), lambda b, n: (0, 0)),
            pl.BlockSpec((B_Q, LANES), lambda b, n: (0, 0)),
            pl.BlockSpec((LANES, B_Q), lambda b, n: (0, 0)),
        ],
        out_specs=[
            pl.BlockSpec((WINDOW, B_Q), lambda b, n: (b * nblk + n, 0)),
            pl.BlockSpec((None, WINDOW, B_KV), lambda b, n: (b, 0, 0)),
            pl.BlockSpec((None, WINDOW, B_KV), lambda b, n: (b, 0, 0)),
        ],
        scratch_shapes=[pltpu.VMEM((B_KV_HEADS, WINDOW, LANES), BF16)] * 3,
        out_shape=[
            jax.ShapeDtypeStruct((rows, B_Q), BF16),
            jax.ShapeDtypeStruct((nb, WINDOW, B_KV), F32),
            jax.ShapeDtypeStruct((nb, WINDOW, B_KV), F32),
        ],
        compiler_params=pltpu.CompilerParams(
            dimension_semantics=("arbitrary", "arbitrary"), vmem_limit_bytes=VMEM_LIMIT),
        name="swa_prompt",
    )(sinks, proj, proj, proj, cos, sin, k_meta, v_meta, gq, gk, bd, bdt)


def _swa_sample_kernel(sinks_ref, q_ref, k_ref, v_ref, y_ref, *, n_keys, n_new):
    nseq, rq, _ = q_ref.shape
    t = lax.broadcasted_iota(jnp.int32, (rq, n_keys), 0)
    r = lax.broadcasted_iota(jnp.int32, (rq, n_keys), 1)
    wj = r - N_META
    win_pos = PAST_LEN - WINDOW + wj
    nm = r - N_META - WINDOW
    ok = ((r < N_META)
          | ((wj >= 0) & (wj < WINDOW) & (win_pos >= N_META) & (wj >= t + 1))
          | ((nm >= 0) & (nm <= t) & (nm > t - WINDOW) & (nm < n_new)))
    lo_q = lax.broadcasted_iota(jnp.int32, (rq, LANES), 1) < HALF_TILE
    lo_k = lax.broadcasted_iota(jnp.int32, (n_keys, LANES), 1) < HALF_TILE
    scale = B_HD ** -0.5
    heads = range(B_HEADS)
    scores, values = [], []
    for s in range(nseq):
        q = q_ref[s]
        k = k_ref[s]
        v = v_ref[s]
        kv = []
        for g in range(B_KV_HEADS):
            tl = slice((g // 2) * LANES, (g // 2 + 1) * LANES)
            k2, _, _ = _kv_tiles(k[:, tl], g % 2, lo_k)
            _, vlo, vhi = _kv_tiles(v[:, tl], g % 2, lo_k)
            kv.append((k2, vlo, vhi))
        for h in heads:
            qt = q[:, (h // 2) * LANES:(h // 2 + 1) * LANES]
            qm = (jnp.where(lo_q, qt, 0.0) if h % 2 == 0 else jnp.where(lo_q, 0.0, qt)).astype(BF16)
            scores.append([jnp.where(ok, _nt(qm, kv[h // B_GROUP][0]) * scale, -jnp.inf)])
            values.append([kv[h // B_GROUP][1 + h % 2]])
    outs = _sink_softmax_pv(scores, values, [sinks_ref[h] for _s in range(nseq) for h in heads])
    for s in range(nseq):
        for t2 in range(B_HEADS // 2):
            pair = outs[s * B_HEADS + 2 * t2] + outs[s * B_HEADS + 2 * t2 + 1]
            y_ref[s, :, t2 * LANES:(t2 + 1) * LANES] = pair.astype(BF16)


def _swa_sample(q8, kk, vv, sinks, n_new):
    nb, rq, _ = q8.shape
    n_keys = kk.shape[1]
    nseq = SAMPLE_SEQS_PER_STEP
    assert nb % nseq == 0
    return pl.pallas_call(
        functools.partial(_swa_sample_kernel, n_keys=n_keys, n_new=n_new),
        grid=(nb // nseq,),
        in_specs=[
            pl.BlockSpec(memory_space=pltpu.SMEM),
            pl.BlockSpec((nseq, rq, B_Q), lambda b: (b, 0, 0)),
            pl.BlockSpec((nseq, n_keys, B_KV), lambda b: (b, 0, 0)),
            pl.BlockSpec((nseq, n_keys, B_KV), lambda b: (b, 0, 0)),
        ],
        out_specs=pl.BlockSpec((nseq, rq, B_Q), lambda b: (b, 0, 0)),
        out_shape=jax.ShapeDtypeStruct((nb, rq, B_Q), BF16),
        compiler_params=pltpu.CompilerParams(
            dimension_semantics=("arbitrary",), vmem_limit_bytes=VMEM_LIMIT),
        name="swa_sample",
    )(sinks, q8, kk, vv)


def _rope_tables(pos):
    half = B_HD // 2
    inv_freq = ROPE_THETA ** (-jnp.arange(half, dtype=F32) / half)
    ang = pos.astype(F32)[:, None] * inv_freq[None, :]
    cos, sin = jnp.cos(ang), jnp.sin(ang)
    return jnp.concatenate([cos, cos, cos, cos], 1), jnp.concatenate([-sin, sin, -sin, sin], 1)


ROUTE_ROWS = 512
DMA_UNROLL = 8


def _row(ref, r):
    return ref.at[pl.ds(r, 1)]


def _dispatch_kernel(pend_ref, dest_ref, xa_ref, xb_ref, xs_hbm, zeros_vmem, ring, sem_zero, sem, *, nta, nt, nb):
    i = pl.program_id(0)
    tm = ROUTE_ROWS
    slot = i % 2

    def drain(s):
        for _k in range(TOP_K):
            pltpu.make_async_copy(ring.at[s], xs_hbm.at[pl.ds(0, tm)], sem.at[s]).wait()

    @pl.when(i == 0)
    def _():
        zeros_vmem[...] = jnp.zeros_like(zeros_vmem)
        fill = lambda row0: pltpu.make_async_copy(zeros_vmem, xs_hbm.at[pl.ds(row0, MOE_ROWS)], sem_zero)
        for e in range(N_EXPERTS):
            fill(jnp.maximum(pend_ref[e] - MOE_ROWS, 0)).start()
        for e in range(N_EXPERTS):
            fill(0).wait()
        n_used = pend_ref[N_EXPERTS - 1] // MOE_ROWS

        def fill_tail(b, c):
            fill(b * MOE_ROWS).start()
            fill(0).wait()
            return c

        lax.fori_loop(n_used, nb, fill_tail, 0)

    @pl.when(i >= 2)
    def _():
        drain(slot)

    @pl.when(i < nta)
    def _():
        ring[slot] = xa_ref[...]

    @pl.when(i >= nta)
    def _():
        ring[slot] = xb_ref[...]

    def body(r, c):
        src = ring.at[slot, pl.ds(r, 1)]
        for k in range(TOP_K):
            pltpu.make_async_copy(src, _row(xs_hbm, dest_ref[0, TOP_K * r + k]), sem.at[slot]).start(priority=k)
        return c

    lax.fori_loop(0, tm, body, 0, unroll=DMA_UNROLL)

    @pl.when(i == nt - 1)
    def _():
        if nt >= 2:
            drain(1 - slot)
        drain(slot)


def _dispatch(xa, xb, dest, pend, n_slots):
    assert xa.shape[0] % ROUTE_ROWS == 0 and xb.shape[0] % ROUTE_ROWS == 0
    nta = xa.shape[0] // ROUTE_ROWS
    nt = nta + xb.shape[0] // ROUTE_ROWS
    slab_rows = (ROUTE_ROWS, ROW_SLABS, SLAB)
    grid_spec = pltpu.PrefetchScalarGridSpec(
        num_scalar_prefetch=1,
        grid=(nt,),
        in_specs=[
            pl.BlockSpec((None, 1, TOP_K * ROUTE_ROWS), lambda i, pe: (i, 0, 0), memory_space=pltpu.SMEM),
            pl.BlockSpec(slab_rows, lambda i, pe: (jnp.minimum(i, nta - 1), 0, 0)),
            pl.BlockSpec(slab_rows, lambda i, pe: (jnp.maximum(i - nta, 0), 0, 0)),
        ],
        out_specs=pl.BlockSpec(memory_space=pl.ANY),
        scratch_shapes=[
            pltpu.VMEM((MOE_ROWS, ROW_SLABS, SLAB), F32),
            pltpu.VMEM((2,) + slab_rows, F32),
            pltpu.SemaphoreType.DMA(()),
            pltpu.SemaphoreType.DMA((2,)),
        ],
    )
    return pl.pallas_call(
        functools.partial(_dispatch_kernel, nta=nta, nt=nt, nb=n_slots // MOE_ROWS),
        grid_spec=grid_spec,
        out_shape=jax.ShapeDtypeStruct((n_slots, ROW_SLABS, SLAB), F32),
        compiler_params=pltpu.CompilerParams(
            dimension_semantics=("arbitrary",), vmem_limit_bytes=VMEM_LIMIT, has_side_effects=True),
        name="moe_dispatch",
    )(pend, dest.reshape(nt, 1, TOP_K * ROUTE_ROWS), xa, xb)


def _combine_kernel(dest_ref, dest_next_ref, gate_ref, h_ref, ys_hbm, o_ref, buf, sem, *, nt):
    i = pl.program_id(0)
    tm = ROUTE_ROWS
    slot = i % 2

    def issue(dref, s):
        def body(r, c):
            for k in range(TOP_K):
                pltpu.make_async_copy(_row(ys_hbm, dref[0, TOP_K * r + k]), buf.at[s, k, pl.ds(r, 1)],
                                      sem.at[s]).start(priority=k)
            return c

        lax.fori_loop(0, tm, body, 0, unroll=DMA_UNROLL)

    @pl.when(i == 0)
    def _():
        issue(dest_ref, 0)

    @pl.when(i < nt - 1)
    def _():
        issue(dest_next_ref, 1 - slot)

    for k in range(TOP_K):
        pltpu.make_async_copy(ys_hbm.at[pl.ds(0, tm)], buf.at[slot, k], sem.at[slot]).wait()
    g = gate_ref[...]
    for s in range(ROW_SLABS):
        cols = slice(s * SLAB, (s + 1) * SLAB)
        o_ref[:, cols] = h_ref[:, cols] + (g[:, 0:1] * buf[slot, 0, :, s, :] + g[:, 1:2] * buf[slot, 1, :, s, :])


def _combine(h2, gates, dest, ys):
    t = h2.shape[0]
    assert t % ROUTE_ROWS == 0
    nt = t // ROUTE_ROWS
    dest3 = dest.reshape(nt, 1, TOP_K * ROUTE_ROWS)
    idx = lambda f: pl.BlockSpec((None, 1, TOP_K * ROUTE_ROWS), f, memory_space=pltpu.SMEM)
    return pl.pallas_call(
        functools.partial(_combine_kernel, nt=nt),
        grid=(nt,),
        in_specs=[
            idx(lambda i: (i, 0, 0)),
            idx(lambda i: (jnp.minimum(i + 1, nt - 1), 0, 0)),
            pl.BlockSpec((ROUTE_ROWS, ROUTER_PAD), lambda i: (i, 0)),
            pl.BlockSpec((ROUTE_ROWS, D_MODEL), lambda i: (i, 0)),
            pl.BlockSpec(memory_space=pl.ANY),
        ],
        out_specs=pl.BlockSpec((ROUTE_ROWS, D_MODEL), lambda i: (i, 0)),
        out_shape=jax.ShapeDtypeStruct((t, D_MODEL), F32),
        scratch_shapes=[
            pltpu.VMEM((2, TOP_K, ROUTE_ROWS, ROW_SLABS, SLAB), F32),
            pltpu.SemaphoreType.DMA((2,)),
        ],
        compiler_params=pltpu.CompilerParams(
            dimension_semantics=("arbitrary",), vmem_limit_bytes=VMEM_LIMIT),
        name="moe_combine",
    )(dest3, dest3, gates, h2, ys)


def _moe_plan(route, counts):
    T = route.shape[0]
    expert = route[:, :TOP_K]
    rank = route[:, TOP_K:2 * TOP_K]
    nb = -(-(T * TOP_K) // MOE_ROWS) + N_EXPERTS
    cnt = counts[0, :N_EXPERTS].astype(jnp.int32)
    padded = (cnt + MOE_ROWS - 1) // MOE_ROWS * MOE_ROWS
    pend = jnp.cumsum(padded).astype(jnp.int32)
    dest = (pend - padded)[expert] + rank
    starts = jnp.arange(nb, dtype=jnp.int32) * MOE_ROWS
    blk_e = jnp.minimum(jnp.sum((pend[None, :] <= starts[:, None]).astype(jnp.int32), 1), N_EXPERTS - 1)
    n_used = pend[-1:] // MOE_ROWS
    ids = jnp.arange(N_EXPERTS, dtype=jnp.int32)
    later = jnp.where(jnp.logical_and(ids[None, :] > ids[:, None], cnt[None, :] > 0), ids[None, :], N_EXPERTS)
    nxt = jnp.min(later, 1)
    nxt_e = jnp.where(nxt < N_EXPERTS, nxt, ids)[blk_e]
    order = jnp.cumsum((cnt > 0).astype(jnp.int32)) - 1
    slot = (order[blk_e] & 1).astype(jnp.int32)
    return dest, pend, blk_e, n_used, nxt_e.astype(jnp.int32), slot, nb * MOE_ROWS


def _lane_row(vals, lane0):
    return jnp.zeros((1, SMALL_PAD), F32).at[0, lane0:lane0 + vals.shape[0]].set(vals)


def kernel(x_prompt, x_sample, state_delta, state_conv, cache_swa_k, cache_swa_v, meta_tokens,
           norm1_w, w_in, conv_w, a_log, dt_bias, a_norm_w, w_up_a, q_norm_w, k_norm_w, sinks,
           w_up_b, w_o, norm2_w, w_router_group, b_router_group, w_router_expert, b_router_expert,
           w_gate, w_up, w_down):
    Bp, Sp, _ = x_prompt.shape
    Bs, Ss, _ = x_sample.shape
    n_s = Bs * Ss
    l = 0
    w = w_in[l]
    offs = [0]
    for s in (A_CONV_DIM, A_HEADS, A_HEADS, A_V, B_Q, B_KV, B_KV, D_MODEL, D_MODEL):
        offs.append(offs[-1] + s)
    seg = lambda a: w[:, offs[a]:offs[a + 1]]
    w_main = jnp.concatenate([seg(7), seg(8), seg(0), seg(3), seg(4), seg(5), seg(6)], 1).astype(BF16)
    w_small = jnp.concatenate(
        [seg(1), seg(2), jnp.zeros((D_MODEL, SMALL_PAD - 2 * A_HEADS), F32)], 1).astype(BF16)
    wa = w_up_a[l].astype(BF16)
    wb = w_up_b[l].astype(BF16)
    wo = w_o[l].astype(BF16)
    w_router = jnp.concatenate(
        [w_router_group[l], w_router_expert[l],
         jnp.zeros((D_MODEL, ROUTER_PAD - N_GROUPS - N_EXPERTS), F32)], 1).astype(BF16)
    b_router = jnp.concatenate(
        [b_router_group[l], b_router_expert[l],
         jnp.zeros((ROUTER_PAD - N_GROUPS - N_EXPERTS,), F32)])[None, :]
    g1 = norm1_w[l][None, :]
    g2 = norm2_w[l][None, :]
    alog_row = _lane_row(a_log[l], LANE_DECAY)
    dtb_row = _lane_row(dt_bias[l], LANE_DECAY)
    anw_row = a_norm_w[l][None, :]

    xp = x_prompt.reshape(Bp * Sp, D_MODEL)
    xs = x_sample.reshape(n_s, D_MODEL)
    x_small = jnp.concatenate([xs, meta_tokens], 0)
    n_small = x_small.shape[0]

    proj_p, small_p = _inproj(xp, g1, w_main, w_small, tm=1024)
    proj_s, small_s = _inproj(x_small, g1, w_main, w_small, tm=n_small)

    dm = functools.partial(_delta_mixer, conv_w=conv_w[l], alog_row=alog_row, dtb_row=dtb_row, anw_row=anw_row)
    pad_rows = lambda a, n: jnp.concatenate([a, jnp.zeros((n - a.shape[0],) + a.shape[1:], a.dtype)], 0)
    meta_proj = pad_rows(proj_s[n_s:], A_CHUNK)
    meta_small = pad_rows(small_s[n_s:], A_CHUNK)
    zero_prev = jnp.zeros((1, SUBLANES, A_CONV_DIM), F32)
    zero_state = jnp.zeros((1, A_HEADS, A_DK, A_DV), F32)
    _, s_meta = dm(meta_proj, meta_small, zero_prev, zero_state, nb=1, nc=1, C=A_CHUNK, n_valid=N_META,
                   shared_first=True)
    meta_tail = proj_s[n_s + N_META - SUBLANES:, COL_QKV:COL_QKV + A_CONV_DIM][None]
    ya_p, sdelta_p = dm(proj_p, small_p, meta_tail, s_meta, nb=Bp, nc=Sp // A_CHUNK, C=A_CHUNK,
                        n_valid=A_CHUNK, shared_first=True, nsub=DELTA_CHUNKS_PER_STEP)
    conv_p = proj_p.reshape(Bp, Sp, PROJ_MAIN)[:, Sp - (A_CONV - 1):, COL_QKV:COL_QKV + A_CONV_DIM]
    CS = SUBLANES
    samp_proj = jnp.pad(proj_s[:n_s].reshape(Bs, Ss, PROJ_MAIN), ((0, 0), (0, CS - Ss), (0, 0)))
    samp_small = jnp.pad(small_s[:n_s].reshape(Bs, Ss, SMALL_PAD), ((0, 0), (0, CS - Ss), (0, 0)))
    samp_prev = jnp.pad(state_conv[l], ((0, 0), (SUBLANES - (A_CONV - 1), 0), (0, 0)))
    ya_s8, sdelta_s = dm(samp_proj.reshape(Bs * CS, PROJ_MAIN), samp_small.reshape(Bs * CS, SMALL_PAD),
                         samp_prev, state_delta[l], nb=Bs, nc=1, C=CS, n_valid=Ss, shared_first=False,
                         nsub=SAMPLE_SEQS_PER_STEP, chained=False)
    ya_s = ya_s8.reshape(Bs, CS, A_V)[:, :Ss].reshape(n_s, A_V)
    qkv_s = proj_s[:n_s, COL_QKV:COL_QKV + A_CONV_DIM].reshape(Bs, Ss, A_CONV_DIM)
    conv_s = jnp.concatenate([state_conv[l], qkv_s], 1)[:, -(A_CONV - 1):]

    gq = jnp.tile(q_norm_w[l], B_HEADS)[None]
    gk = jnp.tile(k_norm_w[l], B_KV_HEADS)[None]
    bd = ((jnp.arange(B_Q) // B_HD)[:, None] == jnp.arange(LANES)[None, :]).astype(BF16)
    bdt = bd.T
    pos_small = jnp.concatenate([PAST_LEN + jnp.arange(n_s, dtype=jnp.int32) % Ss,
                                 jnp.arange(N_META, dtype=jnp.int32)])
    cos_s, sin_s = _rope_tables(pos_small)
    q_rot_s, k_rot_s = _rope_small(proj_s, cos_s, sin_s, gq, gk, bd, bdt)
    k_meta = k_rot_s[n_s:]
    v_meta = proj_s[n_s:, COL_VB:COL_VB + B_KV]
    cos_p, sin_p = _rope_tables(N_META + jnp.arange(Sp, dtype=jnp.int32))
    yb_p, k_last, v_last = _swa_prompt(proj_p, cos_p, sin_p, k_meta, v_meta, gq, gk, bd, bdt, sinks[l],
                                       Bp, Sp // WINDOW)
    cache_shape = (N_META + WINDOW, B_KV_HEADS, B_HD)
    bcast_meta = lambda a: jnp.broadcast_to(a[None], (Bp, N_META, B_KV))
    swk_p = jnp.concatenate([bcast_meta(k_meta), k_last], 1).reshape((Bp,) + cache_shape)
    swv_p = jnp.concatenate([bcast_meta(v_meta), v_last], 1).reshape((Bp,) + cache_shape)
    n_cache = N_META + WINDOW
    key_pad = -(n_cache + Ss) % SUBLANES
    zpad = jnp.zeros((Bs, key_pad, B_KV), F32)
    kk = jnp.concatenate([cache_swa_k[l].reshape(Bs, n_cache, B_KV), k_rot_s[:n_s].reshape(Bs, Ss, B_KV), zpad], 1)
    vv = jnp.concatenate([cache_swa_v[l].reshape(Bs, n_cache, B_KV),
                          proj_s[:n_s, COL_VB:COL_VB + B_KV].reshape(Bs, Ss, B_KV), zpad], 1)
    q8 = jnp.pad(q_rot_s[:n_s].reshape(Bs, Ss, B_Q), ((0, 0), (0, SUBLANES - Ss), (0, 0)))
    yb_s = _swa_sample(q8, kk, vv, sinks[l], n_new=Ss)[:, :Ss].reshape(n_s, B_Q)
    new_cache = lambda t: jnp.concatenate(
        [t[:, :N_META], t[:, n_cache + Ss - WINDOW:n_cache + Ss]], 1).reshape((Bs,) + cache_shape)
    swk_s, swv_s = new_cache(kk), new_cache(vv)

    merged_p = _merge(ya_p, yb_p, proj_p, wa, wb, tm=512)
    merged_s = _merge(ya_s, yb_s, proj_s[:n_s], wa, wb, tm=256)
    cnt0 = jnp.zeros((1, ROUTER_PAD), F32)
    h2_p, xn2_p, gt_p, rt_p, cnt_p = _outproj(merged_p, xp, wo, g2, w_router, b_router, cnt0, tm=256)
    h2_s, xn2_s, gt_s, rt_s, cnt = _outproj(merged_s, xs, wo, g2, w_router, b_router, cnt_p, tm=256)

    cat = lambda a, b: jnp.concatenate([a, b], 0)
    n_p = Bp * Sp
    dest, pend, blk_e, n_used, nxt_e, wslot, n_slots = _moe_plan(cat(rt_p, rt_s), cnt)
    xs_slots = _dispatch(xn2_p, xn2_s, dest, pend, n_slots)
    ys = _moe_ffn(xs_slots, blk_e, n_used, nxt_e, wslot, w_gate[l], w_up[l], w_down[l])
    y_prompt = _combine(h2_p, gt_p, dest[:n_p], ys).reshape(Bp, Sp, D_MODEL)
    y_sample = _combine(h2_s, gt_s, dest[n_p:], ys).reshape(Bs, Ss, D_MODEL)
    return (y_prompt, y_sample, sdelta_p[None], conv_p[None], swk_p[None], swv_p[None],
            sdelta_s[None], conv_s[None], swk_s[None], swv_s[None])
```

```python
import functools

import jax
import jax.numpy as jnp
from jax import lax
from jax.experimental import pallas as pl
from jax.experimental.pallas import tpu as pltpu

F32 = jnp.float32
BF16 = jnp.bfloat16

D_MODEL = 2048
N_META = 16
A_HEADS = 8
A_DK = 128
A_DV = 128
A_CONV = 4
A_CHUNK = 64
A_QK = A_HEADS * A_DK
A_V = A_HEADS * A_DV
A_CONV_DIM = 2 * A_QK + A_V
B_HEADS = 16
B_KV_HEADS = 4
B_HD = 64
B_GROUP = B_HEADS // B_KV_HEADS
B_Q = B_HEADS * B_HD
B_KV = B_KV_HEADS * B_HD
WINDOW = 128
ROPE_THETA = 10000.0
PAST_LEN = 16384
N_GROUPS = 4
EXPERTS_PER_GROUP = 8
N_EXPERTS = N_GROUPS * EXPERTS_PER_GROUP
TOP_K = 2
D_EXPERT = 512
EPS = 1e-6

COL_GA = 0
COL_GB = COL_GA + D_MODEL
COL_QKV = COL_GB + D_MODEL
COL_Z = COL_QKV + A_CONV_DIM
COL_QB = COL_Z + A_V
COL_KB = COL_QB + B_Q
COL_VB = COL_KB + B_KV
PROJ_MAIN = COL_VB + B_KV
SMALL_PAD = 128
ROUTER_PAD = 128
LANE_BETA = 0
LANE_DECAY = A_HEADS

SUBLANES = 8
MOE_ROWS = 256
DELTA_CHUNKS_PER_STEP = 2
SAMPLE_SEQS_PER_STEP = 8
ROW_SLABS = 8
SLAB = D_MODEL // ROW_SLABS
VMEM_LIMIT = 56 * 1024 * 1024


def _row_chunk(tm):
    return 256 if tm % 256 == 0 else tm


def _bdot(a, b):
    return jnp.dot(a, b, preferred_element_type=F32)


def _store_slabs(ref3, val):
    for s in range(ROW_SLABS):
        ref3[:, s, :] = val[:, s * SLAB:(s + 1) * SLAB]


def _load_slabs(ref3):
    return jnp.concatenate([ref3[:, s, :] for s in range(ROW_SLABS)], 1)


def _inproj_kernel(x_ref, g_ref, w_ref, ws_ref, o_ref, os_ref, xn_ref, *, tm):
    rc = _row_chunk(tm)

    @pl.when(pl.program_id(1) == 0)
    def _():
        def norm_rows(sl):
            x = x_ref[sl, :]
            ms = jnp.mean(x * x, axis=-1, keepdims=True)
            xn_ref[sl, :] = ((x * lax.rsqrt(ms + EPS)) * g_ref[...]).astype(BF16)

        def body(r, c):
            norm_rows(pl.ds(pl.multiple_of(r * rc, rc), rc))
            return c

        if tm == rc:
            norm_rows(pl.ds(0, tm))
        else:
            lax.fori_loop(0, tm // rc, body, 0)
        os_ref[...] = _bdot(xn_ref[...], ws_ref[...])

    o_ref[...] = _bdot(xn_ref[...], w_ref[...])


def _inproj(x, gain, w_main, w_small, tm, tn=512):
    m = x.shape[0]
    assert m % tm == 0 and PROJ_MAIN % tn == 0
    return pl.pallas_call(
        functools.partial(_inproj_kernel, tm=tm),
        grid=(m // tm, PROJ_MAIN // tn),
        in_specs=[
            pl.BlockSpec((tm, D_MODEL), lambda i, j: (i, 0)),
            pl.BlockSpec((1, D_MODEL), lambda i, j: (0, 0)),
            pl.BlockSpec((D_MODEL, tn), lambda i, j: (0, j)),
            pl.BlockSpec((D_MODEL, SMALL_PAD), lambda i, j: (0, 0)),
        ],
        out_specs=[
            pl.BlockSpec((tm, tn), lambda i, j: (i, j)),
            pl.BlockSpec((tm, SMALL_PAD), lambda i, j: (i, 0)),
        ],
        out_shape=[
            jax.ShapeDtypeStruct((m, PROJ_MAIN), F32),
            jax.ShapeDtypeStruct((m, SMALL_PAD), F32),
        ],
        scratch_shapes=[pltpu.VMEM((tm, D_MODEL), BF16)],
        compiler_params=pltpu.CompilerParams(
            dimension_semantics=("arbitrary", "arbitrary"), vmem_limit_bytes=VMEM_LIMIT),
        name="inproj",
    )(x, gain, w_main, w_small)


def _dot_exact_lhs(a_bf16, b):
    b1 = b.astype(BF16)
    r1 = b - b1.astype(F32)
    b2 = r1.astype(BF16)
    b3 = (r1 - b2.astype(F32)).astype(BF16)
    return _bdot(a_bf16, b1) + (_bdot(a_bf16, b2) + _bdot(a_bf16, b3))


def _unit_lower_inverse_offsets(ms, ri, ci, c):
    same = lambda n: (ri >> n) == (ci >> n)
    d = lambda a, b: _bdot(a.astype(BF16), b.astype(BF16))
    blk8 = same(3)
    n1 = [jnp.where(blk8, m, 0.0) for m in ms]
    n2 = [d(a, a) for a in n1]
    n3 = [d(a, b) for a, b in zip(n1, n2)]
    n4 = [d(b, b) for b in n2]
    qs = [(b - a) - t for a, b, t in zip(n1, n2, n3)]
    qs = [(q + f) + d(q, f) for q, f in zip(qs, n4)]
    lg = 3
    while (1 << lg) < c:
        sel = jnp.logical_and(same(lg + 1), jnp.logical_not(same(lg)))
        offs = [jnp.where(sel, m, 0.0) for m in ms]
        ts = [o + d(q, o) for q, o in zip(qs, offs)]
        qs = [(q - t) - d(t, q) for q, t in zip(qs, ts)]
        lg += 1
    return qs


def _delta_kernel(qc, kc, vc, zc, qp, kp, vp, qf, kf, vf, sm_ref, cw_ref, alog_ref, dtb_ref, anw_ref, s0_ref,
                  ya_ref, s_ref, *, C, n_valid, nsub, chained):
    R = nsub * C
    first = pl.program_id(1) == 0

    @pl.when(first)
    def _():
        s_ref[...] = s0_ref[...]

    def keep(a):
        if n_valid == C:
            return a
        rows = lax.broadcasted_iota(jnp.int32, (a.shape[0], 1), 0)
        return jnp.where((rows & (C - 1)) < n_valid, a, 0.0)

    ri = lax.broadcasted_iota(jnp.int32, (C, C), 0)
    ci = lax.broadcasted_iota(jnp.int32, (C, C), 1)
    lower = ri >= ci
    strict = ri > ci

    def conv_rows(prev, cur, col0):
        n = cur.shape[0]
        x = jnp.concatenate([prev, cur], 0)
        w = cw_ref[:, col0:col0 + A_QK]
        acc = x[SUBLANES:SUBLANES + n] * w[A_CONV - 1:A_CONV]
        for s in range(1, A_CONV):
            acc = acc + x[SUBLANES - s:SUBLANES - s + n] * w[A_CONV - 1 - s:A_CONV - s]
        return acc * jax.nn.sigmoid(acc)

    def conv(cur_ref, prev_ref, first_ref, col0):
        if chained:
            return conv_rows(jnp.where(first, first_ref[0], prev_ref[...]), cur_ref[...], col0)
        return jnp.concatenate(
            [conv_rows(first_ref[s], cur_ref[s * C:(s + 1) * C, :], col0) for s in range(nsub)], 0)

    qx = conv(qc, qp, qf, 0)
    kx = conv(kc, kp, kf, A_QK)
    vx = conv(vc, vp, vf, 2 * A_QK)

    sm = sm_ref[...]
    beta_all = keep(jax.nn.sigmoid(sm))
    g_all = keep(-jnp.exp(alog_ref[...]) * jax.nn.softplus(sm + dtb_ref[...]))
    rr = lax.broadcasted_iota(jnp.int32, (R, R), 0)
    rc = lax.broadcasted_iota(jnp.int32, (R, R), 1)
    sh = C.bit_length() - 1
    same_chunk_lower = jnp.logical_and(rr >= rc, (rr >> sh) == (rc >> sh))
    g_cum = _dot_exact_lhs(same_chunk_lower.astype(BF16), g_all)
    g_cum_t = g_cum.T
    e_g = jnp.exp(g_cum)

    heads = range(A_HEADS)
    subs = range(nsub)
    items = [(s, h) for s in subs for h in heads]
    hs = [slice(h * A_DK, (h + 1) * A_DK) for h in heads]
    rs = [slice(s * C, (s + 1) * C) for s in subs]
    ld = lambda h: slice(LANE_DECAY + h, LANE_DECAY + h + 1)
    qs = [qx[rs[s], hs[h]] for s, h in items]
    qs = [q * lax.rsqrt(jnp.sum(q * q, -1, keepdims=True) + EPS) * (A_DK ** -0.5) for q in qs]
    ks = [kx[rs[s], hs[h]] for s, h in items]
    ks = [keep(k * lax.rsqrt(jnp.sum(k * k, -1, keepdims=True) + EPS)) for k in ks]
    vs = [keep(vx[rs[s], hs[h]]) for s, h in items]
    betas = [beta_all[rs[s], LANE_BETA + h:LANE_BETA + h + 1] for s, h in items]
    egs = [e_g[rs[s], ld(h)] for s, h in items]
    gammas = [jnp.exp(jnp.where(lower, g_cum[rs[s], ld(h)] - g_cum_t[ld(h), rs[s]], -jnp.inf))
              for s, h in items]
    kbs = [k * b for k, b in zip(ks, betas)]
    a1s = [lax.dot_general(jnp.concatenate([kb, q], 0).astype(BF16), k.astype(BF16),
                           (((1,), (1,)), ((), ())), preferred_element_type=F32)
           for kb, q, k in zip(kbs, qs, ks)]
    ms = [jnp.where(strict, a1[:C] * gm, 0.0) for a1, gm in zip(a1s, gammas)]
    attns = [a1[C:] * gm for a1, gm in zip(a1s, gammas)]
    rhss = [jnp.concatenate([v * b, kb * eg], 1) for v, b, kb, eg in zip(vs, betas, kbs, egs)]
    sols = [r + _bdot(q.astype(BF16), r.astype(BF16))
            for q, r in zip(_unit_lower_inverse_offsets(ms, ri, ci, C), rhss)]
    wq_lhs = [jnp.concatenate([sol[:, A_DV:], q * eg], 0).astype(BF16) for sol, q, eg in zip(sols, qs, egs)]
    g_lasts = [g_cum[(s + 1) * C - 1:(s + 1) * C, :] for s in subs]
    kg_ts = [(k * jnp.exp(g_lasts[s][:, ld(h)] - g_cum[rs[s], ld(h)])).T
             for (s, h), k in zip(items, ks)]
    states = [s_ref[0, h] for h in heads]
    for s in subs:
        if not chained:
            states = [s_ref[s, h] for h in heads]
        at = lambda xs: xs[s * A_HEADS:(s + 1) * A_HEADS]
        wqs = [_bdot(l, st.astype(BF16)) for l, st in zip(at(wq_lhs), states)]
        v_news = [sol[:, :A_DV] - wq[:C] for sol, wq in zip(at(sols), wqs)]
        r2s = [_bdot(jnp.concatenate([a, kg], 0).astype(BF16), vn.astype(BF16))
               for a, kg, vn in zip(at(attns), at(kg_ts), v_news)]
        e_last = jnp.exp(g_lasts[s])
        states = [st * e_last[:, ld(h)] + r2[C:] for h, st, r2 in zip(heads, states, r2s)]
        for h in heads:
            o = wqs[h][C:] + r2s[h][:C]
            on = (o * lax.rsqrt(jnp.mean(o * o, -1, keepdims=True) + EPS)) * anw_ref[...]
            zh = zc[rs[s], hs[h]]
            ya_ref[rs[s], hs[h]] = (on * (zh * jax.nn.sigmoid(zh))).astype(BF16)
        if not chained:
            for h in heads:
                s_ref[s, h] = states[h]
    if chained:
        for h in heads:
            s_ref[0, h] = states[h]


def _delta_mixer(proj, small, first_prev, s0, conv_w, alog_row, dtb_row, anw_row, *, nb, nc, C, n_valid,
                 shared_first, nsub=1, chained=True):
    rows = nb * nc * C
    assert proj.shape[0] == rows and C % SUBLANES == 0 and C & (C - 1) == 0
    R = nsub * C
    if chained:
        assert nc % nsub == 0
        nseq, steps = 1, nc // nsub
    else:
        assert nc == 1 and nb % nsub == 0 and not shared_first
        nseq, steps, nb = nsub, 1, nb // nsub
    cpb = R // SUBLANES
    cq, ck, cv, cz = (COL_QKV // A_QK, COL_QKV // A_QK + 1, COL_QKV // A_QK + 2, COL_Z // A_V)
    fb = (lambda b: 0) if shared_first else (lambda b: b)
    cur = lambda col: pl.BlockSpec((R, A_QK), lambda b, c: (b * steps + c, col))
    prev = lambda col: pl.BlockSpec(
        (SUBLANES, A_QK), lambda b, c: (jnp.maximum((b * steps + c) * cpb - 1, 0), col))
    frst = lambda j: pl.BlockSpec((nseq, SUBLANES, A_QK), lambda b, c: (fb(b), 0, j))
    row1 = lambda n: pl.BlockSpec((1, n), lambda b, c: (0, 0))
    return pl.pallas_call(
        functools.partial(_delta_kernel, C=C, n_valid=n_valid, nsub=nsub, chained=chained),
        grid=(nb, steps),
        in_specs=[
            cur(cq), cur(ck), cur(cv), cur(cz),
            prev(cq), prev(ck), prev(cv),
            frst(0), frst(1), frst(2),
            pl.BlockSpec((R, SMALL_PAD), lambda b, c: (b * steps + c, 0)),
            pl.BlockSpec((A_CONV, A_CONV_DIM), lambda b, c: (0, 0)),
            row1(SMALL_PAD), row1(SMALL_PAD), row1(A_DV),
            pl.BlockSpec((nseq, A_HEADS, A_DK, A_DV), lambda b, c: (fb(b), 0, 0, 0)),
        ],
        out_specs=[
            pl.BlockSpec((R, A_V), lambda b, c: (b * steps + c, 0)),
            pl.BlockSpec((nseq, A_HEADS, A_DK, A_DV), lambda b, c: (b, 0, 0, 0)),
        ],
        out_shape=[
            jax.ShapeDtypeStruct((rows, A_V), BF16),
            jax.ShapeDtypeStruct((nb * nseq, A_HEADS, A_DK, A_DV), F32),
        ],
        compiler_params=pltpu.CompilerParams(
            dimension_semantics=("arbitrary", "arbitrary"), vmem_limit_bytes=VMEM_LIMIT),
        name="delta_mixer",
    )(proj, proj, proj, proj, proj, proj, proj, first_prev, first_prev, first_prev, small,
      conv_w, alog_row, dtb_row, anw_row, s0)


def _merge_kernel(ya_ref, yb_ref, ga_ref, gb_ref, wa_ref, wb_ref, o_ref):
    ua = _bdot(ya_ref[...].astype(BF16), wa_ref[...])
    ub = _bdot(yb_ref[...].astype(BF16), wb_ref[...])
    merged = jax.nn.sigmoid(ga_ref[...]) * ua + jax.nn.sigmoid(gb_ref[...]) * ub
    o_ref[...] = merged.astype(BF16)


def _merge(ya, yb, proj, wa, wb, tm):
    m = ya.shape[0]
    assert m % tm == 0
    return pl.pallas_call(
        _merge_kernel,
        grid=(m // tm,),
        in_specs=[
            pl.BlockSpec((tm, A_V), lambda i: (i, 0)),
            pl.BlockSpec((tm, B_Q), lambda i: (i, 0)),
            pl.BlockSpec((tm, D_MODEL), lambda i: (i, COL_GA // D_MODEL)),
            pl.BlockSpec((tm, D_MODEL), lambda i: (i, COL_GB // D_MODEL)),
            pl.BlockSpec((A_V, D_MODEL), lambda i: (0, 0)),
            pl.BlockSpec((B_Q, D_MODEL), lambda i: (0, 0)),
        ],
        out_specs=pl.BlockSpec((tm, D_MODEL), lambda i: (i, 0)),
        out_shape=jax.ShapeDtypeStruct((m, D_MODEL), BF16),
        compiler_params=pltpu.CompilerParams(
            dimension_semantics=("arbitrary",), vmem_limit_bytes=VMEM_LIMIT),
        name="merge",
    )(ya, yb, proj, proj, wa, wb)


LANE_SENTINEL = ROUTER_PAD - 1


def _lane_argmax(vals, eligible, lane):
    top = jnp.max(jnp.where(eligible, vals, -jnp.inf), -1, keepdims=True)
    idx = jnp.min(jnp.where(jnp.logical_and(eligible, vals == top), lane, LANE_SENTINEL), -1, keepdims=True)
    return top, idx


def _masked_softmax(logits, eligible):
    z = jnp.where(eligible, logits, -jnp.inf)
    e = jnp.exp(z - jnp.max(z, -1, keepdims=True))
    return e / jnp.sum(e, -1, keepdims=True)


def _outproj_kernel(m_ref, h_ref, wo_ref, g_ref, wr_ref, br_ref, cnt_in_ref,
                    h2_ref, xn_ref, gate_ref, route_ref, cnt_ref):
    tm = h_ref.shape[0]

    @pl.when(pl.program_id(0) == 0)
    def _():
        cnt_ref[...] = cnt_in_ref[...]

    h2 = h_ref[...] + _bdot(m_ref[...], wo_ref[...])
    h2_ref[...] = h2
    ms = jnp.mean(h2 * h2, axis=-1, keepdims=True)
    xn = (h2 * lax.rsqrt(ms + EPS)) * g_ref[...]
    _store_slabs(xn_ref, xn)
    lg = _bdot(xn.astype(BF16), wr_ref[...]) + br_ref[...]

    lane = lax.broadcasted_iota(jnp.int32, (tm, ROUTER_PAD), 1)
    is_g = lane < N_GROUPS
    p_grp, grp = _lane_argmax(_masked_softmax(lg, is_g), is_g, lane)
    ex = lane - N_GROUPS
    is_e = jnp.logical_and(jnp.logical_and(ex >= 0, ex < N_EXPERTS), (ex >> 3) == grp)
    pe = _masked_softmax(lg, is_e)
    p1, i1 = _lane_argmax(pe, is_e, lane)
    rest = jnp.logical_and(is_e, lane != i1)
    p2, i2 = _lane_argmax(pe, rest, lane)
    den = p1 + p2
    gate_ref[...] = jnp.where(lane == 0, (p_grp * p1) / den, jnp.where(lane == 1, (p_grp * p2) / den, 0.0))

    e1 = i1 - N_GROUPS
    e2 = i2 - N_GROUPS
    oh1 = lane == e1
    oh2 = lane == e2
    ri = lax.broadcasted_iota(jnp.int32, (tm, tm), 0)
    ci = lax.broadcasted_iota(jnp.int32, (tm, tm), 1)
    below = (ri > ci).astype(BF16)
    f1 = oh1.astype(F32)
    f2 = oh2.astype(F32)
    tot1 = jnp.sum(f1, 0, keepdims=True)
    run = cnt_ref[...]
    before1 = run + _bdot(below, oh1.astype(BF16))
    before2 = run + tot1 + _bdot(below, oh2.astype(BF16))
    rank1 = jnp.sum(f1 * before1, -1, keepdims=True).astype(jnp.int32)
    rank2 = jnp.sum(f2 * before2, -1, keepdims=True).astype(jnp.int32)
    cnt_ref[...] = run + tot1 + jnp.sum(f2, 0, keepdims=True)
    route_ref[...] = jnp.where(lane == 0, e1, jnp.where(lane == 1, e2, jnp.where(
        lane == 2, rank1, jnp.where(lane == 3, rank2, 0))))


def _outproj(merged, h, wo, gain2, w_router, b_router, counts_in, tm):
    m = h.shape[0]
    assert m % tm == 0
    row = lambda n: pl.BlockSpec((1, n), lambda i: (0, 0))
    tile = lambda n: pl.BlockSpec((tm, n), lambda i: (i, 0))
    return pl.pallas_call(
        _outproj_kernel,
        grid=(m // tm,),
        in_specs=[
            tile(D_MODEL), tile(D_MODEL),
            pl.BlockSpec((D_MODEL, D_MODEL), lambda i: (0, 0)),
            row(D_MODEL),
            pl.BlockSpec((D_MODEL, ROUTER_PAD), lambda i: (0, 0)),
            row(ROUTER_PAD), row(ROUTER_PAD),
        ],
        out_specs=[tile(D_MODEL), pl.BlockSpec((tm, ROW_SLABS, SLAB), lambda i: (i, 0, 0)),
                   tile(ROUTER_PAD), tile(ROUTER_PAD), row(ROUTER_PAD)],
        out_shape=[
            jax.ShapeDtypeStruct((m, D_MODEL), F32),
            jax.ShapeDtypeStruct((m, ROW_SLABS, SLAB), F32),
            jax.ShapeDtypeStruct((m, ROUTER_PAD), F32),
            jax.ShapeDtypeStruct((m, ROUTER_PAD), jnp.int32),
            jax.ShapeDtypeStruct((1, ROUTER_PAD), F32),
        ],
        compiler_params=pltpu.CompilerParams(
            dimension_semantics=("arbitrary",), vmem_limit_bytes=VMEM_LIMIT),
        name="outproj",
    )(merged, h, wo, gain2, w_router, b_router, counts_in)


def _moe_kernel(blk_e_ref, nused_ref, nxt_e_ref, slot_ref, x_ref, wg_hbm, wu_hbm, wd_hbm, o_ref,
                wg32, wu32, wd32, wgb, wub, wdb, sem):
    i = pl.program_id(0)
    e = blk_e_ref[i]
    e_prev = blk_e_ref[jnp.maximum(i - 1, 0)]
    used = i < nused_ref[0]

    def copies(expert, slot):
        return (pltpu.make_async_copy(wg_hbm.at[expert], wg32.at[slot], sem.at[slot]),
                pltpu.make_async_copy(wu_hbm.at[expert], wu32.at[slot], sem.at[slot]),
                pltpu.make_async_copy(wd_hbm.at[expert], wd32.at[slot], sem.at[slot]))

    @pl.when(i == 0)
    def _():
        for c in copies(e, 0):
            c.start()

    @pl.when(jnp.logical_and(used, jnp.logical_or(i == 0, e != e_prev)))
    def _():
        slot = slot_ref[i]
        for c in copies(e, slot):
            c.wait()
        nxt = nxt_e_ref[i]

        @pl.when(nxt != e)
        def _():
            for c in copies(nxt, 1 - slot):
                c.start()

        def cast_in(r, c):
            sl = pl.ds(pl.multiple_of(r * 256, 256), 256)
            wgb[sl, :] = wg32[slot, sl, :].astype(BF16)
            wub[sl, :] = wu32[slot, sl, :].astype(BF16)
            return c

        lax.fori_loop(0, D_MODEL // 256, cast_in, 0)

        def cast_out(r, c):
            sl = pl.ds(pl.multiple_of(r * 128, 128), 128)
            wdb[sl, :] = wd32[slot, sl, :].astype(BF16)
            return c

        lax.fori_loop(0, D_EXPERT // 128, cast_out, 0)

    @pl.when(used)
    def _():
        x = _load_slabs(x_ref).astype(BF16)
        g = _bdot(x, wgb[...])
        u = _bdot(x, wub[...])
        hb = (g * jax.nn.sigmoid(g)) * u
        _store_slabs(o_ref, _bdot(hb.astype(BF16), wdb[...]))

    @pl.when(jnp.logical_not(used))
    def _():
        o_ref[...] = jnp.zeros_like(o_ref)


def _moe_ffn(xs, blk_e, n_used, nxt_e, slot, w_gate, w_up, w_down):
    p = xs.shape[0]
    nb = p // MOE_ROWS
    grid_spec = pltpu.PrefetchScalarGridSpec(
        num_scalar_prefetch=4,
        grid=(nb,),
        in_specs=[
            pl.BlockSpec((MOE_ROWS, ROW_SLABS, SLAB), lambda i, be, nu, nx, sl: (jnp.minimum(i, nu[0] - 1), 0, 0)),
            pl.BlockSpec(memory_space=pl.ANY),
            pl.BlockSpec(memory_space=pl.ANY),
            pl.BlockSpec(memory_space=pl.ANY),
        ],
        out_specs=pl.BlockSpec((MOE_ROWS, ROW_SLABS, SLAB), lambda i, be, nu, nx, sl: (i, 0, 0)),
        scratch_shapes=[
            pltpu.VMEM((2, D_MODEL, D_EXPERT), F32),
            pltpu.VMEM((2, D_MODEL, D_EXPERT), F32),
            pltpu.VMEM((2, D_EXPERT, D_MODEL), F32),
            pltpu.VMEM((D_MODEL, D_EXPERT), BF16),
            pltpu.VMEM((D_MODEL, D_EXPERT), BF16),
            pltpu.VMEM((D_EXPERT, D_MODEL), BF16),
            pltpu.SemaphoreType.DMA((2,)),
        ],
    )
    return pl.pallas_call(
        _moe_kernel,
        grid_spec=grid_spec,
        out_shape=jax.ShapeDtypeStruct((p, ROW_SLABS, SLAB), F32),
        compiler_params=pltpu.CompilerParams(
            dimension_semantics=("arbitrary",), vmem_limit_bytes=VMEM_LIMIT),
        name="moe_ffn",
    )(blk_e, n_used, nxt_e, slot, xs, w_gate, w_up, w_down)


HALF_TILE = 64
SWA_HEADS_PER_STAGE = 16
LANES = 128


def _dot_exact_rhs(a, b01):
    a1 = a.astype(BF16)
    r1 = a - a1.astype(F32)
    a2 = r1.astype(BF16)
    a3 = (r1 - a2.astype(F32)).astype(BF16)
    return _bdot(a1, b01) + (_bdot(a2, b01) + _bdot(a3, b01))


def _group_rms_rope(x, gain, cos128, sin128, bd, bdt):
    r, w = x.shape
    ssq = _dot_exact_rhs(x * x, bd)
    rs = lax.rsqrt(ssq * (1.0 / B_HD) + EPS)
    y = (x * _dot_exact_rhs(rs, bdt)) * gain
    reps = w // LANES
    cosw = jnp.concatenate([cos128] * reps, 1)
    sinw = jnp.concatenate([sin128] * reps, 1)
    lane = lax.broadcasted_iota(jnp.int32, (r, w), 1)
    swapped = jnp.where((lane & (B_HD // 2)) == 0,
                        pltpu.roll(y, w - B_HD // 2, 1), pltpu.roll(y, B_HD // 2, 1))
    return y * cosw + swapped * sinw


def _kv_tiles(tile, odd, lo):
    rolled = pltpu.roll(tile, HALF_TILE, 1)
    dup = jnp.where(lo, rolled, tile) if odd else jnp.where(lo, tile, rolled)
    return dup.astype(BF16), jnp.where(lo, dup, 0.0).astype(BF16), jnp.where(lo, 0.0, dup).astype(BF16)


def _nt(a, b):
    return lax.dot_general(a, b, (((1,), (1,)), ((), ())), preferred_element_type=F32)


def _sink_softmax_pv(score_lists, value_lists, sinks):
    ms = []
    for scores, sink in zip(score_lists, sinks):
        m = jnp.max(scores[0], -1, keepdims=True)
        for s in scores[1:]:
            m = jnp.maximum(m, jnp.max(s, -1, keepdims=True))
        ms.append(jnp.maximum(m, sink))
    es = [[jnp.exp(s - m) for s in scores] for scores, m in zip(score_lists, ms)]
    dens = []
    for e_blocks, m, sink in zip(es, ms, sinks):
        den = jnp.exp(sink - m)
        for e in e_blocks:
            den = den + jnp.sum(e, -1, keepdims=True)
        dens.append(den)
    pvs = [[_bdot(e.astype(BF16), v) for e, v in zip(e_blocks, values)]
           for e_blocks, values in zip(es, value_lists)]
    outs = []
    for pv, den in zip(pvs, dens):
        acc = pv[0]
        for x in pv[1:]:
            acc = acc + x
        outs.append(acc / den)
    return outs


def _rope_small_kernel(q_ref, k_ref, cos_ref, sin_ref, gq_ref, gk_ref, bd_ref, bdt_ref, qo_ref, ko_ref):
    cos = cos_ref[...]
    sin = sin_ref[...]
    qo_ref[...] = _group_rms_rope(q_ref[...], gq_ref[...], cos, sin, bd_ref[...], bdt_ref[...])
    ko_ref[...] = _group_rms_rope(k_ref[...], gk_ref[...], cos, sin, bd_ref[:B_KV, :], bdt_ref[:, :B_KV])


def _rope_small(proj, cos, sin, gq, gk, bd, bdt):
    m = proj.shape[0]
    full = lambda shape: pl.BlockSpec(shape, lambda i: (0, 0))
    return pl.pallas_call(
        _rope_small_kernel,
        grid=(1,),
        in_specs=[
            pl.BlockSpec((m, B_Q), lambda i: (0, COL_QB // B_Q)),
            pl.BlockSpec((m, B_KV), lambda i: (0, COL_KB // B_KV)),
            full((m, LANES)), full((m, LANES)), full((1, B_Q)), full((1, B_KV)),
            full((B_Q, LANES)), full((LANES, B_Q)),
        ],
        out_specs=[full((m, B_Q)), full((m, B_KV))],
        out_shape=[jax.ShapeDtypeStruct((m, B_Q), F32), jax.ShapeDtypeStruct((m, B_KV), F32)],
        compiler_params=pltpu.CompilerParams(
            dimension_semantics=("arbitrary",), vmem_limit_bytes=VMEM_LIMIT),
        name="rope_small",
    )(proj, proj, cos, sin, gq, gk, bd, bdt)


def _swa_prompt_kernel(sinks_ref, q_ref, k_ref, v_ref, cos_ref, sin_ref, km_ref, vm_ref, gq_ref, gk_ref,
                       bd_ref, bdt_ref, y_ref, kc_ref, vc_ref, kprev, vloprev, vhiprev):
    n = pl.program_id(1)

    @pl.when(n == 0)
    def _():
        kprev[...] = jnp.zeros_like(kprev)
        vloprev[...] = jnp.zeros_like(vloprev)
        vhiprev[...] = jnp.zeros_like(vhiprev)

    cos = cos_ref[...]
    sin = sin_ref[...]
    q = _group_rms_rope(q_ref[...], gq_ref[...], cos, sin, bd_ref[...], bdt_ref[...])
    k = _group_rms_rope(k_ref[...], gk_ref[...], cos, sin, bd_ref[:B_KV, :], bdt_ref[:, :B_KV])
    v = v_ref[...]
    kc_ref[...] = k
    vc_ref[...] = v
    km = km_ref[...]
    vm = vm_ref[...]

    qi = lax.broadcasted_iota(jnp.int32, (WINDOW, WINDOW), 0)
    kj = lax.broadcasted_iota(jnp.int32, (WINDOW, WINDOW), 1)
    cur_ok = kj <= qi
    prev_ok = jnp.logical_and(kj > qi, n > 0)
    lo = lax.broadcasted_iota(jnp.int32, (WINDOW, LANES), 1) < HALF_TILE
    lo_m = lax.broadcasted_iota(jnp.int32, (N_META, LANES), 1) < HALF_TILE
    scale = B_HD ** -0.5

    kv = []
    for g in range(B_KV_HEADS):
        tl = slice((g // 2) * LANES, (g // 2 + 1) * LANES)
        k2c, _, _ = _kv_tiles(k[:, tl], g % 2, lo)
        _, vlo_c, vhi_c = _kv_tiles(v[:, tl], g % 2, lo)
        k2m, _, _ = _kv_tiles(km[:, tl], g % 2, lo_m)
        _, vlo_m, vhi_m = _kv_tiles(vm[:, tl], g % 2, lo_m)
        kv.append(((k2m, kprev[g], k2c), (vlo_m, vloprev[g], vlo_c), (vhi_m, vhiprev[g], vhi_c)))
    heads = range(B_HEADS)
    qms = []
    for h in heads:
        qt = q[:, (h // 2) * LANES:(h // 2 + 1) * LANES]
        qms.append((jnp.where(lo, qt, 0.0) if h % 2 == 0 else jnp.where(lo, 0.0, qt)).astype(BF16))
    for h0 in range(0, B_HEADS, SWA_HEADS_PER_STAGE):
        hg = range(h0, h0 + SWA_HEADS_PER_STAGE)
        raw = [[_nt(qms[h], kk) * scale for kk in kv[h // B_GROUP][0]] for h in hg]
        scores = [[sm, jnp.where(prev_ok, sp, -jnp.inf), jnp.where(cur_ok, sc, -jnp.inf)] for sm, sp, sc in raw]
        outs = _sink_softmax_pv(scores, [kv[h // B_GROUP][1 + h % 2] for h in hg], [sinks_ref[h] for h in hg])
        for t in range(SWA_HEADS_PER_STAGE // 2):
            y_ref[:, (h0 // 2 + t) * LANES:(h0 // 2 + t + 1) * LANES] = (outs[2 * t] + outs[2 * t + 1]).astype(BF16)
    for g in range(B_KV_HEADS):
        kprev[g] = kv[g][0][2]
        vloprev[g] = kv[g][1][2]
        vhiprev[g] = kv[g][2][2]


def _swa_prompt(proj, cos, sin, k_meta, v_meta, gq, gk, bd, bdt, sinks, nb, nblk):
    rows = nb * nblk * WINDOW
    assert proj.shape[0] == rows
    return pl.pallas_call(
        _swa_prompt_kernel,
        grid=(nb, nblk),
        in_specs=[
            pl.BlockSpec(memory_space=pltpu.SMEM),
            pl.BlockSpec((WINDOW, B_Q), lambda b, n: (b * nblk + n, COL_QB // B_Q)),
            pl.BlockSpec((WINDOW, B_KV), lambda b, n: (b * nblk + n, COL_KB // B_KV)),
            pl.BlockSpec((WINDOW, B_KV), lambda b, n: (b * nblk + n, COL_VB // B_KV)),
            pl.BlockSpec((WINDOW, LANES), lambda b, n: (n, 0)),
            pl.BlockSpec((WINDOW, LANES), lambda b, n: (n, 0)),
            pl.BlockSpec((N_META, B_KV), lambda b, n: (0, 0)),
            pl.BlockSpec((N_META, B_KV), lambda b, n: (0, 0)),
            pl.BlockSpec((1, B_Q), lambda b, n: (0, 0)),
            pl.BlockSpec((1, B_KV), lambda b, n: (0, 0)),
            pl.BlockSpec((B_Q, LANES), lambda b, n: (0, 0)),
            pl.BlockSpec((LANES, B_Q), lambda b, n: (0, 0)),
        ],
        out_specs=[
            pl.BlockSpec((WINDOW, B_Q), lambda b, n: (b * nblk + n, 0)),
            pl.BlockSpec((None, WINDOW, B_KV), lambda b, n: (b, 0, 0)),
            pl.BlockSpec((None, WINDOW, B_KV), lambda b, n: (b, 0, 0)),
        ],
        scratch_shapes=[pltpu.VMEM((B_KV_HEADS, WINDOW, LANES), BF16)] * 3,
        out_shape=[
            jax.ShapeDtypeStruct((rows, B_Q), BF16),
            jax.ShapeDtypeStruct((nb, WINDOW, B_KV), F32),
            jax.ShapeDtypeStruct((nb, WINDOW, B_KV), F32),
        ],
        compiler_params=pltpu.CompilerParams(
            dimension_semantics=("arbitrary", "arbitrary"), vmem_limit_bytes=VMEM_LIMIT),
        name="swa_prompt",
    )(sinks, proj, proj, proj, cos, sin, k_meta, v_meta, gq, gk, bd, bdt)


def _swa_sample_kernel(sinks_ref, q_ref, k_ref, v_ref, y_ref, *, n_keys, n_new):
    nseq, rq, _ = q_ref.shape
    t = lax.broadcasted_iota(jnp.int32, (rq, n_keys), 0)
    r = lax.broadcasted_iota(jnp.int32, (rq, n_keys), 1)
    wj = r - N_META
    win_pos = PAST_LEN - WINDOW + wj
    nm = r - N_META - WINDOW
    ok = ((r < N_META)
          | ((wj >= 0) & (wj < WINDOW) & (win_pos >= N_META) & (wj >= t + 1))
          | ((nm >= 0) & (nm <= t) & (nm > t - WINDOW) & (nm < n_new)))
    lo_q = lax.broadcasted_iota(jnp.int32, (rq, LANES), 1) < HALF_TILE
    lo_k = lax.broadcasted_iota(jnp.int32, (n_keys, LANES), 1) < HALF_TILE
    scale = B_HD ** -0.5
    heads = range(B_HEADS)
    scores, values = [], []
    for s in range(nseq):
        q = q_ref[s]
        k = k_ref[s]
        v = v_ref[s]
        kv = []
        for g in range(B_KV_HEADS):
            tl = slice((g // 2) * LANES, (g // 2 + 1) * LANES)
            k2, _, _ = _kv_tiles(k[:, tl], g % 2, lo_k)
            _, vlo, vhi = _kv_tiles(v[:, tl], g % 2, lo_k)
            kv.append((k2, vlo, vhi))
        for h in heads:
            qt = q[:, (h // 2) * LANES:(h // 2 + 1) * LANES]
            qm = (jnp.where(lo_q, qt, 0.0) if h % 2 == 0 else jnp.where(lo_q, 0.0, qt)).astype(BF16)
            scores.append([jnp.where(ok, _nt(qm, kv[h // B_GROUP][0]) * scale, -jnp.inf)])
            values.append([kv[h // B_GROUP][1 + h % 2]])
    outs = _sink_softmax_pv(scores, values, [sinks_ref[h] for _s in range(nseq) for h in heads])
    for s in range(nseq):
        for t2 in range(B_HEADS // 2):
            pair = outs[s * B_HEADS + 2 * t2] + outs[s * B_HEADS + 2 * t2 + 1]
            y_ref[s, :, t2 * LANES:(t2 + 1) * LANES] = pair.astype(BF16)


def _swa_sample(q8, kk, vv, sinks, n_new):
    nb, rq, _ = q8.shape
    n_keys = kk.shape[1]
    nseq = SAMPLE_SEQS_PER_STEP
    assert nb % nseq == 0
    return pl.pallas_call(
        functools.partial(_swa_sample_kernel, n_keys=n_keys, n_new=n_new),
        grid=(nb // nseq,),
        in_specs=[
            pl.BlockSpec(memory_space=pltpu.SMEM),
            pl.BlockSpec((nseq, rq, B_Q), lambda b: (b, 0, 0)),
            pl.BlockSpec((nseq, n_keys, B_KV), lambda b: (b, 0, 0)),
            pl.BlockSpec((nseq, n_keys, B_KV), lambda b: (b, 0, 0)),
        ],
        out_specs=pl.BlockSpec((nseq, rq, B_Q), lambda b: (b, 0, 0)),
        out_shape=jax.ShapeDtypeStruct((nb, rq, B_Q), BF16),
        compiler_params=pltpu.CompilerParams(
            dimension_semantics=("arbitrary",), vmem_limit_bytes=VMEM_LIMIT),
        name="swa_sample",
    )(sinks, q8, kk, vv)


def _rope_tables(pos):
    half = B_HD // 2
    inv_freq = ROPE_THETA ** (-jnp.arange(half, dtype=F32) / half)
    ang = pos.astype(F32)[:, None] * inv_freq[None, :]
    cos, sin = jnp.cos(ang), jnp.sin(ang)
    return jnp.concatenate([cos, cos, cos, cos], 1), jnp.concatenate([-sin, sin, -sin, sin], 1)


ROUTE_ROWS = 256
DMA_UNROLL = 8


def _row(ref, r):
    return ref.at[pl.ds(r, 1)]


def _dispatch_kernel(pend_ref, dest_ref, xa_ref, xb_ref, xs_hbm, zeros_vmem, ring, sem_zero, sem, *, nta, nt, nb):
    i = pl.program_id(0)
    tm = ROUTE_ROWS
    slot = i % 2

    def drain(s):
        for _k in range(TOP_K):
            pltpu.make_async_copy(ring.at[s], xs_hbm.at[pl.ds(0, tm)], sem.at[s]).wait()

    @pl.when(i == 0)
    def _():
        zeros_vmem[...] = jnp.zeros_like(zeros_vmem)
        fill = lambda row0: pltpu.make_async_copy(zeros_vmem, xs_hbm.at[pl.ds(row0, MOE_ROWS)], sem_zero)
        for e in range(N_EXPERTS):
            fill(jnp.maximum(pend_ref[e] - MOE_ROWS, 0)).start()
        for e in range(N_EXPERTS):
            fill(0).wait()
        n_used = pend_ref[N_EXPERTS - 1] // MOE_ROWS

        def fill_tail(b, c):
            fill(b * MOE_ROWS).start()
            fill(0).wait()
            return c

        lax.fori_loop(n_used, nb, fill_tail, 0)

    @pl.when(i >= 2)
    def _():
        drain(slot)

    @pl.when(i < nta)
    def _():
        ring[slot] = xa_ref[...]

    @pl.when(i >= nta)
    def _():
        ring[slot] = xb_ref[...]

    def body(r, c):
        src = ring.at[slot, pl.ds(r, 1)]
        for k in range(TOP_K):
            pltpu.make_async_copy(src, _row(xs_hbm, dest_ref[0, TOP_K * r + k]), sem.at[slot]).start(priority=k)
        return c

    lax.fori_loop(0, tm, body, 0, unroll=DMA_UNROLL)

    @pl.when(i == nt - 1)
    def _():
        if nt >= 2:
            drain(1 - slot)
        drain(slot)


def _dispatch(xa, xb, dest, pend, n_slots):
    assert xa.shape[0] % ROUTE_ROWS == 0 and xb.shape[0] % ROUTE_ROWS == 0
    nta = xa.shape[0] // ROUTE_ROWS
    nt = nta + xb.shape[0] // ROUTE_ROWS
    slab_rows = (ROUTE_ROWS, ROW_SLABS, SLAB)
    grid_spec = pltpu.PrefetchScalarGridSpec(
        num_scalar_prefetch=1,
        grid=(nt,),
        in_specs=[
            pl.BlockSpec((None, 1, TOP_K * ROUTE_ROWS), lambda i, pe: (i, 0, 0), memory_space=pltpu.SMEM),
            pl.BlockSpec(slab_rows, lambda i, pe: (jnp.minimum(i, nta - 1), 0, 0)),
            pl.BlockSpec(slab_rows, lambda i, pe: (jnp.maximum(i - nta, 0), 0, 0)),
        ],
        out_specs=pl.BlockSpec(memory_space=pl.ANY),
        scratch_shapes=[
            pltpu.VMEM((MOE_ROWS, ROW_SLABS, SLAB), F32),
            pltpu.VMEM((2,) + slab_rows, F32),
            pltpu.SemaphoreType.DMA(()),
            pltpu.SemaphoreType.DMA((2,)),
        ],
    )
    return pl.pallas_call(
        functools.partial(_dispatch_kernel, nta=nta, nt=nt, nb=n_slots // MOE_ROWS),
        grid_spec=grid_spec,
        out_shape=jax.ShapeDtypeStruct((n_slots, ROW_SLABS, SLAB), F32),
        compiler_params=pltpu.CompilerParams(
            dimension_semantics=("arbitrary",), vmem_limit_bytes=VMEM_LIMIT, has_side_effects=True),
        name="moe_dispatch",
    )(pend, dest.reshape(nt, 1, TOP_K * ROUTE_ROWS), xa, xb)


def _combine_kernel(dest_ref, dest_next_ref, gate_ref, h_ref, ys_hbm, o_ref, buf, sem, *, nt):
    i = pl.program_id(0)
    tm = ROUTE_ROWS
    slot = i % 2

    def issue(dref, s):
        def body(r, c):
            for k in range(TOP_K):
                pltpu.make_async_copy(_row(ys_hbm, dref[0, TOP_K * r + k]), buf.at[s, k, pl.ds(r, 1)],
                                      sem.at[s]).start(priority=k)
            return c

        lax.fori_loop(0, tm, body, 0, unroll=DMA_UNROLL)

    @pl.when(i == 0)
    def _():
        issue(dest_ref, 0)

    @pl.when(i < nt - 1)
    def _():
        issue(dest_next_ref, 1 - slot)

    for k in range(TOP_K):
        pltpu.make_async_copy(ys_hbm.at[pl.ds(0, tm)], buf.at[slot, k], sem.at[slot]).wait()
    g = gate_ref[...]
    for s in range(ROW_SLABS):
        cols = slice(s * SLAB, (s + 1) * SLAB)
        o_ref[:, cols] = h_ref[:, cols] + (g[:, 0:1] * buf[slot, 0, :, s, :] + g[:, 1:2] * buf[slot, 1, :, s, :])


def _combine(h2, gates, dest, ys):
    t = h2.shape[0]
    assert t % ROUTE_ROWS == 0
    nt = t // ROUTE_ROWS
    dest3 = dest.reshape(nt, 1, TOP_K * ROUTE_ROWS)
    idx = lambda f: pl.BlockSpec((None, 1, TOP_K * ROUTE_ROWS), f, memory_space=pltpu.SMEM)
    return pl.pallas_call(
        functools.partial(_combine_kernel, nt=nt),
        grid=(nt,),
        in_specs=[
            idx(lambda i: (i, 0, 0)),
            idx(lambda i: (jnp.minimum(i + 1, nt - 1), 0, 0)),
            pl.BlockSpec((ROUTE_ROWS, ROUTER_PAD), lambda i: (i, 0)),
            pl.BlockSpec((ROUTE_ROWS, D_MODEL), lambda i: (i, 0)),
            pl.BlockSpec(memory_space=pl.ANY),
        ],
        out_specs=pl.BlockSpec((ROUTE_ROWS, D_MODEL), lambda i: (i, 0)),
        out_shape=jax.ShapeDtypeStruct((t, D_MODEL), F32),
        scratch_shapes=[
            pltpu.VMEM((2, TOP_K, ROUTE_ROWS, ROW_SLABS, SLAB), F32),
            pltpu.SemaphoreType.DMA((2,)),
        ],
        compiler_params=pltpu.CompilerParams(
            dimension_semantics=("arbitrary",), vmem_limit_bytes=VMEM_LIMIT),
        name="moe_combine",
    )(dest3, dest3, gates, h2, ys)


def _moe_plan(route, counts):
    T = route.shape[0]
    expert = route[:, :TOP_K]
    rank = route[:, TOP_K:2 * TOP_K]
    nb = -(-(T * TOP_K) // MOE_ROWS) + N_EXPERTS
    cnt = counts[0, :N_EXPERTS].astype(jnp.int32)
    padded = (cnt + MOE_ROWS - 1) // MOE_ROWS * MOE_ROWS
    pend = jnp.cumsum(padded).astype(jnp.int32)
    dest = (pend - padded)[expert] + rank
    starts = jnp.arange(nb, dtype=jnp.int32) * MOE_ROWS
    blk_e = jnp.minimum(jnp.sum((pend[None, :] <= starts[:, None]).astype(jnp.int32), 1), N_EXPERTS - 1)
    n_used = pend[-1:] // MOE_ROWS
    ids = jnp.arange(N_EXPERTS, dtype=jnp.int32)
    later = jnp.where(jnp.logical_and(ids[None, :] > ids[:, None], cnt[None, :] > 0), ids[None, :], N_EXPERTS)
    nxt = jnp.min(later, 1)
    nxt_e = jnp.where(nxt < N_EXPERTS, nxt, ids)[blk_e]
    order = jnp.cumsum((cnt > 0).astype(jnp.int32)) - 1
    slot = (order[blk_e] & 1).astype(jnp.int32)
    return dest, pend, blk_e, n_used, nxt_e.astype(jnp.int32), slot, nb * MOE_ROWS


def _lane_row(vals, lane0):
    return jnp.zeros((1, SMALL_PAD), F32).at[0, lane0:lane0 + vals.shape[0]].set(vals)


def kernel(x_prompt, x_sample, state_delta, state_conv, cache_swa_k, cache_swa_v, meta_tokens,
           norm1_w, w_in, conv_w, a_log, dt_bias, a_norm_w, w_up_a, q_norm_w, k_norm_w, sinks,
           w_up_b, w_o, norm2_w, w_router_group, b_router_group, w_router_expert, b_router_expert,
           w_gate, w_up, w_down):
    Bp, Sp, _ = x_prompt.shape
    Bs, Ss, _ = x_sample.shape
    n_s = Bs * Ss
    l = 0
    w = w_in[l]
    offs = [0]
    for s in (A_CONV_DIM, A_HEADS, A_HEADS, A_V, B_Q, B_KV, B_KV, D_MODEL, D_MODEL):
        offs.append(offs[-1] + s)
    seg = lambda a: w[:, offs[a]:offs[a + 1]]
    w_main = jnp.concatenate([seg(7), seg(8), seg(0), seg(3), seg(4), seg(5), seg(6)], 1).astype(BF16)
    w_small = jnp.concatenate(
        [seg(1), seg(2), jnp.zeros((D_MODEL, SMALL_PAD - 2 * A_HEADS), F32)], 1).astype(BF16)
    wa = w_up_a[l].astype(BF16)
    wb = w_up_b[l].astype(BF16)
    wo = w_o[l].astype(BF16)
    w_router = jnp.concatenate(
        [w_router_group[l], w_router_expert[l],
         jnp.zeros((D_MODEL, ROUTER_PAD - N_GROUPS - N_EXPERTS), F32)], 1).astype(BF16)
    b_router = jnp.concatenate(
        [b_router_group[l], b_router_expert[l],
         jnp.zeros((ROUTER_PAD - N_GROUPS - N_EXPERTS,), F32)])[None, :]
    g1 = norm1_w[l][None, :]
    g2 = norm2_w[l][None, :]
    alog_row = _lane_row(a_log[l], LANE_DECAY)
    dtb_row = _lane_row(dt_bias[l], LANE_DECAY)
    anw_row = a_norm_w[l][None, :]

    xp = x_prompt.reshape(Bp * Sp, D_MODEL)
    xs = x_sample.reshape(n_s, D_MODEL)
    x_small = jnp.concatenate([xs, meta_tokens], 0)
    n_small = x_small.shape[0]

    proj_p, small_p = _inproj(xp, g1, w_main, w_small, tm=1024)
    proj_s, small_s = _inproj(x_small, g1, w_main, w_small, tm=n_small)

    dm = functools.partial(_delta_mixer, conv_w=conv_w[l], alog_row=alog_row, dtb_row=dtb_row, anw_row=anw_row)
    pad_rows = lambda a, n: jnp.concatenate([a, jnp.zeros((n - a.shape[0],) + a.shape[1:], a.dtype)], 0)
    meta_proj = pad_rows(proj_s[n_s:], A_CHUNK)
    meta_small = pad_rows(small_s[n_s:], A_CHUNK)
    zero_prev = jnp.zeros((1, SUBLANES, A_CONV_DIM), F32)
    zero_state = jnp.zeros((1, A_HEADS, A_DK, A_DV), F32)
    _, s_meta = dm(meta_proj, meta_small, zero_prev, zero_state, nb=1, nc=1, C=A_CHUNK, n_valid=N_META,
                   shared_first=True)
    meta_tail = proj_s[n_s + N_META - SUBLANES:, COL_QKV:COL_QKV + A_CONV_DIM][None]
    ya_p, sdelta_p = dm(proj_p, small_p, meta_tail, s_meta, nb=Bp, nc=Sp // A_CHUNK, C=A_CHUNK,
                        n_valid=A_CHUNK, shared_first=True, nsub=DELTA_CHUNKS_PER_STEP)
    conv_p = proj_p.reshape(Bp, Sp, PROJ_MAIN)[:, Sp - (A_CONV - 1):, COL_QKV:COL_QKV + A_CONV_DIM]
    CS = SUBLANES
    samp_proj = jnp.pad(proj_s[:n_s].reshape(Bs, Ss, PROJ_MAIN), ((0, 0), (0, CS - Ss), (0, 0)))
    samp_small = jnp.pad(small_s[:n_s].reshape(Bs, Ss, SMALL_PAD), ((0, 0), (0, CS - Ss), (0, 0)))
    samp_prev = jnp.pad(state_conv[l], ((0, 0), (SUBLANES - (A_CONV - 1), 0), (0, 0)))
    ya_s8, sdelta_s = dm(samp_proj.reshape(Bs * CS, PROJ_MAIN), samp_small.reshape(Bs * CS, SMALL_PAD),
                         samp_prev, state_delta[l], nb=Bs, nc=1, C=CS, n_valid=Ss, shared_first=False,
                         nsub=SAMPLE_SEQS_PER_STEP, chained=False)
    ya_s = ya_s8.reshape(Bs, CS, A_V)[:, :Ss].reshape(n_s, A_V)
    qkv_s = proj_s[:n_s, COL_QKV:COL_QKV + A_CONV_DIM].reshape(Bs, Ss, A_CONV_DIM)
    conv_s = jnp.concatenate([state_conv[l], qkv_s], 1)[:, -(A_CONV - 1):]

    gq = jnp.tile(q_norm_w[l], B_HEADS)[None]
    gk = jnp.tile(k_norm_w[l], B_KV_HEADS)[None]
    bd = ((jnp.arange(B_Q) // B_HD)[:, None] == jnp.arange(LANES)[None, :]).astype(BF16)
    bdt = bd.T
    pos_small = jnp.concatenate([PAST_LEN + jnp.arange(n_s, dtype=jnp.int32) % Ss,
                                 jnp.arange(N_META, dtype=jnp.int32)])
    cos_s, sin_s = _rope_tables(pos_small)
    q_rot_s, k_rot_s = _rope_small(proj_s, cos_s, sin_s, gq, gk, bd, bdt)
    k_meta = k_rot_s[n_s:]
    v_meta = proj_s[n_s:, COL_VB:COL_VB + B_KV]
    cos_p, sin_p = _rope_tables(N_META + jnp.arange(Sp, dtype=jnp.int32))
    yb_p, k_last, v_last = _swa_prompt(proj_p, cos_p, sin_p, k_meta, v_meta, gq, gk, bd, bdt, sinks[l],
                                       Bp, Sp // WINDOW)
    cache_shape = (N_META + WINDOW, B_KV_HEADS, B_HD)
    bcast_meta = lambda a: jnp.broadcast_to(a[None], (Bp, N_META, B_KV))
    swk_p = jnp.concatenate([bcast_meta(k_meta), k_last], 1).reshape((Bp,) + cache_shape)
    swv_p = jnp.concatenate([bcast_meta(v_meta), v_last], 1).reshape((Bp,) + cache_shape)
    n_cache = N_META + WINDOW
    key_pad = -(n_cache + Ss) % SUBLANES
    zpad = jnp.zeros((Bs, key_pad, B_KV), F32)
    kk = jnp.concatenate([cache_swa_k[l].reshape(Bs, n_cache, B_KV), k_rot_s[:n_s].reshape(Bs, Ss, B_KV), zpad], 1)
    vv = jnp.concatenate([cache_swa_v[l].reshape(Bs, n_cache, B_KV),
                          proj_s[:n_s, COL_VB:COL_VB + B_KV].reshape(Bs, Ss, B_KV), zpad], 1)
    q8 = jnp.pad(q_rot_s[:n_s].reshape(Bs, Ss, B_Q), ((0, 0), (0, SUBLANES - Ss), (0, 0)))
    yb_s = _swa_sample(q8, kk, vv, sinks[l], n_new=Ss)[:, :Ss].reshape(n_s, B_Q)
    new_cache = lambda t: jnp.concatenate(
        [t[:, :N_META], t[:, n_cache + Ss - WINDOW:n_cache + Ss]], 1).reshape((Bs,) + cache_shape)
    swk_s, swv_s = new_cache(kk), new_cache(vv)

    merged_p = _merge(ya_p, yb_p, proj_p, wa, wb, tm=512)
    merged_s = _merge(ya_s, yb_s, proj_s[:n_s], wa, wb, tm=256)
    cnt0 = jnp.zeros((1, ROUTER_PAD), F32)
    h2_p, xn2_p, gt_p, rt_p, cnt_p = _outproj(merged_p, xp, wo, g2, w_router, b_router, cnt0, tm=256)
    h2_s, xn2_s, gt_s, rt_s, cnt = _outproj(merged_s, xs, wo, g2, w_router, b_router, cnt_p, tm=256)

    cat = lambda a, b: jnp.concatenate([a, b], 0)
    n_p = Bp * Sp
    dest, pend, blk_e, n_used, nxt_e, wslot, n_slots = _moe_plan(cat(rt_p, rt_s), cnt)
    xs_slots = _dispatch(xn2_p, xn2_s, dest, pend, n_slots)
    ys = _moe_ffn(xs_slots, blk_e, n_used, nxt_e, wslot, w_gate[l], w_up[l], w_down[l])
    y_prompt = _combine(h2_p, gt_p, dest[:n_p], ys).reshape(Bp, Sp, D_MODEL)
    y_sample = _combine(h2_s, gt_s, dest[n_p:], ys).reshape(Bs, Ss, D_MODEL)
    return (y_prompt, y_sample, sdelta_p[None], conv_p[None], swk_p[None], swv_p[None],
            sdelta_s[None], conv_s[None], swk_s[None], swv_s[None])
```

```python
import functools

import jax
import jax.numpy as jnp
from jax import lax
from jax.experimental import pallas as pl
from jax.experimental.pallas import tpu as pltpu

F32 = jnp.float32
BF16 = jnp.bfloat16

D_MODEL = 2048
N_META = 16
A_HEADS = 8
A_DK = 128
A_DV = 128
A_CONV = 4
A_CHUNK = 64
A_QK = A_HEADS * A_DK
A_V = A_HEADS * A_DV
A_CONV_DIM = 2 * A_QK + A_V
B_HEADS = 16
B_KV_HEADS = 4
B_HD = 64
B_GROUP = B_HEADS // B_KV_HEADS
B_Q = B_HEADS * B_HD
B_KV = B_KV_HEADS * B_HD
WINDOW = 128
ROPE_THETA = 10000.0
PAST_LEN = 16384
N_GROUPS = 4
EXPERTS_PER_GROUP = 8
N_EXPERTS = N_GROUPS * EXPERTS_PER_GROUP
TOP_K = 2
D_EXPERT = 512
EPS = 1e-6

COL_GA = 0
COL_GB = COL_GA + D_MODEL
COL_QKV = COL_GB + D_MODEL
COL_Z = COL_QKV + A_CONV_DIM
COL_QB = COL_Z + A_V
COL_KB = COL_QB + B_Q
COL_VB = COL_KB + B_KV
PROJ_MAIN = COL_VB + B_KV
SMALL_PAD = 128
ROUTER_PAD = 128
LANE_BETA = 0
LANE_DECAY = A_HEADS

SUBLANES = 8
MOE_ROWS = 256
DELTA_CHUNKS_PER_STEP = 4
SAMPLE_SEQS_PER_STEP = 8
ROW_SLABS = 8
SLAB = D_MODEL // ROW_SLABS
VMEM_LIMIT = 56 * 1024 * 1024


def _row_chunk(tm):
    return 256 if tm % 256 == 0 else tm


def _bdot(a, b):
    return jnp.dot(a, b, preferred_element_type=F32)


def _store_slabs(ref3, val):
    for s in range(ROW_SLABS):
        ref3[:, s, :] = val[:, s * SLAB:(s + 1) * SLAB]


def _load_slabs(ref3):
    return jnp.concatenate([ref3[:, s, :] for s in range(ROW_SLABS)], 1)


def _inproj_kernel(x_ref, g_ref, w_ref, ws_ref, o_ref, os_ref, xn_ref, *, tm):
    rc = _row_chunk(tm)

    @pl.when(pl.program_id(1) == 0)
    def _():
        def norm_rows(sl):
            x = x_ref[sl, :]
            ms = jnp.mean(x * x, axis=-1, keepdims=True)
            xn_ref[sl, :] = ((x * lax.rsqrt(ms + EPS)) * g_ref[...]).astype(BF16)

        def body(r, c):
            norm_rows(pl.ds(pl.multiple_of(r * rc, rc), rc))
            return c

        if tm == rc:
            norm_rows(pl.ds(0, tm))
        else:
            lax.fori_loop(0, tm // rc, body, 0)
        os_ref[...] = _bdot(xn_ref[...], ws_ref[...])

    o_ref[...] = _bdot(xn_ref[...], w_ref[...])


def _inproj(x, gain, w_main, w_small, tm, tn=512):
    m = x.shape[0]
    assert m % tm == 0 and PROJ_MAIN % tn == 0
    return pl.pallas_call(
        functools.partial(_inproj_kernel, tm=tm),
        grid=(m // tm, PROJ_MAIN // tn),
        in_specs=[
            pl.BlockSpec((tm, D_MODEL), lambda i, j: (i, 0)),
            pl.BlockSpec((1, D_MODEL), lambda i, j: (0, 0)),
            pl.BlockSpec((D_MODEL, tn), lambda i, j: (0, j)),
            pl.BlockSpec((D_MODEL, SMALL_PAD), lambda i, j: (0, 0)),
        ],
        out_specs=[
            pl.BlockSpec((tm, tn), lambda i, j: (i, j)),
            pl.BlockSpec((tm, SMALL_PAD), lambda i, j: (i, 0)),
        ],
        out_shape=[
            jax.ShapeDtypeStruct((m, PROJ_MAIN), F32),
            jax.ShapeDtypeStruct((m, SMALL_PAD), F32),
        ],
        scratch_shapes=[pltpu.VMEM((tm, D_MODEL), BF16)],
        compiler_params=pltpu.CompilerParams(
            dimension_semantics=("arbitrary", "arbitrary"), vmem_limit_bytes=VMEM_LIMIT),
        name="inproj",
    )(x, gain, w_main, w_small)


def _dot_exact_lhs(a_bf16, b):
    b1 = b.astype(BF16)
    r1 = b - b1.astype(F32)
    b2 = r1.astype(BF16)
    b3 = (r1 - b2.astype(F32)).astype(BF16)
    return _bdot(a_bf16, b1) + (_bdot(a_bf16, b2) + _bdot(a_bf16, b3))


def _unit_lower_inverse_offsets(ms, ri, ci, c):
    same = lambda n: (ri >> n) == (ci >> n)
    d = lambda a, b: _bdot(a.astype(BF16), b.astype(BF16))
    blk8 = same(3)
    n1 = [jnp.where(blk8, m, 0.0) for m in ms]
    n2 = [d(a, a) for a in n1]
    n3 = [d(a, b) for a, b in zip(n1, n2)]
    n4 = [d(b, b) for b in n2]
    qs = [(b - a) - t for a, b, t in zip(n1, n2, n3)]
    qs = [(q + f) + d(q, f) for q, f in zip(qs, n4)]
    lg = 3
    while (1 << lg) < c:
        sel = jnp.logical_and(same(lg + 1), jnp.logical_not(same(lg)))
        offs = [jnp.where(sel, m, 0.0) for m in ms]
        ts = [o + d(q, o) for q, o in zip(qs, offs)]
        qs = [(q - t) - d(t, q) for q, t in zip(qs, ts)]
        lg += 1
    return qs


def _delta_kernel(qc, kc, vc, zc, qp, kp, vp, qf, kf, vf, sm_ref, cw_ref, alog_ref, dtb_ref, anw_ref, s0_ref,
                  ya_ref, s_ref, *, C, n_valid, nsub, chained):
    R = nsub * C
    first = pl.program_id(1) == 0

    @pl.when(first)
    def _():
        s_ref[...] = s0_ref[...]

    def keep(a):
        if n_valid == C:
            return a
        rows = lax.broadcasted_iota(jnp.int32, (a.shape[0], 1), 0)
        return jnp.where((rows & (C - 1)) < n_valid, a, 0.0)

    ri = lax.broadcasted_iota(jnp.int32, (C, C), 0)
    ci = lax.broadcasted_iota(jnp.int32, (C, C), 1)
    lower = ri >= ci
    strict = ri > ci

    def conv_rows(prev, cur, col0):
        n = cur.shape[0]
        x = jnp.concatenate([prev, cur], 0)
        w = cw_ref[:, col0:col0 + A_QK]
        acc = x[SUBLANES:SUBLANES + n] * w[A_CONV - 1:A_CONV]
        for s in range(1, A_CONV):
            acc = acc + x[SUBLANES - s:SUBLANES - s + n] * w[A_CONV - 1 - s:A_CONV - s]
        return acc * jax.nn.sigmoid(acc)

    def conv(cur_ref, prev_ref, first_ref, col0):
        if chained:
            return conv_rows(jnp.where(first, first_ref[0], prev_ref[...]), cur_ref[...], col0)
        return jnp.concatenate(
            [conv_rows(first_ref[s], cur_ref[s * C:(s + 1) * C, :], col0) for s in range(nsub)], 0)

    qx = conv(qc, qp, qf, 0)
    kx = conv(kc, kp, kf, A_QK)
    vx = conv(vc, vp, vf, 2 * A_QK)

    sm = sm_ref[...]
    beta_all = keep(jax.nn.sigmoid(sm))
    g_all = keep(-jnp.exp(alog_ref[...]) * jax.nn.softplus(sm + dtb_ref[...]))
    rr = lax.broadcasted_iota(jnp.int32, (R, R), 0)
    rc = lax.broadcasted_iota(jnp.int32, (R, R), 1)
    sh = C.bit_length() - 1
    same_chunk_lower = jnp.logical_and(rr >= rc, (rr >> sh) == (rc >> sh))
    g_cum = _dot_exact_lhs(same_chunk_lower.astype(BF16), g_all)
    g_cum_t = g_cum.T
    e_g = jnp.exp(g_cum)

    heads = range(A_HEADS)
    subs = range(nsub)
    items = [(s, h) for s in subs for h in heads]
    hs = [slice(h * A_DK, (h + 1) * A_DK) for h in heads]
    rs = [slice(s * C, (s + 1) * C) for s in subs]
    ld = lambda h: slice(LANE_DECAY + h, LANE_DECAY + h + 1)
    qs = [qx[rs[s], hs[h]] for s, h in items]
    qs = [q * lax.rsqrt(jnp.sum(q * q, -1, keepdims=True) + EPS) * (A_DK ** -0.5) for q in qs]
    ks = [kx[rs[s], hs[h]] for s, h in items]
    ks = [keep(k * lax.rsqrt(jnp.sum(k * k, -1, keepdims=True) + EPS)) for k in ks]
    vs = [keep(vx[rs[s], hs[h]]) for s, h in items]
    betas = [beta_all[rs[s], LANE_BETA + h:LANE_BETA + h + 1] for s, h in items]
    egs = [e_g[rs[s], ld(h)] for s, h in items]
    gammas = [jnp.exp(jnp.where(lower, g_cum[rs[s], ld(h)] - g_cum_t[ld(h), rs[s]], -jnp.inf))
              for s, h in items]
    kbs = [k * b for k, b in zip(ks, betas)]
    a1s = [lax.dot_general(jnp.concatenate([kb, q], 0).astype(BF16), k.astype(BF16),
                           (((1,), (1,)), ((), ())), preferred_element_type=F32)
           for kb, q, k in zip(kbs, qs, ks)]
    ms = [jnp.where(strict, a1[:C] * gm, 0.0) for a1, gm in zip(a1s, gammas)]
    attns = [a1[C:] * gm for a1, gm in zip(a1s, gammas)]
    rhss = [jnp.concatenate([v * b, kb * eg], 1) for v, b, kb, eg in zip(vs, betas, kbs, egs)]
    sols = [r + _bdot(q.astype(BF16), r.astype(BF16))
            for q, r in zip(_unit_lower_inverse_offsets(ms, ri, ci, C), rhss)]
    wq_lhs = [jnp.concatenate([sol[:, A_DV:], q * eg], 0).astype(BF16) for sol, q, eg in zip(sols, qs, egs)]
    g_lasts = [g_cum[(s + 1) * C - 1:(s + 1) * C, :] for s in subs]
    kg_ts = [(k * jnp.exp(g_lasts[s][:, ld(h)] - g_cum[rs[s], ld(h)])).T
             for (s, h), k in zip(items, ks)]
    states = [s_ref[0, h] for h in heads]
    for s in subs:
        if not chained:
            states = [s_ref[s, h] for h in heads]
        at = lambda xs: xs[s * A_HEADS:(s + 1) * A_HEADS]
        wqs = [_bdot(l, st.astype(BF16)) for l, st in zip(at(wq_lhs), states)]
        v_news = [sol[:, :A_DV] - wq[:C] for sol, wq in zip(at(sols), wqs)]
        r2s = [_bdot(jnp.concatenate([a, kg], 0).astype(BF16), vn.astype(BF16))
               for a, kg, vn in zip(at(attns), at(kg_ts), v_news)]
        e_last = jnp.exp(g_lasts[s])
        states = [st * e_last[:, ld(h)] + r2[C:] for h, st, r2 in zip(heads, states, r2s)]
        for h in heads:
            o = wqs[h][C:] + r2s[h][:C]
            on = (o * lax.rsqrt(jnp.mean(o * o, -1, keepdims=True) + EPS)) * anw_ref[...]
            zh = zc[rs[s], hs[h]]
            ya_ref[rs[s], hs[h]] = (on * (zh * jax.nn.sigmoid(zh))).astype(BF16)
        if not chained:
            for h in heads:
                s_ref[s, h] = states[h]
    if chained:
        for h in heads:
            s_ref[0, h] = states[h]


def _delta_mixer(proj, small, first_prev, s0, conv_w, alog_row, dtb_row, anw_row, *, nb, nc, C, n_valid,
                 shared_first, nsub=1, chained=True):
    rows = nb * nc * C
    assert proj.shape[0] == rows and C % SUBLANES == 0 and C & (C - 1) == 0
    R = nsub * C
    if chained:
        assert nc % nsub == 0
        nseq, steps = 1, nc // nsub
    else:
        assert nc == 1 and nb % nsub == 0 and not shared_first
        nseq, steps, nb = nsub, 1, nb // nsub
    cpb = R // SUBLANES
    cq, ck, cv, cz = (COL_QKV // A_QK, COL_QKV // A_QK + 1, COL_QKV // A_QK + 2, COL_Z // A_V)
    fb = (lambda b: 0) if shared_first else (lambda b: b)
    cur = lambda col: pl.BlockSpec((R, A_QK), lambda b, c: (b * steps + c, col))
    prev = lambda col: pl.BlockSpec(
        (SUBLANES, A_QK), lambda b, c: (jnp.maximum((b * steps + c) * cpb - 1, 0), col))
    frst = lambda j: pl.BlockSpec((nseq, SUBLANES, A_QK), lambda b, c: (fb(b), 0, j))
    row1 = lambda n: pl.BlockSpec((1, n), lambda b, c: (0, 0))
    return pl.pallas_call(
        functools.partial(_delta_kernel, C=C, n_valid=n_valid, nsub=nsub, chained=chained),
        grid=(nb, steps),
        in_specs=[
            cur(cq), cur(ck), cur(cv), cur(cz),
            prev(cq), prev(ck), prev(cv),
            frst(0), frst(1), frst(2),
            pl.BlockSpec((R, SMALL_PAD), lambda b, c: (b * steps + c, 0)),
            pl.BlockSpec((A_CONV, A_CONV_DIM), lambda b, c: (0, 0)),
            row1(SMALL_PAD), row1(SMALL_PAD), row1(A_DV),
            pl.BlockSpec((nseq, A_HEADS, A_DK, A_DV), lambda b, c: (fb(b), 0, 0, 0)),
        ],
        out_specs=[
            pl.BlockSpec((R, A_V), lambda b, c: (b * steps + c, 0)),
            pl.BlockSpec((nseq, A_HEADS, A_DK, A_DV), lambda b, c: (b, 0, 0, 0)),
        ],
        out_shape=[
            jax.ShapeDtypeStruct((rows, A_V), BF16),
            jax.ShapeDtypeStruct((nb * nseq, A_HEADS, A_DK, A_DV), F32),
        ],
        compiler_params=pltpu.CompilerParams(
            dimension_semantics=("arbitrary", "arbitrary"), vmem_limit_bytes=VMEM_LIMIT),
        name="delta_mixer",
    )(proj, proj, proj, proj, proj, proj, proj, first_prev, first_prev, first_prev, small,
      conv_w, alog_row, dtb_row, anw_row, s0)


def _merge_kernel(ya_ref, yb_ref, ga_ref, gb_ref, wa_ref, wb_ref, o_ref):
    ua = _bdot(ya_ref[...].astype(BF16), wa_ref[...])
    ub = _bdot(yb_ref[...].astype(BF16), wb_ref[...])
    merged = jax.nn.sigmoid(ga_ref[...]) * ua + jax.nn.sigmoid(gb_ref[...]) * ub
    o_ref[...] = merged.astype(BF16)


def _merge(ya, yb, proj, wa, wb, tm):
    m = ya.shape[0]
    assert m % tm == 0
    return pl.pallas_call(
        _merge_kernel,
        grid=(m // tm,),
        in_specs=[
            pl.BlockSpec((tm, A_V), lambda i: (i, 0)),
            pl.BlockSpec((tm, B_Q), lambda i: (i, 0)),
            pl.BlockSpec((tm, D_MODEL), lambda i: (i, COL_GA // D_MODEL)),
            pl.BlockSpec((tm, D_MODEL), lambda i: (i, COL_GB // D_MODEL)),
            pl.BlockSpec((A_V, D_MODEL), lambda i: (0, 0)),
            pl.BlockSpec((B_Q, D_MODEL), lambda i: (0, 0)),
        ],
        out_specs=pl.BlockSpec((tm, D_MODEL), lambda i: (i, 0)),
        out_shape=jax.ShapeDtypeStruct((m, D_MODEL), BF16),
        compiler_params=pltpu.CompilerParams(
            dimension_semantics=("arbitrary",), vmem_limit_bytes=VMEM_LIMIT),
        name="merge",
    )(ya, yb, proj, proj, wa, wb)


LANE_SENTINEL = ROUTER_PAD - 1


def _lane_argmax(vals, eligible, lane):
    top = jnp.max(jnp.where(eligible, vals, -jnp.inf), -1, keepdims=True)
    idx = jnp.min(jnp.where(jnp.logical_and(eligible, vals == top), lane, LANE_SENTINEL), -1, keepdims=True)
    return top, idx


def _masked_softmax(logits, eligible):
    z = jnp.where(eligible, logits, -jnp.inf)
    e = jnp.exp(z - jnp.max(z, -1, keepdims=True))
    return e / jnp.sum(e, -1, keepdims=True)


def _outproj_kernel(m_ref, h_ref, wo_ref, g_ref, wr_ref, br_ref, cnt_in_ref,
                    h2_ref, xn_ref, gate_ref, route_ref, cnt_ref):
    tm = h_ref.shape[0]

    @pl.when(pl.program_id(0) == 0)
    def _():
        cnt_ref[...] = cnt_in_ref[...]

    h2 = h_ref[...] + _bdot(m_ref[...], wo_ref[...])
    h2_ref[...] = h2
    ms = jnp.mean(h2 * h2, axis=-1, keepdims=True)
    xn = (h2 * lax.rsqrt(ms + EPS)) * g_ref[...]
    _store_slabs(xn_ref, xn)
    lg = _bdot(xn.astype(BF16), wr_ref[...]) + br_ref[...]

    lane = lax.broadcasted_iota(jnp.int32, (tm, ROUTER_PAD), 1)
    is_g = lane < N_GROUPS
    p_grp, grp = _lane_argmax(_masked_softmax(lg, is_g), is_g, lane)
    ex = lane - N_GROUPS
    is_e = jnp.logical_and(jnp.logical_and(ex >= 0, ex < N_EXPERTS), (ex >> 3) == grp)
    pe = _masked_softmax(lg, is_e)
    p1, i1 = _lane_argmax(pe, is_e, lane)
    rest = jnp.logical_and(is_e, lane != i1)
    p2, i2 = _lane_argmax(pe, rest, lane)
    den = p1 + p2
    gate_ref[...] = jnp.where(lane == 0, (p_grp * p1) / den, jnp.where(lane == 1, (p_grp * p2) / den, 0.0))

    e1 = i1 - N_GROUPS
    e2 = i2 - N_GROUPS
    oh1 = lane == e1
    oh2 = lane == e2
    ri = lax.broadcasted_iota(jnp.int32, (tm, tm), 0)
    ci = lax.broadcasted_iota(jnp.int32, (tm, tm), 1)
    below = (ri > ci).astype(BF16)
    f1 = oh1.astype(F32)
    f2 = oh2.astype(F32)
    tot1 = jnp.sum(f1, 0, keepdims=True)
    run = cnt_ref[...]
    before1 = run + _bdot(below, oh1.astype(BF16))
    before2 = run + tot1 + _bdot(below, oh2.astype(BF16))
    rank1 = jnp.sum(f1 * before1, -1, keepdims=True).astype(jnp.int32)
    rank2 = jnp.sum(f2 * before2, -1, keepdims=True).astype(jnp.int32)
    cnt_ref[...] = run + tot1 + jnp.sum(f2, 0, keepdims=True)
    route_ref[...] = jnp.where(lane == 0, e1, jnp.where(lane == 1, e2, jnp.where(
        lane == 2, rank1, jnp.where(lane == 3, rank2, 0))))


def _outproj(merged, h, wo, gain2, w_router, b_router, counts_in, tm):
    m = h.shape[0]
    assert m % tm == 0
    row = lambda n: pl.BlockSpec((1, n), lambda i: (0, 0))
    tile = lambda n: pl.BlockSpec((tm, n), lambda i: (i, 0))
    return pl.pallas_call(
        _outproj_kernel,
        grid=(m // tm,),
        in_specs=[
            tile(D_MODEL), tile(D_MODEL),
            pl.BlockSpec((D_MODEL, D_MODEL), lambda i: (0, 0)),
            row(D_MODEL),
            pl.BlockSpec((D_MODEL, ROUTER_PAD), lambda i: (0, 0)),
            row(ROUTER_PAD), row(ROUTER_PAD),
        ],
        out_specs=[tile(D_MODEL), pl.BlockSpec((tm, ROW_SLABS, SLAB), lambda i: (i, 0, 0)),
                   tile(ROUTER_PAD), tile(ROUTER_PAD), row(ROUTER_PAD)],
        out_shape=[
            jax.ShapeDtypeStruct((m, D_MODEL), F32),
            jax.ShapeDtypeStruct((m, ROW_SLABS, SLAB), F32),
            jax.ShapeDtypeStruct((m, ROUTER_PAD), F32),
            jax.ShapeDtypeStruct((m, ROUTER_PAD), jnp.int32),
            jax.ShapeDtypeStruct((1, ROUTER_PAD), F32),
        ],
        compiler_params=pltpu.CompilerParams(
            dimension_semantics=("arbitrary",), vmem_limit_bytes=VMEM_LIMIT),
        name="outproj",
    )(merged, h, wo, gain2, w_router, b_router, counts_in)


def _moe_kernel(blk_e_ref, nused_ref, nxt_e_ref, slot_ref, x_ref, wg_hbm, wu_hbm, wd_hbm, o_ref,
                wg32, wu32, wd32, wgb, wub, wdb, sem):
    i = pl.program_id(0)
    e = blk_e_ref[i]
    e_prev = blk_e_ref[jnp.maximum(i - 1, 0)]
    used = i < nused_ref[0]

    def copies(expert, slot):
        return (pltpu.make_async_copy(wg_hbm.at[expert], wg32.at[slot], sem.at[slot]),
                pltpu.make_async_copy(wu_hbm.at[expert], wu32.at[slot], sem.at[slot]),
                pltpu.make_async_copy(wd_hbm.at[expert], wd32.at[slot], sem.at[slot]))

    @pl.when(i == 0)
    def _():
        for c in copies(e, 0):
            c.start()

    @pl.when(jnp.logical_and(used, jnp.logical_or(i == 0, e != e_prev)))
    def _():
        slot = slot_ref[i]
        for c in copies(e, slot):
            c.wait()
        nxt = nxt_e_ref[i]

        @pl.when(nxt != e)
        def _():
            for c in copies(nxt, 1 - slot):
                c.start()

        def cast_in(r, c):
            sl = pl.ds(pl.multiple_of(r * 256, 256), 256)
            wgb[sl, :] = wg32[slot, sl, :].astype(BF16)
            wub[sl, :] = wu32[slot, sl, :].astype(BF16)
            return c

        lax.fori_loop(0, D_MODEL // 256, cast_in, 0)

        def cast_out(r, c):
            sl = pl.ds(pl.multiple_of(r * 128, 128), 128)
            wdb[sl, :] = wd32[slot, sl, :].astype(BF16)
            return c

        lax.fori_loop(0, D_EXPERT // 128, cast_out, 0)

    @pl.when(used)
    def _():
        x = _load_slabs(x_ref).astype(BF16)
        g = _bdot(x, wgb[...])
        u = _bdot(x, wub[...])
        hb = (g * jax.nn.sigmoid(g)) * u
        _store_slabs(o_ref, _bdot(hb.astype(BF16), wdb[...]))

    @pl.when(jnp.logical_not(used))
    def _():
        o_ref[...] = jnp.zeros_like(o_ref)


def _moe_ffn(xs, blk_e, n_used, nxt_e, slot, w_gate, w_up, w_down):
    p = xs.shape[0]
    nb = p // MOE_ROWS
    grid_spec = pltpu.PrefetchScalarGridSpec(
        num_scalar_prefetch=4,
        grid=(nb,),
        in_specs=[
            pl.BlockSpec((MOE_ROWS, ROW_SLABS, SLAB), lambda i, be, nu, nx, sl: (jnp.minimum(i, nu[0] - 1), 0, 0)),
            pl.BlockSpec(memory_space=pl.ANY),
            pl.BlockSpec(memory_space=pl.ANY),
            pl.BlockSpec(memory_space=pl.ANY),
        ],
        out_specs=pl.BlockSpec((MOE_ROWS, ROW_SLABS, SLAB), lambda i, be, nu, nx, sl: (i, 0, 0)),
        scratch_shapes=[
            pltpu.VMEM((2, D_MODEL, D_EXPERT), F32),
            pltpu.VMEM((2, D_MODEL, D_EXPERT), F32),
            pltpu.VMEM((2, D_EXPERT, D_MODEL), F32),
            pltpu.VMEM((D_MODEL, D_EXPERT), BF16),
            pltpu.VMEM((D_MODEL, D_EXPERT), BF16),
            pltpu.VMEM((D_EXPERT, D_MODEL), BF16),
            pltpu.SemaphoreType.DMA((2,)),
        ],
    )
    return pl.pallas_call(
        _moe_kernel,
        grid_spec=grid_spec,
        out_shape=jax.ShapeDtypeStruct((p, ROW_SLABS, SLAB), F32),
        compiler_params=pltpu.CompilerParams(
            dimension_semantics=("arbitrary",), vmem_limit_bytes=VMEM_LIMIT),
        name="moe_ffn",
    )(blk_e, n_used, nxt_e, slot, xs, w_gate, w_up, w_down)


HALF_TILE = 64
SWA_HEADS_PER_STAGE = 16
LANES = 128


def _dot_exact_rhs(a, b01):
    a1 = a.astype(BF16)
    r1 = a - a1.astype(F32)
    a2 = r1.astype(BF16)
    a3 = (r1 - a2.astype(F32)).astype(BF16)
    return _bdot(a1, b01) + (_bdot(a2, b01) + _bdot(a3, b01))


def _group_rms_rope(x, gain, cos128, sin128, bd, bdt):
    r, w = x.shape
    ssq = _dot_exact_rhs(x * x, bd)
    rs = lax.rsqrt(ssq * (1.0 / B_HD) + EPS)
    y = (x * _dot_exact_rhs(rs, bdt)) * gain
    reps = w // LANES
    cosw = jnp.concatenate([cos128] * reps, 1)
    sinw = jnp.concatenate([sin128] * reps, 1)
    lane = lax.broadcasted_iota(jnp.int32, (r, w), 1)
    swapped = jnp.where((lane & (B_HD // 2)) == 0,
                        pltpu.roll(y, w - B_HD // 2, 1), pltpu.roll(y, B_HD // 2, 1))
    return y * cosw + swapped * sinw


def _kv_tiles(tile, odd, lo):
    rolled = pltpu.roll(tile, HALF_TILE, 1)
    dup = jnp.where(lo, rolled, tile) if odd else jnp.where(lo, tile, rolled)
    return dup.astype(BF16), jnp.where(lo, dup, 0.0).astype(BF16), jnp.where(lo, 0.0, dup).astype(BF16)


def _nt(a, b):
    return lax.dot_general(a, b, (((1,), (1,)), ((), ())), preferred_element_type=F32)


def _sink_softmax_pv(score_lists, value_lists, sinks):
    ms = []
    for scores, sink in zip(score_lists, sinks):
        m = jnp.max(scores[0], -1, keepdims=True)
        for s in scores[1:]:
            m = jnp.maximum(m, jnp.max(s, -1, keepdims=True))
        ms.append(jnp.maximum(m, sink))
    es = [[jnp.exp(s - m) for s in scores] for scores, m in zip(score_lists, ms)]
    dens = []
    for e_blocks, m, sink in zip(es, ms, sinks):
        den = jnp.exp(sink - m)
        for e in e_blocks:
            den = den + jnp.sum(e, -1, keepdims=True)
        dens.append(den)
    pvs = [[_bdot(e.astype(BF16), v) for e, v in zip(e_blocks, values)]
           for e_blocks, values in zip(es, value_lists)]
    outs = []
    for pv, den in zip(pvs, dens):
        acc = pv[0]
        for x in pv[1:]:
            acc = acc + x
        outs.append(acc / den)
    return outs


def _rope_small_kernel(q_ref, k_ref, cos_ref, sin_ref, gq_ref, gk_ref, bd_ref, bdt_ref, qo_ref, ko_ref):
    cos = cos_ref[...]
    sin = sin_ref[...]
    qo_ref[...] = _group_rms_rope(q_ref[...], gq_ref[...], cos, sin, bd_ref[...], bdt_ref[...])
    ko_ref[...] = _group_rms_rope(k_ref[...], gk_ref[...], cos, sin, bd_ref[:B_KV, :], bdt_ref[:, :B_KV])


def _rope_small(proj, cos, sin, gq, gk, bd, bdt):
    m = proj.shape[0]
    full = lambda shape: pl.BlockSpec(shape, lambda i: (0, 0))
    return pl.pallas_call(
        _rope_small_kernel,
        grid=(1,),
        in_specs=[
            pl.BlockSpec((m, B_Q), lambda i: (0, COL_QB // B_Q)),
            pl.BlockSpec((m, B_KV), lambda i: (0, COL_KB // B_KV)),
            full((m, LANES)), full((m, LANES)), full((1, B_Q)), full((1, B_KV)),
            full((B_Q, LANES)), full((LANES, B_Q)),
        ],
        out_specs=[full((m, B_Q)), full((m, B_KV))],
        out_shape=[jax.ShapeDtypeStruct((m, B_Q), F32), jax.ShapeDtypeStruct((m, B_KV), F32)],
        compiler_params=pltpu.CompilerParams(
            dimension_semantics=("arbitrary",), vmem_limit_bytes=VMEM_LIMIT),
        name="rope_small",
    )(proj, proj, cos, sin, gq, gk, bd, bdt)


def _swa_prompt_kernel(sinks_ref, q_ref, k_ref, v_ref, cos_ref, sin_ref, km_ref, vm_ref, gq_ref, gk_ref,
                       bd_ref, bdt_ref, y_ref, kc_ref, vc_ref, kprev, vloprev, vhiprev):
    n = pl.program_id(1)

    @pl.when(n == 0)
    def _():
        kprev[...] = jnp.zeros_like(kprev)
        vloprev[...] = jnp.zeros_like(vloprev)
        vhiprev[...] = jnp.zeros_like(vhiprev)

    cos = cos_ref[...]
    sin = sin_ref[...]
    q = _group_rms_rope(q_ref[...], gq_ref[...], cos, sin, bd_ref[...], bdt_ref[...])
    k = _group_rms_rope(k_ref[...], gk_ref[...], cos, sin, bd_ref[:B_KV, :], bdt_ref[:, :B_KV])
    v = v_ref[...]
    kc_ref[...] = k
    vc_ref[...] = v
    km = km_ref[...]
    vm = vm_ref[...]

    qi = lax.broadcasted_iota(jnp.int32, (WINDOW, WINDOW), 0)
    kj = lax.broadcasted_iota(jnp.int32, (WINDOW, WINDOW), 1)
    cur_ok = kj <= qi
    prev_ok = jnp.logical_and(kj > qi, n > 0)
    lo = lax.broadcasted_iota(jnp.int32, (WINDOW, LANES), 1) < HALF_TILE
    lo_m = lax.broadcasted_iota(jnp.int32, (N_META, LANES), 1) < HALF_TILE
    scale = B_HD ** -0.5

    kv = []
    for g in range(B_KV_HEADS):
        tl = slice((g // 2) * LANES, (g // 2 + 1) * LANES)
        k2c, _, _ = _kv_tiles(k[:, tl], g % 2, lo)
        _, vlo_c, vhi_c = _kv_tiles(v[:, tl], g % 2, lo)
        k2m, _, _ = _kv_tiles(km[:, tl], g % 2, lo_m)
        _, vlo_m, vhi_m = _kv_tiles(vm[:, tl], g % 2, lo_m)
        kv.append(((k2m, kprev[g], k2c), (vlo_m, vloprev[g], vlo_c), (vhi_m, vhiprev[g], vhi_c)))
    heads = range(B_HEADS)
    qms = []
    for h in heads:
        qt = q[:, (h // 2) * LANES:(h // 2 + 1) * LANES]
        qms.append((jnp.where(lo, qt, 0.0) if h % 2 == 0 else jnp.where(lo, 0.0, qt)).astype(BF16))
    for h0 in range(0, B_HEADS, SWA_HEADS_PER_STAGE):
        hg = range(h0, h0 + SWA_HEADS_PER_STAGE)
        raw = [[_nt(qms[h], kk) * scale for kk in kv[h // B_GROUP][0]] for h in hg]
        scores = [[sm, jnp.where(prev_ok, sp, -jnp.inf), jnp.where(cur_ok, sc, -jnp.inf)] for sm, sp, sc in raw]
        outs = _sink_softmax_pv(scores, [kv[h // B_GROUP][1 + h % 2] for h in hg], [sinks_ref[h] for h in hg])
        for t in range(SWA_HEADS_PER_STAGE // 2):
            y_ref[:, (h0 // 2 + t) * LANES:(h0 // 2 + t + 1) * LANES] = (outs[2 * t] + outs[2 * t + 1]).astype(BF16)
    for g in range(B_KV_HEADS):
        kprev[g] = kv[g][0][2]
        vloprev[g] = kv[g][1][2]
        vhiprev[g] = kv[g][2][2]


def _swa_prompt(proj, cos, sin, k_meta, v_meta, gq, gk, bd, bdt, sinks, nb, nblk):
    rows = nb * nblk * WINDOW
    assert proj.shape[0] == rows
    return pl.pallas_call(
        _swa_prompt_kernel,
        grid=(nb, nblk),
        in_specs=[
            pl.BlockSpec(memory_space=pltpu.SMEM),
            pl.BlockSpec((WINDOW, B_Q), lambda b, n: (b * nblk + n, COL_QB // B_Q)),
            pl.BlockSpec((WINDOW, B_KV), lambda b, n: (b * nblk + n, COL_KB // B_KV)),
            pl.BlockSpec((WINDOW, B_KV), lambda b, n: (b * nblk + n, COL_VB // B_KV)),
            pl.BlockSpec((WINDOW, LANES), lambda b, n: (n, 0)),
            pl.BlockSpec((WINDOW, LANES), lambda b, n: (n, 0)),
            pl.BlockSpec((N_META, B_KV), lambda b, n: (0, 0)),
            pl.BlockSpec((N_META, B_KV), lambda b, n: (0, 0)),
            pl.BlockSpec((1, B_Q), lambda b, n: (0, 0)),
            pl.BlockSpec((1, B_KV), lambda b, n: (0, 0)),
            pl.BlockSpec((B_Q, LANES), lambda b, n: (0, 0)),
            pl.BlockSpec((LANES, B_Q), lambda b, n: (0, 0)),
        ],
        out_specs=[
            pl.BlockSpec((WINDOW, B_Q), lambda b, n: (b * nblk + n, 0)),
            pl.BlockSpec((None, WINDOW, B_KV), lambda b, n: (b, 0, 0)),
            pl.BlockSpec((None, WINDOW, B_KV), lambda b, n: (b, 0, 0)),
        ],
        scratch_shapes=[pltpu.VMEM((B_KV_HEADS, WINDOW, LANES), BF16)] * 3,
        out_shape=[
            jax.ShapeDtypeStruct((rows, B_Q), BF16),
            jax.ShapeDtypeStruct((nb, WINDOW, B_KV), F32),
            jax.ShapeDtypeStruct((nb, WINDOW, B_KV), F32),
        ],
        compiler_params=pltpu.CompilerParams(
            dimension_semantics=("arbitrary", "arbitrary"), vmem_limit_bytes=VMEM_LIMIT),
        name="swa_prompt",
    )(sinks, proj, proj, proj, cos, sin, k_meta, v_meta, gq, gk, bd, bdt)


def _swa_sample_kernel(sinks_ref, q_ref, k_ref, v_ref, y_ref, *, n_keys, n_new):
    nseq, rq, _ = q_ref.shape
    t = lax.broadcasted_iota(jnp.int32, (rq, n_keys), 0)
    r = lax.broadcasted_iota(jnp.int32, (rq, n_keys), 1)
    wj = r - N_META
    win_pos = PAST_LEN - WINDOW + wj
    nm = r - N_META - WINDOW
    ok = ((r < N_META)
          | ((wj >= 0) & (wj < WINDOW) & (win_pos >= N_META) & (wj >= t + 1))
          | ((nm >= 0) & (nm <= t) & (nm > t - WINDOW) & (nm < n_new)))
    lo_q = lax.broadcasted_iota(jnp.int32, (rq, LANES), 1) < HALF_TILE
    lo_k = lax.broadcasted_iota(jnp.int32, (n_keys, LANES), 1) < HALF_TILE
    scale = B_HD ** -0.5
    heads = range(B_HEADS)
    scores, values = [], []
    for s in range(nseq):
        q = q_ref[s]
        k = k_ref[s]
        v = v_ref[s]
        kv = []
        for g in range(B_KV_HEADS):
            tl = slice((g // 2) * LANES, (g // 2 + 1) * LANES)
            k2, _, _ = _kv_tiles(k[:, tl], g % 2, lo_k)
            _, vlo, vhi = _kv_tiles(v[:, tl], g % 2, lo_k)
            kv.append((k2, vlo, vhi))
        for h in heads:
            qt = q[:, (h // 2) * LANES:(h // 2 + 1) * LANES]
            qm = (jnp.where(lo_q, qt, 0.0) if h % 2 == 0 else jnp.where(lo_q, 0.0, qt)).astype(BF16)
            scores.append([jnp.where(ok, _nt(qm, kv[h // B_GROUP][0]) * scale, -jnp.inf)])
            values.append([kv[h // B_GROUP][1 + h % 2]])
    outs = _sink_softmax_pv(scores, values, [sinks_ref[h] for _s in range(nseq) for h in heads])
    for s in range(nseq):
        for t2 in range(B_HEADS // 2):
            pair = outs[s * B_HEADS + 2 * t2] + outs[s * B_HEADS + 2 * t2 + 1]
            y_ref[s, :, t2 * LANES:(t2 + 1) * LANES] = pair.astype(BF16)


def _swa_sample(q8, kk, vv, sinks, n_new):
    nb, rq, _ = q8.shape
    n_keys = kk.shape[1]
    nseq = SAMPLE_SEQS_PER_STEP
    assert nb % nseq == 0
    return pl.pallas_call(
        functools.partial(_swa_sample_kernel, n_keys=n_keys, n_new=n_new),
        grid=(nb // nseq,),
        in_specs=[
            pl.BlockSpec(memory_space=pltpu.SMEM),
            pl.BlockSpec((nseq, rq, B_Q), lambda b: (b, 0, 0)),
            pl.BlockSpec((nseq, n_keys, B_KV), lambda b: (b, 0, 0)),
            pl.BlockSpec((nseq, n_keys, B_KV), lambda b: (b, 0, 0)),
        ],
        out_specs=pl.BlockSpec((nseq, rq, B_Q), lambda b: (b, 0, 0)),
        out_shape=jax.ShapeDtypeStruct((nb, rq, B_Q), BF16),
        compiler_params=pltpu.CompilerParams(
            dimension_semantics=("arbitrary",), vmem_limit_bytes=VMEM_LIMIT),
        name="swa_sample",
    )(sinks, q8, kk, vv)


def _rope_tables(pos):
    half = B_HD // 2
    inv_freq = ROPE_THETA ** (-jnp.arange(half, dtype=F32) / half)
    ang = pos.astype(F32)[:, None] * inv_freq[None, :]
    cos, sin = jnp.cos(ang), jnp.sin(ang)
    return jnp.concatenate([cos, cos, cos, cos], 1), jnp.concatenate([-sin, sin, -sin, sin], 1)


ROUTE_ROWS = 256
DMA_UNROLL = 8


def _row(ref, r):
    return ref.at[pl.ds(r, 1)]


def _dispatch_kernel(pend_ref, dest_ref, xa_ref, xb_ref, xs_hbm, zeros_vmem, ring, sem_zero, sem, *, nta, nt, nb):
    i = pl.program_id(0)
    tm = ROUTE_ROWS
    slot = i % 2

    def drain(s):
        for _k in range(TOP_K):
            pltpu.make_async_copy(ring.at[s], xs_hbm.at[pl.ds(0, tm)], sem.at[s]).wait()

    @pl.when(i == 0)
    def _():
        zeros_vmem[...] = jnp.zeros_like(zeros_vmem)
        fill = lambda row0: pltpu.make_async_copy(zeros_vmem, xs_hbm.at[pl.ds(row0, MOE_ROWS)], sem_zero)
        for e in range(N_EXPERTS):
            fill(jnp.maximum(pend_ref[e] - MOE_ROWS, 0)).start()
        for e in range(N_EXPERTS):
            fill(0).wait()
        n_used = pend_ref[N_EXPERTS - 1] // MOE_ROWS

        def fill_tail(b, c):
            fill(b * MOE_ROWS).start()
            fill(0).wait()
            return c

        lax.fori_loop(n_used, nb, fill_tail, 0)

    @pl.when(i >= 2)
    def _():
        drain(slot)

    @pl.when(i < nta)
    def _():
        ring[slot] = xa_ref[...]

    @pl.when(i >= nta)
    def _():
        ring[slot] = xb_ref[...]

    def body(r, c):
        src = ring.at[slot, pl.ds(r, 1)]
        for k in range(TOP_K):
            pltpu.make_async_copy(src, _row(xs_hbm, dest_ref[0, TOP_K * r + k]), sem.at[slot]).start(priority=k)
        return c

    lax.fori_loop(0, tm, body, 0, unroll=DMA_UNROLL)

    @pl.when(i == nt - 1)
    def _():
        if nt >= 2:
            drain(1 - slot)
        drain(slot)


def _dispatch(xa, xb, dest, pend, n_slots):
    assert xa.shape[0] % ROUTE_ROWS == 0 and xb.shape[0] % ROUTE_ROWS == 0
    nta = xa.shape[0] // ROUTE_ROWS
    nt = nta + xb.shape[0] // ROUTE_ROWS
    slab_rows = (ROUTE_ROWS, ROW_SLABS, SLAB)
    grid_spec = pltpu.PrefetchScalarGridSpec(
        num_scalar_prefetch=1,
        grid=(nt,),
        in_specs=[
            pl.BlockSpec((None, 1, TOP_K * ROUTE_ROWS), lambda i, pe: (i, 0, 0), memory_space=pltpu.SMEM),
            pl.BlockSpec(slab_rows, lambda i, pe: (jnp.minimum(i, nta - 1), 0, 0)),
            pl.BlockSpec(slab_rows, lambda i, pe: (jnp.maximum(i - nta, 0), 0, 0)),
        ],
        out_specs=pl.BlockSpec(memory_space=pl.ANY),
        scratch_shapes=[
            pltpu.VMEM((MOE_ROWS, ROW_SLABS, SLAB), F32),
            pltpu.VMEM((2,) + slab_rows, F32),
            pltpu.SemaphoreType.DMA(()),
            pltpu.SemaphoreType.DMA((2,)),
        ],
    )
    return pl.pallas_call(
        functools.partial(_dispatch_kernel, nta=nta, nt=nt, nb=n_slots // MOE_ROWS),
        grid_spec=grid_spec,
        out_shape=jax.ShapeDtypeStruct((n_slots, ROW_SLABS, SLAB), F32),
        compiler_params=pltpu.CompilerParams(
            dimension_semantics=("arbitrary",), vmem_limit_bytes=VMEM_LIMIT, has_side_effects=True),
        name="moe_dispatch",
    )(pend, dest.reshape(nt, 1, TOP_K * ROUTE_ROWS), xa, xb)


def _combine_kernel(dest_ref, dest_next_ref, gate_ref, h_ref, ys_hbm, o_ref, buf, sem, *, nt):
    i = pl.program_id(0)
    tm = ROUTE_ROWS
    slot = i % 2

    def issue(dref, s):
        def body(r, c):
            for k in range(TOP_K):
                pltpu.make_async_copy(_row(ys_hbm, dref[0, TOP_K * r + k]), buf.at[s, k, pl.ds(r, 1)],
                                      sem.at[s]).start(priority=k)
            return c

        lax.fori_loop(0, tm, body, 0, unroll=DMA_UNROLL)

    @pl.when(i == 0)
    def _():
        issue(dest_ref, 0)

    @pl.when(i < nt - 1)
    def _():
        issue(dest_next_ref, 1 - slot)

    for k in range(TOP_K):
        pltpu.make_async_copy(ys_hbm.at[pl.ds(0, tm)], buf.at[slot, k], sem.at[slot]).wait()
    g = gate_ref[...]
    for s in range(ROW_SLABS):
        cols = slice(s * SLAB, (s + 1) * SLAB)
        o_ref[:, cols] = h_ref[:, cols] + (g[:, 0:1] * buf[slot, 0, :, s, :] + g[:, 1:2] * buf[slot, 1, :, s, :])


def _combine(h2, gates, dest, ys):
    t = h2.shape[0]
    assert t % ROUTE_ROWS == 0
    nt = t // ROUTE_ROWS
    dest3 = dest.reshape(nt, 1, TOP_K * ROUTE_ROWS)
    idx = lambda f: pl.BlockSpec((None, 1, TOP_K * ROUTE_ROWS), f, memory_space=pltpu.SMEM)
    return pl.pallas_call(
        functools.partial(_combine_kernel, nt=nt),
        grid=(nt,),
        in_specs=[
            idx(lambda i: (i, 0, 0)),
            idx(lambda i: (jnp.minimum(i + 1, nt - 1), 0, 0)),
            pl.BlockSpec((ROUTE_ROWS, ROUTER_PAD), lambda i: (i, 0)),
            pl.BlockSpec((ROUTE_ROWS, D_MODEL), lambda i: (i, 0)),
            pl.BlockSpec(memory_space=pl.ANY),
        ],
        out_specs=pl.BlockSpec((ROUTE_ROWS, D_MODEL), lambda i: (i, 0)),
        out_shape=jax.ShapeDtypeStruct((t, D_MODEL), F32),
        scratch_shapes=[
            pltpu.VMEM((2, TOP_K, ROUTE_ROWS, ROW_SLABS, SLAB), F32),
            pltpu.SemaphoreType.DMA((2,)),
        ],
        compiler_params=pltpu.CompilerParams(
            dimension_semantics=("arbitrary",), vmem_limit_bytes=VMEM_LIMIT),
        name="moe_combine",
    )(dest3, dest3, gates, h2, ys)


def _moe_plan(route, counts):
    T = route.shape[0]
    expert = route[:, :TOP_K]
    rank = route[:, TOP_K:2 * TOP_K]
    nb = -(-(T * TOP_K) // MOE_ROWS) + N_EXPERTS
    cnt = counts[0, :N_EXPERTS].astype(jnp.int32)
    padded = (cnt + MOE_ROWS - 1) // MOE_ROWS * MOE_ROWS
    pend = jnp.cumsum(padded).astype(jnp.int32)
    dest = (pend - padded)[expert] + rank
    starts = jnp.arange(nb, dtype=jnp.int32) * MOE_ROWS
    blk_e = jnp.minimum(jnp.sum((pend[None, :] <= starts[:, None]).astype(jnp.int32), 1), N_EXPERTS - 1)
    n_used = pend[-1:] // MOE_ROWS
    ids = jnp.arange(N_EXPERTS, dtype=jnp.int32)
    later = jnp.where(jnp.logical_and(ids[None, :] > ids[:, None], cnt[None, :] > 0), ids[None, :], N_EXPERTS)
    nxt = jnp.min(later, 1)
    nxt_e = jnp.where(nxt < N_EXPERTS, nxt, ids)[blk_e]
    order = jnp.cumsum((cnt > 0).astype(jnp.int32)) - 1
    slot = (order[blk_e] & 1).astype(jnp.int32)
    return dest, pend, blk_e, n_used, nxt_e.astype(jnp.int32), slot, nb * MOE_ROWS


def _lane_row(vals, lane0):
    return jnp.zeros((1, SMALL_PAD), F32).at[0, lane0:lane0 + vals.shape[0]].set(vals)


def kernel(x_prompt, x_sample, state_delta, state_conv, cache_swa_k, cache_swa_v, meta_tokens,
           norm1_w, w_in, conv_w, a_log, dt_bias, a_norm_w, w_up_a, q_norm_w, k_norm_w, sinks,
           w_up_b, w_o, norm2_w, w_router_group, b_router_group, w_router_expert, b_router_expert,
           w_gate, w_up, w_down):
    Bp, Sp, _ = x_prompt.shape
    Bs, Ss, _ = x_sample.shape
    n_s = Bs * Ss
    l = 0
    w = w_in[l]
    offs = [0]
    for s in (A_CONV_DIM, A_HEADS, A_HEADS, A_V, B_Q, B_KV, B_KV, D_MODEL, D_MODEL):
        offs.append(offs[-1] + s)
    seg = lambda a: w[:, offs[a]:offs[a + 1]]
    w_main = jnp.concatenate([seg(7), seg(8), seg(0), seg(3), seg(4), seg(5), seg(6)], 1).astype(BF16)
    w_small = jnp.concatenate(
        [seg(1), seg(2), jnp.zeros((D_MODEL, SMALL_PAD - 2 * A_HEADS), F32)], 1).astype(BF16)
    wa = w_up_a[l].astype(BF16)
    wb = w_up_b[l].astype(BF16)
    wo = w_o[l].astype(BF16)
    w_router = jnp.concatenate(
        [w_router_group[l], w_router_expert[l],
         jnp.zeros((D_MODEL, ROUTER_PAD - N_GROUPS - N_EXPERTS), F32)], 1).astype(BF16)
    b_router = jnp.concatenate(
        [b_router_group[l], b_router_expert[l],
         jnp.zeros((ROUTER_PAD - N_GROUPS - N_EXPERTS,), F32)])[None, :]
    g1 = norm1_w[l][None, :]
    g2 = norm2_w[l][None, :]
    alog_row = _lane_row(a_log[l], LANE_DECAY)
    dtb_row = _lane_row(dt_bias[l], LANE_DECAY)
    anw_row = a_norm_w[l][None, :]

    xp = x_prompt.reshape(Bp * Sp, D_MODEL)
    xs = x_sample.reshape(n_s, D_MODEL)
    x_small = jnp.concatenate([xs, meta_tokens], 0)
    n_small = x_small.shape[0]

    proj_p, small_p = _inproj(xp, g1, w_main, w_small, tm=1024)
    proj_s, small_s = _inproj(x_small, g1, w_main, w_small, tm=n_small)

    dm = functools.partial(_delta_mixer, conv_w=conv_w[l], alog_row=alog_row, dtb_row=dtb_row, anw_row=anw_row)
    pad_rows = lambda a, n: jnp.concatenate([a, jnp.zeros((n - a.shape[0],) + a.shape[1:], a.dtype)], 0)
    meta_proj = pad_rows(proj_s[n_s:], A_CHUNK)
    meta_small = pad_rows(small_s[n_s:], A_CHUNK)
    zero_prev = jnp.zeros((1, SUBLANES, A_CONV_DIM), F32)
    zero_state = jnp.zeros((1, A_HEADS, A_DK, A_DV), F32)
    _, s_meta = dm(meta_proj, meta_small, zero_prev, zero_state, nb=1, nc=1, C=A_CHUNK, n_valid=N_META,
                   shared_first=True)
    meta_tail = proj_s[n_s + N_META - SUBLANES:, COL_QKV:COL_QKV + A_CONV_DIM][None]
    ya_p, sdelta_p = dm(proj_p, small_p, meta_tail, s_meta, nb=Bp, nc=Sp // A_CHUNK, C=A_CHUNK,
                        n_valid=A_CHUNK, shared_first=True, nsub=DELTA_CHUNKS_PER_STEP)
    conv_p = proj_p.reshape(Bp, Sp, PROJ_MAIN)[:, Sp - (A_CONV - 1):, COL_QKV:COL_QKV + A_CONV_DIM]
    CS = SUBLANES
    samp_proj = jnp.pad(proj_s[:n_s].reshape(Bs, Ss, PROJ_MAIN), ((0, 0), (0, CS - Ss), (0, 0)))
    samp_small = jnp.pad(small_s[:n_s].reshape(Bs, Ss, SMALL_PAD), ((0, 0), (0, CS - Ss), (0, 0)))
    samp_prev = jnp.pad(state_conv[l], ((0, 0), (SUBLANES - (A_CONV - 1), 0), (0, 0)))
    ya_s8, sdelta_s = dm(samp_proj.reshape(Bs * CS, PROJ_MAIN), samp_small.reshape(Bs * CS, SMALL_PAD),
                         samp_prev, state_delta[l], nb=Bs, nc=1, C=CS, n_valid=Ss, shared_first=False,
                         nsub=SAMPLE_SEQS_PER_STEP, chained=False)
    ya_s = ya_s8.reshape(Bs, CS, A_V)[:, :Ss].reshape(n_s, A_V)
    qkv_s = proj_s[:n_s, COL_QKV:COL_QKV + A_CONV_DIM].reshape(Bs, Ss, A_CONV_DIM)
    conv_s = jnp.concatenate([state_conv[l], qkv_s], 1)[:, -(A_CONV - 1):]

    gq = jnp.tile(q_norm_w[l], B_HEADS)[None]
    gk = jnp.tile(k_norm_w[l], B_KV_HEADS)[None]
    bd = ((jnp.arange(B_Q) // B_HD)[:, None] == jnp.arange(LANES)[None, :]).astype(BF16)
    bdt = bd.T
    pos_small = jnp.concatenate([PAST_LEN + jnp.arange(n_s, dtype=jnp.int32) % Ss,
                                 jnp.arange(N_META, dtype=jnp.int32)])
    cos_s, sin_s = _rope_tables(pos_small)
    q_rot_s, k_rot_s = _rope_small(proj_s, cos_s, sin_s, gq, gk, bd, bdt)
    k_meta = k_rot_s[n_s:]
    v_meta = proj_s[n_s:, COL_VB:COL_VB + B_KV]
    cos_p, sin_p = _rope_tables(N_META + jnp.arange(Sp, dtype=jnp.int32))
    yb_p, k_last, v_last = _swa_prompt(proj_p, cos_p, sin_p, k_meta, v_meta, gq, gk, bd, bdt, sinks[l],
                                       Bp, Sp // WINDOW)
    cache_shape = (N_META + WINDOW, B_KV_HEADS, B_HD)
    bcast_meta = lambda a: jnp.broadcast_to(a[None], (Bp, N_META, B_KV))
    swk_p = jnp.concatenate([bcast_meta(k_meta), k_last], 1).reshape((Bp,) + cache_shape)
    swv_p = jnp.concatenate([bcast_meta(v_meta), v_last], 1).reshape((Bp,) + cache_shape)
    n_cache = N_META + WINDOW
    key_pad = -(n_cache + Ss) % SUBLANES
    zpad = jnp.zeros((Bs, key_pad, B_KV), F32)
    kk = jnp.concatenate([cache_swa_k[l].reshape(Bs, n_cache, B_KV), k_rot_s[:n_s].reshape(Bs, Ss, B_KV), zpad], 1)
    vv = jnp.concatenate([cache_swa_v[l].reshape(Bs, n_cache, B_KV),
                          proj_s[:n_s, COL_VB:COL_VB + B_KV].reshape(Bs, Ss, B_KV), zpad], 1)
    q8 = jnp.pad(q_rot_s[:n_s].reshape(Bs, Ss, B_Q), ((0, 0), (0, SUBLANES - Ss), (0, 0)))
    yb_s = _swa_sample(q8, kk, vv, sinks[l], n_new=Ss)[:, :Ss].reshape(n_s, B_Q)
    new_cache = lambda t: jnp.concatenate(
        [t[:, :N_META], t[:, n_cache + Ss - WINDOW:n_cache + Ss]], 1).reshape((Bs,) + cache_shape)
    swk_s, swv_s = new_cache(kk), new_cache(vv)

    merged_p = _merge(ya_p, yb_p, proj_p, wa, wb, tm=512)
    merged_s = _merge(ya_s, yb_s, proj_s[:n_s], wa, wb, tm=256)
    cnt0 = jnp.zeros((1, ROUTER_PAD), F32)
    h2_p, xn2_p, gt_p, rt_p, cnt_p = _outproj(merged_p, xp, wo, g2, w_router, b_router, cnt0, tm=256)
    h2_s, xn2_s, gt_s, rt_s, cnt = _outproj(merged_s, xs, wo, g2, w_router, b_router, cnt_p, tm=256)

    cat = lambda a, b: jnp.concatenate([a, b], 0)
    n_p = Bp * Sp
    dest, pend, blk_e, n_used, nxt_e, wslot, n_slots = _moe_plan(cat(rt_p, rt_s), cnt)
    xs_slots = _dispatch(xn2_p, xn2_s, dest, pend, n_slots)
    ys = _moe_ffn(xs_slots, blk_e, n_used, nxt_e, wslot, w_gate[l], w_up[l], w_down[l])
    y_prompt = _combine(h2_p, gt_p, dest[:n_p], ys).reshape(Bp, Sp, D_MODEL)
    y_sample = _combine(h2_s, gt_s, dest[n_p:], ys).reshape(Bs, Ss, D_MODEL)
    return (y_prompt, y_sample, sdelta_p[None], conv_p[None], swk_p[None], swv_p[None],
            sdelta_s[None], conv_s[None], swk_s[None], swv_s[None])
```
